```python
import numpy as np
import jax
import jax.numpy as jnp
from jax import lax

D_MODEL = 2048
BATCH = 4
SEQ = 4096
DEPTH = 2

HEAD_DIM = 128
NSA_HEADS = 8
NSA_GROUPS = 2
NSA_HPG = NSA_HEADS // NSA_GROUPS
NSA_WIDTH = NSA_HEADS * HEAD_DIM
NSA_KV_WIDTH = NSA_GROUPS * HEAD_DIM
CMP_BLOCK = 32
CMP_STRIDE = 16
SEL_BLOCK = 64
SEL_TOPN = 16
WINDOW = 512
WIN_Q_BLOCK = 128
SEL_Q_CHUNK = 64
ROPE_DIM = HEAD_DIM // 4
ROPE_THETA = 500000.0
SEL_FORCE = 1.0e6
NEG = -1.0e30
MLSTM_HEADS = 4
MLSTM_DIM = 256
MLSTM_WIDTH = MLSTM_HEADS * MLSTM_DIM
MLSTM_CHUNK = 64
CONV_WIDTH = 4
RET_HEADS = 8
RET_QK_DIM = 256
RET_V_DIM = 512
RET_QK_WIDTH = RET_HEADS * RET_QK_DIM
RET_V_WIDTH = RET_HEADS * RET_V_DIM
RET_CHUNK = 64
RET_ROPE_THETA = 10000.0
FFN_DIM = 5632
N_EXPERTS = 8
TOP_K = 2
MOE_BLOCK = 256
RMS_EPS = 1e-6
POS_OFFSET_RANGE = 1024

EVEN_SPLITS = (NSA_WIDTH, NSA_KV_WIDTH, NSA_KV_WIDTH, NSA_KV_WIDTH, NSA_KV_WIDTH, NSA_KV_WIDTH, NSA_KV_WIDTH, NSA_HEADS * 3, MLSTM_WIDTH, MLSTM_WIDTH, MLSTM_WIDTH, MLSTM_HEADS, MLSTM_HEADS, MLSTM_WIDTH)
EVEN_IN = sum(EVEN_SPLITS)
ODD_SPLITS = (RET_QK_WIDTH, RET_QK_WIDTH, RET_V_WIDTH, RET_V_WIDTH)
ODD_IN = sum(ODD_SPLITS)
N_EVEN = (DEPTH + 1) // 2
N_ODD = DEPTH // 2

kernel_name = 'hybrid_nsa_mlstm_retention_moe'


def _split(y, sizes):
    return jnp.split(y, np.cumsum(sizes)[:-1].tolist(), axis=-1)


def _heads(u, n):
    return u.reshape(u.shape[0], u.shape[1], n, -1)


def rmsnorm(x, g):
    xf = x.astype(jnp.float32)
    y = xf * lax.rsqrt(jnp.mean(xf * xf, axis=-1, keepdims=True) + RMS_EPS)
    return (y * g.astype(jnp.float32)).astype(x.dtype)


def rope(x, pos, rot_dim, theta):
    half = rot_dim // 2
    inv = jnp.power(jnp.float32(theta), -jnp.arange(half, dtype=jnp.float32) * (2.0 / rot_dim))
    ang = pos.astype(jnp.float32)[..., None] * inv
    cos = jnp.cos(ang)[:, :, None, :]
    sin = jnp.sin(ang)[:, :, None, :]
    xf = x.astype(jnp.float32)
    x1, x2 = xf[..., :half], xf[..., half:rot_dim]
    out = jnp.concatenate([x1 * cos - x2 * sin, x1 * sin + x2 * cos, xf[..., rot_dim:]], axis=-1)
    return out.astype(x.dtype)


def _to_chunks(u, size):
    b, s, h = u.shape[:3]
    u = u.astype(jnp.float32).reshape((b, s // size, size, h) + u.shape[3:])
    return jnp.moveaxis(u, (1, 3), (0, 2))


def _from_chunks(u):
    u = jnp.moveaxis(u, (0, 2), (1, 3))
    return u.reshape((u.shape[0], u.shape[1] * u.shape[2]) + u.shape[3:])


def causal_conv_silu(u, w):
    width = w.shape[0]
    s = u.shape[1]
    up = jnp.pad(u, ((0, 0), (width - 1, 0), (0, 0)))
    y = sum(up[:, i:i + s] * w[i] for i in range(width))
    return jax.nn.silu(y)


def swiglu(h, wg, wu, wd):
    return (jax.nn.silu(h @ wg) * (h @ wu)) @ wd


def nsa_attention(q, k_cmp, v_cmp, k_sel, v_sel, k_win, v_win, gate_pre, pos, w_cmp_k, w_cmp_v, pe_cmp, q_gain, k_gain):
    f32 = jnp.float32
    b, s = q.shape[:2]
    g, p, d = NSA_GROUPS, NSA_HPG, HEAD_DIM
    scale = d ** -0.5
    t = jnp.arange(s)
    q = rope(rmsnorm(q, q_gain), pos, ROPE_DIM, ROPE_THETA).reshape(b, s, g, p, d)

    nc = s // CMP_STRIDE - 1
    def compress(tok, w):
        halves = tok.reshape(b, s // CMP_STRIDE, CMP_STRIDE, g, d)
        blk = jnp.concatenate([halves[:, :-1], halves[:, 1:]], axis=2)
        blk = blk + pe_cmp[None, None, :, None, :].astype(blk.dtype)
        return jnp.einsum('bclgd,lde->bcge', blk, w.reshape(CMP_BLOCK, d, d))
    cmp_pos = pos[:, CMP_BLOCK - 1::CMP_STRIDE][:, :nc]
    kc = rope(rmsnorm(compress(k_cmp, w_cmp_k), k_gain), cmp_pos, ROPE_DIM, ROPE_THETA)
    vc = compress(v_cmp, w_cmp_v).astype(f32)
    cmp_end = jnp.arange(nc) * CMP_STRIDE + CMP_BLOCK - 1
    cmask = cmp_end[None, :] <= t[:, None]
    s_c = jnp.einsum('bsgpd,bcgd->bgpsc', q, kc).astype(f32) * scale
    p_c = jax.nn.softmax(jnp.where(cmask, s_c, NEG), axis=-1) * cmask
    o_cmp = jnp.einsum('bgpsc,bcgd->bsgpd', p_c, vc)

    ns = s // SEL_BLOCK
    c_start = np.arange(nc) * CMP_STRIDE
    j_start = np.arange(ns) * SEL_BLOCK
    overlap = jnp.asarray(((c_start[:, None] < j_start[None, :] + SEL_BLOCK) & (c_start[:, None] + CMP_BLOCK > j_start[None, :])).astype(np.float32))
    imp = jnp.einsum('bgpsc,cj->bgsj', p_c, overlap)
    jb = jnp.arange(ns)[None, :]
    cur = (t // SEL_BLOCK)[:, None]
    forced = (jb == 0) | (jb == cur) | (jb == cur - 1)
    imp = jnp.where(jb <= cur, jnp.where(forced, SEL_FORCE, imp), NEG)
    n_top = min(SEL_TOPN, ns)
    _, sel = lax.top_k(imp, n_top)

    ksb = rope(rmsnorm(k_sel, k_gain), pos, ROPE_DIM, ROPE_THETA).reshape(b, ns, SEL_BLOCK, g, d).transpose(0, 3, 1, 2, 4)
    vsb = v_sel.reshape(b, ns, SEL_BLOCK, g, d).transpose(0, 3, 1, 2, 4)
    nq = s // SEL_Q_CHUNK
    q_ch = q.reshape(b, nq, SEL_Q_CHUNK, g, p, d).transpose(1, 0, 3, 4, 2, 5)
    sel_ch = sel.reshape(b, g, nq, SEL_Q_CHUNK, n_top).transpose(2, 0, 1, 3, 4)
    t_ch = t.reshape(nq, SEL_Q_CHUNK)
    bi = jnp.arange(b)[:, None, None, None]
    gi = jnp.arange(g)[None, :, None, None]
    koff = jnp.arange(SEL_BLOCK)
    def sel_chunk(args):
        qc, idx, tq = args
        kg = ksb[bi, gi, idx]
        vg = vsb[bi, gi, idx]
        sc = jnp.einsum('bgpqd,bgqnkd->bgpqnk', qc, kg).astype(f32) * scale
        kpos = idx[..., None] * SEL_BLOCK + koff
        m = (kpos <= tq[None, None, :, None, None])[:, :, None]
        sc = jnp.where(m, sc, NEG).reshape(b, g, p, SEL_Q_CHUNK, n_top * SEL_BLOCK)
        pr = jax.nn.softmax(sc, axis=-1).reshape(b, g, p, SEL_Q_CHUNK, n_top, SEL_BLOCK)
        return jnp.einsum('bgpqnk,bgqnkd->bgpqd', pr, vg.astype(f32))
    o_sel = lax.map(sel_chunk, (q_ch, sel_ch, t_ch))
    o_sel = o_sel.transpose(1, 0, 4, 2, 3, 5).reshape(b, s, g, p, d)

    kwt = rope(rmsnorm(k_win, k_gain), pos, ROPE_DIM, ROPE_THETA)
    nb = s // WIN_Q_BLOCK
    nprev = WINDOW // WIN_Q_BLOCK
    def band(u):
        up = jnp.pad(u, ((0, 0), (WINDOW, 0), (0, 0), (0, 0))).reshape(b, nb + nprev, WIN_Q_BLOCK, g, d)
        return jnp.concatenate([up[:, i:i + nb] for i in range(nprev + 1)], axis=2)
    kb, vb = band(kwt), band(v_win)
    qb = q.reshape(b, nb, WIN_Q_BLOCK, g, p, d)
    s_w = jnp.einsum('bnqgpd,bnkgd->bngpqk', qb, kb).astype(f32) * scale
    blk0 = jnp.arange(nb)[:, None] * WIN_Q_BLOCK
    qpos = blk0 + jnp.arange(WIN_Q_BLOCK)
    kpos = blk0 - WINDOW + jnp.arange(WINDOW + WIN_Q_BLOCK)
    wmask = (kpos[:, None, :] <= qpos[:, :, None]) & (kpos[:, None, :] > qpos[:, :, None] - WINDOW) & (kpos[:, None, :] >= 0)
    p_w = jax.nn.softmax(jnp.where(wmask[None, :, None, None], s_w, NEG), axis=-1)
    o_win = jnp.einsum('bngpqk,bnkgd->bnqgpd', p_w, vb.astype(f32)).reshape(b, s, g, p, d)

    gt = jax.nn.sigmoid(gate_pre.astype(f32)).reshape(b, s, g, p, 3)
    o = gt[..., 0:1] * o_cmp + gt[..., 1:2] * o_sel + gt[..., 2:3] * o_win
    return o.reshape(b, s, NSA_HEADS * d)


def mlstm(q, k, v, i_pre, f_pre, f_bias):
    f32 = jnp.float32
    b, s, nh, dh = q.shape
    L = MLSTM_CHUNK
    qc = _to_chunks(q, L)
    kc = _to_chunks(k, L) * (dh ** -0.5)
    vc = _to_chunks(v, L)
    ic = _to_chunks(i_pre, L)
    lfc = jax.nn.log_sigmoid(_to_chunks(f_pre, L) + f_bias.astype(f32)[:, None])
    causal = jnp.tril(jnp.ones((L, L), dtype=bool))
    def step(carry, xs):
        C, n, m = carry
        q_, k_, v_, ig, lf = xs
        a = jnp.cumsum(lf, axis=-1)
        Dlog = jnp.where(causal, a[..., :, None] - a[..., None, :] + ig[..., None, :], NEG)
        inter = a + m[..., None]
        mt = jnp.maximum(inter, Dlog.max(axis=-1))
        W = jnp.exp(Dlog - mt[..., None]) * jnp.einsum('bhtd,bhsd->bhts', q_, k_)
        e = jnp.exp(inter - mt)
        num = e[..., None] * jnp.einsum('bhtd,bhde->bhte', q_, C) + jnp.einsum('bhts,bhse->bhte', W, v_)
        den = e * jnp.einsum('bhtd,bhd->bht', q_, n) + W.sum(axis=-1)
        h = num / jnp.maximum(jnp.abs(den), jnp.exp(-mt))[..., None]
        aL = a[..., -1]
        gs = aL[..., None] - a + ig
        m_new = jnp.maximum(aL + m, gs.max(axis=-1))
        decay = jnp.exp(aL + m - m_new)
        wk = jnp.exp(gs - m_new[..., None])[..., None] * k_
        C = decay[..., None, None] * C + jnp.einsum('bhsd,bhse->bhde', wk, v_)
        n = decay[..., None] * n + wk.sum(axis=-2)
        return (C, n, m_new), h
    init = (jnp.zeros((b, nh, dh, dh), f32), jnp.zeros((b, nh, dh), f32), jnp.zeros((b, nh), f32))
    _, h = lax.scan(step, init, (qc, kc, vc, ic, lfc))
    return _from_chunks(h)


def retention(q, k, v, pos):
    f32 = jnp.float32
    h_, dk = q.shape[2], q.shape[3]
    L = RET_CHUNK
    q = rope(q, pos, dk, RET_ROPE_THETA)
    k = rope(k, pos, dk, RET_ROPE_THETA)
    log_g = jnp.log1p(-jnp.exp2(-5.0 - jnp.arange(h_, dtype=f32)))
    idx = jnp.arange(L, dtype=f32)
    diff = idx[:, None] - idx[None, :]
    Dm = jnp.where(diff >= 0, jnp.exp(jnp.maximum(diff, 0.0) * log_g[:, None, None]), 0.0)
    xi = jnp.exp((idx + 1.0) * log_g[:, None])
    zeta = jnp.exp((L - 1.0 - idx) * log_g[:, None])
    chunk_decay = jnp.exp(L * log_g)
    qc = _to_chunks(q, L)
    kc = _to_chunks(k, L) * (dk ** -0.5)
    vc = _to_chunks(v, L)
    def step(R, xs):
        q_, k_, v_ = xs
        inner = jnp.einsum('bhts,bhse->bhte', jnp.einsum('bhtd,bhsd->bhts', q_, k_) * Dm, v_)
        cross = jnp.einsum('bhtd,bhde->bhte', q_, R) * xi[..., None]
        R = chunk_decay[:, None, None] * R + jnp.einsum('bhsd,bhse->bhde', k_ * zeta[..., None], v_)
        return R, inner + cross
    b = q.shape[0]
    R0 = jnp.zeros((b, h_, dk, v.shape[3]), f32)
    _, o = lax.scan(step, R0, (qc, kc, vc))
    return _from_chunks(o)


def moe_swiglu(x, w_router, w_gate, w_up, w_down):
    f32 = jnp.float32
    b, s, dm = x.shape
    n = b * s
    xt = x.reshape(n, dm)
    logits = (xt @ w_router).astype(f32)
    top_val, top_idx = lax.top_k(logits, TOP_K)
    gate = jax.nn.softmax(top_val, axis=-1)
    flat_e = top_idx.reshape(-1)
    flat_tok = jnp.arange(n * TOP_K, dtype=jnp.int32) // TOP_K
    flat_gate = gate.reshape(-1)
    order = jnp.argsort(flat_e)
    se, stok, sgate = flat_e[order], flat_tok[order], flat_gate[order]
    counts = jnp.bincount(flat_e, length=N_EXPERTS)
    padded = (counts + MOE_BLOCK - 1) // MOE_BLOCK * MOE_BLOCK
    start = jnp.cumsum(counts) - counts
    pend = jnp.cumsum(padded)
    pstart = pend - padded
    dest = pstart[se] + jnp.arange(n * TOP_K, dtype=jnp.int32) - start[se]
    n_blk = -(-(n * TOP_K) // MOE_BLOCK) + N_EXPERTS
    n_rows = n_blk * MOE_BLOCK
    row_tok = jnp.full((n_rows,), n, dtype=jnp.int32).at[dest].set(stok)
    x_ext = jnp.concatenate([xt, jnp.zeros((1, dm), xt.dtype)], axis=0)
    xb = x_ext[row_tok].reshape(n_blk, MOE_BLOCK, dm)
    blk_e = jnp.minimum(jnp.searchsorted(pend, jnp.arange(n_blk) * MOE_BLOCK, side='right'), N_EXPERTS - 1)
    def expert_block(args):
        xb_, e = args
        return (jax.nn.silu(xb_ @ w_gate[e]) * (xb_ @ w_up[e])) @ w_down[e]
    yb = lax.map(expert_block, (xb, blk_e)).reshape(n_rows, dm)
    y = jnp.zeros((n, dm), f32).at[stok].add(yb[dest].astype(f32) * sgate[:, None])
    return y.astype(x.dtype).reshape(b, s, dm)


def setup_inputs(seed: int = 0) -> dict:
    key = jax.random.key(seed)
    keys = iter(jax.random.split(key, 40))
    f32 = jnp.float32
    def dense(shape, fan_in):
        return jax.random.normal(next(keys), shape, f32) * (fan_in ** -0.5)
    def gain(shape):
        return 1.0 + 0.02 * jax.random.normal(next(keys), shape, f32)
    def small(shape, scale):
        return scale * jax.random.normal(next(keys), shape, f32)
    ne, no = N_EVEN, N_ODD
    x = jax.random.normal(next(keys), (BATCH, SEQ, D_MODEL), f32)
    offset = jax.random.randint(next(keys), (BATCH, 1), 0, POS_OFFSET_RANGE, dtype=jnp.int32)
    positions = offset + jnp.arange(SEQ, dtype=jnp.int32)[None, :]
    return {
        'x': x,
        'positions': positions,
        'norm_mix_even': gain((ne, D_MODEL)),
        'w_in_even': dense((ne, D_MODEL, EVEN_IN), D_MODEL),
        'nsa_q_gain': gain((ne, HEAD_DIM)),
        'nsa_k_gain': gain((ne, HEAD_DIM)),
        'w_cmp_k': dense((ne, CMP_BLOCK * HEAD_DIM, HEAD_DIM), CMP_BLOCK * HEAD_DIM),
        'w_cmp_v': dense((ne, CMP_BLOCK * HEAD_DIM, HEAD_DIM), CMP_BLOCK * HEAD_DIM),
        'pe_cmp': small((ne, CMP_BLOCK, HEAD_DIM), 0.1),
        'mlstm_conv': dense((ne, CONV_WIDTH, 2 * MLSTM_WIDTH), CONV_WIDTH),
        'mlstm_f_bias': jnp.linspace(3.0, 6.0, MLSTM_HEADS, dtype=f32)[None, :] + small((ne, MLSTM_HEADS), 0.1),
        'mlstm_norm': gain((ne, MLSTM_WIDTH)),
        'w_out_even': dense((ne, NSA_WIDTH + MLSTM_WIDTH, D_MODEL), NSA_WIDTH + MLSTM_WIDTH),
        'norm_ffn_even': gain((ne, D_MODEL)),
        'ffn_gate': dense((ne, D_MODEL, FFN_DIM), D_MODEL),
        'ffn_up': dense((ne, D_MODEL, FFN_DIM), D_MODEL),
        'ffn_down': dense((ne, FFN_DIM, D_MODEL), FFN_DIM),
        'norm_mix_odd': gain((no, D_MODEL)),
        'w_in_odd': dense((no, D_MODEL, ODD_IN), D_MODEL),
        'ret_norm': gain((no, RET_V_WIDTH)),
        'w_out_odd': dense((no, RET_V_WIDTH, D_MODEL), RET_V_WIDTH),
        'norm_ffn_odd': gain((no, D_MODEL)),
        'w_router': dense((no, D_MODEL, N_EXPERTS), D_MODEL),
        'exp_gate': dense((no, N_EXPERTS, D_MODEL, FFN_DIM), D_MODEL),
        'exp_up': dense((no, N_EXPERTS, D_MODEL, FFN_DIM), D_MODEL),
        'exp_down': dense((no, N_EXPERTS, FFN_DIM, D_MODEL), FFN_DIM),
    }


def reference(x, positions, norm_mix_even, w_in_even, nsa_q_gain, nsa_k_gain, w_cmp_k, w_cmp_v, pe_cmp, mlstm_conv, mlstm_f_bias, mlstm_norm, w_out_even, norm_ffn_even, ffn_gate, ffn_up, ffn_down, norm_mix_odd, w_in_odd, ret_norm, w_out_odd, norm_ffn_odd, w_router, exp_gate, exp_up, exp_down):
    b, s, _ = x.shape
    for layer in range(DEPTH):
        j = layer // 2
        if layer % 2 == 0:
            h = rmsnorm(x, norm_mix_even[j])
            (q_a, kc_a, vc_a, ks_a, vs_a, kw_a, vw_a, g_a, q_b, k_b, v_b, i_b, f_b, o_b) = _split(h @ w_in_even[j], EVEN_SPLITS)
            o_nsa = nsa_attention(_heads(q_a, NSA_HEADS), _heads(kc_a, NSA_GROUPS), _heads(vc_a, NSA_GROUPS), _heads(ks_a, NSA_GROUPS), _heads(vs_a, NSA_GROUPS), _heads(kw_a, NSA_GROUPS), _heads(vw_a, NSA_GROUPS), _heads(g_a, NSA_HEADS), positions, w_cmp_k[j], w_cmp_v[j], pe_cmp[j], nsa_q_gain[j], nsa_k_gain[j])
            qk_b = causal_conv_silu(jnp.concatenate([q_b, k_b], axis=-1), mlstm_conv[j])
            h_b = mlstm(_heads(qk_b[..., :MLSTM_WIDTH], MLSTM_HEADS), _heads(qk_b[..., MLSTM_WIDTH:], MLSTM_HEADS), _heads(v_b, MLSTM_HEADS), i_b, f_b, mlstm_f_bias[j])
            h_b = rmsnorm(h_b, mlstm_norm[j].reshape(MLSTM_HEADS, MLSTM_DIM)).reshape(b, s, MLSTM_WIDTH) * jax.nn.sigmoid(o_b.astype(jnp.float32))
            mixed = jnp.concatenate([o_nsa, h_b], axis=-1).astype(x.dtype)
            x = x + mixed @ w_out_even[j]
            x = x + swiglu(rmsnorm(x, norm_ffn_even[j]), ffn_gate[j], ffn_up[j], ffn_down[j])
        else:
            h = rmsnorm(x, norm_mix_odd[j])
            q_c, k_c, v_c, g_c = _split(h @ w_in_odd[j], ODD_SPLITS)
            y = retention(_heads(q_c, RET_HEADS), _heads(k_c, RET_HEADS), _heads(v_c, RET_HEADS), positions)
            y = rmsnorm(y, ret_norm[j].reshape(RET_HEADS, RET_V_DIM)).reshape(b, s, RET_V_WIDTH) * jax.nn.silu(g_c.astype(jnp.float32))
            x = x + y.astype(x.dtype) @ w_out_odd[j]
            x = x + moe_swiglu(rmsnorm(x, norm_ffn_odd[j]), w_router[j], exp_gate[j], exp_up[j], exp_down[j])
    return x
```

```python
import functools

import numpy as np
import jax
import jax.numpy as jnp
from jax import lax
from jax.experimental import pallas as pl
from jax.experimental.pallas import tpu as pltpu

F32 = jnp.float32
BF16 = jnp.bfloat16
I32 = jnp.int32
HIGHEST = lax.Precision.HIGHEST

HEAD_DIM = 128
NSA_HEADS = 8
NSA_GROUPS = 2
NSA_HPG = NSA_HEADS // NSA_GROUPS
NSA_WIDTH = NSA_HEADS * HEAD_DIM
NSA_KV_WIDTH = NSA_GROUPS * HEAD_DIM
NSA_SLAB = NSA_WIDTH + 6 * NSA_KV_WIDTH
CMP_BLOCK = 32
CMP_STRIDE = 16
SEL_BLOCK = 64
SEL_TOPN = 16
WINDOW = 512
ROPE_DIM = HEAD_DIM // 4
ROPE_THETA = 500000.0
SEL_FORCE = 1.0e6
NEG = -1.0e30
LOWEST = -3.0e38
MLSTM_HEADS = 4
MLSTM_DIM = 256
MLSTM_WIDTH = MLSTM_HEADS * MLSTM_DIM
CONV_WIDTH = 4
RET_HEADS = 8
RET_QK_DIM = 256
RET_V_DIM = 512
RET_QK_WIDTH = RET_HEADS * RET_QK_DIM
RET_V_WIDTH = RET_HEADS * RET_V_DIM
RET_ROPE_THETA = 10000.0
N_EXPERTS = 8
RMS_EPS = 1e-6

LANES = 128
BF16_SUBLANES = 16
V7X_VMEM_BYTES = 64 * 1024 * 1024
VMEM_LIMIT = V7X_VMEM_BYTES - 8 * 1024 * 1024

CHUNK = 256
NSA_TQ = 128
NSA_KT = 512
MOE_TM = 512
GATE_I_LANE = 16
GATE_F_LANE = 20


def _cparams(sem, vmem=VMEM_LIMIT):
    return pltpu.CompilerParams(dimension_semantics=sem, vmem_limit_bytes=vmem)


def _tile(n, target, quantum):
    if n <= target:
        return n
    t = (target // quantum) * quantum
    while t > quantum and n % t:
        t -= quantum
    assert n % t == 0, (n, target, quantum)
    return t


def _sigmoid(x):
    return 1.0 / (1.0 + jnp.exp(-x))


def _dot_nt(a, b):
    return lax.dot_general(a, b, (((1,), (1,)), ((), ())), preferred_element_type=F32)


def _dot_tn(a, b):
    return lax.dot_general(a, b, (((0,), (0,)), ((), ())), preferred_element_type=F32)


def _rmsnorm_kernel(x_ref, g_ref, o_ref):
    x = x_ref[...]
    y = x * lax.rsqrt(jnp.mean(x * x, axis=-1, keepdims=True) + RMS_EPS)
    o_ref[...] = (y * g_ref[...]).astype(o_ref.dtype)


def rmsnorm_rows(x, g, tm=512):
    m, d = x.shape
    tm = _tile(m, tm, 8)
    return pl.pallas_call(
        _rmsnorm_kernel,
        grid=(m // tm,),
        in_specs=[pl.BlockSpec((tm, d), lambda i: (i, 0)), pl.BlockSpec((1, d), lambda i: (0, 0))],
        out_specs=pl.BlockSpec((tm, d), lambda i: (i, 0)),
        out_shape=jax.ShapeDtypeStruct((m, d), BF16),
        compiler_params=_cparams(("parallel",)),
        name="rmsnorm",
    )(x, g.reshape(1, d).astype(F32))


def _gmm_kernel(be_ref, nu_ref, x_ref, w_ref, *rest, has_res):
    if has_res:
        r_ref, o_ref, wb_ref = rest
    else:
        o_ref, wb_ref = rest
    i = pl.program_id(1)
    changed = be_ref[i] != be_ref[jnp.maximum(i - 1, 0)]

    @pl.when((i == 0) | changed)
    def _():
        wb_ref[...] = w_ref[0].astype(BF16)

    @pl.when(i < nu_ref[0])
    def _():
        acc = jnp.dot(x_ref[...], wb_ref[...], preferred_element_type=F32)
        if has_res:
            acc = r_ref[...] + acc
        o_ref[...] = acc.astype(o_ref.dtype)

    @pl.when(i >= nu_ref[0])
    def _():
        o_ref[...] = jnp.zeros_like(o_ref)


def gmm(x, w, blk_e, nused, *, tm, tn, out_dtype, residual=None):
    m, k = x.shape
    e, k2, n = w.shape
    assert k == k2 and m % tm == 0 and n % tn == 0
    nb = m // tm
    in_specs = [
        pl.BlockSpec((tm, k), lambda j, i, be, nu: (jnp.minimum(i, nu[0] - 1), 0)),
        pl.BlockSpec((1, k, tn), lambda j, i, be, nu: (be[i], 0, j)),
    ]
    args = [x, w]
    if residual is not None:
        in_specs.append(pl.BlockSpec((tm, tn), lambda j, i, be, nu: (i, j)))
        args.append(residual)
    return pl.pallas_call(
        functools.partial(_gmm_kernel, has_res=residual is not None),
        grid_spec=pltpu.PrefetchScalarGridSpec(
            num_scalar_prefetch=2,
            grid=(n // tn, nb),
            in_specs=in_specs,
            out_specs=pl.BlockSpec((tm, tn), lambda j, i, be, nu: (i, j)),
            scratch_shapes=[pltpu.VMEM((k, tn), BF16)],
        ),
        out_shape=jax.ShapeDtypeStruct((m, n), out_dtype),
        compiler_params=_cparams(("arbitrary", "arbitrary")),
        name="gmm",
    )(blk_e, nused, *args)


def _gmm_swiglu_kernel(be_ref, nu_ref, x_ref, wg_ref, wu_ref, o_ref, wgb_ref, wub_ref):
    i = pl.program_id(1)
    changed = be_ref[i] != be_ref[jnp.maximum(i - 1, 0)]

    @pl.when((i == 0) | changed)
    def _():
        wgb_ref[...] = wg_ref[0].astype(BF16)
        wub_ref[...] = wu_ref[0].astype(BF16)

    @pl.when(i < nu_ref[0])
    def _():
        x = x_ref[...]
        g = jnp.dot(x, wgb_ref[...], preferred_element_type=F32)
        u = jnp.dot(x, wub_ref[...], preferred_element_type=F32)
        o_ref[...] = (g * _sigmoid(g) * u).astype(o_ref.dtype)

    @pl.when(i >= nu_ref[0])
    def _():
        o_ref[...] = jnp.zeros_like(o_ref)


def gmm_swiglu(x, wg, wu, blk_e, nused, *, tm, tn):
    m, k = x.shape
    e, k2, n = wg.shape
    assert k == k2 and wu.shape == wg.shape and m % tm == 0 and n % tn == 0
    nb = m // tm
    w_spec = pl.BlockSpec((1, k, tn), lambda j, i, be, nu: (be[i], 0, j))
    return pl.pallas_call(
        _gmm_swiglu_kernel,
        grid_spec=pltpu.PrefetchScalarGridSpec(
            num_scalar_prefetch=2,
            grid=(n // tn, nb),
            in_specs=[pl.BlockSpec((tm, k), lambda j, i, be, nu: (jnp.minimum(i, nu[0] - 1), 0)),
                      w_spec, w_spec],
            out_specs=pl.BlockSpec((tm, tn), lambda j, i, be, nu: (i, j)),
            scratch_shapes=[pltpu.VMEM((k, tn), BF16), pltpu.VMEM((k, tn), BF16)],
        ),
        out_shape=jax.ShapeDtypeStruct((m, n), BF16),
        compiler_params=_cparams(("arbitrary", "arbitrary")),
        name="gmm_swiglu",
    )(blk_e, nused, x, wg, wu)


def _dense_blocks(m, tm):
    nb = m // tm
    return jnp.zeros((nb,), I32), jnp.full((1,), nb, I32)


def mm(x, w, *, out_dtype, residual=None, tm=1024, tn=512):
    m, k = x.shape
    n = w.shape[1]
    tm = _tile(m, tm, 16)
    tn = _tile(n, tn, LANES)
    be, nu = _dense_blocks(m, tm)
    return gmm(x, w[None], be, nu, tm=tm, tn=tn, out_dtype=out_dtype, residual=residual)


def mm_swiglu(x, wg, wu, *, tm=1024, tn=512):
    m = x.shape[0]
    tm = _tile(m, tm, 16)
    tn = _tile(wg.shape[1], tn, LANES)
    be, nu = _dense_blocks(m, tm)
    return gmm_swiglu(x, wg[None], wu[None], be, nu, tm=tm, tn=tn)


def _trig_kernel(pos_ref, inv_ref, cos_ref, sin_ref):
    ang = pos_ref[...] * inv_ref[...]
    cos_ref[...] = jnp.cos(ang)
    sin_ref[...] = jnp.sin(ang)


def trig_tables(pos, inv_lane):
    r = pos.size
    pos_b = jnp.broadcast_to(pos.astype(F32).reshape(r, 1), (r, LANES))
    tr = _tile(r, 512, 8)
    spec = pl.BlockSpec((tr, LANES), lambda i: (i, 0))
    return pl.pallas_call(
        _trig_kernel,
        grid=(r // tr,),
        in_specs=[spec, pl.BlockSpec((1, LANES), lambda i: (0, 0))],
        out_specs=[spec, spec],
        out_shape=[jax.ShapeDtypeStruct((r, LANES), F32)] * 2,
        compiler_params=_cparams(("parallel",)),
        name="trig_tables",
    )(pos_b, inv_lane.reshape(1, LANES))


def _nsa_inv_lane():
    half = ROPE_DIM // 2
    inv = jnp.power(jnp.float32(ROPE_THETA), -jnp.arange(half, dtype=F32) * (2.0 / ROPE_DIM))
    return jnp.concatenate([inv, inv, jnp.zeros((LANES - ROPE_DIM,), F32)])


def _ret_inv_lane():
    half = RET_QK_DIM // 2
    return jnp.power(jnp.float32(RET_ROPE_THETA), -jnp.arange(half, dtype=F32) * (2.0 / RET_QK_DIM))


def _norm_rope_head(x, gain, cos, sin):
    half = ROPE_DIM // 2
    y = x * lax.rsqrt(jnp.mean(x * x, axis=-1, keepdims=True) + RMS_EPS) * gain
    lane = lax.broadcasted_iota(I32, y.shape, 1)
    from_hi = jnp.where(lane < half, -sin, 0.0)
    from_lo = jnp.where((lane >= half) & (lane < ROPE_DIM), sin, 0.0)
    return (y * cos + pltpu.roll(y, LANES - half, 1) * from_hi + pltpu.roll(y, half, 1) * from_lo)


def _nsa_prep_kernel(p_ref, cos_ref, sin_ref, qg_ref, kg_ref, o_ref):
    cos = cos_ref[0]
    sin = sin_ref[0]
    scale = HEAD_DIM ** -0.5
    d = HEAD_DIM
    for hd in range(NSA_HEADS):
        sl = slice(hd * d, (hd + 1) * d)
        q = _norm_rope_head(p_ref[0, :, sl].astype(F32), qg_ref[...], cos, sin)
        o_ref[0, :, sl] = (q * scale).astype(BF16)
    for slab in range(6):
        for g in range(NSA_GROUPS):
            off = NSA_WIDTH + slab * NSA_KV_WIDTH + g * d
            sl = slice(off, off + d)
            if slab in (2, 4):
                k = _norm_rope_head(p_ref[0, :, sl].astype(F32), kg_ref[...], cos, sin)
                o_ref[0, :, sl] = k.astype(BF16)
            else:
                o_ref[0, :, sl] = p_ref[0, :, sl]


def nsa_prep(p3, cos, sin, q_gain, k_gain):
    b, s, _ = p3.shape
    t = _tile(s, 256, 16)
    tab = pl.BlockSpec((1, t, LANES), lambda bi, i: (bi, i, 0))
    gain = pl.BlockSpec((1, HEAD_DIM), lambda bi, i: (0, 0))
    blk = pl.BlockSpec((1, t, NSA_SLAB), lambda bi, i: (bi, i, 0))
    return pl.pallas_call(
        _nsa_prep_kernel,
        grid=(b, s // t),
        in_specs=[blk, tab, tab, gain, gain],
        out_specs=blk,
        out_shape=jax.ShapeDtypeStruct((b, s, NSA_SLAB), BF16),
        compiler_params=_cparams(("parallel", "parallel")),
        name="nsa_prep",
    )(p3, cos, sin, q_gain.reshape(1, HEAD_DIM), k_gain.reshape(1, HEAD_DIM))


def _compress_kernel(blk_ref, pe_ref, w_ref, kg_ref, cos_ref, sin_ref, o_ref, *, is_key):
    a = (blk_ref[...].astype(F32) + pe_ref[...]).astype(BF16)
    y = jnp.dot(a, w_ref[...].astype(BF16), preferred_element_type=F32)
    if is_key:
        y = _norm_rope_head(y, kg_ref[...], cos_ref[0], sin_ref[0])
    o_ref[...] = y.astype(BF16)


def compress(blk, pe, w, k_gain, cos_c, sin_c, *, is_key, rows_per_seq, groups):
    r, kdim = blk.shape
    t = rows_per_seq
    tab = pl.BlockSpec((1, t, LANES), lambda i: (i // groups, 0, 0))
    return pl.pallas_call(
        functools.partial(_compress_kernel, is_key=is_key),
        grid=(r // t,),
        in_specs=[pl.BlockSpec((t, kdim), lambda i: (i, 0)),
                  pl.BlockSpec((1, kdim), lambda i: (0, 0)),
                  pl.BlockSpec((kdim, HEAD_DIM), lambda i: (0, 0)),
                  pl.BlockSpec((1, HEAD_DIM), lambda i: (0, 0)),
                  tab, tab],
        out_specs=pl.BlockSpec((t, HEAD_DIM), lambda i: (i, 0)),
        out_shape=jax.ShapeDtypeStruct((r, HEAD_DIM), BF16),
        compiler_params=_cparams(("parallel",)),
        name="nsa_compress",
    )(blk, pe.reshape(1, kdim), w, k_gain.reshape(1, HEAD_DIM), cos_c, sin_c)


def _nsa_attn_kernel(q_ref, kc_ref, vc_ref, ks_ref, vs_ref, kw_ref, vw_ref, et_ref, ov_ref, gate_ref,
                     o_ref, m_sc, l_sc, acc_sc, val_sc, *, tq, kt, wk, ns, n_top):
    i = pl.program_id(2)
    t0 = i * tq
    rows = NSA_HPG * tq
    d = HEAD_DIM
    q2 = q_ref[0]
    q = jnp.concatenate([q2[:, p * d:(p + 1) * d] for p in range(NSA_HPG)], axis=0)
    t_row = t0 + (lax.broadcasted_iota(I32, (rows, 1), 0) & (tq - 1))

    ncp = kc_ref.shape[1]
    s_c = _dot_nt(q, kc_ref[0])
    c_end = lax.broadcasted_iota(I32, (rows, ncp), 1) * CMP_STRIDE + (CMP_BLOCK - 1)
    cmask = c_end <= t_row
    s_c = jnp.where(cmask, s_c, NEG)
    e_c = jnp.where(cmask, jnp.exp(s_c - jnp.max(s_c, axis=-1, keepdims=True)), 0.0)
    den_c = jnp.sum(e_c, axis=-1, keepdims=True)
    p_c = e_c * (1.0 / jnp.maximum(den_c, 1e-30))
    o_cmp = jnp.dot(p_c.astype(BF16), vc_ref[0], preferred_element_type=F32)

    p_grp = p_c[0:tq]
    for p in range(1, NSA_HPG):
        p_grp = p_grp + p_c[p * tq:(p + 1) * tq]
    imp = jnp.dot(p_grp, ov_ref[...], precision=HIGHEST, preferred_element_type=F32)
    jb = lax.broadcasted_iota(I32, (tq, LANES), 1)
    cur = (t0 + lax.broadcasted_iota(I32, (tq, 1), 0)) // SEL_BLOCK
    forced = (jb == 0) | (jb == cur) | (jb == cur - 1)
    val = jnp.where(jb <= cur, jnp.where(forced, SEL_FORCE, imp), NEG)
    val = jnp.where(jb < ns, val, LOWEST)
    val_sc[...] = val.T
    v_t = val_sc[...]
    j_row = lax.broadcasted_iota(I32, (LANES, tq), 0)
    beaten = jnp.zeros((LANES, tq), F32)
    for j2 in range(ns):
        r = val_sc[j2:j2 + 1, :]
        ge = jnp.where(r >= v_t, 1.0, 0.0)
        gt = jnp.where(r > v_t, 1.0, 0.0)
        beaten = beaten + jnp.where(j_row > j2, ge, gt)
    bias_t = jnp.where(j_row < ns, jnp.where(beaten < n_top, 0.0, NEG), 0.0)
    bias = bias_t.T.astype(BF16)
    q_aug = jnp.concatenate([q, jnp.concatenate([bias] * NSA_HPG, axis=0)], axis=1)

    m_sc[...] = jnp.full(m_sc.shape, NEG, F32)
    l_sc[...] = jnp.zeros(l_sc.shape, F32)
    acc_sc[...] = jnp.zeros(acc_sc.shape, F32)

    def sel_step(step, carry):
        k0 = pl.multiple_of(step * kt, kt)
        k_aug = jnp.concatenate([ks_ref[0, pl.ds(k0, kt), :], et_ref[pl.ds(k0, kt), :]], axis=1)
        s = _dot_nt(q_aug, k_aug)
        kpos = k0 + lax.broadcasted_iota(I32, (rows, kt), 1)
        s = jnp.where(kpos <= t_row, s, NEG)
        m_prev = m_sc[...]
        m_new = jnp.maximum(m_prev, jnp.max(s, axis=-1, keepdims=True))
        alpha = jnp.exp(m_prev - m_new)
        p = jnp.exp(s - m_new)
        l_sc[...] = alpha * l_sc[...] + jnp.sum(p, axis=-1, keepdims=True)
        acc_sc[...] = alpha * acc_sc[...] + jnp.dot(p.astype(BF16), vs_ref[0, pl.ds(k0, kt), :],
                                                    preferred_element_type=F32)
        m_sc[...] = m_new
        return carry

    lax.fori_loop(0, (t0 + tq - 1) // kt + 1, sel_step, 0)
    o_sel = acc_sc[...] * (1.0 / l_sc[...])

    w0 = pl.multiple_of(jnp.maximum(t0 + tq - wk, 0), LANES)
    s_w = _dot_nt(q, kw_ref[0, pl.ds(w0, wk), :])
    kpos = w0 + lax.broadcasted_iota(I32, (rows, wk), 1)
    wmask = (kpos <= t_row) & (kpos > t_row - WINDOW)
    s_w = jnp.where(wmask, s_w, NEG)
    e_w = jnp.exp(s_w - jnp.max(s_w, axis=-1, keepdims=True))
    o_win = jnp.dot(e_w.astype(BF16), vw_ref[0, pl.ds(w0, wk), :], preferred_element_type=F32)
    o_win = o_win * (1.0 / jnp.sum(e_w, axis=-1, keepdims=True))

    gt = _sigmoid(gate_ref[0])
    for p in range(NSA_HPG):
        sl = slice(p * tq, (p + 1) * tq)
        o = (gt[:, 3 * p:3 * p + 1] * o_cmp[sl] + gt[:, 3 * p + 1:3 * p + 2] * o_sel[sl]
             + gt[:, 3 * p + 2:3 * p + 3] * o_win[sl])
        o_ref[0, :, p * d:(p + 1) * d] = o.astype(BF16)


def nsa_attention(a2, kc, vc, gates3):
    b, s, _ = a2.shape
    g, d = NSA_GROUPS, HEAD_DIM
    ncp = kc.shape[1]
    tq = NSA_TQ
    kt = min(NSA_KT, s)
    wk = WINDOW + tq
    ns = s // SEL_BLOCK
    assert s % kt == 0 and s >= wk and ns <= LANES and tq == LANES
    n_top = min(SEL_TOPN, ns)
    rows = NSA_HPG * tq

    key_blk = np.arange(s) // SEL_BLOCK
    e_t = jnp.asarray(key_blk[:, None] == np.arange(LANES)[None, :], dtype=BF16)
    c_start = np.arange(ncp) * CMP_STRIDE
    j_start = np.arange(LANES) * SEL_BLOCK
    overlap = ((c_start[:, None] < j_start[None, :] + SEL_BLOCK) & (c_start[:, None] + CMP_BLOCK > j_start[None, :])
               & (np.arange(ncp)[:, None] < s // CMP_STRIDE - 1) & (np.arange(LANES)[None, :] < ns))
    overlap = jnp.asarray(overlap.astype(np.float32))

    q_cols = NSA_HPG * d
    kv0 = NSA_WIDTH // d

    def kv_spec(slab):
        return pl.BlockSpec((1, s, d), lambda bi, gi, i, slab=slab: (bi, 0, kv0 + slab * g + gi))

    cmp_spec = pl.BlockSpec((1, ncp, d), lambda bi, gi, i: (bi * g + gi, 0, 0))
    return pl.pallas_call(
        functools.partial(_nsa_attn_kernel, tq=tq, kt=kt, wk=wk, ns=ns, n_top=n_top),
        grid=(b, g, s // tq),
        in_specs=[pl.BlockSpec((1, tq, q_cols), lambda bi, gi, i: (bi, i, gi)),
                  cmp_spec, cmp_spec,
                  kv_spec(2), kv_spec(3), kv_spec(4), kv_spec(5),
                  pl.BlockSpec((s, LANES), lambda bi, gi, i: (0, 0)),
                  pl.BlockSpec((ncp, LANES), lambda bi, gi, i: (0, 0)),
                  pl.BlockSpec((1, tq, LANES), lambda bi, gi, i: (bi, i, gi))],
        out_specs=pl.BlockSpec((1, tq, q_cols), lambda bi, gi, i: (bi, i, gi)),
        out_shape=jax.ShapeDtypeStruct((b, s, NSA_WIDTH), BF16),
        scratch_shapes=[pltpu.VMEM((rows, 1), F32), pltpu.VMEM((rows, 1), F32),
                        pltpu.VMEM((rows, d), F32), pltpu.VMEM((LANES, tq), F32)],
        compiler_params=_cparams(("parallel", "parallel", "arbitrary")),
        name="nsa_attention",
    )(a2, kc, vc, a2, a2, a2, a2, e_t, overlap, gates3)


def _conv_kernel(cur_ref, prev_ref, w_ref, o_ref, *, tc):
    i = pl.program_id(1)
    cur = cur_ref[0].astype(F32)
    prev = jnp.where(i > 0, prev_ref[0].astype(F32), 0.0)
    pad = prev.shape[0] // 2
    full = jnp.concatenate([prev[pad:], cur], axis=0)
    y = None
    for kk in range(CONV_WIDTH):
        off = pad - (CONV_WIDTH - 1) + kk
        term = w_ref[kk:kk + 1, :] * full[off:off + tc]
        y = term if y is None else y + term
    o_ref[0] = (y * _sigmoid(y)).astype(BF16)


def conv_silu(p3, w, col0, width):
    b, s, _ = p3.shape
    tc = _tile(s, 512, BF16_SUBLANES)
    cw = 512
    assert col0 % cw == 0 and width % cw == 0
    c0 = col0 // cw
    halo = BF16_SUBLANES
    return pl.pallas_call(
        functools.partial(_conv_kernel, tc=tc),
        grid=(b, s // tc, width // cw),
        in_specs=[pl.BlockSpec((1, tc, cw), lambda bi, i, j: (bi, i, c0 + j)),
                  pl.BlockSpec((1, halo, cw), lambda bi, i, j: (bi, jnp.maximum(i * (tc // halo) - 1, 0), c0 + j)),
                  pl.BlockSpec((CONV_WIDTH, cw), lambda bi, i, j: (0, j))],
        out_specs=pl.BlockSpec((1, tc, cw), lambda bi, i, j: (bi, i, j)),
        out_shape=jax.ShapeDtypeStruct((b, s, width), BF16),
        compiler_params=_cparams(("parallel", "parallel", "parallel")),
        name="mlstm_conv",
    )(p3, p3, w)


def _log_sigmoid(x):
    return jnp.minimum(x, 0.0) - jnp.log1p(jnp.exp(-jnp.abs(x)))


def _mlstm_kernel(fb_ref, q_ref, k_ref, v_ref, ob_ref, gate_ref, nw_ref, o_ref, c_sc, n_sc, m_sc, *, cl):
    hd = pl.program_id(1)

    @pl.when(pl.program_id(2) == 0)
    def _():
        c_sc[...] = jnp.zeros(c_sc.shape, F32)
        n_sc[...] = jnp.zeros(n_sc.shape, F32)
        m_sc[...] = jnp.zeros(m_sc.shape, F32)

    q = q_ref[0]
    v = v_ref[0]
    ks32 = k_ref[0].astype(F32) * (MLSTM_DIM ** -0.5)
    ks = ks32.astype(BF16)
    slab = gate_ref[0]
    lane = lax.broadcasted_iota(I32, slab.shape, 1)
    i_col = jnp.sum(jnp.where(lane == GATE_I_LANE + hd, slab, 0.0), axis=-1, keepdims=True)
    f_col = jnp.sum(jnp.where(lane == GATE_F_LANE + hd, slab, 0.0), axis=-1, keepdims=True)
    lf_col = _log_sigmoid(f_col + fb_ref[hd])

    r_i = lax.broadcasted_iota(I32, (cl, cl), 0)
    c_i = lax.broadcasted_iota(I32, (cl, cl), 1)
    eye = r_i == c_i
    tri = c_i <= r_i
    lf_row = jnp.sum(jnp.where(eye, lf_col, 0.0), axis=0, keepdims=True)
    ig_row = jnp.sum(jnp.where(eye, i_col, 0.0), axis=0, keepdims=True)
    a_col = jnp.sum(jnp.where(tri, lf_row, 0.0), axis=1, keepdims=True)
    a_row = jnp.sum(jnp.where(r_i <= c_i, lf_col, 0.0), axis=0, keepdims=True)
    m_prev = m_sc[...]

    dlog = jnp.where(tri, a_col - a_row + ig_row, NEG)
    inter = a_col + m_prev
    mt = jnp.maximum(inter, jnp.max(dlog, axis=-1, keepdims=True))
    wm = jnp.exp(dlog - mt) * _dot_nt(q, ks)
    e_col = jnp.exp(inter - mt)
    num = e_col * jnp.dot(q, c_sc[...].astype(BF16), preferred_element_type=F32) \
        + jnp.dot(wm.astype(BF16), v, preferred_element_type=F32)
    qn = jnp.sum(q.astype(F32) * n_sc[...], axis=-1, keepdims=True)
    den = e_col * qn + jnp.sum(wm, axis=-1, keepdims=True)
    hh = num / jnp.maximum(jnp.abs(den), jnp.exp(-mt))

    a_last = jnp.sum(lf_row, axis=-1, keepdims=True)
    gs = a_last - a_col + i_col
    m_new = jnp.maximum(a_last + m_prev, jnp.max(gs, axis=0, keepdims=True))
    decay = jnp.exp(a_last + m_prev - m_new)
    wk = jnp.exp(gs - m_new) * ks32
    c_sc[...] = decay * c_sc[...] + _dot_tn(wk.astype(BF16), v)
    n_sc[...] = decay * n_sc[...] + jnp.sum(wk, axis=0, keepdims=True)
    m_sc[...] = m_new

    y = hh * lax.rsqrt(jnp.mean(hh * hh, axis=-1, keepdims=True) + RMS_EPS) * nw_ref[...]
    o_ref[0] = (y * _sigmoid(ob_ref[0].astype(F32))).astype(BF16)


def mlstm(qk, p3, gates3, f_bias, norm_w, v_col0, o_col0):
    b, s, _ = qk.shape
    cl = min(CHUNK, s)
    dh = MLSTM_DIM
    nh = MLSTM_HEADS
    assert v_col0 % dh == 0 and o_col0 % dh == 0 and s % cl == 0
    vb, ob = v_col0 // dh, o_col0 // dh
    return pl.pallas_call(
        functools.partial(_mlstm_kernel, cl=cl),
        grid=(b, nh, s // cl),
        in_specs=[pl.BlockSpec(memory_space=pltpu.SMEM),
                  pl.BlockSpec((1, cl, dh), lambda bi, h, c: (bi, c, h)),
                  pl.BlockSpec((1, cl, dh), lambda bi, h, c: (bi, c, nh + h)),
                  pl.BlockSpec((1, cl, dh), lambda bi, h, c: (bi, c, vb + h)),
                  pl.BlockSpec((1, cl, dh), lambda bi, h, c: (bi, c, ob + h)),
                  pl.BlockSpec((1, cl, LANES), lambda bi, h, c: (bi, c, 0)),
                  pl.BlockSpec((1, dh), lambda bi, h, c: (0, h))],
        out_specs=pl.BlockSpec((1, cl, dh), lambda bi, h, c: (bi, c, h)),
        out_shape=jax.ShapeDtypeStruct((b, s, nh * dh), BF16),
        scratch_shapes=[pltpu.VMEM((dh, dh), F32), pltpu.VMEM((1, dh), F32), pltpu.VMEM((1, 1), F32)],
        compiler_params=_cparams(("parallel", "parallel", "arbitrary")),
        name="mlstm",
    )(f_bias.astype(F32), qk, qk, p3, p3, gates3, norm_w.reshape(1, nh * dh).astype(F32))


def _ret_kernel(lg_ref, q_ref, k_ref, v_ref, g_ref, cos_ref, sin_ref, nw_ref, o_ref, r_sc, *, cl):
    hd = pl.program_id(1)

    @pl.when(pl.program_id(2) == 0)
    def _():
        r_sc[...] = jnp.zeros(r_sc.shape, F32)

    lg = lg_ref[hd]
    cos = cos_ref[0]
    sin = sin_ref[0]
    half = RET_QK_DIM // 2

    def rope(x):
        x1, x2 = x[:, :half], x[:, half:]
        return jnp.concatenate([x1 * cos - x2 * sin, x1 * sin + x2 * cos], axis=1)

    qr = rope(q_ref[0].astype(F32)).astype(BF16)
    kr = rope(k_ref[0].astype(F32)) * (RET_QK_DIM ** -0.5)
    v = v_ref[0]

    r_i = lax.broadcasted_iota(I32, (cl, cl), 0)
    c_i = lax.broadcasted_iota(I32, (cl, cl), 1)
    diff = (r_i - c_i).astype(F32)
    dm = jnp.where(diff >= 0, jnp.exp(jnp.maximum(diff, 0.0) * lg), 0.0)
    idx = lax.broadcasted_iota(I32, (cl, 1), 0).astype(F32)
    xi = jnp.exp((idx + 1.0) * lg)
    zeta = jnp.exp((cl - 1.0 - idx) * lg)
    chunk_decay = jnp.exp(jnp.full((1, 1), cl, F32) * lg)

    inner = jnp.dot((_dot_nt(qr, kr.astype(BF16)) * dm).astype(BF16), v, preferred_element_type=F32)
    cross = jnp.dot(qr, r_sc[...].astype(BF16), preferred_element_type=F32) * xi
    r_sc[...] = chunk_decay * r_sc[...] + _dot_tn((kr * zeta).astype(BF16), v)

    y = inner + cross
    y = y * lax.rsqrt(jnp.mean(y * y, axis=-1, keepdims=True) + RMS_EPS) * nw_ref[...]
    gg = g_ref[0].astype(F32)
    o_ref[0] = (y * (gg * _sigmoid(gg))).astype(BF16)


def retention(po3, cos, sin, norm_w):
    b, s, _ = po3.shape
    cl = min(CHUNK, s)
    nh, dk, dv = RET_HEADS, RET_QK_DIM, RET_V_DIM
    log_g = jnp.log1p(-jnp.exp2(-5.0 - jnp.arange(nh, dtype=F32)))
    kb = RET_QK_WIDTH // dk
    vb = 2 * RET_QK_WIDTH // dv
    gb = (2 * RET_QK_WIDTH + RET_V_WIDTH) // dv
    tab = pl.BlockSpec((1, cl, LANES), lambda bi, h, c: (bi, c, 0))
    return pl.pallas_call(
        functools.partial(_ret_kernel, cl=cl),
        grid=(b, nh, s // cl),
        in_specs=[pl.BlockSpec(memory_space=pltpu.SMEM),
                  pl.BlockSpec((1, cl, dk), lambda bi, h, c: (bi, c, h)),
                  pl.BlockSpec((1, cl, dk), lambda bi, h, c: (bi, c, kb + h)),
                  pl.BlockSpec((1, cl, dv), lambda bi, h, c: (bi, c, vb + h)),
                  pl.BlockSpec((1, cl, dv), lambda bi, h, c: (bi, c, gb + h)),
                  tab, tab,
                  pl.BlockSpec((1, dv), lambda bi, h, c: (0, h))],
        out_specs=pl.BlockSpec((1, cl, dv), lambda bi, h, c: (bi, c, h)),
        out_shape=jax.ShapeDtypeStruct((b, s, nh * dv), BF16),
        scratch_shapes=[pltpu.VMEM((dk, dv), F32)],
        compiler_params=_cparams(("parallel", "parallel", "arbitrary")),
        name="retention",
    )(log_g, po3, po3, po3, po3, cos, sin, norm_w.reshape(1, nh * dv).astype(F32))


def _router_kernel(x_ref, g_ref, wr_ref, xn_ref, route_ref, cnt_ref, carry_sc):
    @pl.when(pl.program_id(0) == 0)
    def _():
        carry_sc[...] = jnp.zeros(carry_sc.shape, F32)

    x = x_ref[...]
    t = x.shape[0]
    y = x * lax.rsqrt(jnp.mean(x * x, axis=-1, keepdims=True) + RMS_EPS) * g_ref[...]
    xn_ref[...] = y.astype(BF16)
    logits = jnp.dot(y, wr_ref[...], precision=HIGHEST, preferred_element_type=F32)
    lane = lax.broadcasted_iota(I32, logits.shape, 1)
    lg = jnp.where(lane < N_EXPERTS, logits, LOWEST)
    v1 = jnp.max(lg, axis=-1, keepdims=True)
    i1 = jnp.min(jnp.where(lg == v1, lane, LANES), axis=-1, keepdims=True)
    lg2 = jnp.where(lane == i1, LOWEST, lg)
    v2 = jnp.max(lg2, axis=-1, keepdims=True)
    i2 = jnp.min(jnp.where(lg2 == v2, lane, LANES), axis=-1, keepdims=True)
    e2 = jnp.exp(v2 - v1)
    g1 = 1.0 / (1.0 + e2)
    g2 = e2 / (1.0 + e2)

    chosen = jnp.where((lane == i1) | (lane == i2), 1.0, 0.0)
    r_i = lax.broadcasted_iota(I32, (t, t), 0)
    c_i = lax.broadcasted_iota(I32, (t, t), 1)
    tri = jnp.where(c_i <= r_i, 1.0, 0.0).astype(BF16)
    seen = jnp.dot(tri, chosen.astype(BF16), preferred_element_type=F32) + carry_sc[...]
    rank1 = jnp.sum(jnp.where(lane == i1, seen, 0.0), axis=-1, keepdims=True) - 1.0
    rank2 = jnp.sum(jnp.where(lane == i2, seen, 0.0), axis=-1, keepdims=True) - 1.0
    total = seen[t - 1:t, :]
    carry_sc[...] = total
    cnt_ref[...] = jnp.broadcast_to(total, cnt_ref.shape)

    route = jnp.where(lane == 0, i1.astype(F32), 0.0)
    route = jnp.where(lane == 1, i2.astype(F32), route)
    route = jnp.where(lane == 2, g1, route)
    route = jnp.where(lane == 3, g2, route)
    route = jnp.where(lane == 4, rank1, route)
    route = jnp.where(lane == 5, rank2, route)
    route_ref[...] = route


def moe_route(x, g, w_router, tm=256):
    m, d = x.shape
    tm = _tile(m, tm, 16)
    wr = jnp.zeros((d, LANES), F32).at[:, :N_EXPERTS].set(w_router.astype(F32))
    return pl.pallas_call(
        _router_kernel,
        grid=(m // tm,),
        in_specs=[pl.BlockSpec((tm, d), lambda i: (i, 0)),
                  pl.BlockSpec((1, d), lambda i: (0, 0)),
                  pl.BlockSpec((d, LANES), lambda i: (0, 0))],
        out_specs=[pl.BlockSpec((tm, d), lambda i: (i, 0)),
                   pl.BlockSpec((tm, LANES), lambda i: (i, 0)),
                   pl.BlockSpec((8, LANES), lambda i: (0, 0))],
        out_shape=[jax.ShapeDtypeStruct((m, d), BF16),
                   jax.ShapeDtypeStruct((m, LANES), F32),
                   jax.ShapeDtypeStruct((8, LANES), F32)],
        scratch_shapes=[pltpu.VMEM((1, LANES), F32)],
        compiler_params=_cparams(("arbitrary",)),
        name="moe_route",
    )(x, g.reshape(1, d).astype(F32), wr)


def _dispatch_kernel(idx_ref, src_ref, o_ref, sem, *, rows):
    base = pl.program_id(0) * rows

    def issue(r, carry):
        pltpu.make_async_copy(src_ref.at[idx_ref[base + r]], o_ref.at[r], sem).start()
        return carry

    lax.fori_loop(0, rows, issue, 0)
    pltpu.make_async_copy(src_ref.at[pl.ds(0, rows)], o_ref, sem).wait()


def moe_dispatch(xn, row_tok, rows=256):
    m, d = xn.shape
    n_rows = row_tok.shape[0]
    rows = _tile(n_rows, rows, 8)
    src = xn.reshape(m, d // LANES, LANES)
    out = pl.pallas_call(
        functools.partial(_dispatch_kernel, rows=rows),
        grid_spec=pltpu.PrefetchScalarGridSpec(
            num_scalar_prefetch=1,
            grid=(n_rows // rows,),
            in_specs=[pl.BlockSpec(memory_space=pl.ANY)],
            out_specs=pl.BlockSpec((rows, d // LANES, LANES), lambda i, idx: (i, 0, 0)),
            scratch_shapes=[pltpu.SemaphoreType.DMA(())],
        ),
        out_shape=jax.ShapeDtypeStruct((n_rows, d // LANES, LANES), xn.dtype),
        compiler_params=_cparams(("arbitrary",)),
        name="moe_dispatch",
    )(row_tok, src)
    return out.reshape(n_rows, d)


def _combine_kernel(d1_ref, d2_ref, x_ref, route_ref, yb_ref, o_ref, buf1, buf2, sem, *, rows):
    base = pl.program_id(0) * rows

    def issue(r, carry):
        pltpu.make_async_copy(yb_ref.at[pl.ds(d1_ref[base + r], 1)], buf1.at[pl.ds(r, 1)], sem.at[0]).start()
        pltpu.make_async_copy(yb_ref.at[pl.ds(d2_ref[base + r], 1)], buf2.at[pl.ds(r, 1)], sem.at[1]).start()
        return carry

    lax.fori_loop(0, rows, issue, 0)
    pltpu.make_async_copy(yb_ref.at[pl.ds(0, rows)], buf1, sem.at[0]).wait()
    pltpu.make_async_copy(yb_ref.at[pl.ds(0, rows)], buf2, sem.at[1]).wait()
    route = route_ref[...]
    o_ref[...] = x_ref[...] + (buf1[...] * route[:, 2:3] + buf2[...] * route[:, 3:4])


def moe_combine(x, route, yb, dest1, dest2, rows=256):
    m, d = x.shape
    rows = _tile(m, rows, 8)
    return pl.pallas_call(
        functools.partial(_combine_kernel, rows=rows),
        grid_spec=pltpu.PrefetchScalarGridSpec(
            num_scalar_prefetch=2,
            grid=(m // rows,),
            in_specs=[pl.BlockSpec((rows, d), lambda i, a, b: (i, 0)),
                      pl.BlockSpec((rows, LANES), lambda i, a, b: (i, 0)),
                      pl.BlockSpec(memory_space=pl.ANY)],
            out_specs=pl.BlockSpec((rows, d), lambda i, a, b: (i, 0)),
            scratch_shapes=[pltpu.VMEM((rows, d), F32), pltpu.VMEM((rows, d), F32),
                            pltpu.SemaphoreType.DMA((2,))],
        ),
        out_shape=jax.ShapeDtypeStruct((m, d), F32),
        compiler_params=_cparams(("arbitrary",)),
        name="moe_combine",
    )(dest1, dest2, x, route, yb)


def moe_layer(x, norm_g, w_router, w_gate, w_up, w_down):
    m, d = x.shape
    tm = MOE_TM
    xn, route, cnt = moe_route(x, norm_g, w_router)
    expert = route[:, 0:2].astype(I32)
    rank = route[:, 4:6].astype(I32)
    counts = cnt[0, :N_EXPERTS].astype(I32)
    padded = (counts + tm - 1) // tm * tm
    pend = jnp.cumsum(padded)
    pstart = pend - padded
    dest = pstart[expert] + rank
    n_blk = -(-(2 * m) // tm) + N_EXPERTS
    n_rows = n_blk * tm
    tok = jnp.broadcast_to(jnp.arange(m, dtype=I32)[:, None], (m, 2))
    row_tok = jnp.zeros((n_rows,), I32).at[dest.reshape(-1)].set(tok.reshape(-1))
    nused = (pend[-1] // tm).astype(I32)
    blk = jnp.minimum(jnp.arange(n_blk, dtype=I32), nused - 1)
    blk_e = jnp.minimum(jnp.searchsorted(pend, blk * tm, side='right'), N_EXPERTS - 1).astype(I32)
    nu = nused.reshape(1)

    xb = moe_dispatch(xn, row_tok)
    tf = _tile(w_gate.shape[2], 512, LANES)
    hidden = gmm_swiglu(xb, w_gate, w_up, blk_e, nu, tm=tm, tn=tf)
    yb = gmm(hidden, w_down, blk_e, nu, tm=tm, tn=_tile(d, 512, LANES), out_dtype=F32)
    return moe_combine(x, route, yb, dest[:, 0], dest[:, 1])


def even_layer(x2, b, s, pos, w_norm, w_in, q_gain, k_gain, w_cmp_k, w_cmp_v, pe_cmp, conv_w, f_bias, m_norm,
               w_out, w_norm_ffn, w_gate, w_up, w_down):
    n, dm = x2.shape
    g, d = NSA_GROUPS, HEAD_DIM
    o_gate = NSA_SLAB
    o_qb = o_gate + NSA_HEADS * 3
    o_if = o_qb + 3 * MLSTM_WIDTH
    o_ob = o_if + 2 * MLSTM_HEADS
    w_main = jnp.concatenate([w_in[:, :o_gate], w_in[:, o_qb:o_if], w_in[:, o_ob:o_ob + MLSTM_WIDTH]], axis=1)
    per_g = NSA_HPG * 3
    zeros = functools.partial(jnp.zeros, dtype=w_in.dtype)
    w_small = jnp.concatenate(
        [w_in[:, o_gate:o_gate + per_g], zeros((dm, GATE_I_LANE - per_g)),
         w_in[:, o_if:o_ob], zeros((dm, LANES - GATE_I_LANE - 2 * MLSTM_HEADS)),
         w_in[:, o_gate + per_g:o_qb], zeros((dm, LANES - per_g))], axis=1)

    h = rmsnorm_rows(x2, w_norm)
    p3 = mm(h, w_main, out_dtype=BF16).reshape(b, s, -1)
    gates3 = mm(h, w_small, out_dtype=F32, tn=2 * LANES).reshape(b, s, 2 * LANES)

    cos_n, sin_n = trig_tables(pos, _nsa_inv_lane())
    a2 = nsa_prep(p3, cos_n.reshape(b, s, LANES), sin_n.reshape(b, s, LANES), q_gain, k_gain)
    nh = s // CMP_STRIDE
    cmp_pos = jnp.concatenate([pos[:, CMP_BLOCK - 1::CMP_STRIDE][:, :nh - 1], pos[:, -1:]], axis=1)
    cos_c, sin_c = trig_tables(cmp_pos, _nsa_inv_lane())

    def cmp_blocks(col0):
        tok = p3[:, :, col0:col0 + NSA_KV_WIDTH].reshape(b, nh, CMP_STRIDE, g, d)
        halves = tok.transpose(0, 3, 1, 2, 4).reshape(b, g, nh, CMP_STRIDE * d)
        nxt = jnp.concatenate([halves[:, :, 1:], jnp.zeros_like(halves[:, :, :1])], axis=2)
        return jnp.concatenate([halves, nxt], axis=-1).reshape(b * g * nh, CMP_BLOCK * d)

    tabs = (cos_c.reshape(b, nh, LANES), sin_c.reshape(b, nh, LANES))
    kc = compress(cmp_blocks(NSA_WIDTH), pe_cmp, w_cmp_k, k_gain, *tabs, is_key=True, rows_per_seq=nh, groups=g)
    vc = compress(cmp_blocks(NSA_WIDTH + NSA_KV_WIDTH), pe_cmp, w_cmp_v, k_gain, *tabs, is_key=False,
                  rows_per_seq=nh, groups=g)
    o_nsa = nsa_attention(a2, kc.reshape(b * g, nh, d), vc.reshape(b * g, nh, d), gates3)

    qk = conv_silu(p3, conv_w, NSA_SLAB, 2 * MLSTM_WIDTH)
    h_b = mlstm(qk, p3, gates3, f_bias, m_norm, NSA_SLAB + 2 * MLSTM_WIDTH, NSA_SLAB + 3 * MLSTM_WIDTH)

    mixed = jnp.concatenate([o_nsa, h_b], axis=-1).reshape(n, NSA_WIDTH + MLSTM_WIDTH)
    x2 = mm(mixed, w_out, out_dtype=F32, residual=x2)
    hf = rmsnorm_rows(x2, w_norm_ffn)
    hidden = mm_swiglu(hf, w_gate, w_up)
    return mm(hidden, w_down, out_dtype=F32, residual=x2, tm=512)


def odd_layer(x2, b, s, pos, w_norm, w_in, r_norm, w_out, w_norm_ffn, w_router, e_gate, e_up, e_down):
    n, dm = x2.shape
    h = rmsnorm_rows(x2, w_norm)
    po3 = mm(h, w_in, out_dtype=BF16).reshape(b, s, -1)
    cos_r, sin_r = trig_tables(pos, _ret_inv_lane())
    y = retention(po3, cos_r.reshape(b, s, LANES), sin_r.reshape(b, s, LANES), r_norm)
    x2 = mm(y.reshape(n, RET_V_WIDTH), w_out, out_dtype=F32, residual=x2, tm=512)
    return moe_layer(x2, w_norm_ffn, w_router, e_gate, e_up, e_down)


def kernel(x, positions, norm_mix_even, w_in_even, nsa_q_gain, nsa_k_gain, w_cmp_k, w_cmp_v, pe_cmp, mlstm_conv, mlstm_f_bias, mlstm_norm, w_out_even, norm_ffn_even, ffn_gate, ffn_up, ffn_down, norm_mix_odd, w_in_odd, ret_norm, w_out_odd, norm_ffn_odd, w_router, exp_gate, exp_up, exp_down):
    b, s, dm = x.shape
    depth = norm_mix_even.shape[0] + norm_mix_odd.shape[0]
    x2 = x.reshape(b * s, dm)
    for layer in range(depth):
        j = layer // 2
        if layer % 2 == 0:
            x2 = even_layer(x2, b, s, positions, norm_mix_even[j], w_in_even[j], nsa_q_gain[j], nsa_k_gain[j],
                            w_cmp_k[j], w_cmp_v[j], pe_cmp[j], mlstm_conv[j], mlstm_f_bias[j], mlstm_norm[j],
                            w_out_even[j], norm_ffn_even[j], ffn_gate[j], ffn_up[j], ffn_down[j])
        else:
            x2 = odd_layer(x2, b, s, positions, norm_mix_odd[j], w_in_odd[j], ret_norm[j], w_out_odd[j],
                           norm_ffn_odd[j], w_router[j], exp_gate[j], exp_up[j], exp_down[j])
    return x2.reshape(b, s, dm)
```

```python
import functools

import numpy as np
import jax
import jax.numpy as jnp
from jax import lax
from jax.experimental import pallas as pl
from jax.experimental.pallas import tpu as pltpu

F32 = jnp.float32
BF16 = jnp.bfloat16
I32 = jnp.int32
HIGHEST = lax.Precision.HIGHEST

HEAD_DIM = 128
NSA_HEADS = 8
NSA_GROUPS = 2
NSA_HPG = NSA_HEADS // NSA_GROUPS
NSA_WIDTH = NSA_HEADS * HEAD_DIM
NSA_KV_WIDTH = NSA_GROUPS * HEAD_DIM
NSA_SLAB = NSA_WIDTH + 6 * NSA_KV_WIDTH
CMP_BLOCK = 32
CMP_STRIDE = 16
SEL_BLOCK = 64
SEL_TOPN = 16
WINDOW = 512
ROPE_DIM = HEAD_DIM // 4
ROPE_THETA = 500000.0
SEL_FORCE = 1.0e6
NEG = -1.0e30
LOWEST = -3.0e38
MLSTM_HEADS = 4
MLSTM_DIM = 256
MLSTM_WIDTH = MLSTM_HEADS * MLSTM_DIM
CONV_WIDTH = 4
RET_HEADS = 8
RET_QK_DIM = 256
RET_V_DIM = 512
RET_QK_WIDTH = RET_HEADS * RET_QK_DIM
RET_V_WIDTH = RET_HEADS * RET_V_DIM
RET_ROPE_THETA = 10000.0
N_EXPERTS = 8
RMS_EPS = 1e-6

LANES = 128
BF16_SUBLANES = 16
V7X_VMEM_BYTES = 64 * 1024 * 1024
VMEM_LIMIT = V7X_VMEM_BYTES - 8 * 1024 * 1024

CHUNK = 256
NSA_TQ = 128
NSA_KT = 512
MOE_TM = 512
GATE_I_LANE = 16
GATE_F_LANE = 20


def _cparams(sem, vmem=VMEM_LIMIT):
    return pltpu.CompilerParams(dimension_semantics=sem, vmem_limit_bytes=vmem)


def _tile(n, target, quantum):
    if n <= target:
        return n
    t = (target // quantum) * quantum
    while t > quantum and n % t:
        t -= quantum
    assert n % t == 0, (n, target, quantum)
    return t


def _sigmoid(x):
    return 1.0 / (1.0 + jnp.exp(-x))


def _dot_nt(a, b):
    return lax.dot_general(a, b, (((1,), (1,)), ((), ())), preferred_element_type=F32)


def _dot_tn(a, b):
    return lax.dot_general(a, b, (((0,), (0,)), ((), ())), preferred_element_type=F32)


def _rmsnorm_kernel(x_ref, g_ref, o_ref):
    x = x_ref[...]
    y = x * lax.rsqrt(jnp.mean(x * x, axis=-1, keepdims=True) + RMS_EPS)
    o_ref[...] = (y * g_ref[...]).astype(o_ref.dtype)


def rmsnorm_rows(x, g, tm=512):
    m, d = x.shape
    tm = _tile(m, tm, 8)
    return pl.pallas_call(
        _rmsnorm_kernel,
        grid=(m // tm,),
        in_specs=[pl.BlockSpec((tm, d), lambda i: (i, 0)), pl.BlockSpec((1, d), lambda i: (0, 0))],
        out_specs=pl.BlockSpec((tm, d), lambda i: (i, 0)),
        out_shape=jax.ShapeDtypeStruct((m, d), BF16),
        compiler_params=_cparams(("parallel",)),
        name="rmsnorm",
    )(x, g.reshape(1, d).astype(F32))


def _gmm_kernel(be_ref, nu_ref, x_ref, w_ref, *rest, has_res):
    if has_res:
        r_ref, o_ref, wb_ref = rest
    else:
        o_ref, wb_ref = rest
    i = pl.program_id(1)
    changed = be_ref[i] != be_ref[jnp.maximum(i - 1, 0)]

    @pl.when((i == 0) | changed)
    def _():
        wb_ref[...] = w_ref[0].astype(BF16)

    @pl.when(i < nu_ref[0])
    def _():
        acc = jnp.dot(x_ref[...], wb_ref[...], preferred_element_type=F32)
        if has_res:
            acc = r_ref[...] + acc
        o_ref[...] = acc.astype(o_ref.dtype)

    @pl.when(i >= nu_ref[0])
    def _():
        o_ref[...] = jnp.zeros_like(o_ref)


def _weight_spec(k, tn, single_buffer):
    mode = dict(pipeline_mode=pl.Buffered(1)) if single_buffer else {}
    return pl.BlockSpec((1, k, tn), lambda j, i, be, nu: (be[i], 0, j), **mode)


def gmm(x, w, blk_e, nused, *, tm, tn, out_dtype, residual=None, single_buffer_w=False):
    m, k = x.shape
    e, k2, n = w.shape
    assert k == k2 and m % tm == 0 and n % tn == 0
    nb = m // tm
    in_specs = [
        pl.BlockSpec((tm, k), lambda j, i, be, nu: (jnp.minimum(i, nu[0] - 1), 0)),
        _weight_spec(k, tn, single_buffer_w),
    ]
    args = [x, w]
    if residual is not None:
        in_specs.append(pl.BlockSpec((tm, tn), lambda j, i, be, nu: (i, j)))
        args.append(residual)
    return pl.pallas_call(
        functools.partial(_gmm_kernel, has_res=residual is not None),
        grid_spec=pltpu.PrefetchScalarGridSpec(
            num_scalar_prefetch=2,
            grid=(n // tn, nb),
            in_specs=in_specs,
            out_specs=pl.BlockSpec((tm, tn), lambda j, i, be, nu: (i, j)),
            scratch_shapes=[pltpu.VMEM((k, tn), BF16)],
        ),
        out_shape=jax.ShapeDtypeStruct((m, n), out_dtype),
        compiler_params=_cparams(("arbitrary", "arbitrary")),
        name="gmm",
    )(blk_e, nused, *args)


def _gmm_swiglu_kernel(be_ref, nu_ref, x_ref, wg_ref, wu_ref, o_ref, wgb_ref, wub_ref):
    i = pl.program_id(1)
    changed = be_ref[i] != be_ref[jnp.maximum(i - 1, 0)]

    @pl.when((i == 0) | changed)
    def _():
        wgb_ref[...] = wg_ref[0].astype(BF16)
        wub_ref[...] = wu_ref[0].astype(BF16)

    @pl.when(i < nu_ref[0])
    def _():
        x = x_ref[...]
        g = jnp.dot(x, wgb_ref[...], preferred_element_type=F32)
        u = jnp.dot(x, wub_ref[...], preferred_element_type=F32)
        o_ref[...] = (g * _sigmoid(g) * u).astype(o_ref.dtype)

    @pl.when(i >= nu_ref[0])
    def _():
        o_ref[...] = jnp.zeros_like(o_ref)


def gmm_swiglu(x, wg, wu, blk_e, nused, *, tm, tn, single_buffer_w=False):
    m, k = x.shape
    e, k2, n = wg.shape
    assert k == k2 and wu.shape == wg.shape and m % tm == 0 and n % tn == 0
    nb = m // tm
    w_spec = _weight_spec(k, tn, single_buffer_w)
    return pl.pallas_call(
        _gmm_swiglu_kernel,
        grid_spec=pltpu.PrefetchScalarGridSpec(
            num_scalar_prefetch=2,
            grid=(n // tn, nb),
            in_specs=[pl.BlockSpec((tm, k), lambda j, i, be, nu: (jnp.minimum(i, nu[0] - 1), 0)),
                      w_spec, w_spec],
            out_specs=pl.BlockSpec((tm, tn), lambda j, i, be, nu: (i, j)),
            scratch_shapes=[pltpu.VMEM((k, tn), BF16), pltpu.VMEM((k, tn), BF16)],
        ),
        out_shape=jax.ShapeDtypeStruct((m, n), BF16),
        compiler_params=_cparams(("arbitrary", "arbitrary")),
        name="gmm_swiglu",
    )(blk_e, nused, x, wg, wu)


def _dense_blocks(m, tm):
    nb = m // tm
    return jnp.zeros((nb,), I32), jnp.full((1,), nb, I32)


def mm(x, w, *, out_dtype, residual=None, tm=2048, tn=512):
    m, k = x.shape
    n = w.shape[1]
    tm = _tile(m, tm, 16)
    tn = _tile(n, tn, LANES)
    be, nu = _dense_blocks(m, tm)
    return gmm(x, w[None], be, nu, tm=tm, tn=tn, out_dtype=out_dtype, residual=residual, single_buffer_w=True)


def mm_swiglu(x, wg, wu, *, tm=2048, tn=512):
    m = x.shape[0]
    tm = _tile(m, tm, 16)
    tn = _tile(wg.shape[1], tn, LANES)
    be, nu = _dense_blocks(m, tm)
    return gmm_swiglu(x, wg[None], wu[None], be, nu, tm=tm, tn=tn, single_buffer_w=True)


def _trig_kernel(pos_ref, inv_ref, cos_ref, sin_ref):
    ang = pos_ref[...] * inv_ref[...]
    cos_ref[...] = jnp.cos(ang)
    sin_ref[...] = jnp.sin(ang)


def trig_tables(pos, inv_lane):
    r = pos.size
    pos_b = jnp.broadcast_to(pos.astype(F32).reshape(r, 1), (r, LANES))
    tr = _tile(r, 512, 8)
    spec = pl.BlockSpec((tr, LANES), lambda i: (i, 0))
    return pl.pallas_call(
        _trig_kernel,
        grid=(r // tr,),
        in_specs=[spec, pl.BlockSpec((1, LANES), lambda i: (0, 0))],
        out_specs=[spec, spec],
        out_shape=[jax.ShapeDtypeStruct((r, LANES), F32)] * 2,
        compiler_params=_cparams(("parallel",)),
        name="trig_tables",
    )(pos_b, inv_lane.reshape(1, LANES))


def _nsa_inv_lane():
    half = ROPE_DIM // 2
    inv = jnp.power(jnp.float32(ROPE_THETA), -jnp.arange(half, dtype=F32) * (2.0 / ROPE_DIM))
    return jnp.concatenate([inv, inv, jnp.zeros((LANES - ROPE_DIM,), F32)])


def _ret_inv_lane():
    half = RET_QK_DIM // 2
    return jnp.power(jnp.float32(RET_ROPE_THETA), -jnp.arange(half, dtype=F32) * (2.0 / RET_QK_DIM))


def _norm_rope_head(x, gain, cos, sin):
    half = ROPE_DIM // 2
    y = x * lax.rsqrt(jnp.mean(x * x, axis=-1, keepdims=True) + RMS_EPS) * gain
    lane = lax.broadcasted_iota(I32, y.shape, 1)
    from_hi = jnp.where(lane < half, -sin, 0.0)
    from_lo = jnp.where((lane >= half) & (lane < ROPE_DIM), sin, 0.0)
    return (y * cos + pltpu.roll(y, LANES - half, 1) * from_hi + pltpu.roll(y, half, 1) * from_lo)


def _nsa_prep_kernel(p_ref, cos_ref, sin_ref, qg_ref, kg_ref, o_ref):
    cos = cos_ref[0]
    sin = sin_ref[0]
    scale = HEAD_DIM ** -0.5
    d = HEAD_DIM
    for hd in range(NSA_HEADS):
        sl = slice(hd * d, (hd + 1) * d)
        q = _norm_rope_head(p_ref[0, :, sl].astype(F32), qg_ref[...], cos, sin)
        o_ref[0, :, sl] = (q * scale).astype(BF16)
    for slab in range(6):
        for g in range(NSA_GROUPS):
            off = NSA_WIDTH + slab * NSA_KV_WIDTH + g * d
            sl = slice(off, off + d)
            if slab in (2, 4):
                k = _norm_rope_head(p_ref[0, :, sl].astype(F32), kg_ref[...], cos, sin)
                o_ref[0, :, sl] = k.astype(BF16)
            else:
                o_ref[0, :, sl] = p_ref[0, :, sl]


def nsa_prep(p3, cos, sin, q_gain, k_gain):
    b, s, _ = p3.shape
    t = _tile(s, 256, 16)
    tab = pl.BlockSpec((1, t, LANES), lambda bi, i: (bi, i, 0))
    gain = pl.BlockSpec((1, HEAD_DIM), lambda bi, i: (0, 0))
    blk = pl.BlockSpec((1, t, NSA_SLAB), lambda bi, i: (bi, i, 0))
    return pl.pallas_call(
        _nsa_prep_kernel,
        grid=(b, s // t),
        in_specs=[blk, tab, tab, gain, gain],
        out_specs=blk,
        out_shape=jax.ShapeDtypeStruct((b, s, NSA_SLAB), BF16),
        compiler_params=_cparams(("parallel", "parallel")),
        name="nsa_prep",
    )(p3, cos, sin, q_gain.reshape(1, HEAD_DIM), k_gain.reshape(1, HEAD_DIM))


def _compress_kernel(blk_ref, pe_ref, w_ref, kg_ref, cos_ref, sin_ref, o_ref, *, is_key):
    a = (blk_ref[...].astype(F32) + pe_ref[...]).astype(BF16)
    y = jnp.dot(a, w_ref[...].astype(BF16), preferred_element_type=F32)
    if is_key:
        y = _norm_rope_head(y, kg_ref[...], cos_ref[0], sin_ref[0])
    o_ref[...] = y.astype(BF16)


def compress(blk, pe, w, k_gain, cos_c, sin_c, *, is_key, rows_per_seq, groups):
    r, kdim = blk.shape
    t = rows_per_seq
    tab = pl.BlockSpec((1, t, LANES), lambda i: (i // groups, 0, 0))
    return pl.pallas_call(
        functools.partial(_compress_kernel, is_key=is_key),
        grid=(r // t,),
        in_specs=[pl.BlockSpec((t, kdim), lambda i: (i, 0)),
                  pl.BlockSpec((1, kdim), lambda i: (0, 0)),
                  pl.BlockSpec((kdim, HEAD_DIM), lambda i: (0, 0)),
                  pl.BlockSpec((1, HEAD_DIM), lambda i: (0, 0)),
                  tab, tab],
        out_specs=pl.BlockSpec((t, HEAD_DIM), lambda i: (i, 0)),
        out_shape=jax.ShapeDtypeStruct((r, HEAD_DIM), BF16),
        compiler_params=_cparams(("parallel",)),
        name="nsa_compress",
    )(blk, pe.reshape(1, kdim), w, k_gain.reshape(1, HEAD_DIM), cos_c, sin_c)


def _nsa_attn_kernel(q_ref, kc_ref, vc_ref, ks_ref, vs_ref, kw_ref, vw_ref, et_ref, ov_ref, gate_ref,
                     o_ref, m_sc, l_sc, acc_sc, val_sc, *, tq, kt, wk, ns, n_top):
    i = pl.program_id(2)
    t0 = i * tq
    rows = NSA_HPG * tq
    d = HEAD_DIM
    q2 = q_ref[0]
    q = jnp.concatenate([q2[:, p * d:(p + 1) * d] for p in range(NSA_HPG)], axis=0)
    t_row = t0 + (lax.broadcasted_iota(I32, (rows, 1), 0) & (tq - 1))

    ncp = kc_ref.shape[1]
    s_c = _dot_nt(q, kc_ref[0])
    c_end = lax.broadcasted_iota(I32, (rows, ncp), 1) * CMP_STRIDE + (CMP_BLOCK - 1)
    cmask = c_end <= t_row
    s_c = jnp.where(cmask, s_c, NEG)
    e_c = jnp.where(cmask, jnp.exp(s_c - jnp.max(s_c, axis=-1, keepdims=True)), 0.0)
    den_c = jnp.sum(e_c, axis=-1, keepdims=True)
    p_c = e_c * (1.0 / jnp.maximum(den_c, 1e-30))
    o_cmp = jnp.dot(p_c.astype(BF16), vc_ref[0], preferred_element_type=F32)

    p_grp = p_c[0:tq]
    for p in range(1, NSA_HPG):
        p_grp = p_grp + p_c[p * tq:(p + 1) * tq]
    imp = jnp.dot(p_grp, ov_ref[...], precision=HIGHEST, preferred_element_type=F32)
    jb = lax.broadcasted_iota(I32, (tq, LANES), 1)
    cur = (t0 + lax.broadcasted_iota(I32, (tq, 1), 0)) // SEL_BLOCK
    forced = (jb == 0) | (jb == cur) | (jb == cur - 1)
    val = jnp.where(jb <= cur, jnp.where(forced, SEL_FORCE, imp), NEG)
    val = jnp.where(jb < ns, val, LOWEST)
    val_sc[...] = val.T
    v_t = val_sc[...]
    j_row = lax.broadcasted_iota(I32, (LANES, tq), 0)
    beaten = jnp.zeros((LANES, tq), F32)
    for j2 in range(ns):
        r = val_sc[j2:j2 + 1, :]
        ge = jnp.where(r >= v_t, 1.0, 0.0)
        gt = jnp.where(r > v_t, 1.0, 0.0)
        beaten = beaten + jnp.where(j_row > j2, ge, gt)
    past = j_row < t0 // SEL_BLOCK
    bias_t = jnp.where(past & (beaten < n_top), 0.0, jnp.where(j_row < ns, NEG, 0.0))
    bias = bias_t.T.astype(BF16)
    q_aug = jnp.concatenate([q, jnp.concatenate([bias] * NSA_HPG, axis=0)], axis=1)

    d0 = pl.multiple_of(t0, tq)
    s_d = _dot_nt(q, ks_ref[0, pl.ds(d0, tq), :])
    s_d = jnp.where(d0 + lax.broadcasted_iota(I32, (rows, tq), 1) <= t_row, s_d, NEG)
    m_d = jnp.max(s_d, axis=-1, keepdims=True)
    p_d = jnp.exp(s_d - m_d)
    m_sc[...] = m_d
    l_sc[...] = jnp.sum(p_d, axis=-1, keepdims=True)
    acc_sc[...] = jnp.dot(p_d.astype(BF16), vs_ref[0, pl.ds(d0, tq), :], preferred_element_type=F32)

    n_tiles = (t0 + kt - 1) // kt

    def scores(step):
        k0 = pl.multiple_of(step * kt, kt)
        k_aug = jnp.concatenate([ks_ref[0, pl.ds(k0, kt), :], et_ref[pl.ds(k0, kt), :]], axis=1)
        return _dot_nt(q_aug, k_aug)

    def values(p, step):
        k0 = pl.multiple_of(step * kt, kt)
        return jnp.dot(p, vs_ref[0, pl.ds(k0, kt), :], preferred_element_type=F32)

    def sel_step(step, carry):
        s, p_prev, alpha_prev = carry
        s_next = scores(jnp.minimum(step + 1, n_tiles - 1))
        acc_sc[...] = alpha_prev * acc_sc[...] + values(p_prev, jnp.maximum(step - 1, 0))
        m_prev = m_sc[...]
        m_new = jnp.maximum(m_prev, jnp.max(s, axis=-1, keepdims=True))
        alpha = jnp.exp(m_prev - m_new)
        p = jnp.exp(s - m_new)
        l_sc[...] = alpha * l_sc[...] + jnp.sum(p, axis=-1, keepdims=True)
        m_sc[...] = m_new
        return s_next, p.astype(BF16), alpha

    first = (scores(0), jnp.zeros((rows, kt), BF16), jnp.ones((rows, 1), F32))
    _, p_last, alpha_last = lax.fori_loop(0, n_tiles, sel_step, first)
    acc = alpha_last * acc_sc[...] + values(p_last, jnp.maximum(n_tiles - 1, 0))
    o_sel = acc * (1.0 / l_sc[...])

    w0 = pl.multiple_of(jnp.maximum(t0 + tq - wk, 0), LANES)
    s_w = _dot_nt(q, kw_ref[0, pl.ds(w0, wk), :])
    kpos = w0 + lax.broadcasted_iota(I32, (rows, wk), 1)
    wmask = (kpos <= t_row) & (kpos > t_row - WINDOW)
    s_w = jnp.where(wmask, s_w, NEG)
    e_w = jnp.exp(s_w - jnp.max(s_w, axis=-1, keepdims=True))
    o_win = jnp.dot(e_w.astype(BF16), vw_ref[0, pl.ds(w0, wk), :], preferred_element_type=F32)
    o_win = o_win * (1.0 / jnp.sum(e_w, axis=-1, keepdims=True))

    gt = _sigmoid(gate_ref[0])
    for p in range(NSA_HPG):
        sl = slice(p * tq, (p + 1) * tq)
        o = (gt[:, 3 * p:3 * p + 1] * o_cmp[sl] + gt[:, 3 * p + 1:3 * p + 2] * o_sel[sl]
             + gt[:, 3 * p + 2:3 * p + 3] * o_win[sl])
        o_ref[0, :, p * d:(p + 1) * d] = o.astype(BF16)


def nsa_attention(a2, kc, vc, gates3):
    b, s, _ = a2.shape
    g, d = NSA_GROUPS, HEAD_DIM
    ncp = kc.shape[1]
    tq = NSA_TQ
    kt = min(NSA_KT, s)
    wk = WINDOW + tq
    ns = s // SEL_BLOCK
    assert s % kt == 0 and s >= wk and ns <= LANES and tq == LANES
    n_top = min(SEL_TOPN, ns)
    rows = NSA_HPG * tq

    key_blk = np.arange(s) // SEL_BLOCK
    e_t = jnp.asarray(key_blk[:, None] == np.arange(LANES)[None, :], dtype=BF16)
    c_start = np.arange(ncp) * CMP_STRIDE
    j_start = np.arange(LANES) * SEL_BLOCK
    overlap = ((c_start[:, None] < j_start[None, :] + SEL_BLOCK) & (c_start[:, None] + CMP_BLOCK > j_start[None, :])
               & (np.arange(ncp)[:, None] < s // CMP_STRIDE - 1) & (np.arange(LANES)[None, :] < ns))
    overlap = jnp.asarray(overlap.astype(np.float32))

    q_cols = NSA_HPG * d
    kv0 = NSA_WIDTH // d

    def kv_spec(slab):
        return pl.BlockSpec((1, s, d), lambda bi, gi, i, slab=slab: (bi, 0, kv0 + slab * g + gi))

    cmp_spec = pl.BlockSpec((1, ncp, d), lambda bi, gi, i: (bi * g + gi, 0, 0))
    return pl.pallas_call(
        functools.partial(_nsa_attn_kernel, tq=tq, kt=kt, wk=wk, ns=ns, n_top=n_top),
        grid=(b, g, s // tq),
        in_specs=[pl.BlockSpec((1, tq, q_cols), lambda bi, gi, i: (bi, i, gi)),
                  cmp_spec, cmp_spec,
                  kv_spec(2), kv_spec(3), kv_spec(4), kv_spec(5),
                  pl.BlockSpec((s, LANES), lambda bi, gi, i: (0, 0)),
                  pl.BlockSpec((ncp, LANES), lambda bi, gi, i: (0, 0)),
                  pl.BlockSpec((1, tq, LANES), lambda bi, gi, i: (bi, i, gi))],
        out_specs=pl.BlockSpec((1, tq, q_cols), lambda bi, gi, i: (bi, i, gi)),
        out_shape=jax.ShapeDtypeStruct((b, s, NSA_WIDTH), BF16),
        scratch_shapes=[pltpu.VMEM((rows, 1), F32), pltpu.VMEM((rows, 1), F32),
                        pltpu.VMEM((rows, d), F32), pltpu.VMEM((LANES, tq), F32)],
        compiler_params=_cparams(("parallel", "parallel", "arbitrary")),
        name="nsa_attention",
    )(a2, kc, vc, a2, a2, a2, a2, e_t, overlap, gates3)


def _conv_kernel(cur_ref, prev_ref, w_ref, o_ref, *, tc):
    i = pl.program_id(1)
    cur = cur_ref[0].astype(F32)
    prev = jnp.where(i > 0, prev_ref[0].astype(F32), 0.0)
    pad = prev.shape[0] // 2
    full = jnp.concatenate([prev[pad:], cur], axis=0)
    y = None
    for kk in range(CONV_WIDTH):
        off = pad - (CONV_WIDTH - 1) + kk
        term = w_ref[kk:kk + 1, :] * full[off:off + tc]
        y = term if y is None else y + term
    o_ref[0] = (y * _sigmoid(y)).astype(BF16)


def conv_silu(p3, w, col0, width):
    b, s, _ = p3.shape
    tc = _tile(s, 512, BF16_SUBLANES)
    cw = 512
    assert col0 % cw == 0 and width % cw == 0
    c0 = col0 // cw
    halo = BF16_SUBLANES
    return pl.pallas_call(
        functools.partial(_conv_kernel, tc=tc),
        grid=(b, s // tc, width // cw),
        in_specs=[pl.BlockSpec((1, tc, cw), lambda bi, i, j: (bi, i, c0 + j)),
                  pl.BlockSpec((1, halo, cw), lambda bi, i, j: (bi, jnp.maximum(i * (tc // halo) - 1, 0), c0 + j)),
                  pl.BlockSpec((CONV_WIDTH, cw), lambda bi, i, j: (0, j))],
        out_specs=pl.BlockSpec((1, tc, cw), lambda bi, i, j: (bi, i, j)),
        out_shape=jax.ShapeDtypeStruct((b, s, width), BF16),
        compiler_params=_cparams(("parallel", "parallel", "parallel")),
        name="mlstm_conv",
    )(p3, p3, w)


def _log_sigmoid(x):
    return jnp.minimum(x, 0.0) - jnp.log1p(jnp.exp(-jnp.abs(x)))


def _mlstm_kernel(fb_ref, q_ref, k_ref, v_ref, ob_ref, gate_ref, nw_ref, o_ref, c_sc, n_sc, m_sc, *, cl):
    hd = pl.program_id(1)

    @pl.when(pl.program_id(2) == 0)
    def _():
        c_sc[...] = jnp.zeros(c_sc.shape, F32)
        n_sc[...] = jnp.zeros(n_sc.shape, F32)
        m_sc[...] = jnp.zeros(m_sc.shape, F32)

    q = q_ref[0]
    v = v_ref[0]
    ks32 = k_ref[0].astype(F32) * (MLSTM_DIM ** -0.5)
    ks = ks32.astype(BF16)
    slab = gate_ref[0]
    lane = lax.broadcasted_iota(I32, slab.shape, 1)
    i_col = jnp.sum(jnp.where(lane == GATE_I_LANE + hd, slab, 0.0), axis=-1, keepdims=True)
    f_col = jnp.sum(jnp.where(lane == GATE_F_LANE + hd, slab, 0.0), axis=-1, keepdims=True)
    lf_col = _log_sigmoid(f_col + fb_ref[hd])

    r_i = lax.broadcasted_iota(I32, (cl, cl), 0)
    c_i = lax.broadcasted_iota(I32, (cl, cl), 1)
    eye = r_i == c_i
    tri = c_i <= r_i
    lf_row = jnp.sum(jnp.where(eye, lf_col, 0.0), axis=0, keepdims=True)
    ig_row = jnp.sum(jnp.where(eye, i_col, 0.0), axis=0, keepdims=True)
    a_col = jnp.sum(jnp.where(tri, lf_row, 0.0), axis=1, keepdims=True)
    a_row = jnp.sum(jnp.where(r_i <= c_i, lf_col, 0.0), axis=0, keepdims=True)
    m_prev = m_sc[...]

    dlog = jnp.where(tri, a_col - a_row + ig_row, NEG)
    inter = a_col + m_prev
    mt = jnp.maximum(inter, jnp.max(dlog, axis=-1, keepdims=True))
    wm = jnp.exp(dlog - mt) * _dot_nt(q, ks)
    e_col = jnp.exp(inter - mt)
    num = e_col * jnp.dot(q, c_sc[...].astype(BF16), preferred_element_type=F32) \
        + jnp.dot(wm.astype(BF16), v, preferred_element_type=F32)
    qn = jnp.sum(q.astype(F32) * n_sc[...], axis=-1, keepdims=True)
    den = e_col * qn + jnp.sum(wm, axis=-1, keepdims=True)
    hh = num / jnp.maximum(jnp.abs(den), jnp.exp(-mt))

    a_last = jnp.sum(lf_row, axis=-1, keepdims=True)
    gs = a_last - a_col + i_col
    m_new = jnp.maximum(a_last + m_prev, jnp.max(gs, axis=0, keepdims=True))
    decay = jnp.exp(a_last + m_prev - m_new)
    wk = jnp.exp(gs - m_new) * ks32
    c_sc[...] = decay * c_sc[...] + _dot_tn(wk.astype(BF16), v)
    n_sc[...] = decay * n_sc[...] + jnp.sum(wk, axis=0, keepdims=True)
    m_sc[...] = m_new

    y = hh * lax.rsqrt(jnp.mean(hh * hh, axis=-1, keepdims=True) + RMS_EPS) * nw_ref[...]
    o_ref[0] = (y * _sigmoid(ob_ref[0].astype(F32))).astype(BF16)


def mlstm(qk, p3, gates3, f_bias, norm_w, v_col0, o_col0):
    b, s, _ = qk.shape
    cl = min(CHUNK, s)
    dh = MLSTM_DIM
    nh = MLSTM_HEADS
    assert v_col0 % dh == 0 and o_col0 % dh == 0 and s % cl == 0
    vb, ob = v_col0 // dh, o_col0 // dh
    return pl.pallas_call(
        functools.partial(_mlstm_kernel, cl=cl),
        grid=(b, nh, s // cl),
        in_specs=[pl.BlockSpec(memory_space=pltpu.SMEM),
                  pl.BlockSpec((1, cl, dh), lambda bi, h, c: (bi, c, h)),
                  pl.BlockSpec((1, cl, dh), lambda bi, h, c: (bi, c, nh + h)),
                  pl.BlockSpec((1, cl, dh), lambda bi, h, c: (bi, c, vb + h)),
                  pl.BlockSpec((1, cl, dh), lambda bi, h, c: (bi, c, ob + h)),
                  pl.BlockSpec((1, cl, LANES), lambda bi, h, c: (bi, c, 0)),
                  pl.BlockSpec((1, dh), lambda bi, h, c: (0, h))],
        out_specs=pl.BlockSpec((1, cl, dh), lambda bi, h, c: (bi, c, h)),
        out_shape=jax.ShapeDtypeStruct((b, s, nh * dh), BF16),
        scratch_shapes=[pltpu.VMEM((dh, dh), F32), pltpu.VMEM((1, dh), F32), pltpu.VMEM((1, 1), F32)],
        compiler_params=_cparams(("parallel", "parallel", "arbitrary")),
        name="mlstm",
    )(f_bias.astype(F32), qk, qk, p3, p3, gates3, norm_w.reshape(1, nh * dh).astype(F32))


def _ret_kernel(lg_ref, q_ref, k_ref, v_ref, g_ref, cos_ref, sin_ref, nw_ref, o_ref, r_sc, *, cl):
    hd = pl.program_id(1)

    @pl.when(pl.program_id(2) == 0)
    def _():
        r_sc[...] = jnp.zeros(r_sc.shape, F32)

    lg = lg_ref[hd]
    cos = cos_ref[0]
    sin = sin_ref[0]
    half = RET_QK_DIM // 2

    def rope(x):
        x1, x2 = x[:, :half], x[:, half:]
        return jnp.concatenate([x1 * cos - x2 * sin, x1 * sin + x2 * cos], axis=1)

    qr = rope(q_ref[0].astype(F32)).astype(BF16)
    kr = rope(k_ref[0].astype(F32)) * (RET_QK_DIM ** -0.5)
    v = v_ref[0]

    r_i = lax.broadcasted_iota(I32, (cl, cl), 0)
    c_i = lax.broadcasted_iota(I32, (cl, cl), 1)
    diff = (r_i - c_i).astype(F32)
    dm = jnp.where(diff >= 0, jnp.exp(jnp.maximum(diff, 0.0) * lg), 0.0)
    idx = lax.broadcasted_iota(I32, (cl, 1), 0).astype(F32)
    xi = jnp.exp((idx + 1.0) * lg)
    zeta = jnp.exp((cl - 1.0 - idx) * lg)
    chunk_decay = jnp.exp(jnp.full((1, 1), cl, F32) * lg)

    inner = jnp.dot((_dot_nt(qr, kr.astype(BF16)) * dm).astype(BF16), v, preferred_element_type=F32)
    cross = jnp.dot(qr, r_sc[...].astype(BF16), preferred_element_type=F32) * xi
    r_sc[...] = chunk_decay * r_sc[...] + _dot_tn((kr * zeta).astype(BF16), v)

    y = inner + cross
    y = y * lax.rsqrt(jnp.mean(y * y, axis=-1, keepdims=True) + RMS_EPS) * nw_ref[...]
    gg = g_ref[0].astype(F32)
    o_ref[0] = (y * (gg * _sigmoid(gg))).astype(BF16)


def retention(po3, cos, sin, norm_w):
    b, s, _ = po3.shape
    cl = min(CHUNK, s)
    nh, dk, dv = RET_HEADS, RET_QK_DIM, RET_V_DIM
    log_g = jnp.log1p(-jnp.exp2(-5.0 - jnp.arange(nh, dtype=F32)))
    kb = RET_QK_WIDTH // dk
    vb = 2 * RET_QK_WIDTH // dv
    gb = (2 * RET_QK_WIDTH + RET_V_WIDTH) // dv
    tab = pl.BlockSpec((1, cl, LANES), lambda bi, h, c: (bi, c, 0))
    return pl.pallas_call(
        functools.partial(_ret_kernel, cl=cl),
        grid=(b, nh, s // cl),
        in_specs=[pl.BlockSpec(memory_space=pltpu.SMEM),
                  pl.BlockSpec((1, cl, dk), lambda bi, h, c: (bi, c, h)),
                  pl.BlockSpec((1, cl, dk), lambda bi, h, c: (bi, c, kb + h)),
                  pl.BlockSpec((1, cl, dv), lambda bi, h, c: (bi, c, vb + h)),
                  pl.BlockSpec((1, cl, dv), lambda bi, h, c: (bi, c, gb + h)),
                  tab, tab,
                  pl.BlockSpec((1, dv), lambda bi, h, c: (0, h))],
        out_specs=pl.BlockSpec((1, cl, dv), lambda bi, h, c: (bi, c, h)),
        out_shape=jax.ShapeDtypeStruct((b, s, nh * dv), BF16),
        scratch_shapes=[pltpu.VMEM((dk, dv), F32)],
        compiler_params=_cparams(("parallel", "parallel", "arbitrary")),
        name="retention",
    )(log_g, po3, po3, po3, po3, cos, sin, norm_w.reshape(1, nh * dv).astype(F32))


def _router_kernel(x_ref, g_ref, wr_ref, xn_ref, route_ref, cnt_ref, carry_sc):
    @pl.when(pl.program_id(0) == 0)
    def _():
        carry_sc[...] = jnp.zeros(carry_sc.shape, F32)

    x = x_ref[...]
    t = x.shape[0]
    y = x * lax.rsqrt(jnp.mean(x * x, axis=-1, keepdims=True) + RMS_EPS) * g_ref[...]
    xn_ref[...] = y.astype(BF16)
    logits = jnp.dot(y, wr_ref[...], precision=HIGHEST, preferred_element_type=F32)
    lane = lax.broadcasted_iota(I32, logits.shape, 1)
    lg = jnp.where(lane < N_EXPERTS, logits, LOWEST)
    v1 = jnp.max(lg, axis=-1, keepdims=True)
    i1 = jnp.min(jnp.where(lg == v1, lane, LANES), axis=-1, keepdims=True)
    lg2 = jnp.where(lane == i1, LOWEST, lg)
    v2 = jnp.max(lg2, axis=-1, keepdims=True)
    i2 = jnp.min(jnp.where(lg2 == v2, lane, LANES), axis=-1, keepdims=True)
    e2 = jnp.exp(v2 - v1)
    g1 = 1.0 / (1.0 + e2)
    g2 = e2 / (1.0 + e2)

    chosen = jnp.where((lane == i1) | (lane == i2), 1.0, 0.0)
    r_i = lax.broadcasted_iota(I32, (t, t), 0)
    c_i = lax.broadcasted_iota(I32, (t, t), 1)
    tri = jnp.where(c_i <= r_i, 1.0, 0.0).astype(BF16)
    seen = jnp.dot(tri, chosen.astype(BF16), preferred_element_type=F32) + carry_sc[...]
    rank1 = jnp.sum(jnp.where(lane == i1, seen, 0.0), axis=-1, keepdims=True) - 1.0
    rank2 = jnp.sum(jnp.where(lane == i2, seen, 0.0), axis=-1, keepdims=True) - 1.0
    total = seen[t - 1:t, :]
    carry_sc[...] = total
    cnt_ref[...] = jnp.broadcast_to(total, cnt_ref.shape)

    route = jnp.where(lane == 0, i1.astype(F32), 0.0)
    route = jnp.where(lane == 1, i2.astype(F32), route)
    route = jnp.where(lane == 2, g1, route)
    route = jnp.where(lane == 3, g2, route)
    route = jnp.where(lane == 4, rank1, route)
    route = jnp.where(lane == 5, rank2, route)
    route_ref[...] = route


def moe_route(x, g, w_router, tm=256):
    m, d = x.shape
    tm = _tile(m, tm, 16)
    wr = jnp.zeros((d, LANES), F32).at[:, :N_EXPERTS].set(w_router.astype(F32))
    return pl.pallas_call(
        _router_kernel,
        grid=(m // tm,),
        in_specs=[pl.BlockSpec((tm, d), lambda i: (i, 0)),
                  pl.BlockSpec((1, d), lambda i: (0, 0)),
                  pl.BlockSpec((d, LANES), lambda i: (0, 0))],
        out_specs=[pl.BlockSpec((tm, d), lambda i: (i, 0)),
                   pl.BlockSpec((tm, LANES), lambda i: (i, 0)),
                   pl.BlockSpec((8, LANES), lambda i: (0, 0))],
        out_shape=[jax.ShapeDtypeStruct((m, d), BF16),
                   jax.ShapeDtypeStruct((m, LANES), F32),
                   jax.ShapeDtypeStruct((8, LANES), F32)],
        scratch_shapes=[pltpu.VMEM((1, LANES), F32)],
        compiler_params=_cparams(("arbitrary",)),
        name="moe_route",
    )(x, g.reshape(1, d).astype(F32), wr)


def _dispatch_kernel(idx_ref, src_ref, o_ref, sem, *, rows):
    base = pl.program_id(0) * rows

    def issue(r, carry):
        pltpu.make_async_copy(src_ref.at[idx_ref[base + r]], o_ref.at[r], sem).start()
        return carry

    lax.fori_loop(0, rows, issue, 0)
    pltpu.make_async_copy(src_ref.at[pl.ds(0, rows)], o_ref, sem).wait()


def moe_dispatch(xn, row_tok, rows=256):
    m, d = xn.shape
    n_rows = row_tok.shape[0]
    rows = _tile(n_rows, rows, 8)
    src = xn.reshape(m, d // LANES, LANES)
    out = pl.pallas_call(
        functools.partial(_dispatch_kernel, rows=rows),
        grid_spec=pltpu.PrefetchScalarGridSpec(
            num_scalar_prefetch=1,
            grid=(n_rows // rows,),
            in_specs=[pl.BlockSpec(memory_space=pl.ANY)],
            out_specs=pl.BlockSpec((rows, d // LANES, LANES), lambda i, idx: (i, 0, 0)),
            scratch_shapes=[pltpu.SemaphoreType.DMA(())],
        ),
        out_shape=jax.ShapeDtypeStruct((n_rows, d // LANES, LANES), xn.dtype),
        compiler_params=_cparams(("arbitrary",)),
        name="moe_dispatch",
    )(row_tok, src)
    return out.reshape(n_rows, d)


def _combine_kernel(d1_ref, d2_ref, x_ref, route_ref, yb_ref, o_ref, buf1, buf2, sem, *, rows):
    base = pl.program_id(0) * rows

    def issue(r, carry):
        pltpu.make_async_copy(yb_ref.at[pl.ds(d1_ref[base + r], 1)], buf1.at[pl.ds(r, 1)], sem.at[0]).start()
        pltpu.make_async_copy(yb_ref.at[pl.ds(d2_ref[base + r], 1)], buf2.at[pl.ds(r, 1)], sem.at[1]).start()
        return carry

    lax.fori_loop(0, rows, issue, 0)
    pltpu.make_async_copy(yb_ref.at[pl.ds(0, rows)], buf1, sem.at[0]).wait()
    pltpu.make_async_copy(yb_ref.at[pl.ds(0, rows)], buf2, sem.at[1]).wait()
    route = route_ref[...]
    o_ref[...] = x_ref[...] + (buf1[...] * route[:, 2:3] + buf2[...] * route[:, 3:4])


def moe_combine(x, route, yb, dest1, dest2, rows=256):
    m, d = x.shape
    rows = _tile(m, rows, 8)
    return pl.pallas_call(
        functools.partial(_combine_kernel, rows=rows),
        grid_spec=pltpu.PrefetchScalarGridSpec(
            num_scalar_prefetch=2,
            grid=(m // rows,),
            in_specs=[pl.BlockSpec((rows, d), lambda i, a, b: (i, 0)),
                      pl.BlockSpec((rows, LANES), lambda i, a, b: (i, 0)),
                      pl.BlockSpec(memory_space=pl.ANY)],
            out_specs=pl.BlockSpec((rows, d), lambda i, a, b: (i, 0)),
            scratch_shapes=[pltpu.VMEM((rows, d), F32), pltpu.VMEM((rows, d), F32),
                            pltpu.SemaphoreType.DMA((2,))],
        ),
        out_shape=jax.ShapeDtypeStruct((m, d), F32),
        compiler_params=_cparams(("arbitrary",)),
        name="moe_combine",
    )(dest1, dest2, x, route, yb)


def moe_layer(x, norm_g, w_router, w_gate, w_up, w_down):
    m, d = x.shape
    tm = MOE_TM
    xn, route, cnt = moe_route(x, norm_g, w_router)
    expert = route[:, 0:2].astype(I32)
    rank = route[:, 4:6].astype(I32)
    counts = cnt[0, :N_EXPERTS].astype(I32)
    padded = (counts + tm - 1) // tm * tm
    pend = jnp.cumsum(padded)
    pstart = pend - padded
    dest = pstart[expert] + rank
    n_blk = -(-(2 * m) // tm) + N_EXPERTS
    n_rows = n_blk * tm
    tok = jnp.broadcast_to(jnp.arange(m, dtype=I32)[:, None], (m, 2))
    row_tok = jnp.zeros((n_rows,), I32).at[dest.reshape(-1)].set(tok.reshape(-1))
    nused = (pend[-1] // tm).astype(I32)
    blk = jnp.minimum(jnp.arange(n_blk, dtype=I32), nused - 1)
    blk_e = jnp.minimum(jnp.sum(pend[None, :] <= (blk * tm)[:, None], axis=1), N_EXPERTS - 1).astype(I32)
    nu = nused.reshape(1)

    xb = moe_dispatch(xn, row_tok)
    tf = _tile(w_gate.shape[2], 512, LANES)
    hidden = gmm_swiglu(xb, w_gate, w_up, blk_e, nu, tm=tm, tn=tf)
    yb = gmm(hidden, w_down, blk_e, nu, tm=tm, tn=_tile(d, 512, LANES), out_dtype=F32)
    return moe_combine(x, route, yb, dest[:, 0], dest[:, 1])


def even_layer(x2, b, s, pos, w_norm, w_in, q_gain, k_gain, w_cmp_k, w_cmp_v, pe_cmp, conv_w, f_bias, m_norm,
               w_out, w_norm_ffn, w_gate, w_up, w_down):
    n, dm = x2.shape
    g, d = NSA_GROUPS, HEAD_DIM
    o_gate = NSA_SLAB
    o_qb = o_gate + NSA_HEADS * 3
    o_if = o_qb + 3 * MLSTM_WIDTH
    o_ob = o_if + 2 * MLSTM_HEADS
    w_main = jnp.concatenate([w_in[:, :o_gate], w_in[:, o_qb:o_if], w_in[:, o_ob:o_ob + MLSTM_WIDTH]], axis=1)
    per_g = NSA_HPG * 3
    zeros = functools.partial(jnp.zeros, dtype=w_in.dtype)
    w_small = jnp.concatenate(
        [w_in[:, o_gate:o_gate + per_g], zeros((dm, GATE_I_LANE - per_g)),
         w_in[:, o_if:o_ob], zeros((dm, LANES - GATE_I_LANE - 2 * MLSTM_HEADS)),
         w_in[:, o_gate + per_g:o_qb], zeros((dm, LANES - per_g))], axis=1)

    h = rmsnorm_rows(x2, w_norm)
    p3 = mm(h, w_main, out_dtype=BF16).reshape(b, s, -1)
    gates3 = mm(h, w_small, out_dtype=F32, tn=2 * LANES).reshape(b, s, 2 * LANES)

    cos_n, sin_n = trig_tables(pos, _nsa_inv_lane())
    a2 = nsa_prep(p3, cos_n.reshape(b, s, LANES), sin_n.reshape(b, s, LANES), q_gain, k_gain)
    nh = s // CMP_STRIDE
    cmp_pos = jnp.concatenate([pos[:, CMP_BLOCK - 1::CMP_STRIDE][:, :nh - 1], pos[:, -1:]], axis=1)
    cos_c, sin_c = trig_tables(cmp_pos, _nsa_inv_lane())

    def cmp_blocks(col0):
        tok = p3[:, :, col0:col0 + NSA_KV_WIDTH].reshape(b, nh, CMP_STRIDE, g, d)
        halves = tok.transpose(0, 3, 1, 2, 4).reshape(b, g, nh, CMP_STRIDE * d)
        nxt = jnp.concatenate([halves[:, :, 1:], jnp.zeros_like(halves[:, :, :1])], axis=2)
        return jnp.concatenate([halves, nxt], axis=-1).reshape(b * g * nh, CMP_BLOCK * d)

    tabs = (cos_c.reshape(b, nh, LANES), sin_c.reshape(b, nh, LANES))
    kc = compress(cmp_blocks(NSA_WIDTH), pe_cmp, w_cmp_k, k_gain, *tabs, is_key=True, rows_per_seq=nh, groups=g)
    vc = compress(cmp_blocks(NSA_WIDTH + NSA_KV_WIDTH), pe_cmp, w_cmp_v, k_gain, *tabs, is_key=False,
                  rows_per_seq=nh, groups=g)
    o_nsa = nsa_attention(a2, kc.reshape(b * g, nh, d), vc.reshape(b * g, nh, d), gates3)

    qk = conv_silu(p3, conv_w, NSA_SLAB, 2 * MLSTM_WIDTH)
    h_b = mlstm(qk, p3, gates3, f_bias, m_norm, NSA_SLAB + 2 * MLSTM_WIDTH, NSA_SLAB + 3 * MLSTM_WIDTH)

    mixed = jnp.concatenate([o_nsa, h_b], axis=-1).reshape(n, NSA_WIDTH + MLSTM_WIDTH)
    x2 = mm(mixed, w_out, out_dtype=F32, residual=x2)
    hf = rmsnorm_rows(x2, w_norm_ffn)
    hidden = mm_swiglu(hf, w_gate, w_up)
    return mm(hidden, w_down, out_dtype=F32, residual=x2, tm=1024)


def odd_layer(x2, b, s, pos, w_norm, w_in, r_norm, w_out, w_norm_ffn, w_router, e_gate, e_up, e_down):
    n, dm = x2.shape
    h = rmsnorm_rows(x2, w_norm)
    po3 = mm(h, w_in, out_dtype=BF16, tn=1024).reshape(b, s, -1)
    cos_r, sin_r = trig_tables(pos, _ret_inv_lane())
    y = retention(po3, cos_r.reshape(b, s, LANES), sin_r.reshape(b, s, LANES), r_norm)
    x2 = mm(y.reshape(n, RET_V_WIDTH), w_out, out_dtype=F32, residual=x2, tm=1024)
    return moe_layer(x2, w_norm_ffn, w_router, e_gate, e_up, e_down)


def kernel(x, positions, norm_mix_even, w_in_even, nsa_q_gain, nsa_k_gain, w_cmp_k, w_cmp_v, pe_cmp, mlstm_conv, mlstm_f_bias, mlstm_norm, w_out_even, norm_ffn_even, ffn_gate, ffn_up, ffn_down, norm_mix_odd, w_in_odd, ret_norm, w_out_odd, norm_ffn_odd, w_router, exp_gate, exp_up, exp_down):
    b, s, dm = x.shape
    depth = norm_mix_even.shape[0] + norm_mix_odd.shape[0]
    x2 = x.reshape(b * s, dm)
    for layer in range(depth):
        j = layer // 2
        if layer % 2 == 0:
            x2 = even_layer(x2, b, s, positions, norm_mix_even[j], w_in_even[j], nsa_q_gain[j], nsa_k_gain[j],
                            w_cmp_k[j], w_cmp_v[j], pe_cmp[j], mlstm_conv[j], mlstm_f_bias[j], mlstm_norm[j],
                            w_out_even[j], norm_ffn_even[j], ffn_gate[j], ffn_up[j], ffn_down[j])
        else:
            x2 = odd_layer(x2, b, s, positions, norm_mix_odd[j], w_in_odd[j], ret_norm[j], w_out_odd[j],
                           norm_ffn_odd[j], w_router[j], exp_gate[j], exp_up[j], exp_down[j])
    return x2.reshape(b, s, dm)
```

```python
import functools

import numpy as np
import jax
import jax.numpy as jnp
from jax import lax
from jax.experimental import pallas as pl
from jax.experimental.pallas import tpu as pltpu

F32 = jnp.float32
BF16 = jnp.bfloat16
I32 = jnp.int32
HIGHEST = lax.Precision.HIGHEST

HEAD_DIM = 128
NSA_HEADS = 8
NSA_GROUPS = 2
NSA_HPG = NSA_HEADS // NSA_GROUPS
NSA_WIDTH = NSA_HEADS * HEAD_DIM
NSA_KV_WIDTH = NSA_GROUPS * HEAD_DIM
NSA_SLAB = NSA_WIDTH + 6 * NSA_KV_WIDTH
CMP_BLOCK = 32
CMP_STRIDE = 16
SEL_BLOCK = 64
SEL_TOPN = 16
WINDOW = 512
ROPE_DIM = HEAD_DIM // 4
ROPE_THETA = 500000.0
SEL_FORCE = 1.0e6
NEG = -1.0e30
LOWEST = -3.0e38
MLSTM_HEADS = 4
MLSTM_DIM = 256
MLSTM_WIDTH = MLSTM_HEADS * MLSTM_DIM
CONV_WIDTH = 4
RET_HEADS = 8
RET_QK_DIM = 256
RET_V_DIM = 512
RET_QK_WIDTH = RET_HEADS * RET_QK_DIM
RET_V_WIDTH = RET_HEADS * RET_V_DIM
RET_ROPE_THETA = 10000.0
N_EXPERTS = 8
RMS_EPS = 1e-6

LANES = 128
BF16_SUBLANES = 16
V7X_VMEM_BYTES = 64 * 1024 * 1024
VMEM_LIMIT = V7X_VMEM_BYTES - 8 * 1024 * 1024

CHUNK = 256
NSA_TQ = 128
NSA_KT = 512
MOE_TM = 512
GATE_I_LANE = 16
GATE_F_LANE = 20


def _cparams(sem, vmem=VMEM_LIMIT):
    return pltpu.CompilerParams(dimension_semantics=sem, vmem_limit_bytes=vmem)


def _tile(n, target, quantum):
    if n <= target:
        return n
    t = (target // quantum) * quantum
    while t > quantum and n % t:
        t -= quantum
    assert n % t == 0, (n, target, quantum)
    return t


def _sigmoid(x):
    return 1.0 / (1.0 + jnp.exp(-x))


def _dot_nt(a, b):
    return lax.dot_general(a, b, (((1,), (1,)), ((), ())), preferred_element_type=F32)


def _dot_tn(a, b):
    return lax.dot_general(a, b, (((0,), (0,)), ((), ())), preferred_element_type=F32)


def _rmsnorm_kernel(x_ref, g_ref, o_ref):
    x = x_ref[...]
    y = x * lax.rsqrt(jnp.mean(x * x, axis=-1, keepdims=True) + RMS_EPS)
    o_ref[...] = (y * g_ref[...]).astype(o_ref.dtype)


def rmsnorm_rows(x, g, tm=512):
    m, d = x.shape
    tm = _tile(m, tm, 8)
    return pl.pallas_call(
        _rmsnorm_kernel,
        grid=(m // tm,),
        in_specs=[pl.BlockSpec((tm, d), lambda i: (i, 0)), pl.BlockSpec((1, d), lambda i: (0, 0))],
        out_specs=pl.BlockSpec((tm, d), lambda i: (i, 0)),
        out_shape=jax.ShapeDtypeStruct((m, d), BF16),
        compiler_params=_cparams(("parallel",)),
        name="rmsnorm",
    )(x, g.reshape(1, d).astype(F32))


def _gmm_kernel(be_ref, nu_ref, x_ref, w_ref, *rest, has_res):
    if has_res:
        r_ref, o_ref, wb_ref = rest
    else:
        o_ref, wb_ref = rest
    i = pl.program_id(1)
    changed = be_ref[i] != be_ref[jnp.maximum(i - 1, 0)]

    @pl.when((i == 0) | changed)
    def _():
        wb_ref[...] = w_ref[0].astype(BF16)

    @pl.when(i < nu_ref[0])
    def _():
        acc = jnp.dot(x_ref[...], wb_ref[...], preferred_element_type=F32)
        if has_res:
            acc = r_ref[...] + acc
        o_ref[...] = acc.astype(o_ref.dtype)

    @pl.when(i >= nu_ref[0])
    def _():
        o_ref[...] = jnp.zeros_like(o_ref)


def _weight_spec(k, tn, single_buffer):
    mode = dict(pipeline_mode=pl.Buffered(1)) if single_buffer else {}
    return pl.BlockSpec((1, k, tn), lambda j, i, be, nu: (be[i], 0, j), **mode)


def gmm(x, w, blk_e, nused, *, tm, tn, out_dtype, residual=None, single_buffer_w=False):
    m, k = x.shape
    e, k2, n = w.shape
    assert k == k2 and m % tm == 0 and n % tn == 0
    nb = m // tm
    in_specs = [
        pl.BlockSpec((tm, k), lambda j, i, be, nu: (jnp.minimum(i, nu[0] - 1), 0)),
        _weight_spec(k, tn, single_buffer_w),
    ]
    args = [x, w]
    if residual is not None:
        in_specs.append(pl.BlockSpec((tm, tn), lambda j, i, be, nu: (i, j)))
        args.append(residual)
    return pl.pallas_call(
        functools.partial(_gmm_kernel, has_res=residual is not None),
        grid_spec=pltpu.PrefetchScalarGridSpec(
            num_scalar_prefetch=2,
            grid=(n // tn, nb),
            in_specs=in_specs,
            out_specs=pl.BlockSpec((tm, tn), lambda j, i, be, nu: (i, j)),
            scratch_shapes=[pltpu.VMEM((k, tn), BF16)],
        ),
        out_shape=jax.ShapeDtypeStruct((m, n), out_dtype),
        compiler_params=_cparams(("arbitrary", "arbitrary")),
        name="gmm",
    )(blk_e, nused, *args)


def _gmm_swiglu_kernel(be_ref, nu_ref, x_ref, wg_ref, wu_ref, o_ref, wgb_ref, wub_ref):
    i = pl.program_id(1)
    changed = be_ref[i] != be_ref[jnp.maximum(i - 1, 0)]

    @pl.when((i == 0) | changed)
    def _():
        wgb_ref[...] = wg_ref[0].astype(BF16)
        wub_ref[...] = wu_ref[0].astype(BF16)

    @pl.when(i < nu_ref[0])
    def _():
        x = x_ref[...]
        g = jnp.dot(x, wgb_ref[...], preferred_element_type=F32)
        u = jnp.dot(x, wub_ref[...], preferred_element_type=F32)
        o_ref[...] = (g * _sigmoid(g) * u).astype(o_ref.dtype)

    @pl.when(i >= nu_ref[0])
    def _():
        o_ref[...] = jnp.zeros_like(o_ref)


def gmm_swiglu(x, wg, wu, blk_e, nused, *, tm, tn, single_buffer_w=False):
    m, k = x.shape
    e, k2, n = wg.shape
    assert k == k2 and wu.shape == wg.shape and m % tm == 0 and n % tn == 0
    nb = m // tm
    w_spec = _weight_spec(k, tn, single_buffer_w)
    return pl.pallas_call(
        _gmm_swiglu_kernel,
        grid_spec=pltpu.PrefetchScalarGridSpec(
            num_scalar_prefetch=2,
            grid=(n // tn, nb),
            in_specs=[pl.BlockSpec((tm, k), lambda j, i, be, nu: (jnp.minimum(i, nu[0] - 1), 0)),
                      w_spec, w_spec],
            out_specs=pl.BlockSpec((tm, tn), lambda j, i, be, nu: (i, j)),
            scratch_shapes=[pltpu.VMEM((k, tn), BF16), pltpu.VMEM((k, tn), BF16)],
        ),
        out_shape=jax.ShapeDtypeStruct((m, n), BF16),
        compiler_params=_cparams(("arbitrary", "arbitrary")),
        name="gmm_swiglu",
    )(blk_e, nused, x, wg, wu)


def _dense_blocks(m, tm):
    nb = m // tm
    return jnp.zeros((nb,), I32), jnp.full((1,), nb, I32)


def mm(x, w, *, out_dtype, residual=None, tm=2048, tn=512):
    m, k = x.shape
    n = w.shape[1]
    tm = _tile(m, tm, 16)
    tn = _tile(n, tn, LANES)
    be, nu = _dense_blocks(m, tm)
    return gmm(x, w[None], be, nu, tm=tm, tn=tn, out_dtype=out_dtype, residual=residual, single_buffer_w=True)


def mm_swiglu(x, wg, wu, *, tm=2048, tn=512):
    m = x.shape[0]
    tm = _tile(m, tm, 16)
    tn = _tile(wg.shape[1], tn, LANES)
    be, nu = _dense_blocks(m, tm)
    return gmm_swiglu(x, wg[None], wu[None], be, nu, tm=tm, tn=tn, single_buffer_w=True)


def _trig_kernel(pos_ref, inv_ref, cos_ref, sin_ref):
    ang = pos_ref[...] * inv_ref[...]
    cos_ref[...] = jnp.cos(ang)
    sin_ref[...] = jnp.sin(ang)


def trig_tables(pos, inv_lane):
    r = pos.size
    pos_b = jnp.broadcast_to(pos.astype(F32).reshape(r, 1), (r, LANES))
    tr = _tile(r, 512, 8)
    spec = pl.BlockSpec((tr, LANES), lambda i: (i, 0))
    return pl.pallas_call(
        _trig_kernel,
        grid=(r // tr,),
        in_specs=[spec, pl.BlockSpec((1, LANES), lambda i: (0, 0))],
        out_specs=[spec, spec],
        out_shape=[jax.ShapeDtypeStruct((r, LANES), F32)] * 2,
        compiler_params=_cparams(("parallel",)),
        name="trig_tables",
    )(pos_b, inv_lane.reshape(1, LANES))


def _nsa_inv_lane():
    half = ROPE_DIM // 2
    inv = jnp.power(jnp.float32(ROPE_THETA), -jnp.arange(half, dtype=F32) * (2.0 / ROPE_DIM))
    return jnp.concatenate([inv, inv, jnp.zeros((LANES - ROPE_DIM,), F32)])


def _ret_inv_lane():
    half = RET_QK_DIM // 2
    return jnp.power(jnp.float32(RET_ROPE_THETA), -jnp.arange(half, dtype=F32) * (2.0 / RET_QK_DIM))


def _norm_rope_head(x, gain, cos, sin):
    half = ROPE_DIM // 2
    y = x * lax.rsqrt(jnp.mean(x * x, axis=-1, keepdims=True) + RMS_EPS) * gain
    lane = lax.broadcasted_iota(I32, y.shape, 1)
    from_hi = jnp.where(lane < half, -sin, 0.0)
    from_lo = jnp.where((lane >= half) & (lane < ROPE_DIM), sin, 0.0)
    return (y * cos + pltpu.roll(y, LANES - half, 1) * from_hi + pltpu.roll(y, half, 1) * from_lo)


def _nsa_prep_kernel(p_ref, cos_ref, sin_ref, qg_ref, kg_ref, qt_ref, kn_ref, vt_ref):
    cos = cos_ref[0]
    sin = sin_ref[0]
    scale = HEAD_DIM ** -0.5
    d = HEAD_DIM
    g = NSA_GROUPS
    for hd in range(NSA_HEADS):
        q = _norm_rope_head(p_ref[0, :, hd * d:(hd + 1) * d].astype(F32), qg_ref[...], cos, sin)
        qt_ref[0, hd * d:(hd + 1) * d, :] = (q * scale).T.astype(BF16)
    for n, slab in enumerate((2, 4)):
        for gi in range(g):
            off = NSA_WIDTH + slab * NSA_KV_WIDTH + gi * d
            k = _norm_rope_head(p_ref[0, :, off:off + d].astype(F32), kg_ref[...], cos, sin)
            kn_ref[0, :, (n * g + gi) * d:(n * g + gi + 1) * d] = k.astype(BF16)
    for n, slab in enumerate((3, 5)):
        for gi in range(g):
            off = NSA_WIDTH + slab * NSA_KV_WIDTH + gi * d
            v = p_ref[0, :, off:off + d].astype(F32)
            vt_ref[0, (n * g + gi) * d:(n * g + gi + 1) * d, :] = v.T.astype(BF16)


def nsa_prep(p3, cos, sin, q_gain, k_gain):
    b, s, _ = p3.shape
    t = _tile(s, 256, LANES)
    d = HEAD_DIM
    tab = pl.BlockSpec((1, t, LANES), lambda bi, i: (bi, i, 0))
    gain = pl.BlockSpec((1, d), lambda bi, i: (0, 0))
    kv = 2 * NSA_KV_WIDTH
    return pl.pallas_call(
        _nsa_prep_kernel,
        grid=(b, s // t),
        in_specs=[pl.BlockSpec((1, t, NSA_SLAB), lambda bi, i: (bi, i, 0)), tab, tab, gain, gain],
        out_specs=[pl.BlockSpec((1, NSA_WIDTH, t), lambda bi, i: (bi, 0, i)),
                   pl.BlockSpec((1, t, kv), lambda bi, i: (bi, i, 0)),
                   pl.BlockSpec((1, kv, t), lambda bi, i: (bi, 0, i))],
        out_shape=[jax.ShapeDtypeStruct((b, NSA_WIDTH, s), BF16),
                   jax.ShapeDtypeStruct((b, s, kv), BF16),
                   jax.ShapeDtypeStruct((b, kv, s), BF16)],
        compiler_params=_cparams(("parallel", "parallel")),
        name="nsa_prep",
    )(p3, cos, sin, q_gain.reshape(1, d), k_gain.reshape(1, d))


def _compress_kernel(blk_ref, pe_ref, w_ref, kg_ref, cos_ref, sin_ref, o_ref, *, is_key):
    a = (blk_ref[...].astype(F32) + pe_ref[...]).astype(BF16)
    y = jnp.dot(a, w_ref[...].astype(BF16), preferred_element_type=F32)
    if is_key:
        o_ref[0] = _norm_rope_head(y, kg_ref[...], cos_ref[0], sin_ref[0]).astype(BF16)
    else:
        o_ref[0] = y.T.astype(BF16)


def compress(blk, pe, w, k_gain, cos_c, sin_c, *, is_key, rows_per_seq, groups):
    r, kdim = blk.shape
    t = rows_per_seq
    d = HEAD_DIM
    tab = pl.BlockSpec((1, t, LANES), lambda i: (i // groups, 0, 0))
    out_blk = (1, t, d) if is_key else (1, d, t)
    return pl.pallas_call(
        functools.partial(_compress_kernel, is_key=is_key),
        grid=(r // t,),
        in_specs=[pl.BlockSpec((t, kdim), lambda i: (i, 0)),
                  pl.BlockSpec((1, kdim), lambda i: (0, 0)),
                  pl.BlockSpec((kdim, d), lambda i: (0, 0)),
                  pl.BlockSpec((1, d), lambda i: (0, 0)),
                  tab, tab],
        out_specs=pl.BlockSpec(out_blk, lambda i: (i, 0, 0)),
        out_shape=jax.ShapeDtypeStruct((r // t,) + out_blk[1:], BF16),
        compiler_params=_cparams(("parallel",)),
        name="nsa_compress",
    )(blk, pe.reshape(1, kdim), w, k_gain.reshape(1, d), cos_c, sin_c)


def _nsa_attn_kernel(qt_ref, kc_ref, vct_ref, ks_ref, kw_ref, vst_ref, vwt_ref, et_ref, ovt_ref, gate_ref,
                     o_ref, acc_sc, val_sc, *, tq, kt, wk, ns, n_top):
    i = pl.program_id(2)
    t0 = i * tq
    cols = NSA_HPG * tq
    d = HEAD_DIM
    q_t = jnp.concatenate([qt_ref[0, p * d:(p + 1) * d, :] for p in range(NSA_HPG)], axis=1)
    t_lane = t0 + (lax.broadcasted_iota(I32, (1, cols), 1) & (tq - 1))

    ncp = kc_ref.shape[1]
    s_c = jnp.dot(kc_ref[0], q_t, preferred_element_type=F32)
    c_end = lax.broadcasted_iota(I32, (ncp, cols), 0) * CMP_STRIDE + (CMP_BLOCK - 1)
    cmask = c_end <= t_lane
    s_c = jnp.where(cmask, s_c, NEG)
    e_c = jnp.where(cmask, jnp.exp(s_c - jnp.max(s_c, axis=0, keepdims=True)), 0.0)
    den_c = jnp.sum(e_c, axis=0, keepdims=True)
    p_c = e_c * (1.0 / jnp.maximum(den_c, 1e-30))
    o_cmp = jnp.dot(vct_ref[0], p_c.astype(BF16), preferred_element_type=F32)

    p_grp = p_c[:, 0:tq]
    for p in range(1, NSA_HPG):
        p_grp = p_grp + p_c[:, p * tq:(p + 1) * tq]
    imp = jnp.dot(ovt_ref[...], p_grp, precision=HIGHEST, preferred_element_type=F32)
    jb = lax.broadcasted_iota(I32, (LANES, tq), 0)
    cur = (t0 + lax.broadcasted_iota(I32, (1, tq), 1)) // SEL_BLOCK
    forced = (jb == 0) | (jb == cur) | (jb == cur - 1)
    val = jnp.where(jb <= cur, jnp.where(forced, SEL_FORCE, imp), NEG)
    val = jnp.where(jb < ns, val, LOWEST)
    val_sc[...] = val
    beaten = jnp.zeros((LANES, tq), F32)
    for j2 in range(ns):
        r = val_sc[j2:j2 + 1, :]
        ge = jnp.where(r >= val, 1.0, 0.0)
        gt = jnp.where(r > val, 1.0, 0.0)
        beaten = beaten + jnp.where(jb > j2, ge, gt)
    past = jb < t0 // SEL_BLOCK
    bias = jnp.where(past & (beaten < n_top), 0.0, jnp.where(jb < ns, NEG, 0.0)).astype(BF16)
    q_aug = jnp.concatenate([q_t, jnp.concatenate([bias] * NSA_HPG, axis=1)], axis=0)

    d0 = pl.multiple_of(t0, tq)
    s_d = jnp.dot(ks_ref[0, pl.ds(d0, tq), :], q_t, preferred_element_type=F32)
    s_d = jnp.where(d0 + lax.broadcasted_iota(I32, (tq, cols), 0) <= t_lane, s_d, NEG)
    m_d = jnp.max(s_d, axis=0, keepdims=True)
    p_d = jnp.exp(s_d - m_d)
    l_d = jnp.sum(p_d, axis=0, keepdims=True)
    acc_sc[...] = jnp.dot(vst_ref[0, :, pl.ds(d0, tq)], p_d.astype(BF16), preferred_element_type=F32)

    n_tiles = (t0 + kt - 1) // kt

    def scores(step):
        k0 = pl.multiple_of(step * kt, kt)
        k_aug = jnp.concatenate([ks_ref[0, pl.ds(k0, kt), :], et_ref[pl.ds(k0, kt), :]], axis=1)
        return jnp.dot(k_aug, q_aug, preferred_element_type=F32)

    def values(p, step):
        k0 = pl.multiple_of(step * kt, kt)
        return jnp.dot(vst_ref[0, :, pl.ds(k0, kt)], p, preferred_element_type=F32)

    def sel_step(step, carry):
        s, p_prev, alpha_prev, m_prev, l_prev = carry
        s_next = scores(jnp.minimum(step + 1, n_tiles - 1))
        acc_sc[...] = alpha_prev * acc_sc[...] + values(p_prev, jnp.maximum(step - 1, 0))
        m_new = jnp.maximum(m_prev, jnp.max(s, axis=0, keepdims=True))
        alpha = jnp.exp(m_prev - m_new)
        p = jnp.exp(s - m_new)
        l_new = alpha * l_prev + jnp.sum(p, axis=0, keepdims=True)
        return s_next, p.astype(BF16), alpha, m_new, l_new

    first = (scores(0), jnp.zeros((kt, cols), BF16), jnp.ones((1, cols), F32), m_d, l_d)
    _, p_last, alpha_last, _, l_sel = lax.fori_loop(0, n_tiles, sel_step, first)
    acc = alpha_last * acc_sc[...] + values(p_last, jnp.maximum(n_tiles - 1, 0))
    o_sel = acc * (1.0 / l_sel)

    w0 = pl.multiple_of(jnp.maximum(t0 + tq - wk, 0), LANES)
    s_w = jnp.dot(kw_ref[0, pl.ds(w0, wk), :], q_t, preferred_element_type=F32)
    kpos = w0 + lax.broadcasted_iota(I32, (wk, cols), 0)
    wmask = (kpos <= t_lane) & (kpos > t_lane - WINDOW)
    s_w = jnp.where(wmask, s_w, NEG)
    e_w = jnp.exp(s_w - jnp.max(s_w, axis=0, keepdims=True))
    o_win = jnp.dot(vwt_ref[0, :, pl.ds(w0, wk)], e_w.astype(BF16), preferred_element_type=F32)
    o_win = o_win * (1.0 / jnp.sum(e_w, axis=0, keepdims=True))

    g_t = _sigmoid(gate_ref[0]).T
    for p in range(NSA_HPG):
        sl = slice(p * tq, (p + 1) * tq)
        o_t = (g_t[3 * p:3 * p + 1] * o_cmp[:, sl] + g_t[3 * p + 1:3 * p + 2] * o_sel[:, sl]
               + g_t[3 * p + 2:3 * p + 3] * o_win[:, sl])
        o_ref[0, :, p * d:(p + 1) * d] = o_t.T.astype(BF16)


def nsa_attention(q_t, k_n, v_t, kc, vc_t, gates3):
    b, _, s = q_t.shape
    g, d = NSA_GROUPS, HEAD_DIM
    ncp = kc.shape[1]
    tq = NSA_TQ
    kt = min(NSA_KT, s)
    wk = WINDOW + tq
    ns = s // SEL_BLOCK
    assert s % kt == 0 and s >= wk and ns <= LANES and tq == LANES
    n_top = min(SEL_TOPN, ns)
    cols = NSA_HPG * tq

    key_blk = np.arange(s) // SEL_BLOCK
    e_t = jnp.asarray(key_blk[:, None] == np.arange(LANES)[None, :], dtype=BF16)
    c_start = np.arange(ncp) * CMP_STRIDE
    j_start = np.arange(LANES) * SEL_BLOCK
    overlap = ((c_start[:, None] < j_start[None, :] + SEL_BLOCK) & (c_start[:, None] + CMP_BLOCK > j_start[None, :])
               & (np.arange(ncp)[:, None] < s // CMP_STRIDE - 1) & (np.arange(LANES)[None, :] < ns))
    overlap_t = jnp.asarray(overlap.T.astype(np.float32))

    q_rows = NSA_HPG * d
    return pl.pallas_call(
        functools.partial(_nsa_attn_kernel, tq=tq, kt=kt, wk=wk, ns=ns, n_top=n_top),
        grid=(b, g, s // tq),
        in_specs=[pl.BlockSpec((1, q_rows, tq), lambda bi, gi, i: (bi, gi, i)),
                  pl.BlockSpec((1, ncp, d), lambda bi, gi, i: (bi * g + gi, 0, 0)),
                  pl.BlockSpec((1, d, ncp), lambda bi, gi, i: (bi * g + gi, 0, 0)),
                  pl.BlockSpec((1, s, d), lambda bi, gi, i: (bi, 0, gi)),
                  pl.BlockSpec((1, s, d), lambda bi, gi, i: (bi, 0, g + gi)),
                  pl.BlockSpec((1, d, s), lambda bi, gi, i: (bi, gi, 0)),
                  pl.BlockSpec((1, d, s), lambda bi, gi, i: (bi, g + gi, 0)),
                  pl.BlockSpec((s, LANES), lambda bi, gi, i: (0, 0)),
                  pl.BlockSpec((LANES, ncp), lambda bi, gi, i: (0, 0)),
                  pl.BlockSpec((1, tq, LANES), lambda bi, gi, i: (bi, i, gi))],
        out_specs=pl.BlockSpec((1, tq, q_rows), lambda bi, gi, i: (bi, i, gi)),
        out_shape=jax.ShapeDtypeStruct((b, s, NSA_WIDTH), BF16),
        scratch_shapes=[pltpu.VMEM((d, cols), F32), pltpu.VMEM((LANES, tq), F32)],
        compiler_params=_cparams(("parallel", "parallel", "arbitrary")),
        name="nsa_attention",
    )(q_t, kc, vc_t, k_n, k_n, v_t, v_t, e_t, overlap_t, gates3)


def _conv_kernel(cur_ref, prev_ref, w_ref, o_ref, *, tc):
    i = pl.program_id(1)
    cur = cur_ref[0].astype(F32)
    prev = jnp.where(i > 0, prev_ref[0].astype(F32), 0.0)
    pad = prev.shape[0] // 2
    full = jnp.concatenate([prev[pad:], cur], axis=0)
    y = None
    for kk in range(CONV_WIDTH):
        off = pad - (CONV_WIDTH - 1) + kk
        term = w_ref[kk:kk + 1, :] * full[off:off + tc]
        y = term if y is None else y + term
    o_ref[0] = (y * _sigmoid(y)).astype(BF16)


def conv_silu(p3, w, col0, width):
    b, s, _ = p3.shape
    tc = _tile(s, 512, BF16_SUBLANES)
    cw = 512
    assert col0 % cw == 0 and width % cw == 0
    c0 = col0 // cw
    halo = BF16_SUBLANES
    return pl.pallas_call(
        functools.partial(_conv_kernel, tc=tc),
        grid=(b, s // tc, width // cw),
        in_specs=[pl.BlockSpec((1, tc, cw), lambda bi, i, j: (bi, i, c0 + j)),
                  pl.BlockSpec((1, halo, cw), lambda bi, i, j: (bi, jnp.maximum(i * (tc // halo) - 1, 0), c0 + j)),
                  pl.BlockSpec((CONV_WIDTH, cw), lambda bi, i, j: (0, j))],
        out_specs=pl.BlockSpec((1, tc, cw), lambda bi, i, j: (bi, i, j)),
        out_shape=jax.ShapeDtypeStruct((b, s, width), BF16),
        compiler_params=_cparams(("parallel", "parallel", "parallel")),
        name="mlstm_conv",
    )(p3, p3, w)


def _log_sigmoid(x):
    return jnp.minimum(x, 0.0) - jnp.log1p(jnp.exp(-jnp.abs(x)))


def _mlstm_kernel(fb_ref, q_ref, k_ref, v_ref, ob_ref, gate_ref, nw_ref, o_ref, c_sc, n_sc, m_sc, *, cl):
    hd = pl.program_id(1)

    @pl.when(pl.program_id(2) == 0)
    def _():
        c_sc[...] = jnp.zeros(c_sc.shape, F32)
        n_sc[...] = jnp.zeros(n_sc.shape, F32)
        m_sc[...] = jnp.zeros(m_sc.shape, F32)

    q = q_ref[0]
    v = v_ref[0]
    ks32 = k_ref[0].astype(F32) * (MLSTM_DIM ** -0.5)
    ks = ks32.astype(BF16)
    slab = gate_ref[0]
    lane = lax.broadcasted_iota(I32, slab.shape, 1)
    i_col = jnp.sum(jnp.where(lane == GATE_I_LANE + hd, slab, 0.0), axis=-1, keepdims=True)
    f_col = jnp.sum(jnp.where(lane == GATE_F_LANE + hd, slab, 0.0), axis=-1, keepdims=True)
    lf_col = _log_sigmoid(f_col + fb_ref[hd])

    r_i = lax.broadcasted_iota(I32, (cl, cl), 0)
    c_i = lax.broadcasted_iota(I32, (cl, cl), 1)
    eye = r_i == c_i
    tri = c_i <= r_i
    lf_row = jnp.sum(jnp.where(eye, lf_col, 0.0), axis=0, keepdims=True)
    ig_row = jnp.sum(jnp.where(eye, i_col, 0.0), axis=0, keepdims=True)
    a_col = jnp.sum(jnp.where(tri, lf_row, 0.0), axis=1, keepdims=True)
    a_row = jnp.sum(jnp.where(r_i <= c_i, lf_col, 0.0), axis=0, keepdims=True)
    m_prev = m_sc[...]

    dlog = jnp.where(tri, a_col - a_row + ig_row, NEG)
    inter = a_col + m_prev
    mt = jnp.maximum(inter, jnp.max(dlog, axis=-1, keepdims=True))
    wm = jnp.exp(dlog - mt) * _dot_nt(q, ks)
    e_col = jnp.exp(inter - mt)
    num = e_col * jnp.dot(q, c_sc[...].astype(BF16), preferred_element_type=F32) \
        + jnp.dot(wm.astype(BF16), v, preferred_element_type=F32)
    qn = jnp.sum(q.astype(F32) * n_sc[...], axis=-1, keepdims=True)
    den = e_col * qn + jnp.sum(wm, axis=-1, keepdims=True)
    hh = num / jnp.maximum(jnp.abs(den), jnp.exp(-mt))

    a_last = jnp.sum(lf_row, axis=-1, keepdims=True)
    gs = a_last - a_col + i_col
    m_new = jnp.maximum(a_last + m_prev, jnp.max(gs, axis=0, keepdims=True))
    decay = jnp.exp(a_last + m_prev - m_new)
    wk = jnp.exp(gs - m_new) * ks32
    c_sc[...] = decay * c_sc[...] + _dot_tn(wk.astype(BF16), v)
    n_sc[...] = decay * n_sc[...] + jnp.sum(wk, axis=0, keepdims=True)
    m_sc[...] = m_new

    y = hh * lax.rsqrt(jnp.mean(hh * hh, axis=-1, keepdims=True) + RMS_EPS) * nw_ref[...]
    o_ref[0] = (y * _sigmoid(ob_ref[0].astype(F32))).astype(BF16)


def mlstm(qk, p3, gates3, f_bias, norm_w, v_col0, o_col0):
    b, s, _ = qk.shape
    cl = min(CHUNK, s)
    dh = MLSTM_DIM
    nh = MLSTM_HEADS
    assert v_col0 % dh == 0 and o_col0 % dh == 0 and s % cl == 0
    vb, ob = v_col0 // dh, o_col0 // dh
    return pl.pallas_call(
        functools.partial(_mlstm_kernel, cl=cl),
        grid=(b, nh, s // cl),
        in_specs=[pl.BlockSpec(memory_space=pltpu.SMEM),
                  pl.BlockSpec((1, cl, dh), lambda bi, h, c: (bi, c, h)),
                  pl.BlockSpec((1, cl, dh), lambda bi, h, c: (bi, c, nh + h)),
                  pl.BlockSpec((1, cl, dh), lambda bi, h, c: (bi, c, vb + h)),
                  pl.BlockSpec((1, cl, dh), lambda bi, h, c: (bi, c, ob + h)),
                  pl.BlockSpec((1, cl, LANES), lambda bi, h, c: (bi, c, 0)),
                  pl.BlockSpec((1, dh), lambda bi, h, c: (0, h))],
        out_specs=pl.BlockSpec((1, cl, dh), lambda bi, h, c: (bi, c, h)),
        out_shape=jax.ShapeDtypeStruct((b, s, nh * dh), BF16),
        scratch_shapes=[pltpu.VMEM((dh, dh), F32), pltpu.VMEM((1, dh), F32), pltpu.VMEM((1, 1), F32)],
        compiler_params=_cparams(("parallel", "parallel", "arbitrary")),
        name="mlstm",
    )(f_bias.astype(F32), qk, qk, p3, p3, gates3, norm_w.reshape(1, nh * dh).astype(F32))


def _ret_kernel(lg_ref, q_ref, k_ref, v_ref, g_ref, cos_ref, sin_ref, nw_ref, o_ref, r_sc, *, cl):
    hd = pl.program_id(1)

    @pl.when(pl.program_id(2) == 0)
    def _():
        r_sc[...] = jnp.zeros(r_sc.shape, F32)

    lg = lg_ref[hd]
    cos = cos_ref[0]
    sin = sin_ref[0]
    half = RET_QK_DIM // 2

    def rope(x):
        x1, x2 = x[:, :half], x[:, half:]
        return jnp.concatenate([x1 * cos - x2 * sin, x1 * sin + x2 * cos], axis=1)

    qr = rope(q_ref[0].astype(F32)).astype(BF16)
    kr = rope(k_ref[0].astype(F32)) * (RET_QK_DIM ** -0.5)
    v = v_ref[0]

    r_i = lax.broadcasted_iota(I32, (cl, cl), 0)
    c_i = lax.broadcasted_iota(I32, (cl, cl), 1)
    diff = (r_i - c_i).astype(F32)
    dm = jnp.where(diff >= 0, jnp.exp(jnp.maximum(diff, 0.0) * lg), 0.0)
    idx = lax.broadcasted_iota(I32, (cl, 1), 0).astype(F32)
    xi = jnp.exp((idx + 1.0) * lg)
    zeta = jnp.exp((cl - 1.0 - idx) * lg)
    chunk_decay = jnp.exp(jnp.full((1, 1), cl, F32) * lg)

    inner = jnp.dot((_dot_nt(qr, kr.astype(BF16)) * dm).astype(BF16), v, preferred_element_type=F32)
    cross = jnp.dot(qr, r_sc[...].astype(BF16), preferred_element_type=F32) * xi
    r_sc[...] = chunk_decay * r_sc[...] + _dot_tn((kr * zeta).astype(BF16), v)

    y = inner + cross
    y = y * lax.rsqrt(jnp.mean(y * y, axis=-1, keepdims=True) + RMS_EPS) * nw_ref[...]
    gg = g_ref[0].astype(F32)
    o_ref[0] = (y * (gg * _sigmoid(gg))).astype(BF16)


def retention(po3, cos, sin, norm_w):
    b, s, _ = po3.shape
    cl = min(CHUNK, s)
    nh, dk, dv = RET_HEADS, RET_QK_DIM, RET_V_DIM
    log_g = jnp.log1p(-jnp.exp2(-5.0 - jnp.arange(nh, dtype=F32)))
    kb = RET_QK_WIDTH // dk
    vb = 2 * RET_QK_WIDTH // dv
    gb = (2 * RET_QK_WIDTH + RET_V_WIDTH) // dv
    tab = pl.BlockSpec((1, cl, LANES), lambda bi, h, c: (bi, c, 0))
    return pl.pallas_call(
        functools.partial(_ret_kernel, cl=cl),
        grid=(b, nh, s // cl),
        in_specs=[pl.BlockSpec(memory_space=pltpu.SMEM),
                  pl.BlockSpec((1, cl, dk), lambda bi, h, c: (bi, c, h)),
                  pl.BlockSpec((1, cl, dk), lambda bi, h, c: (bi, c, kb + h)),
                  pl.BlockSpec((1, cl, dv), lambda bi, h, c: (bi, c, vb + h)),
                  pl.BlockSpec((1, cl, dv), lambda bi, h, c: (bi, c, gb + h)),
                  tab, tab,
                  pl.BlockSpec((1, dv), lambda bi, h, c: (0, h))],
        out_specs=pl.BlockSpec((1, cl, dv), lambda bi, h, c: (bi, c, h)),
        out_shape=jax.ShapeDtypeStruct((b, s, nh * dv), BF16),
        scratch_shapes=[pltpu.VMEM((dk, dv), F32)],
        compiler_params=_cparams(("parallel", "parallel", "arbitrary")),
        name="retention",
    )(log_g, po3, po3, po3, po3, cos, sin, norm_w.reshape(1, nh * dv).astype(F32))


def _router_kernel(x_ref, g_ref, wr_ref, xn_ref, route_ref, cnt_ref, carry_sc):
    @pl.when(pl.program_id(0) == 0)
    def _():
        carry_sc[...] = jnp.zeros(carry_sc.shape, F32)

    x = x_ref[...]
    t = x.shape[0]
    y = x * lax.rsqrt(jnp.mean(x * x, axis=-1, keepdims=True) + RMS_EPS) * g_ref[...]
    xn_ref[...] = y.astype(BF16)
    logits = jnp.dot(y, wr_ref[...], precision=HIGHEST, preferred_element_type=F32)
    lane = lax.broadcasted_iota(I32, logits.shape, 1)
    lg = jnp.where(lane < N_EXPERTS, logits, LOWEST)
    v1 = jnp.max(lg, axis=-1, keepdims=True)
    i1 = jnp.min(jnp.where(lg == v1, lane, LANES), axis=-1, keepdims=True)
    lg2 = jnp.where(lane == i1, LOWEST, lg)
    v2 = jnp.max(lg2, axis=-1, keepdims=True)
    i2 = jnp.min(jnp.where(lg2 == v2, lane, LANES), axis=-1, keepdims=True)
    e2 = jnp.exp(v2 - v1)
    g1 = 1.0 / (1.0 + e2)
    g2 = e2 / (1.0 + e2)

    chosen = jnp.where((lane == i1) | (lane == i2), 1.0, 0.0)
    r_i = lax.broadcasted_iota(I32, (t, t), 0)
    c_i = lax.broadcasted_iota(I32, (t, t), 1)
    tri = jnp.where(c_i <= r_i, 1.0, 0.0).astype(BF16)
    seen = jnp.dot(tri, chosen.astype(BF16), preferred_element_type=F32) + carry_sc[...]
    rank1 = jnp.sum(jnp.where(lane == i1, seen, 0.0), axis=-1, keepdims=True) - 1.0
    rank2 = jnp.sum(jnp.where(lane == i2, seen, 0.0), axis=-1, keepdims=True) - 1.0
    total = seen[t - 1:t, :]
    carry_sc[...] = total
    cnt_ref[...] = jnp.broadcast_to(total, cnt_ref.shape)

    route = jnp.where(lane == 0, i1.astype(F32), 0.0)
    route = jnp.where(lane == 1, i2.astype(F32), route)
    route = jnp.where(lane == 2, g1, route)
    route = jnp.where(lane == 3, g2, route)
    route = jnp.where(lane == 4, rank1, route)
    route = jnp.where(lane == 5, rank2, route)
    route_ref[...] = route


def moe_route(x, g, w_router, tm=256):
    m, d = x.shape
    tm = _tile(m, tm, 16)
    wr = jnp.zeros((d, LANES), F32).at[:, :N_EXPERTS].set(w_router.astype(F32))
    return pl.pallas_call(
        _router_kernel,
        grid=(m // tm,),
        in_specs=[pl.BlockSpec((tm, d), lambda i: (i, 0)),
                  pl.BlockSpec((1, d), lambda i: (0, 0)),
                  pl.BlockSpec((d, LANES), lambda i: (0, 0))],
        out_specs=[pl.BlockSpec((tm, d), lambda i: (i, 0)),
                   pl.BlockSpec((tm, LANES), lambda i: (i, 0)),
                   pl.BlockSpec((8, LANES), lambda i: (0, 0))],
        out_shape=[jax.ShapeDtypeStruct((m, d), BF16),
                   jax.ShapeDtypeStruct((m, LANES), F32),
                   jax.ShapeDtypeStruct((8, LANES), F32)],
        scratch_shapes=[pltpu.VMEM((1, LANES), F32)],
        compiler_params=_cparams(("arbitrary",)),
        name="moe_route",
    )(x, g.reshape(1, d).astype(F32), wr)


def _dispatch_kernel(idx_ref, src_ref, o_ref, sem, *, rows):
    base = pl.program_id(0) * rows

    def issue(r, carry):
        pltpu.make_async_copy(src_ref.at[idx_ref[base + r]], o_ref.at[r], sem).start()
        return carry

    lax.fori_loop(0, rows, issue, 0)
    pltpu.make_async_copy(src_ref.at[pl.ds(0, rows)], o_ref, sem).wait()


def moe_dispatch(xn, row_tok, rows=2048):
    m, d = xn.shape
    n_rows = row_tok.shape[0]
    rows = _tile(n_rows, rows, 8)
    src = xn.reshape(m, d // LANES, LANES)
    out = pl.pallas_call(
        functools.partial(_dispatch_kernel, rows=rows),
        grid_spec=pltpu.PrefetchScalarGridSpec(
            num_scalar_prefetch=1,
            grid=(n_rows // rows,),
            in_specs=[pl.BlockSpec(memory_space=pl.ANY)],
            out_specs=pl.BlockSpec((rows, d // LANES, LANES), lambda i, idx: (i, 0, 0)),
            scratch_shapes=[pltpu.SemaphoreType.DMA(())],
        ),
        out_shape=jax.ShapeDtypeStruct((n_rows, d // LANES, LANES), xn.dtype),
        compiler_params=_cparams(("arbitrary",)),
        name="moe_dispatch",
    )(row_tok, src)
    return out.reshape(n_rows, d)


def _combine_kernel(d1_ref, d2_ref, x_ref, route_ref, yb_ref, o_ref, buf1, buf2, sem, *, rows):
    base = pl.program_id(0) * rows

    def issue(r8, carry):
        r0 = pl.multiple_of(r8 * 8, 8)
        for u in range(8):
            pltpu.make_async_copy(yb_ref.at[pl.ds(d1_ref[base + r0 + u], 1)], buf1.at[pl.ds(r0 + u, 1)],
                                  sem.at[0]).start()
            pltpu.make_async_copy(yb_ref.at[pl.ds(d2_ref[base + r0 + u], 1)], buf2.at[pl.ds(r0 + u, 1)],
                                  sem.at[1]).start()
        return carry

    lax.fori_loop(0, rows // 8, issue, 0)
    pltpu.make_async_copy(yb_ref.at[pl.ds(0, rows)], buf1, sem.at[0]).wait()
    pltpu.make_async_copy(yb_ref.at[pl.ds(0, rows)], buf2, sem.at[1]).wait()
    route = route_ref[...]
    o_ref[...] = x_ref[...] + (buf1[...] * route[:, 2:3] + buf2[...] * route[:, 3:4])


def moe_combine(x, route, yb, dest1, dest2, rows=256):
    m, d = x.shape
    rows = _tile(m, rows, 8)
    return pl.pallas_call(
        functools.partial(_combine_kernel, rows=rows),
        grid_spec=pltpu.PrefetchScalarGridSpec(
            num_scalar_prefetch=2,
            grid=(m // rows,),
            in_specs=[pl.BlockSpec((rows, d), lambda i, a, b: (i, 0)),
                      pl.BlockSpec((rows, LANES), lambda i, a, b: (i, 0)),
                      pl.BlockSpec(memory_space=pl.ANY)],
            out_specs=pl.BlockSpec((rows, d), lambda i, a, b: (i, 0)),
            scratch_shapes=[pltpu.VMEM((rows, d), F32), pltpu.VMEM((rows, d), F32),
                            pltpu.SemaphoreType.DMA((2,))],
        ),
        out_shape=jax.ShapeDtypeStruct((m, d), F32),
        compiler_params=_cparams(("arbitrary",)),
        name="moe_combine",
    )(dest1, dest2, x, route, yb)


def moe_layer(x, norm_g, w_router, w_gate, w_up, w_down):
    m, d = x.shape
    tm = MOE_TM
    xn, route, cnt = moe_route(x, norm_g, w_router)
    expert = route[:, 0:2].astype(I32)
    rank = route[:, 4:6].astype(I32)
    counts = cnt[0, :N_EXPERTS].astype(I32)
    padded = (counts + tm - 1) // tm * tm
    pend = jnp.cumsum(padded)
    pstart = pend - padded
    dest = pstart[expert] + rank
    n_blk = -(-(2 * m) // tm) + N_EXPERTS
    n_rows = n_blk * tm
    tok = jnp.broadcast_to(jnp.arange(m, dtype=I32)[:, None], (m, 2))
    row_tok = jnp.zeros((n_rows,), I32).at[dest.reshape(-1)].set(tok.reshape(-1))
    nused = (pend[-1] // tm).astype(I32)
    blk = jnp.minimum(jnp.arange(n_blk, dtype=I32), nused - 1)
    blk_e = jnp.minimum(jnp.sum(pend[None, :] <= (blk * tm)[:, None], axis=1), N_EXPERTS - 1).astype(I32)
    nu = nused.reshape(1)

    xb = moe_dispatch(xn, row_tok)
    tf = _tile(w_gate.shape[2], 512, LANES)
    hidden = gmm_swiglu(xb, w_gate, w_up, blk_e, nu, tm=tm, tn=tf)
    yb = gmm(hidden, w_down, blk_e, nu, tm=tm, tn=_tile(d, 512, LANES), out_dtype=F32)
    return moe_combine(x, route, yb, dest[:, 0], dest[:, 1])


def even_layer(x2, b, s, pos, w_norm, w_in, q_gain, k_gain, w_cmp_k, w_cmp_v, pe_cmp, conv_w, f_bias, m_norm,
               w_out, w_norm_ffn, w_gate, w_up, w_down):
    n, dm = x2.shape
    g, d = NSA_GROUPS, HEAD_DIM
    o_gate = NSA_SLAB
    o_qb = o_gate + NSA_HEADS * 3
    o_if = o_qb + 3 * MLSTM_WIDTH
    o_ob = o_if + 2 * MLSTM_HEADS
    w_main = jnp.concatenate([w_in[:, :o_gate], w_in[:, o_qb:o_if], w_in[:, o_ob:o_ob + MLSTM_WIDTH]], axis=1)
    per_g = NSA_HPG * 3
    zeros = functools.partial(jnp.zeros, dtype=w_in.dtype)
    w_small = jnp.concatenate(
        [w_in[:, o_gate:o_gate + per_g], zeros((dm, GATE_I_LANE - per_g)),
         w_in[:, o_if:o_ob], zeros((dm, LANES - GATE_I_LANE - 2 * MLSTM_HEADS)),
         w_in[:, o_gate + per_g:o_qb], zeros((dm, LANES - per_g))], axis=1)

    h = rmsnorm_rows(x2, w_norm)
    p3 = mm(h, w_main, out_dtype=BF16).reshape(b, s, -1)
    gates3 = mm(h, w_small, out_dtype=F32, tn=2 * LANES).reshape(b, s, 2 * LANES)

    cos_n, sin_n = trig_tables(pos, _nsa_inv_lane())
    q_t, k_n, v_t = nsa_prep(p3, cos_n.reshape(b, s, LANES), sin_n.reshape(b, s, LANES), q_gain, k_gain)
    nh = s // CMP_STRIDE
    cmp_pos = jnp.concatenate([pos[:, CMP_BLOCK - 1::CMP_STRIDE][:, :nh - 1], pos[:, -1:]], axis=1)
    cos_c, sin_c = trig_tables(cmp_pos, _nsa_inv_lane())

    def cmp_blocks(col0):
        tok = p3[:, :, col0:col0 + NSA_KV_WIDTH].reshape(b, nh, CMP_STRIDE, g, d)
        halves = tok.transpose(0, 3, 1, 2, 4).reshape(b, g, nh, CMP_STRIDE * d)
        nxt = jnp.concatenate([halves[:, :, 1:], jnp.zeros_like(halves[:, :, :1])], axis=2)
        return jnp.concatenate([halves, nxt], axis=-1).reshape(b * g * nh, CMP_BLOCK * d)

    tabs = (cos_c.reshape(b, nh, LANES), sin_c.reshape(b, nh, LANES))
    kc = compress(cmp_blocks(NSA_WIDTH), pe_cmp, w_cmp_k, k_gain, *tabs, is_key=True, rows_per_seq=nh, groups=g)
    vc = compress(cmp_blocks(NSA_WIDTH + NSA_KV_WIDTH), pe_cmp, w_cmp_v, k_gain, *tabs, is_key=False,
                  rows_per_seq=nh, groups=g)
    o_nsa = nsa_attention(q_t, k_n, v_t, kc, vc, gates3)

    qk = conv_silu(p3, conv_w, NSA_SLAB, 2 * MLSTM_WIDTH)
    h_b = mlstm(qk, p3, gates3, f_bias, m_norm, NSA_SLAB + 2 * MLSTM_WIDTH, NSA_SLAB + 3 * MLSTM_WIDTH)

    mixed = jnp.concatenate([o_nsa, h_b], axis=-1).reshape(n, NSA_WIDTH + MLSTM_WIDTH)
    x2 = mm(mixed, w_out, out_dtype=F32, residual=x2)
    hf = rmsnorm_rows(x2, w_norm_ffn)
    hidden = mm_swiglu(hf, w_gate, w_up, tm=1024)
    return mm(hidden, w_down, out_dtype=F32, residual=x2, tm=1024)


def odd_layer(x2, b, s, pos, w_norm, w_in, r_norm, w_out, w_norm_ffn, w_router, e_gate, e_up, e_down):
    n, dm = x2.shape
    h = rmsnorm_rows(x2, w_norm)
    po3 = mm(h, w_in, out_dtype=BF16, tn=1024).reshape(b, s, -1)
    cos_r, sin_r = trig_tables(pos, _ret_inv_lane())
    y = retention(po3, cos_r.reshape(b, s, LANES), sin_r.reshape(b, s, LANES), r_norm)
    x2 = mm(y.reshape(n, RET_V_WIDTH), w_out, out_dtype=F32, residual=x2, tm=1024)
    return moe_layer(x2, w_norm_ffn, w_router, e_gate, e_up, e_down)


def kernel(x, positions, norm_mix_even, w_in_even, nsa_q_gain, nsa_k_gain, w_cmp_k, w_cmp_v, pe_cmp, mlstm_conv, mlstm_f_bias, mlstm_norm, w_out_even, norm_ffn_even, ffn_gate, ffn_up, ffn_down, norm_mix_odd, w_in_odd, ret_norm, w_out_odd, norm_ffn_odd, w_router, exp_gate, exp_up, exp_down):
    b, s, dm = x.shape
    depth = norm_mix_even.shape[0] + norm_mix_odd.shape[0]
    x2 = x.reshape(b * s, dm)
    for layer in range(depth):
        j = layer // 2
        if layer % 2 == 0:
            x2 = even_layer(x2, b, s, positions, norm_mix_even[j], w_in_even[j], nsa_q_gain[j], nsa_k_gain[j],
                            w_cmp_k[j], w_cmp_v[j], pe_cmp[j], mlstm_conv[j], mlstm_f_bias[j], mlstm_norm[j],
                            w_out_even[j], norm_ffn_even[j], ffn_gate[j], ffn_up[j], ffn_down[j])
        else:
            x2 = odd_layer(x2, b, s, positions, norm_mix_odd[j], w_in_odd[j], ret_norm[j], w_out_odd[j],
                           norm_ffn_odd[j], w_router[j], exp_gate[j], exp_up[j], exp_down[j])
    return x2.reshape(b, s, dm)
```

```python
import functools

import numpy as np
import jax
import jax.numpy as jnp
from jax import lax
from jax.experimental import pallas as pl
from jax.experimental.pallas import tpu as pltpu

F32 = jnp.float32
BF16 = jnp.bfloat16
I32 = jnp.int32
HIGHEST = lax.Precision.HIGHEST

HEAD_DIM = 128
NSA_HEADS = 8
NSA_GROUPS = 2
NSA_HPG = NSA_HEADS // NSA_GROUPS
NSA_WIDTH = NSA_HEADS * HEAD_DIM
NSA_KV_WIDTH = NSA_GROUPS * HEAD_DIM
NSA_SLAB = NSA_WIDTH + 6 * NSA_KV_WIDTH
CMP_BLOCK = 32
CMP_STRIDE = 16
SEL_BLOCK = 64
SEL_TOPN = 16
WINDOW = 512
ROPE_DIM = HEAD_DIM // 4
ROPE_THETA = 500000.0
SEL_FORCE = 1.0e6
NEG = -1.0e30
LOWEST = -3.0e38
MLSTM_HEADS = 4
MLSTM_DIM = 256
MLSTM_WIDTH = MLSTM_HEADS * MLSTM_DIM
CONV_WIDTH = 4
RET_HEADS = 8
RET_QK_DIM = 256
RET_V_DIM = 512
RET_QK_WIDTH = RET_HEADS * RET_QK_DIM
RET_V_WIDTH = RET_HEADS * RET_V_DIM
RET_ROPE_THETA = 10000.0
N_EXPERTS = 8
RMS_EPS = 1e-6

LANES = 128
BF16_SUBLANES = 16
V7X_VMEM_BYTES = 64 * 1024 * 1024
VMEM_LIMIT = V7X_VMEM_BYTES - 8 * 1024 * 1024

CHUNK = 256
NSA_TQ = 128
LOG2_E = 1.4426950408889634
V_ROWS = HEAD_DIM + BF16_SUBLANES
NSA_KT = 256
SOFTMAX_ROWS = 64
MOE_TM = 512
GATE_I_LANE = 16
GATE_F_LANE = 20


def _cparams(sem, vmem=VMEM_LIMIT):
    return pltpu.CompilerParams(dimension_semantics=sem, vmem_limit_bytes=vmem)


def _tile(n, target, quantum):
    if n <= target:
        return n
    t = (target // quantum) * quantum
    while t > quantum and n % t:
        t -= quantum
    assert n % t == 0, (n, target, quantum)
    return t


def _sigmoid(x):
    return 1.0 / (1.0 + jnp.exp(-x))


def _dot_nt(a, b):
    return lax.dot_general(a, b, (((1,), (1,)), ((), ())), preferred_element_type=F32)


def _dot_tn(a, b):
    return lax.dot_general(a, b, (((0,), (0,)), ((), ())), preferred_element_type=F32)


def _rmsnorm_kernel(x_ref, g_ref, o_ref):
    x = x_ref[...]
    y = x * lax.rsqrt(jnp.mean(x * x, axis=-1, keepdims=True) + RMS_EPS)
    o_ref[...] = (y * g_ref[...]).astype(o_ref.dtype)


def rmsnorm_rows(x, g, tm=512):
    m, d = x.shape
    tm = _tile(m, tm, 8)
    return pl.pallas_call(
        _rmsnorm_kernel,
        grid=(m // tm,),
        in_specs=[pl.BlockSpec((tm, d), lambda i: (i, 0)), pl.BlockSpec((1, d), lambda i: (0, 0))],
        out_specs=pl.BlockSpec((tm, d), lambda i: (i, 0)),
        out_shape=jax.ShapeDtypeStruct((m, d), BF16),
        compiler_params=_cparams(("parallel",)),
        name="rmsnorm",
    )(x, g.reshape(1, d).astype(F32))


def _gmm_kernel(be_ref, nu_ref, x_ref, w_ref, *rest, has_res):
    if has_res:
        r_ref, o_ref, wb_ref = rest
    else:
        o_ref, wb_ref = rest
    i = pl.program_id(1)
    changed = be_ref[i] != be_ref[jnp.maximum(i - 1, 0)]

    @pl.when((i == 0) | changed)
    def _():
        wb_ref[...] = w_ref[0].astype(BF16)

    @pl.when(i < nu_ref[0])
    def _():
        acc = jnp.dot(x_ref[...], wb_ref[...], preferred_element_type=F32)
        if has_res:
            acc = r_ref[...] + acc
        o_ref[...] = acc.astype(o_ref.dtype)

    @pl.when(i >= nu_ref[0])
    def _():
        o_ref[...] = jnp.zeros_like(o_ref)


def _weight_spec(k, tn, single_buffer):
    mode = dict(pipeline_mode=pl.Buffered(1)) if single_buffer else {}
    return pl.BlockSpec((1, k, tn), lambda j, i, be, nu: (be[i], 0, j), **mode)


def gmm(x, w, blk_e, nused, *, tm, tn, out_dtype, residual=None, single_buffer_w=False):
    m, k = x.shape
    e, k2, n = w.shape
    assert k == k2 and m % tm == 0 and n % tn == 0
    nb = m // tm
    in_specs = [
        pl.BlockSpec((tm, k), lambda j, i, be, nu: (jnp.minimum(i, nu[0] - 1), 0)),
        _weight_spec(k, tn, single_buffer_w),
    ]
    args = [x, w]
    if residual is not None:
        in_specs.append(pl.BlockSpec((tm, tn), lambda j, i, be, nu: (i, j)))
        args.append(residual)
    return pl.pallas_call(
        functools.partial(_gmm_kernel, has_res=residual is not None),
        grid_spec=pltpu.PrefetchScalarGridSpec(
            num_scalar_prefetch=2,
            grid=(n // tn, nb),
            in_specs=in_specs,
            out_specs=pl.BlockSpec((tm, tn), lambda j, i, be, nu: (i, j)),
            scratch_shapes=[pltpu.VMEM((k, tn), BF16)],
        ),
        out_shape=jax.ShapeDtypeStruct((m, n), out_dtype),
        compiler_params=_cparams(("arbitrary", "arbitrary")),
        name="gmm",
    )(blk_e, nused, *args)


def _gmm_swiglu_kernel(be_ref, nu_ref, x_ref, wg_ref, wu_ref, o_ref, wgb_ref, wub_ref):
    i = pl.program_id(1)
    changed = be_ref[i] != be_ref[jnp.maximum(i - 1, 0)]

    @pl.when((i == 0) | changed)
    def _():
        wgb_ref[...] = wg_ref[0].astype(BF16)
        wub_ref[...] = wu_ref[0].astype(BF16)

    @pl.when(i < nu_ref[0])
    def _():
        x = x_ref[...]
        g = jnp.dot(x, wgb_ref[...], preferred_element_type=F32)
        u = jnp.dot(x, wub_ref[...], preferred_element_type=F32)
        o_ref[...] = (g * _sigmoid(g) * u).astype(o_ref.dtype)

    @pl.when(i >= nu_ref[0])
    def _():
        o_ref[...] = jnp.zeros_like(o_ref)


def gmm_swiglu(x, wg, wu, blk_e, nused, *, tm, tn, single_buffer_w=False):
    m, k = x.shape
    e, k2, n = wg.shape
    assert k == k2 and wu.shape == wg.shape and m % tm == 0 and n % tn == 0
    nb = m // tm
    w_spec = _weight_spec(k, tn, single_buffer_w)
    return pl.pallas_call(
        _gmm_swiglu_kernel,
        grid_spec=pltpu.PrefetchScalarGridSpec(
            num_scalar_prefetch=2,
            grid=(n // tn, nb),
            in_specs=[pl.BlockSpec((tm, k), lambda j, i, be, nu: (jnp.minimum(i, nu[0] - 1), 0)),
                      w_spec, w_spec],
            out_specs=pl.BlockSpec((tm, tn), lambda j, i, be, nu: (i, j)),
            scratch_shapes=[pltpu.VMEM((k, tn), BF16), pltpu.VMEM((k, tn), BF16)],
        ),
        out_shape=jax.ShapeDtypeStruct((m, n), BF16),
        compiler_params=_cparams(("arbitrary", "arbitrary")),
        name="gmm_swiglu",
    )(blk_e, nused, x, wg, wu)


def _dense_blocks(m, tm):
    nb = m // tm
    return jnp.zeros((nb,), I32), jnp.full((1,), nb, I32)


def mm(x, w, *, out_dtype, residual=None, tm=2048, tn=512):
    m, k = x.shape
    n = w.shape[1]
    tm = _tile(m, tm, 16)
    tn = _tile(n, tn, LANES)
    be, nu = _dense_blocks(m, tm)
    return gmm(x, w[None], be, nu, tm=tm, tn=tn, out_dtype=out_dtype, residual=residual, single_buffer_w=True)


def mm_swiglu(x, wg, wu, *, tm=2048, tn=512):
    m = x.shape[0]
    tm = _tile(m, tm, 16)
    tn = _tile(wg.shape[1], tn, LANES)
    be, nu = _dense_blocks(m, tm)
    return gmm_swiglu(x, wg[None], wu[None], be, nu, tm=tm, tn=tn, single_buffer_w=True)


def _trig_kernel(pos_ref, inv_ref, cos_ref, sin_ref):
    ang = pos_ref[...] * inv_ref[...]
    cos_ref[...] = jnp.cos(ang)
    sin_ref[...] = jnp.sin(ang)


def trig_tables(pos, inv_lane):
    r = pos.size
    pos_b = jnp.broadcast_to(pos.astype(F32).reshape(r, 1), (r, LANES))
    tr = _tile(r, 512, 8)
    spec = pl.BlockSpec((tr, LANES), lambda i: (i, 0))
    return pl.pallas_call(
        _trig_kernel,
        grid=(r // tr,),
        in_specs=[spec, pl.BlockSpec((1, LANES), lambda i: (0, 0))],
        out_specs=[spec, spec],
        out_shape=[jax.ShapeDtypeStruct((r, LANES), F32)] * 2,
        compiler_params=_cparams(("parallel",)),
        name="trig_tables",
    )(pos_b, inv_lane.reshape(1, LANES))


def _nsa_inv_lane():
    half = ROPE_DIM // 2
    inv = jnp.power(jnp.float32(ROPE_THETA), -jnp.arange(half, dtype=F32) * (2.0 / ROPE_DIM))
    return jnp.concatenate([inv, inv, jnp.zeros((LANES - ROPE_DIM,), F32)])


def _ret_inv_lane():
    half = RET_QK_DIM // 2
    return jnp.power(jnp.float32(RET_ROPE_THETA), -jnp.arange(half, dtype=F32) * (2.0 / RET_QK_DIM))


def _norm_rope_head(x, gain, cos, sin):
    half = ROPE_DIM // 2
    y = x * lax.rsqrt(jnp.mean(x * x, axis=-1, keepdims=True) + RMS_EPS) * gain
    lane = lax.broadcasted_iota(I32, y.shape, 1)
    from_hi = jnp.where(lane < half, -sin, 0.0)
    from_lo = jnp.where((lane >= half) & (lane < ROPE_DIM), sin, 0.0)
    return (y * cos + pltpu.roll(y, LANES - half, 1) * from_hi + pltpu.roll(y, half, 1) * from_lo)


def _nsa_prep_kernel(p_ref, cos_ref, sin_ref, qg_ref, kg_ref, qt_ref, kn_ref, vt_ref):
    cos = cos_ref[0]
    sin = sin_ref[0]
    scale = HEAD_DIM ** -0.5 * LOG2_E
    d = HEAD_DIM
    g = NSA_GROUPS
    for hd in range(NSA_HEADS):
        q = _norm_rope_head(p_ref[0, :, hd * d:(hd + 1) * d].astype(F32), qg_ref[...], cos, sin)
        qt_ref[0, hd * d:(hd + 1) * d, :] = (q * scale).T.astype(BF16)
    for n, slab in enumerate((2, 4)):
        for gi in range(g):
            off = NSA_WIDTH + slab * NSA_KV_WIDTH + gi * d
            k = _norm_rope_head(p_ref[0, :, off:off + d].astype(F32), kg_ref[...], cos, sin)
            kn_ref[0, :, (n * g + gi) * d:(n * g + gi + 1) * d] = k.astype(BF16)
    for n, slab in enumerate((3, 5)):
        for gi in range(g):
            off = NSA_WIDTH + slab * NSA_KV_WIDTH + gi * d
            v = p_ref[0, :, off:off + d].astype(F32)
            r0 = (n * g + gi) * V_ROWS
            vt_ref[0, r0:r0 + d, :] = v.T.astype(BF16)
            vt_ref[0, r0 + d:r0 + V_ROWS, :] = jnp.ones((V_ROWS - d, v.shape[0]), BF16)


def nsa_prep(p3, cos, sin, q_gain, k_gain):
    b, s, _ = p3.shape
    t = _tile(s, 256, LANES)
    d = HEAD_DIM
    tab = pl.BlockSpec((1, t, LANES), lambda bi, i: (bi, i, 0))
    gain = pl.BlockSpec((1, d), lambda bi, i: (0, 0))
    kv = 2 * NSA_KV_WIDTH
    vr = 2 * NSA_GROUPS * V_ROWS
    return pl.pallas_call(
        _nsa_prep_kernel,
        grid=(b, s // t),
        in_specs=[pl.BlockSpec((1, t, NSA_SLAB), lambda bi, i: (bi, i, 0)), tab, tab, gain, gain],
        out_specs=[pl.BlockSpec((1, NSA_WIDTH, t), lambda bi, i: (bi, 0, i)),
                   pl.BlockSpec((1, t, kv), lambda bi, i: (bi, i, 0)),
                   pl.BlockSpec((1, vr, t), lambda bi, i: (bi, 0, i))],
        out_shape=[jax.ShapeDtypeStruct((b, NSA_WIDTH, s), BF16),
                   jax.ShapeDtypeStruct((b, s, kv), BF16),
                   jax.ShapeDtypeStruct((b, vr, s), BF16)],
        compiler_params=_cparams(("parallel", "parallel")),
        name="nsa_prep",
    )(p3, cos, sin, q_gain.reshape(1, d), k_gain.reshape(1, d))


def _compress_kernel(blk_ref, pe_ref, w_ref, kg_ref, cos_ref, sin_ref, o_ref, *, is_key):
    a = (blk_ref[...].astype(F32) + pe_ref[...]).astype(BF16)
    y = jnp.dot(a, w_ref[...].astype(BF16), preferred_element_type=F32)
    if is_key:
        o_ref[0] = _norm_rope_head(y, kg_ref[...], cos_ref[0], sin_ref[0]).astype(BF16)
    else:
        o_ref[0] = y.T.astype(BF16)


def compress(blk, pe, w, k_gain, cos_c, sin_c, *, is_key, rows_per_seq, groups):
    r, kdim = blk.shape
    t = rows_per_seq
    d = HEAD_DIM
    tab = pl.BlockSpec((1, t, LANES), lambda i: (i // groups, 0, 0))
    out_blk = (1, t, d) if is_key else (1, d, t)
    return pl.pallas_call(
        functools.partial(_compress_kernel, is_key=is_key),
        grid=(r // t,),
        in_specs=[pl.BlockSpec((t, kdim), lambda i: (i, 0)),
                  pl.BlockSpec((1, kdim), lambda i: (0, 0)),
                  pl.BlockSpec((kdim, d), lambda i: (0, 0)),
                  pl.BlockSpec((1, d), lambda i: (0, 0)),
                  tab, tab],
        out_specs=pl.BlockSpec(out_blk, lambda i: (i, 0, 0)),
        out_shape=jax.ShapeDtypeStruct((r // t,) + out_blk[1:], BF16),
        compiler_params=_cparams(("parallel",)),
        name="nsa_compress",
    )(blk, pe.reshape(1, kdim), w, k_gain.reshape(1, d), cos_c, sin_c)


def _nsa_attn_kernel(qt_ref, kc_ref, vct_ref, ks_ref, kw_ref, vst_ref, vwt_ref, et_ref, ovt_ref, gate_ref,
                     o_ref, acc_sc, val_sc, sa_sc, sb_sc, pa_sc, pb_sc, *, tq, kt, wk, ns, n_top):
    i = pl.program_id(2)
    t0 = i * tq
    cols = NSA_HPG * tq
    d = HEAD_DIM
    q_t = jnp.concatenate([qt_ref[0, p * d:(p + 1) * d, :] for p in range(NSA_HPG)], axis=1)
    t_lane = t0 + (lax.broadcasted_iota(I32, (1, cols), 1) & (tq - 1))

    ncp = kc_ref.shape[1]
    s_c = jnp.dot(kc_ref[0], q_t, preferred_element_type=F32)
    c_end = lax.broadcasted_iota(I32, (ncp, cols), 0) * CMP_STRIDE + (CMP_BLOCK - 1)
    cmask = c_end <= t_lane
    s_c = jnp.where(cmask, s_c, NEG)
    e_c = jnp.where(cmask, jnp.exp2(s_c - jnp.max(s_c, axis=0, keepdims=True)), 0.0)
    den_c = jnp.sum(e_c, axis=0, keepdims=True)
    p_c = e_c * (1.0 / jnp.maximum(den_c, 1e-30))
    o_cmp = jnp.dot(vct_ref[0], p_c.astype(BF16), preferred_element_type=F32)

    p_grp = p_c[:, 0:tq]
    for p in range(1, NSA_HPG):
        p_grp = p_grp + p_c[:, p * tq:(p + 1) * tq]
    imp = jnp.dot(ovt_ref[...], p_grp, precision=HIGHEST, preferred_element_type=F32)
    jb = lax.broadcasted_iota(I32, (LANES, tq), 0)
    cur = (t0 + lax.broadcasted_iota(I32, (1, tq), 1)) // SEL_BLOCK
    forced = (jb == 0) | (jb == cur) | (jb == cur - 1)
    val = jnp.where(jb <= cur, jnp.where(forced, SEL_FORCE, imp), NEG)
    val = jnp.where(jb < ns, val, LOWEST)
    val_sc[...] = val
    beaten = jnp.zeros((LANES, tq), F32)
    for j2 in range(ns):
        r = val_sc[j2:j2 + 1, :]
        ge = jnp.where(r >= val, 1.0, 0.0)
        gt = jnp.where(r > val, 1.0, 0.0)
        beaten = beaten + jnp.where(jb > j2, ge, gt)
    past = jb < t0 // SEL_BLOCK
    bias = jnp.where(past & (beaten < n_top), 0.0, jnp.where(jb < ns, NEG, 0.0)).astype(BF16)
    q_aug = jnp.concatenate([q_t, jnp.concatenate([bias] * NSA_HPG, axis=1)], axis=0)

    d0 = pl.multiple_of(t0, tq)
    s_d = jnp.dot(ks_ref[0, pl.ds(d0, tq), :], q_t, preferred_element_type=F32)
    s_d = jnp.where(d0 + lax.broadcasted_iota(I32, (tq, cols), 0) <= t_lane, s_d, NEG)
    m_d = jnp.max(s_d, axis=0, keepdims=True)
    p_d = jnp.exp2(s_d - m_d)
    acc_sc[...] = jnp.dot(vst_ref[0, :, pl.ds(d0, tq)], p_d.astype(BF16), preferred_element_type=F32)

    n_pairs = (t0 + 2 * kt - 1) // (2 * kt)
    last_a = jnp.maximum(n_pairs - 1, 0) * (2 * kt)

    def scores(k0):
        k0 = pl.multiple_of(k0, kt)
        k_aug = jnp.concatenate([ks_ref[0, pl.ds(k0, kt), :], et_ref[pl.ds(k0, kt), :]], axis=1)
        return jnp.dot(k_aug, q_aug, preferred_element_type=F32)

    def values(p_ref, k0):
        k0 = pl.multiple_of(k0, kt)
        return jnp.dot(vst_ref[0, :, pl.ds(k0, kt)], p_ref[...], preferred_element_type=F32)

    def softmax_update(s_ref, p_ref, m_prev):
        m_new = jnp.maximum(m_prev, jnp.max(s_ref[...], axis=0, keepdims=True))
        for r0 in range(0, kt, SOFTMAX_ROWS):
            p_ref[r0:r0 + SOFTMAX_ROWS, :] = jnp.exp2(s_ref[r0:r0 + SOFTMAX_ROWS, :] - m_new).astype(BF16)
        return m_new, jnp.exp2(m_prev - m_new)

    def sel_step(j, carry):
        m_run, alpha_b = carry
        k0 = j * (2 * kt)
        sb_sc[...] = scores(k0 + kt)
        acc_sc[...] = alpha_b * acc_sc[...] + values(pb_sc, jnp.maximum(k0 - kt, 0))
        m_run, alpha_a = softmax_update(sa_sc, pa_sc, m_run)
        sa_sc[...] = scores(jnp.minimum(k0 + 2 * kt, last_a))
        acc_sc[...] = alpha_a * acc_sc[...] + values(pa_sc, k0)
        return softmax_update(sb_sc, pb_sc, m_run)

    sa_sc[...] = scores(0)
    pb_sc[...] = jnp.zeros(pb_sc.shape, BF16)
    _, alpha_last = lax.fori_loop(0, n_pairs, sel_step, (m_d, jnp.ones((1, cols), F32)))
    acc = alpha_last * acc_sc[...] + values(pb_sc, last_a + kt)
    o_sel = acc[:d] * (1.0 / acc[d:d + 1])

    w0 = pl.multiple_of(jnp.maximum(t0 + tq - wk, 0), LANES)
    s_w = jnp.dot(kw_ref[0, pl.ds(w0, wk), :], q_t, preferred_element_type=F32)
    kpos = w0 + lax.broadcasted_iota(I32, (wk, cols), 0)
    wmask = (kpos <= t_lane) & (kpos > t_lane - WINDOW)
    s_w = jnp.where(wmask, s_w, NEG)
    e_w = jnp.exp2(s_w - jnp.max(s_w, axis=0, keepdims=True))
    o_win = jnp.dot(vwt_ref[0, :, pl.ds(w0, wk)], e_w.astype(BF16), preferred_element_type=F32)
    o_win = o_win[:d] * (1.0 / o_win[d:d + 1])

    g_t = _sigmoid(gate_ref[0]).T
    for p in range(NSA_HPG):
        sl = slice(p * tq, (p + 1) * tq)
        o_t = (g_t[3 * p:3 * p + 1] * o_cmp[:, sl] + g_t[3 * p + 1:3 * p + 2] * o_sel[:, sl]
               + g_t[3 * p + 2:3 * p + 3] * o_win[:, sl])
        o_ref[0, :, p * d:(p + 1) * d] = o_t.T.astype(BF16)


def nsa_attention(q_t, k_n, v_t, kc, vc_t, gates3):
    b, _, s = q_t.shape
    g, d = NSA_GROUPS, HEAD_DIM
    ncp = kc.shape[1]
    tq = NSA_TQ
    kt = min(NSA_KT, s)
    wk = WINDOW + tq
    ns = s // SEL_BLOCK
    assert s % (2 * kt) == 0 and kt % SOFTMAX_ROWS == 0 and s >= wk and ns <= LANES and tq == LANES
    n_top = min(SEL_TOPN, ns)
    cols = NSA_HPG * tq

    key_blk = np.arange(s) // SEL_BLOCK
    e_t = jnp.asarray(key_blk[:, None] == np.arange(LANES)[None, :], dtype=BF16)
    c_start = np.arange(ncp) * CMP_STRIDE
    j_start = np.arange(LANES) * SEL_BLOCK
    overlap = ((c_start[:, None] < j_start[None, :] + SEL_BLOCK) & (c_start[:, None] + CMP_BLOCK > j_start[None, :])
               & (np.arange(ncp)[:, None] < s // CMP_STRIDE - 1) & (np.arange(LANES)[None, :] < ns))
    overlap_t = jnp.asarray(overlap.T.astype(np.float32))

    q_rows = NSA_HPG * d
    return pl.pallas_call(
        functools.partial(_nsa_attn_kernel, tq=tq, kt=kt, wk=wk, ns=ns, n_top=n_top),
        grid=(b, g, s // tq),
        in_specs=[pl.BlockSpec((1, q_rows, tq), lambda bi, gi, i: (bi, gi, i)),
                  pl.BlockSpec((1, ncp, d), lambda bi, gi, i: (bi * g + gi, 0, 0)),
                  pl.BlockSpec((1, d, ncp), lambda bi, gi, i: (bi * g + gi, 0, 0)),
                  pl.BlockSpec((1, s, d), lambda bi, gi, i: (bi, 0, gi)),
                  pl.BlockSpec((1, s, d), lambda bi, gi, i: (bi, 0, g + gi)),
                  pl.BlockSpec((1, V_ROWS, s), lambda bi, gi, i: (bi, gi, 0)),
                  pl.BlockSpec((1, V_ROWS, s), lambda bi, gi, i: (bi, g + gi, 0)),
                  pl.BlockSpec((s, LANES), lambda bi, gi, i: (0, 0)),
                  pl.BlockSpec((LANES, ncp), lambda bi, gi, i: (0, 0)),
                  pl.BlockSpec((1, tq, LANES), lambda bi, gi, i: (bi, i, gi))],
        out_specs=pl.BlockSpec((1, tq, q_rows), lambda bi, gi, i: (bi, i, gi)),
        out_shape=jax.ShapeDtypeStruct((b, s, NSA_WIDTH), BF16),
        scratch_shapes=[pltpu.VMEM((V_ROWS, cols), F32), pltpu.VMEM((LANES, tq), F32),
                        pltpu.VMEM((kt, cols), F32), pltpu.VMEM((kt, cols), F32),
                        pltpu.VMEM((kt, cols), BF16), pltpu.VMEM((kt, cols), BF16)],
        compiler_params=_cparams(("parallel", "parallel", "arbitrary")),
        name="nsa_attention",
    )(q_t, kc, vc_t, k_n, k_n, v_t, v_t, e_t, overlap_t, gates3)


def _conv_kernel(cur_ref, prev_ref, w_ref, o_ref, *, tc):
    i = pl.program_id(1)
    cur = cur_ref[0].astype(F32)
    prev = jnp.where(i > 0, prev_ref[0].astype(F32), 0.0)
    pad = prev.shape[0] // 2
    full = jnp.concatenate([prev[pad:], cur], axis=0)
    y = None
    for kk in range(CONV_WIDTH):
        off = pad - (CONV_WIDTH - 1) + kk
        term = w_ref[kk:kk + 1, :] * full[off:off + tc]
        y = term if y is None else y + term
    o_ref[0] = (y * _sigmoid(y)).astype(BF16)


def conv_silu(p3, w, col0, width):
    b, s, _ = p3.shape
    tc = _tile(s, 512, BF16_SUBLANES)
    cw = 512
    assert col0 % cw == 0 and width % cw == 0
    c0 = col0 // cw
    halo = BF16_SUBLANES
    return pl.pallas_call(
        functools.partial(_conv_kernel, tc=tc),
        grid=(b, s // tc, width // cw),
        in_specs=[pl.BlockSpec((1, tc, cw), lambda bi, i, j: (bi, i, c0 + j)),
                  pl.BlockSpec((1, halo, cw), lambda bi, i, j: (bi, jnp.maximum(i * (tc // halo) - 1, 0), c0 + j)),
                  pl.BlockSpec((CONV_WIDTH, cw), lambda bi, i, j: (0, j))],
        out_specs=pl.BlockSpec((1, tc, cw), lambda bi, i, j: (bi, i, j)),
        out_shape=jax.ShapeDtypeStruct((b, s, width), BF16),
        compiler_params=_cparams(("parallel", "parallel", "parallel")),
        name="mlstm_conv",
    )(p3, p3, w)


def _log_sigmoid(x):
    return jnp.minimum(x, 0.0) - jnp.log1p(jnp.exp(-jnp.abs(x)))


def _mlstm_kernel(fb_ref, q_ref, k_ref, v_ref, ob_ref, gate_ref, nw_ref, o_ref, c_sc, n_sc, m_sc, *, cl):
    @pl.when(pl.program_id(1) == 0)
    def _():
        c_sc[...] = jnp.zeros(c_sc.shape, F32)
        n_sc[...] = jnp.zeros(n_sc.shape, F32)
        m_sc[...] = jnp.zeros(m_sc.shape, F32)

    dh = MLSTM_DIM
    slab = gate_ref[0]
    lane = lax.broadcasted_iota(I32, slab.shape, 1)
    r_i = lax.broadcasted_iota(I32, (cl, cl), 0)
    c_i = lax.broadcasted_iota(I32, (cl, cl), 1)
    eye = r_i == c_i
    tri = c_i <= r_i

    def head(hd, carry):
        off = pl.multiple_of(hd * dh, dh)
        q = q_ref[0, :, pl.ds(off, dh)]
        v = v_ref[0, :, pl.ds(off, dh)]
        ks32 = k_ref[0, :, pl.ds(off, dh)].astype(F32) * (dh ** -0.5)
        ks = ks32.astype(BF16)
        i_col = jnp.sum(jnp.where(lane == GATE_I_LANE + hd, slab, 0.0), axis=-1, keepdims=True)
        f_col = jnp.sum(jnp.where(lane == GATE_F_LANE + hd, slab, 0.0), axis=-1, keepdims=True)
        lf_col = _log_sigmoid(f_col + fb_ref[hd])

        lf_row = jnp.sum(jnp.where(eye, lf_col, 0.0), axis=0, keepdims=True)
        ig_row = jnp.sum(jnp.where(eye, i_col, 0.0), axis=0, keepdims=True)
        a_col = jnp.sum(jnp.where(tri, lf_row, 0.0), axis=1, keepdims=True)
        a_row = jnp.sum(jnp.where(r_i <= c_i, lf_col, 0.0), axis=0, keepdims=True)
        m_prev = m_sc[hd]

        dlog = jnp.where(tri, a_col - a_row + ig_row, NEG)
        inter = a_col + m_prev
        mt = jnp.maximum(inter, jnp.max(dlog, axis=-1, keepdims=True))
        wm = jnp.exp(dlog - mt) * _dot_nt(q, ks)
        e_col = jnp.exp(inter - mt)
        num = e_col * jnp.dot(q, c_sc[hd].astype(BF16), preferred_element_type=F32) \
            + jnp.dot(wm.astype(BF16), v, preferred_element_type=F32)
        qn = jnp.sum(q.astype(F32) * n_sc[hd], axis=-1, keepdims=True)
        den = e_col * qn + jnp.sum(wm, axis=-1, keepdims=True)
        hh = num / jnp.maximum(jnp.abs(den), jnp.exp(-mt))

        a_last = jnp.sum(lf_row, axis=-1, keepdims=True)
        gs = a_last - a_col + i_col
        m_new = jnp.maximum(a_last + m_prev, jnp.max(gs, axis=0, keepdims=True))
        decay = jnp.exp(a_last + m_prev - m_new)
        wk = jnp.exp(gs - m_new) * ks32
        c_sc[hd] = decay * c_sc[hd] + _dot_tn(wk.astype(BF16), v)
        n_sc[hd] = decay * n_sc[hd] + jnp.sum(wk, axis=0, keepdims=True)
        m_sc[hd] = m_new

        y = hh * lax.rsqrt(jnp.mean(hh * hh, axis=-1, keepdims=True) + RMS_EPS) * nw_ref[:, pl.ds(off, dh)]
        o_ref[0, :, pl.ds(off, dh)] = (y * _sigmoid(ob_ref[0, :, pl.ds(off, dh)].astype(F32))).astype(BF16)
        return carry

    lax.fori_loop(0, MLSTM_HEADS, head, 0)


def mlstm(qk, p3, gates3, f_bias, norm_w, v_col0, o_col0):
    b, s, _ = qk.shape
    cl = min(CHUNK, s)
    dh = MLSTM_DIM
    nh = MLSTM_HEADS
    w = nh * dh
    assert v_col0 % w == 0 and o_col0 % w == 0 and s % cl == 0
    vb, ob = v_col0 // w, o_col0 // w
    return pl.pallas_call(
        functools.partial(_mlstm_kernel, cl=cl),
        grid=(b, s // cl),
        in_specs=[pl.BlockSpec(memory_space=pltpu.SMEM),
                  pl.BlockSpec((1, cl, w), lambda bi, c: (bi, c, 0)),
                  pl.BlockSpec((1, cl, w), lambda bi, c: (bi, c, 1)),
                  pl.BlockSpec((1, cl, w), lambda bi, c: (bi, c, vb)),
                  pl.BlockSpec((1, cl, w), lambda bi, c: (bi, c, ob)),
                  pl.BlockSpec((1, cl, LANES), lambda bi, c: (bi, c, 0)),
                  pl.BlockSpec((1, w), lambda bi, c: (0, 0))],
        out_specs=pl.BlockSpec((1, cl, w), lambda bi, c: (bi, c, 0)),
        out_shape=jax.ShapeDtypeStruct((b, s, w), BF16),
        scratch_shapes=[pltpu.VMEM((nh, dh, dh), F32), pltpu.VMEM((nh, 1, dh), F32), pltpu.VMEM((nh, 1, 1), F32)],
        compiler_params=_cparams(("parallel", "arbitrary")),
        name="mlstm",
    )(f_bias.astype(F32), qk, qk, p3, p3, gates3, norm_w.reshape(1, w).astype(F32))


def _ret_kernel(cd_ref, q_ref, k_ref, v_ref, g_ref, cos_ref, sin_ref, nw_ref, dm_ref, xi_ref, zeta_ref,
                o_ref, r_sc, *, cl):
    @pl.when(pl.program_id(1) == 0)
    def _():
        r_sc[...] = jnp.zeros(r_sc.shape, F32)

    cos = cos_ref[0]
    sin = sin_ref[0]
    dk, dv = RET_QK_DIM, RET_V_DIM
    half = dk // 2
    scale = dk ** -0.5

    def rope(x):
        x1, x2 = x[:, :half], x[:, half:]
        return x1 * cos - x2 * sin, x1 * sin + x2 * cos

    def head(hd, carry):
        qo = pl.multiple_of(hd * dk, dk)
        vo = pl.multiple_of(hd * dv, dv)
        q1, q2 = rope(q_ref[0, :, pl.ds(qo, dk)].astype(F32))
        qr = jnp.concatenate([q1, q2], axis=1).astype(BF16)
        k1, k2 = rope(k_ref[0, :, pl.ds(qo, dk)].astype(F32))
        zeta = zeta_ref[hd] * scale
        kr = jnp.concatenate([k1 * scale, k2 * scale], axis=1).astype(BF16)
        kz = jnp.concatenate([k1 * zeta, k2 * zeta], axis=1).astype(BF16)
        v = v_ref[0, :, pl.ds(vo, dv)]

        inner = jnp.dot((_dot_nt(qr, kr) * dm_ref[hd]).astype(BF16), v, preferred_element_type=F32)
        xi = xi_ref[hd]
        cross = jnp.dot(qr, r_sc[hd].astype(BF16), preferred_element_type=F32)
        cross = cross * jnp.concatenate([xi] * (dv // LANES), axis=1)
        r_sc[hd] = cd_ref[hd] * r_sc[hd] + _dot_tn(kz, v)

        y = inner + cross
        y = y * lax.rsqrt(jnp.mean(y * y, axis=-1, keepdims=True) + RMS_EPS) * nw_ref[:, pl.ds(vo, dv)]
        gg = g_ref[0, :, pl.ds(vo, dv)].astype(F32)
        o_ref[0, :, pl.ds(vo, dv)] = (y * (gg * _sigmoid(gg))).astype(BF16)
        return carry

    lax.fori_loop(0, RET_HEADS, head, 0)


def retention(po3, cos, sin, norm_w):
    b, s, _ = po3.shape
    cl = min(CHUNK, s)
    nh, dk, dv = RET_HEADS, RET_QK_DIM, RET_V_DIM
    log_g = jnp.log1p(-jnp.exp2(-5.0 - jnp.arange(nh, dtype=F32)))
    idx = jnp.arange(cl, dtype=F32)
    diff = idx[:, None] - idx[None, :]
    dm = jnp.where(diff >= 0, jnp.exp(jnp.maximum(diff, 0.0) * log_g[:, None, None]), 0.0)
    xi = jnp.broadcast_to(jnp.exp((idx + 1.0) * log_g[:, None])[..., None], (nh, cl, LANES))
    zeta = jnp.broadcast_to(jnp.exp((cl - 1.0 - idx) * log_g[:, None])[..., None], (nh, cl, LANES))
    chunk_decay = jnp.exp(cl * log_g)
    qk_w, v_w = RET_QK_WIDTH, RET_V_WIDTH
    assert v_w == 2 * qk_w
    tab = pl.BlockSpec((1, cl, LANES), lambda bi, c: (bi, c, 0))

    def table(shape):
        return pl.BlockSpec(shape, lambda bi, c: (0, 0, 0))

    return pl.pallas_call(
        functools.partial(_ret_kernel, cl=cl),
        grid=(b, s // cl),
        in_specs=[pl.BlockSpec(memory_space=pltpu.SMEM),
                  pl.BlockSpec((1, cl, qk_w), lambda bi, c: (bi, c, 0)),
                  pl.BlockSpec((1, cl, qk_w), lambda bi, c: (bi, c, 1)),
                  pl.BlockSpec((1, cl, v_w), lambda bi, c: (bi, c, 1)),
                  pl.BlockSpec((1, cl, v_w), lambda bi, c: (bi, c, 2)),
                  tab, tab,
                  pl.BlockSpec((1, v_w), lambda bi, c: (0, 0)),
                  table((nh, cl, cl)), table((nh, cl, LANES)), table((nh, cl, LANES))],
        out_specs=pl.BlockSpec((1, cl, v_w), lambda bi, c: (bi, c, 0)),
        out_shape=jax.ShapeDtypeStruct((b, s, v_w), BF16),
        scratch_shapes=[pltpu.VMEM((nh, dk, dv), F32)],
        compiler_params=_cparams(("parallel", "arbitrary")),
        name="retention",
    )(chunk_decay, po3, po3, po3, po3, cos, sin, norm_w.reshape(1, v_w).astype(F32), dm, xi, zeta)


def _router_kernel(x_ref, g_ref, wr_ref, xn_ref, route_ref, cnt_ref, carry_sc):
    @pl.when(pl.program_id(0) == 0)
    def _():
        carry_sc[...] = jnp.zeros(carry_sc.shape, F32)

    x = x_ref[...]
    t = x.shape[0]
    y = x * lax.rsqrt(jnp.mean(x * x, axis=-1, keepdims=True) + RMS_EPS) * g_ref[...]
    xn_ref[...] = y.astype(BF16)
    logits = jnp.dot(y, wr_ref[...], precision=HIGHEST, preferred_element_type=F32)
    lane = lax.broadcasted_iota(I32, logits.shape, 1)
    lg = jnp.where(lane < N_EXPERTS, logits, LOWEST)
    v1 = jnp.max(lg, axis=-1, keepdims=True)
    i1 = jnp.min(jnp.where(lg == v1, lane, LANES), axis=-1, keepdims=True)
    lg2 = jnp.where(lane == i1, LOWEST, lg)
    v2 = jnp.max(lg2, axis=-1, keepdims=True)
    i2 = jnp.min(jnp.where(lg2 == v2, lane, LANES), axis=-1, keepdims=True)
    e2 = jnp.exp(v2 - v1)
    g1 = 1.0 / (1.0 + e2)
    g2 = e2 / (1.0 + e2)

    chosen = jnp.where((lane == i1) | (lane == i2), 1.0, 0.0)
    r_i = lax.broadcasted_iota(I32, (t, t), 0)
    c_i = lax.broadcasted_iota(I32, (t, t), 1)
    tri = jnp.where(c_i <= r_i, 1.0, 0.0).astype(BF16)
    seen = jnp.dot(tri, chosen.astype(BF16), preferred_element_type=F32) + carry_sc[...]
    rank1 = jnp.sum(jnp.where(lane == i1, seen, 0.0), axis=-1, keepdims=True) - 1.0
    rank2 = jnp.sum(jnp.where(lane == i2, seen, 0.0), axis=-1, keepdims=True) - 1.0
    total = seen[t - 1:t, :]
    carry_sc[...] = total
    cnt_ref[...] = jnp.broadcast_to(total, cnt_ref.shape)

    route = jnp.where(lane == 0, i1.astype(F32), 0.0)
    route = jnp.where(lane == 1, i2.astype(F32), route)
    route = jnp.where(lane == 2, g1, route)
    route = jnp.where(lane == 3, g2, route)
    route = jnp.where(lane == 4, rank1, route)
    route = jnp.where(lane == 5, rank2, route)
    route_ref[...] = route


def moe_route(x, g, w_router, tm=256):
    m, d = x.shape
    tm = _tile(m, tm, 16)
    wr = jnp.zeros((d, LANES), F32).at[:, :N_EXPERTS].set(w_router.astype(F32))
    return pl.pallas_call(
        _router_kernel,
        grid=(m // tm,),
        in_specs=[pl.BlockSpec((tm, d), lambda i: (i, 0)),
                  pl.BlockSpec((1, d), lambda i: (0, 0)),
                  pl.BlockSpec((d, LANES), lambda i: (0, 0))],
        out_specs=[pl.BlockSpec((tm, d), lambda i: (i, 0)),
                   pl.BlockSpec((tm, LANES), lambda i: (i, 0)),
                   pl.BlockSpec((8, LANES), lambda i: (0, 0))],
        out_shape=[jax.ShapeDtypeStruct((m, d), BF16),
                   jax.ShapeDtypeStruct((m, LANES), F32),
                   jax.ShapeDtypeStruct((8, LANES), F32)],
        scratch_shapes=[pltpu.VMEM((1, LANES), F32)],
        compiler_params=_cparams(("arbitrary",)),
        name="moe_route",
    )(x, g.reshape(1, d).astype(F32), wr)


def _dispatch_kernel(idx_ref, src_ref, o_ref, sem, *, rows):
    base = pl.program_id(0) * rows

    def issue(r, carry):
        pltpu.make_async_copy(src_ref.at[idx_ref[base + r]], o_ref.at[r], sem).start()
        return carry

    lax.fori_loop(0, rows, issue, 0)
    pltpu.make_async_copy(src_ref.at[pl.ds(0, rows)], o_ref, sem).wait()


def moe_dispatch(xn, row_tok, rows=2048):
    m, d = xn.shape
    n_rows = row_tok.shape[0]
    rows = _tile(n_rows, rows, 8)
    src = xn.reshape(m, d // LANES, LANES)
    out = pl.pallas_call(
        functools.partial(_dispatch_kernel, rows=rows),
        grid_spec=pltpu.PrefetchScalarGridSpec(
            num_scalar_prefetch=1,
            grid=(n_rows // rows,),
            in_specs=[pl.BlockSpec(memory_space=pl.ANY)],
            out_specs=pl.BlockSpec((rows, d // LANES, LANES), lambda i, idx: (i, 0, 0)),
            scratch_shapes=[pltpu.SemaphoreType.DMA(())],
        ),
        out_shape=jax.ShapeDtypeStruct((n_rows, d // LANES, LANES), xn.dtype),
        compiler_params=_cparams(("arbitrary",)),
        name="moe_dispatch",
    )(row_tok, src)
    return out.reshape(n_rows, d)


def _combine_kernel(d1_ref, d2_ref, x_ref, route_ref, yb_ref, o_ref, buf1, buf2, sem, *, rows):
    base = pl.program_id(0) * rows

    def issue(r8, carry):
        r0 = pl.multiple_of(r8 * 8, 8)
        for u in range(8):
            pltpu.make_async_copy(yb_ref.at[pl.ds(d1_ref[base + r0 + u], 1)], buf1.at[pl.ds(r0 + u, 1)],
                                  sem.at[0]).start()
            pltpu.make_async_copy(yb_ref.at[pl.ds(d2_ref[base + r0 + u], 1)], buf2.at[pl.ds(r0 + u, 1)],
                                  sem.at[1]).start()
        return carry

    lax.fori_loop(0, rows // 8, issue, 0)
    pltpu.make_async_copy(yb_ref.at[pl.ds(0, rows)], buf1, sem.at[0]).wait()
    pltpu.make_async_copy(yb_ref.at[pl.ds(0, rows)], buf2, sem.at[1]).wait()
    route = route_ref[...]
    o_ref[...] = x_ref[...] + (buf1[...] * route[:, 2:3] + buf2[...] * route[:, 3:4])


def moe_combine(x, route, yb, dest1, dest2, rows=256):
    m, d = x.shape
    rows = _tile(m, rows, 8)
    return pl.pallas_call(
        functools.partial(_combine_kernel, rows=rows),
        grid_spec=pltpu.PrefetchScalarGridSpec(
            num_scalar_prefetch=2,
            grid=(m // rows,),
            in_specs=[pl.BlockSpec((rows, d), lambda i, a, b: (i, 0)),
                      pl.BlockSpec((rows, LANES), lambda i, a, b: (i, 0)),
                      pl.BlockSpec(memory_space=pl.ANY)],
            out_specs=pl.BlockSpec((rows, d), lambda i, a, b: (i, 0)),
            scratch_shapes=[pltpu.VMEM((rows, d), F32), pltpu.VMEM((rows, d), F32),
                            pltpu.SemaphoreType.DMA((2,))],
        ),
        out_shape=jax.ShapeDtypeStruct((m, d), F32),
        compiler_params=_cparams(("arbitrary",)),
        name="moe_combine",
    )(dest1, dest2, x, route, yb)


def moe_layer(x, norm_g, w_router, w_gate, w_up, w_down):
    m, d = x.shape
    tm = MOE_TM
    xn, route, cnt = moe_route(x, norm_g, w_router)
    expert = route[:, 0:2].astype(I32)
    rank = route[:, 4:6].astype(I32)
    counts = cnt[0, :N_EXPERTS].astype(I32)
    padded = (counts + tm - 1) // tm * tm
    pend = jnp.cumsum(padded)
    pstart = pend - padded
    dest = pstart[expert] + rank
    n_blk = -(-(2 * m) // tm) + N_EXPERTS
    n_rows = n_blk * tm
    tok = jnp.broadcast_to(jnp.arange(m, dtype=I32)[:, None], (m, 2))
    row_tok = jnp.zeros((n_rows,), I32).at[dest.reshape(-1)].set(tok.reshape(-1))
    nused = (pend[-1] // tm).astype(I32)
    blk = jnp.minimum(jnp.arange(n_blk, dtype=I32), nused - 1)
    blk_e = jnp.minimum(jnp.sum(pend[None, :] <= (blk * tm)[:, None], axis=1), N_EXPERTS - 1).astype(I32)
    nu = nused.reshape(1)

    xb = moe_dispatch(xn, row_tok)
    tf = _tile(w_gate.shape[2], 512, LANES)
    hidden = gmm_swiglu(xb, w_gate, w_up, blk_e, nu, tm=tm, tn=tf)
    yb = gmm(hidden, w_down, blk_e, nu, tm=tm, tn=_tile(d, 512, LANES), out_dtype=F32)
    return moe_combine(x, route, yb, dest[:, 0], dest[:, 1])


def even_layer(x2, b, s, pos, w_norm, w_in, q_gain, k_gain, w_cmp_k, w_cmp_v, pe_cmp, conv_w, f_bias, m_norm,
               w_out, w_norm_ffn, w_gate, w_up, w_down):
    n, dm = x2.shape
    g, d = NSA_GROUPS, HEAD_DIM
    o_gate = NSA_SLAB
    o_qb = o_gate + NSA_HEADS * 3
    o_if = o_qb + 3 * MLSTM_WIDTH
    o_ob = o_if + 2 * MLSTM_HEADS
    w_ml = jnp.concatenate([w_in[:, o_qb:o_if], w_in[:, o_ob:o_ob + MLSTM_WIDTH]], axis=1)
    per_g = NSA_HPG * 3
    zeros = functools.partial(jnp.zeros, dtype=w_in.dtype)
    w_small = jnp.concatenate(
        [w_in[:, o_gate:o_gate + per_g], zeros((dm, GATE_I_LANE - per_g)),
         w_in[:, o_if:o_ob], zeros((dm, LANES - GATE_I_LANE - 2 * MLSTM_HEADS)),
         w_in[:, o_gate + per_g:o_qb], zeros((dm, LANES - per_g))], axis=1)

    h = rmsnorm_rows(x2, w_norm)
    p3 = mm(h, w_in[:, :o_gate], out_dtype=BF16).reshape(b, s, NSA_SLAB)
    pm3 = mm(h, w_ml, out_dtype=BF16).reshape(b, s, 4 * MLSTM_WIDTH)
    gates3 = mm(h, w_small, out_dtype=F32, tn=2 * LANES).reshape(b, s, 2 * LANES)

    cos_n, sin_n = trig_tables(pos, _nsa_inv_lane())
    q_t, k_n, v_t = nsa_prep(p3, cos_n.reshape(b, s, LANES), sin_n.reshape(b, s, LANES), q_gain, k_gain)
    nh = s // CMP_STRIDE
    cmp_pos = jnp.concatenate([pos[:, CMP_BLOCK - 1::CMP_STRIDE][:, :nh - 1], pos[:, -1:]], axis=1)
    cos_c, sin_c = trig_tables(cmp_pos, _nsa_inv_lane())

    def cmp_blocks(col0):
        tok = p3[:, :, col0:col0 + NSA_KV_WIDTH].reshape(b, nh, CMP_STRIDE, g, d)
        halves = tok.transpose(0, 3, 1, 2, 4).reshape(b, g, nh, CMP_STRIDE * d)
        nxt = jnp.concatenate([halves[:, :, 1:], jnp.zeros_like(halves[:, :, :1])], axis=2)
        return jnp.concatenate([halves, nxt], axis=-1).reshape(b * g * nh, CMP_BLOCK * d)

    tabs = (cos_c.reshape(b, nh, LANES), sin_c.reshape(b, nh, LANES))
    kc = compress(cmp_blocks(NSA_WIDTH), pe_cmp, w_cmp_k, k_gain, *tabs, is_key=True, rows_per_seq=nh, groups=g)
    vc = compress(cmp_blocks(NSA_WIDTH + NSA_KV_WIDTH), pe_cmp, w_cmp_v, k_gain, *tabs, is_key=False,
                  rows_per_seq=nh, groups=g)
    o_nsa = nsa_attention(q_t, k_n, v_t, kc, vc, gates3)

    qk = conv_silu(pm3, conv_w, 0, 2 * MLSTM_WIDTH)
    h_b = mlstm(qk, pm3, gates3, f_bias, m_norm, 2 * MLSTM_WIDTH, 3 * MLSTM_WIDTH)

    mixed = jnp.concatenate([o_nsa, h_b], axis=-1).reshape(n, NSA_WIDTH + MLSTM_WIDTH)
    x2 = mm(mixed, w_out, out_dtype=F32, residual=x2)
    hf = rmsnorm_rows(x2, w_norm_ffn)
    hidden = mm_swiglu(hf, w_gate, w_up, tm=1024)
    return mm(hidden, w_down, out_dtype=F32, residual=x2, tm=1024)


def odd_layer(x2, b, s, pos, w_norm, w_in, r_norm, w_out, w_norm_ffn, w_router, e_gate, e_up, e_down):
    n, dm = x2.shape
    h = rmsnorm_rows(x2, w_norm)
    po3 = mm(h, w_in, out_dtype=BF16, tn=1024).reshape(b, s, -1)
    cos_r, sin_r = trig_tables(pos, _ret_inv_lane())
    y = retention(po3, cos_r.reshape(b, s, LANES), sin_r.reshape(b, s, LANES), r_norm)
    x2 = mm(y.reshape(n, RET_V_WIDTH), w_out, out_dtype=F32, residual=x2, tm=1024)
    return moe_layer(x2, w_norm_ffn, w_router, e_gate, e_up, e_down)


def kernel(x, positions, norm_mix_even, w_in_even, nsa_q_gain, nsa_k_gain, w_cmp_k, w_cmp_v, pe_cmp, mlstm_conv, mlstm_f_bias, mlstm_norm, w_out_even, norm_ffn_even, ffn_gate, ffn_up, ffn_down, norm_mix_odd, w_in_odd, ret_norm, w_out_odd, norm_ffn_odd, w_router, exp_gate, exp_up, exp_down):
    b, s, dm = x.shape
    depth = norm_mix_even.shape[0] + norm_mix_odd.shape[0]
    x2 = x.reshape(b * s, dm)
    for layer in range(depth):
        j = layer // 2
        if layer % 2 == 0:
            x2 = even_layer(x2, b, s, positions, norm_mix_even[j], w_in_even[j], nsa_q_gain[j], nsa_k_gain[j],
                            w_cmp_k[j], w_cmp_v[j], pe_cmp[j], mlstm_conv[j], mlstm_f_bias[j], mlstm_norm[j],
                            w_out_even[j], norm_ffn_even[j], ffn_gate[j], ffn_up[j], ffn_down[j])
        else:
            x2 = odd_layer(x2, b, s, positions, norm_mix_odd[j], w_in_odd[j], ret_norm[j], w_out_odd[j],
                           norm_ffn_odd[j], w_router[j], exp_gate[j], exp_up[j], exp_down[j])
    return x2.reshape(b, s, dm)
```

```python
import functools

import numpy as np
import jax
import jax.numpy as jnp
from jax import lax
from jax.experimental import pallas as pl
from jax.experimental.pallas import tpu as pltpu

F32 = jnp.float32
BF16 = jnp.bfloat16
I32 = jnp.int32
HIGHEST = lax.Precision.HIGHEST

HEAD_DIM = 128
NSA_HEADS = 8
NSA_GROUPS = 2
NSA_HPG = NSA_HEADS // NSA_GROUPS
NSA_WIDTH = NSA_HEADS * HEAD_DIM
NSA_KV_WIDTH = NSA_GROUPS * HEAD_DIM
NSA_SLAB = NSA_WIDTH + 6 * NSA_KV_WIDTH
CMP_BLOCK = 32
CMP_STRIDE = 16
SEL_BLOCK = 64
SEL_TOPN = 16
WINDOW = 512
ROPE_DIM = HEAD_DIM // 4
ROPE_THETA = 500000.0
SEL_FORCE = 1.0e6
NEG = -1.0e30
LOWEST = -3.0e38
MLSTM_HEADS = 4
MLSTM_DIM = 256
MLSTM_WIDTH = MLSTM_HEADS * MLSTM_DIM
CONV_WIDTH = 4
RET_HEADS = 8
RET_QK_DIM = 256
RET_V_DIM = 512
RET_QK_WIDTH = RET_HEADS * RET_QK_DIM
RET_V_WIDTH = RET_HEADS * RET_V_DIM
RET_ROPE_THETA = 10000.0
N_EXPERTS = 8
RMS_EPS = 1e-6

LANES = 128
BF16_SUBLANES = 16
V7X_VMEM_BYTES = 64 * 1024 * 1024
VMEM_LIMIT = V7X_VMEM_BYTES - 8 * 1024 * 1024

CHUNK = 256
NSA_TQ = 128
LOG2_E = 1.4426950408889634
V_ROWS = HEAD_DIM + BF16_SUBLANES
NSA_KT = 256
SOFTMAX_ROWS = 64
MOE_TM = 512
GATE_I_LANE = 16
GATE_F_LANE = 20


def _cparams(sem, vmem=VMEM_LIMIT):
    return pltpu.CompilerParams(dimension_semantics=sem, vmem_limit_bytes=vmem)


def _tile(n, target, quantum):
    if n <= target:
        return n
    t = (target // quantum) * quantum
    while t > quantum and n % t:
        t -= quantum
    assert n % t == 0, (n, target, quantum)
    return t


def _sigmoid(x):
    return 1.0 / (1.0 + jnp.exp(-x))


def _dot_nt(a, b):
    return lax.dot_general(a, b, (((1,), (1,)), ((), ())), preferred_element_type=F32)


def _dot_tn(a, b):
    return lax.dot_general(a, b, (((0,), (0,)), ((), ())), preferred_element_type=F32)


def _rmsnorm_kernel(x_ref, g_ref, o_ref):
    x = x_ref[...]
    y = x * lax.rsqrt(jnp.mean(x * x, axis=-1, keepdims=True) + RMS_EPS)
    o_ref[...] = (y * g_ref[...]).astype(o_ref.dtype)


def rmsnorm_rows(x, g, tm=512):
    m, d = x.shape
    tm = _tile(m, tm, 8)
    return pl.pallas_call(
        _rmsnorm_kernel,
        grid=(m // tm,),
        in_specs=[pl.BlockSpec((tm, d), lambda i: (i, 0)), pl.BlockSpec((1, d), lambda i: (0, 0))],
        out_specs=pl.BlockSpec((tm, d), lambda i: (i, 0)),
        out_shape=jax.ShapeDtypeStruct((m, d), BF16),
        compiler_params=_cparams(("parallel",)),
        name="rmsnorm",
    )(x, g.reshape(1, d).astype(F32))


def _gmm_kernel(be_ref, nu_ref, x_ref, w_ref, *rest, has_res):
    if has_res:
        r_ref, o_ref, wb_ref = rest
    else:
        o_ref, wb_ref = rest
    i = pl.program_id(1)
    changed = be_ref[i] != be_ref[jnp.maximum(i - 1, 0)]

    @pl.when((i == 0) | changed)
    def _():
        wb_ref[...] = w_ref[0].astype(BF16)

    @pl.when(i < nu_ref[0])
    def _():
        acc = jnp.dot(x_ref[...], wb_ref[...], preferred_element_type=F32)
        if has_res:
            acc = r_ref[...] + acc
        o_ref[...] = acc.astype(o_ref.dtype)

    @pl.when(i >= nu_ref[0])
    def _():
        o_ref[...] = jnp.zeros_like(o_ref)


def _weight_spec(k, tn, single_buffer):
    mode = dict(pipeline_mode=pl.Buffered(1)) if single_buffer else {}
    return pl.BlockSpec((1, k, tn), lambda j, i, be, nu: (be[i], 0, j), **mode)


def gmm(x, w, blk_e, nused, *, tm, tn, out_dtype, residual=None, single_buffer_w=False):
    m, k = x.shape
    e, k2, n = w.shape
    assert k == k2 and m % tm == 0 and n % tn == 0
    nb = m // tm
    in_specs = [
        pl.BlockSpec((tm, k), lambda j, i, be, nu: (jnp.minimum(i, nu[0] - 1), 0)),
        _weight_spec(k, tn, single_buffer_w),
    ]
    args = [x, w]
    if residual is not None:
        in_specs.append(pl.BlockSpec((tm, tn), lambda j, i, be, nu: (i, j)))
        args.append(residual)
    return pl.pallas_call(
        functools.partial(_gmm_kernel, has_res=residual is not None),
        grid_spec=pltpu.PrefetchScalarGridSpec(
            num_scalar_prefetch=2,
            grid=(n // tn, nb),
            in_specs=in_specs,
            out_specs=pl.BlockSpec((tm, tn), lambda j, i, be, nu: (i, j)),
            scratch_shapes=[pltpu.VMEM((k, tn), BF16)],
        ),
        out_shape=jax.ShapeDtypeStruct((m, n), out_dtype),
        compiler_params=_cparams(("arbitrary", "arbitrary")),
        name="gmm",
    )(blk_e, nused, *args)


def _gmm_swiglu_kernel(be_ref, nu_ref, x_ref, wg_ref, wu_ref, o_ref, wgb_ref, wub_ref):
    i = pl.program_id(1)
    changed = be_ref[i] != be_ref[jnp.maximum(i - 1, 0)]

    @pl.when((i == 0) | changed)
    def _():
        wgb_ref[...] = wg_ref[0].astype(BF16)
        wub_ref[...] = wu_ref[0].astype(BF16)

    @pl.when(i < nu_ref[0])
    def _():
        x = x_ref[...]
        g = jnp.dot(x, wgb_ref[...], preferred_element_type=F32)
        u = jnp.dot(x, wub_ref[...], preferred_element_type=F32)
        o_ref[...] = (g * _sigmoid(g) * u).astype(o_ref.dtype)

    @pl.when(i >= nu_ref[0])
    def _():
        o_ref[...] = jnp.zeros_like(o_ref)


def gmm_swiglu(x, wg, wu, blk_e, nused, *, tm, tn, single_buffer_w=False):
    m, k = x.shape
    e, k2, n = wg.shape
    assert k == k2 and wu.shape == wg.shape and m % tm == 0 and n % tn == 0
    nb = m // tm
    w_spec = _weight_spec(k, tn, single_buffer_w)
    return pl.pallas_call(
        _gmm_swiglu_kernel,
        grid_spec=pltpu.PrefetchScalarGridSpec(
            num_scalar_prefetch=2,
            grid=(n // tn, nb),
            in_specs=[pl.BlockSpec((tm, k), lambda j, i, be, nu: (jnp.minimum(i, nu[0] - 1), 0)),
                      w_spec, w_spec],
            out_specs=pl.BlockSpec((tm, tn), lambda j, i, be, nu: (i, j)),
            scratch_shapes=[pltpu.VMEM((k, tn), BF16), pltpu.VMEM((k, tn), BF16)],
        ),
        out_shape=jax.ShapeDtypeStruct((m, n), BF16),
        compiler_params=_cparams(("arbitrary", "arbitrary")),
        name="gmm_swiglu",
    )(blk_e, nused, x, wg, wu)


def _dense_blocks(m, tm):
    nb = m // tm
    return jnp.zeros((nb,), I32), jnp.full((1,), nb, I32)


def mm(x, w, *, out_dtype, residual=None, tm=2048, tn=512):
    m, k = x.shape
    n = w.shape[1]
    tm = _tile(m, tm, 16)
    tn = _tile(n, tn, LANES)
    be, nu = _dense_blocks(m, tm)
    return gmm(x, w[None], be, nu, tm=tm, tn=tn, out_dtype=out_dtype, residual=residual, single_buffer_w=True)


def mm_swiglu(x, wg, wu, *, tm=2048, tn=512):
    m = x.shape[0]
    tm = _tile(m, tm, 16)
    tn = _tile(wg.shape[1], tn, LANES)
    be, nu = _dense_blocks(m, tm)
    return gmm_swiglu(x, wg[None], wu[None], be, nu, tm=tm, tn=tn, single_buffer_w=True)


def _trig_kernel(pos_ref, inv_ref, cos_ref, sin_ref):
    ang = pos_ref[...] * inv_ref[...]
    cos_ref[...] = jnp.cos(ang)
    sin_ref[...] = jnp.sin(ang)


def trig_tables(pos, inv_lane):
    r = pos.size
    pos_b = jnp.broadcast_to(pos.astype(F32).reshape(r, 1), (r, LANES))
    tr = _tile(r, 512, 8)
    spec = pl.BlockSpec((tr, LANES), lambda i: (i, 0))
    return pl.pallas_call(
        _trig_kernel,
        grid=(r // tr,),
        in_specs=[spec, pl.BlockSpec((1, LANES), lambda i: (0, 0))],
        out_specs=[spec, spec],
        out_shape=[jax.ShapeDtypeStruct((r, LANES), F32)] * 2,
        compiler_params=_cparams(("parallel",)),
        name="trig_tables",
    )(pos_b, inv_lane.reshape(1, LANES))


def _nsa_inv_lane():
    half = ROPE_DIM // 2
    inv = jnp.power(jnp.float32(ROPE_THETA), -jnp.arange(half, dtype=F32) * (2.0 / ROPE_DIM))
    return jnp.concatenate([inv, inv, jnp.zeros((LANES - ROPE_DIM,), F32)])


def _ret_inv_lane():
    half = RET_QK_DIM // 2
    return jnp.power(jnp.float32(RET_ROPE_THETA), -jnp.arange(half, dtype=F32) * (2.0 / RET_QK_DIM))


def _norm_rope_head(x, gain, cos, sin):
    half = ROPE_DIM // 2
    y = x * lax.rsqrt(jnp.mean(x * x, axis=-1, keepdims=True) + RMS_EPS) * gain
    lane = lax.broadcasted_iota(I32, y.shape, 1)
    from_hi = jnp.where(lane < half, -sin, 0.0)
    from_lo = jnp.where((lane >= half) & (lane < ROPE_DIM), sin, 0.0)
    return (y * cos + pltpu.roll(y, LANES - half, 1) * from_hi + pltpu.roll(y, half, 1) * from_lo)


def _nsa_prep_kernel(p_ref, cos_ref, sin_ref, qg_ref, kg_ref, qt_ref, kn_ref, vt_ref):
    cos = cos_ref[0]
    sin = sin_ref[0]
    scale = HEAD_DIM ** -0.5 * LOG2_E
    d = HEAD_DIM
    g = NSA_GROUPS
    for hd in range(NSA_HEADS):
        q = _norm_rope_head(p_ref[0, :, hd * d:(hd + 1) * d].astype(F32), qg_ref[...], cos, sin)
        qt_ref[0, hd * d:(hd + 1) * d, :] = (q * scale).T.astype(BF16)
    for n, slab in enumerate((2, 4)):
        for gi in range(g):
            off = NSA_WIDTH + slab * NSA_KV_WIDTH + gi * d
            k = _norm_rope_head(p_ref[0, :, off:off + d].astype(F32), kg_ref[...], cos, sin)
            kn_ref[0, :, (n * g + gi) * d:(n * g + gi + 1) * d] = k.astype(BF16)
    for n, slab in enumerate((3, 5)):
        for gi in range(g):
            off = NSA_WIDTH + slab * NSA_KV_WIDTH + gi * d
            v = p_ref[0, :, off:off + d].astype(F32)
            r0 = (n * g + gi) * V_ROWS
            vt_ref[0, r0:r0 + d, :] = v.T.astype(BF16)
            vt_ref[0, r0 + d:r0 + V_ROWS, :] = jnp.ones((V_ROWS - d, v.shape[0]), BF16)


def nsa_prep(p3, cos, sin, q_gain, k_gain):
    b, s, _ = p3.shape
    t = _tile(s, 256, LANES)
    d = HEAD_DIM
    tab = pl.BlockSpec((1, t, LANES), lambda bi, i: (bi, i, 0))
    gain = pl.BlockSpec((1, d), lambda bi, i: (0, 0))
    kv = 2 * NSA_KV_WIDTH
    vr = 2 * NSA_GROUPS * V_ROWS
    return pl.pallas_call(
        _nsa_prep_kernel,
        grid=(b, s // t),
        in_specs=[pl.BlockSpec((1, t, NSA_SLAB), lambda bi, i: (bi, i, 0)), tab, tab, gain, gain],
        out_specs=[pl.BlockSpec((1, NSA_WIDTH, t), lambda bi, i: (bi, 0, i)),
                   pl.BlockSpec((1, t, kv), lambda bi, i: (bi, i, 0)),
                   pl.BlockSpec((1, vr, t), lambda bi, i: (bi, 0, i))],
        out_shape=[jax.ShapeDtypeStruct((b, NSA_WIDTH, s), BF16),
                   jax.ShapeDtypeStruct((b, s, kv), BF16),
                   jax.ShapeDtypeStruct((b, vr, s), BF16)],
        compiler_params=_cparams(("parallel", "parallel")),
        name="nsa_prep",
    )(p3, cos, sin, q_gain.reshape(1, d), k_gain.reshape(1, d))


def _compress_kernel(blk_ref, pe_ref, w_ref, kg_ref, cos_ref, sin_ref, o_ref, *, is_key):
    a = (blk_ref[...].astype(F32) + pe_ref[...]).astype(BF16)
    y = jnp.dot(a, w_ref[...].astype(BF16), preferred_element_type=F32)
    if is_key:
        o_ref[0] = _norm_rope_head(y, kg_ref[...], cos_ref[0], sin_ref[0]).astype(BF16)
    else:
        o_ref[0] = y.T.astype(BF16)


def compress(blk, pe, w, k_gain, cos_c, sin_c, *, is_key, rows_per_seq, groups):
    r, kdim = blk.shape
    t = rows_per_seq
    d = HEAD_DIM
    tab = pl.BlockSpec((1, t, LANES), lambda i: (i // groups, 0, 0))
    out_blk = (1, t, d) if is_key else (1, d, t)
    return pl.pallas_call(
        functools.partial(_compress_kernel, is_key=is_key),
        grid=(r // t,),
        in_specs=[pl.BlockSpec((t, kdim), lambda i: (i, 0)),
                  pl.BlockSpec((1, kdim), lambda i: (0, 0)),
                  pl.BlockSpec((kdim, d), lambda i: (0, 0)),
                  pl.BlockSpec((1, d), lambda i: (0, 0)),
                  tab, tab],
        out_specs=pl.BlockSpec(out_blk, lambda i: (i, 0, 0)),
        out_shape=jax.ShapeDtypeStruct((r // t,) + out_blk[1:], BF16),
        compiler_params=_cparams(("parallel",)),
        name="nsa_compress",
    )(blk, pe.reshape(1, kdim), w, k_gain.reshape(1, d), cos_c, sin_c)


def _nsa_attn_kernel(qt_ref, kc_ref, vct_ref, ks_ref, kw_ref, vst_ref, vwt_ref, et_ref, ovt_ref, gate_ref,
                     o_ref, acc_sc, val_sc, sa_sc, sb_sc, pa_sc, pb_sc, *, tq, kt, wk, ns, n_top):
    i = pl.program_id(2)
    t0 = i * tq
    cols = NSA_HPG * tq
    d = HEAD_DIM
    q_t = jnp.concatenate([qt_ref[0, p * d:(p + 1) * d, :] for p in range(NSA_HPG)], axis=1)
    t_lane = t0 + (lax.broadcasted_iota(I32, (1, cols), 1) & (tq - 1))

    ncp = kc_ref.shape[1]
    s_c = jnp.dot(kc_ref[0], q_t, preferred_element_type=F32)
    c_end = lax.broadcasted_iota(I32, (ncp, cols), 0) * CMP_STRIDE + (CMP_BLOCK - 1)
    cmask = c_end <= t_lane
    s_c = jnp.where(cmask, s_c, NEG)
    e_c = jnp.where(cmask, jnp.exp2(s_c - jnp.max(s_c, axis=0, keepdims=True)), 0.0)
    den_c = jnp.sum(e_c, axis=0, keepdims=True)
    p_c = e_c * (1.0 / jnp.maximum(den_c, 1e-30))
    o_cmp = jnp.dot(vct_ref[0], p_c.astype(BF16), preferred_element_type=F32)

    p_grp = p_c[:, 0:tq]
    for p in range(1, NSA_HPG):
        p_grp = p_grp + p_c[:, p * tq:(p + 1) * tq]
    imp = jnp.dot(ovt_ref[...], p_grp, precision=HIGHEST, preferred_element_type=F32)
    jb = lax.broadcasted_iota(I32, (LANES, tq), 0)
    cur = (t0 + lax.broadcasted_iota(I32, (1, tq), 1)) // SEL_BLOCK
    forced = (jb == 0) | (jb == cur) | (jb == cur - 1)
    val = jnp.where(jb <= cur, jnp.where(forced, SEL_FORCE, imp), NEG)
    val = jnp.where(jb < ns, val, LOWEST)
    val_sc[...] = val
    beaten = jnp.zeros((LANES, tq), F32)
    for j2 in range(ns):
        r = val_sc[j2:j2 + 1, :]
        ge = jnp.where(r >= val, 1.0, 0.0)
        gt = jnp.where(r > val, 1.0, 0.0)
        beaten = beaten + jnp.where(jb > j2, ge, gt)
    past = jb < t0 // SEL_BLOCK
    bias = jnp.where(past & (beaten < n_top), 0.0, jnp.where(jb < ns, NEG, 0.0)).astype(BF16)
    q_aug = jnp.concatenate([q_t, jnp.concatenate([bias] * NSA_HPG, axis=1)], axis=0)

    d0 = pl.multiple_of(t0, tq)
    s_d = jnp.dot(ks_ref[0, pl.ds(d0, tq), :], q_t, preferred_element_type=F32)
    s_d = jnp.where(d0 + lax.broadcasted_iota(I32, (tq, cols), 0) <= t_lane, s_d, NEG)
    m_d = jnp.max(s_d, axis=0, keepdims=True)
    p_d = jnp.exp2(s_d - m_d)
    acc_sc[...] = jnp.dot(vst_ref[0, :, pl.ds(d0, tq)], p_d.astype(BF16), preferred_element_type=F32)

    n_pairs = (t0 + 2 * kt - 1) // (2 * kt)
    last_a = jnp.maximum(n_pairs - 1, 0) * (2 * kt)

    def scores(k0):
        k0 = pl.multiple_of(k0, kt)
        k_aug = jnp.concatenate([ks_ref[0, pl.ds(k0, kt), :], et_ref[pl.ds(k0, kt), :]], axis=1)
        return jnp.dot(k_aug, q_aug, preferred_element_type=F32)

    def values(p_ref, k0):
        k0 = pl.multiple_of(k0, kt)
        return jnp.dot(vst_ref[0, :, pl.ds(k0, kt)], p_ref[...], preferred_element_type=F32)

    def softmax_update(s_ref, p_ref, m_prev):
        m_new = jnp.maximum(m_prev, jnp.max(s_ref[...], axis=0, keepdims=True))
        for r0 in range(0, kt, SOFTMAX_ROWS):
            p_ref[r0:r0 + SOFTMAX_ROWS, :] = jnp.exp2(s_ref[r0:r0 + SOFTMAX_ROWS, :] - m_new).astype(BF16)
        return m_new, jnp.exp2(m_prev - m_new)

    def sel_step(j, carry):
        m_run, alpha_b = carry
        k0 = j * (2 * kt)
        sb_sc[...] = scores(k0 + kt)
        acc_sc[...] = alpha_b * acc_sc[...] + values(pb_sc, jnp.maximum(k0 - kt, 0))
        m_run, alpha_a = softmax_update(sa_sc, pa_sc, m_run)
        sa_sc[...] = scores(jnp.minimum(k0 + 2 * kt, last_a))
        acc_sc[...] = alpha_a * acc_sc[...] + values(pa_sc, k0)
        return softmax_update(sb_sc, pb_sc, m_run)

    sa_sc[...] = scores(0)
    pb_sc[...] = jnp.zeros(pb_sc.shape, BF16)
    _, alpha_last = lax.fori_loop(0, n_pairs, sel_step, (m_d, jnp.ones((1, cols), F32)))
    acc = alpha_last * acc_sc[...] + values(pb_sc, last_a + kt)
    o_sel = acc[:d] * (1.0 / acc[d:d + 1])

    w0 = pl.multiple_of(jnp.maximum(t0 + tq - wk, 0), LANES)
    s_w = jnp.dot(kw_ref[0, pl.ds(w0, wk), :], q_t, preferred_element_type=F32)
    kpos = w0 + lax.broadcasted_iota(I32, (wk, cols), 0)
    wmask = (kpos <= t_lane) & (kpos > t_lane - WINDOW)
    s_w = jnp.where(wmask, s_w, NEG)
    e_w = jnp.exp2(s_w - jnp.max(s_w, axis=0, keepdims=True))
    o_win = jnp.dot(vwt_ref[0, :, pl.ds(w0, wk)], e_w.astype(BF16), preferred_element_type=F32)
    o_win = o_win[:d] * (1.0 / o_win[d:d + 1])

    g_t = _sigmoid(gate_ref[0]).T
    for p in range(NSA_HPG):
        sl = slice(p * tq, (p + 1) * tq)
        o_t = (g_t[3 * p:3 * p + 1] * o_cmp[:, sl] + g_t[3 * p + 1:3 * p + 2] * o_sel[:, sl]
               + g_t[3 * p + 2:3 * p + 3] * o_win[:, sl])
        o_ref[0, :, p * d:(p + 1) * d] = o_t.T.astype(BF16)


def nsa_attention(q_t, k_n, v_t, kc, vc_t, gates3):
    b, _, s = q_t.shape
    g, d = NSA_GROUPS, HEAD_DIM
    ncp = kc.shape[1]
    tq = NSA_TQ
    kt = min(NSA_KT, s)
    wk = WINDOW + tq
    ns = s // SEL_BLOCK
    assert s % (2 * kt) == 0 and kt % SOFTMAX_ROWS == 0 and s >= wk and ns <= LANES and tq == LANES
    n_top = min(SEL_TOPN, ns)
    cols = NSA_HPG * tq

    key_blk = np.arange(s) // SEL_BLOCK
    e_t = jnp.asarray(key_blk[:, None] == np.arange(LANES)[None, :], dtype=BF16)
    c_start = np.arange(ncp) * CMP_STRIDE
    j_start = np.arange(LANES) * SEL_BLOCK
    overlap = ((c_start[:, None] < j_start[None, :] + SEL_BLOCK) & (c_start[:, None] + CMP_BLOCK > j_start[None, :])
               & (np.arange(ncp)[:, None] < s // CMP_STRIDE - 1) & (np.arange(LANES)[None, :] < ns))
    overlap_t = jnp.asarray(overlap.T.astype(np.float32))

    q_rows = NSA_HPG * d
    return pl.pallas_call(
        functools.partial(_nsa_attn_kernel, tq=tq, kt=kt, wk=wk, ns=ns, n_top=n_top),
        grid=(b, g, s // tq),
        in_specs=[pl.BlockSpec((1, q_rows, tq), lambda bi, gi, i: (bi, gi, i)),
                  pl.BlockSpec((1, ncp, d), lambda bi, gi, i: (bi * g + gi, 0, 0)),
                  pl.BlockSpec((1, d, ncp), lambda bi, gi, i: (bi * g + gi, 0, 0)),
                  pl.BlockSpec((1, s, d), lambda bi, gi, i: (bi, 0, gi)),
                  pl.BlockSpec((1, s, d), lambda bi, gi, i: (bi, 0, g + gi)),
                  pl.BlockSpec((1, V_ROWS, s), lambda bi, gi, i: (bi, gi, 0)),
                  pl.BlockSpec((1, V_ROWS, s), lambda bi, gi, i: (bi, g + gi, 0)),
                  pl.BlockSpec((s, LANES), lambda bi, gi, i: (0, 0)),
                  pl.BlockSpec((LANES, ncp), lambda bi, gi, i: (0, 0)),
                  pl.BlockSpec((1, tq, LANES), lambda bi, gi, i: (bi, i, gi))],
        out_specs=pl.BlockSpec((1, tq, q_rows), lambda bi, gi, i: (bi, i, gi)),
        out_shape=jax.ShapeDtypeStruct((b, s, NSA_WIDTH), BF16),
        scratch_shapes=[pltpu.VMEM((V_ROWS, cols), F32), pltpu.VMEM((LANES, tq), F32),
                        pltpu.VMEM((kt, cols), F32), pltpu.VMEM((kt, cols), F32),
                        pltpu.VMEM((kt, cols), BF16), pltpu.VMEM((kt, cols), BF16)],
        compiler_params=_cparams(("parallel", "parallel", "arbitrary")),
        name="nsa_attention",
    )(q_t, kc, vc_t, k_n, k_n, v_t, v_t, e_t, overlap_t, gates3)


def _conv_kernel(cur_ref, prev_ref, w_ref, o_ref, *, tc):
    i = pl.program_id(1)
    cur = cur_ref[0].astype(F32)
    prev = jnp.where(i > 0, prev_ref[0].astype(F32), 0.0)
    pad = prev.shape[0] // 2
    full = jnp.concatenate([prev[pad:], cur], axis=0)
    y = None
    for kk in range(CONV_WIDTH):
        off = pad - (CONV_WIDTH - 1) + kk
        term = w_ref[kk:kk + 1, :] * full[off:off + tc]
        y = term if y is None else y + term
    o_ref[0] = (y * _sigmoid(y)).astype(BF16)


def conv_silu(p3, w, col0, width):
    b, s, _ = p3.shape
    tc = _tile(s, 512, BF16_SUBLANES)
    cw = 512
    assert col0 % cw == 0 and width % cw == 0
    c0 = col0 // cw
    halo = BF16_SUBLANES
    return pl.pallas_call(
        functools.partial(_conv_kernel, tc=tc),
        grid=(b, s // tc, width // cw),
        in_specs=[pl.BlockSpec((1, tc, cw), lambda bi, i, j: (bi, i, c0 + j)),
                  pl.BlockSpec((1, halo, cw), lambda bi, i, j: (bi, jnp.maximum(i * (tc // halo) - 1, 0), c0 + j)),
                  pl.BlockSpec((CONV_WIDTH, cw), lambda bi, i, j: (0, j))],
        out_specs=pl.BlockSpec((1, tc, cw), lambda bi, i, j: (bi, i, j)),
        out_shape=jax.ShapeDtypeStruct((b, s, width), BF16),
        compiler_params=_cparams(("parallel", "parallel", "parallel")),
        name="mlstm_conv",
    )(p3, p3, w)


def _log_sigmoid(x):
    return jnp.minimum(x, 0.0) - jnp.log1p(jnp.exp(-jnp.abs(x)))


def _mlstm_kernel(fb_ref, q_ref, k_ref, v_ref, ob_ref, gate_ref, nw_ref, o_ref, c_sc, n_sc, m_sc, *, cl):
    @pl.when(pl.program_id(1) == 0)
    def _():
        c_sc[...] = jnp.zeros(c_sc.shape, F32)
        n_sc[...] = jnp.zeros(n_sc.shape, F32)
        m_sc[...] = jnp.zeros(m_sc.shape, F32)

    dh = MLSTM_DIM
    slab = gate_ref[0]
    lane = lax.broadcasted_iota(I32, slab.shape, 1)
    r_i = lax.broadcasted_iota(I32, (cl, cl), 0)
    c_i = lax.broadcasted_iota(I32, (cl, cl), 1)
    eye = r_i == c_i
    tri = c_i <= r_i

    def head(hd, carry):
        off = pl.multiple_of(hd * dh, dh)
        q = q_ref[0, :, pl.ds(off, dh)]
        v = v_ref[0, :, pl.ds(off, dh)]
        ks32 = k_ref[0, :, pl.ds(off, dh)].astype(F32) * (dh ** -0.5)
        ks = ks32.astype(BF16)
        i_col = jnp.sum(jnp.where(lane == GATE_I_LANE + hd, slab, 0.0), axis=-1, keepdims=True)
        f_col = jnp.sum(jnp.where(lane == GATE_F_LANE + hd, slab, 0.0), axis=-1, keepdims=True)
        lf_col = _log_sigmoid(f_col + fb_ref[hd])

        lf_row = jnp.sum(jnp.where(eye, lf_col, 0.0), axis=0, keepdims=True)
        ig_row = jnp.sum(jnp.where(eye, i_col, 0.0), axis=0, keepdims=True)
        a_col = jnp.sum(jnp.where(tri, lf_row, 0.0), axis=1, keepdims=True)
        a_row = jnp.sum(jnp.where(r_i <= c_i, lf_col, 0.0), axis=0, keepdims=True)
        m_prev = m_sc[hd]

        dlog = jnp.where(tri, a_col - a_row + ig_row, NEG)
        inter = a_col + m_prev
        mt = jnp.maximum(inter, jnp.max(dlog, axis=-1, keepdims=True))
        wm = jnp.exp(dlog - mt) * _dot_nt(q, ks)
        e_col = jnp.exp(inter - mt)
        num = e_col * jnp.dot(q, c_sc[hd].astype(BF16), preferred_element_type=F32) \
            + jnp.dot(wm.astype(BF16), v, preferred_element_type=F32)
        qn = jnp.sum(q.astype(F32) * n_sc[hd], axis=-1, keepdims=True)
        den = e_col * qn + jnp.sum(wm, axis=-1, keepdims=True)
        hh = num / jnp.maximum(jnp.abs(den), jnp.exp(-mt))

        a_last = jnp.sum(lf_row, axis=-1, keepdims=True)
        gs = a_last - a_col + i_col
        m_new = jnp.maximum(a_last + m_prev, jnp.max(gs, axis=0, keepdims=True))
        decay = jnp.exp(a_last + m_prev - m_new)
        wk = jnp.exp(gs - m_new) * ks32
        c_sc[hd] = decay * c_sc[hd] + _dot_tn(wk.astype(BF16), v)
        n_sc[hd] = decay * n_sc[hd] + jnp.sum(wk, axis=0, keepdims=True)
        m_sc[hd] = m_new

        y = hh * lax.rsqrt(jnp.mean(hh * hh, axis=-1, keepdims=True) + RMS_EPS) * nw_ref[:, pl.ds(off, dh)]
        o_ref[0, :, pl.ds(off, dh)] = (y * _sigmoid(ob_ref[0, :, pl.ds(off, dh)].astype(F32))).astype(BF16)
        return carry

    lax.fori_loop(0, MLSTM_HEADS, head, 0)


def mlstm(qk, p3, gates3, f_bias, norm_w, v_col0, o_col0):
    b, s, _ = qk.shape
    cl = min(CHUNK, s)
    dh = MLSTM_DIM
    nh = MLSTM_HEADS
    w = nh * dh
    assert v_col0 % w == 0 and o_col0 % w == 0 and s % cl == 0
    vb, ob = v_col0 // w, o_col0 // w
    return pl.pallas_call(
        functools.partial(_mlstm_kernel, cl=cl),
        grid=(b, s // cl),
        in_specs=[pl.BlockSpec(memory_space=pltpu.SMEM),
                  pl.BlockSpec((1, cl, w), lambda bi, c: (bi, c, 0)),
                  pl.BlockSpec((1, cl, w), lambda bi, c: (bi, c, 1)),
                  pl.BlockSpec((1, cl, w), lambda bi, c: (bi, c, vb)),
                  pl.BlockSpec((1, cl, w), lambda bi, c: (bi, c, ob)),
                  pl.BlockSpec((1, cl, LANES), lambda bi, c: (bi, c, 0)),
                  pl.BlockSpec((1, w), lambda bi, c: (0, 0))],
        out_specs=pl.BlockSpec((1, cl, w), lambda bi, c: (bi, c, 0)),
        out_shape=jax.ShapeDtypeStruct((b, s, w), BF16),
        scratch_shapes=[pltpu.VMEM((nh, dh, dh), F32), pltpu.VMEM((nh, 1, dh), F32), pltpu.VMEM((nh, 1, 1), F32)],
        compiler_params=_cparams(("parallel", "arbitrary")),
        name="mlstm",
    )(f_bias.astype(F32), qk, qk, p3, p3, gates3, norm_w.reshape(1, w).astype(F32))


def _ret_kernel(cd_ref, q_ref, k_ref, v_ref, g_ref, cos_ref, sin_ref, nw_ref, dm_ref, xi_ref, zeta_ref,
                o_ref, r_sc, *, cl):
    @pl.when(pl.program_id(1) == 0)
    def _():
        r_sc[...] = jnp.zeros(r_sc.shape, F32)

    cos = cos_ref[0]
    sin = sin_ref[0]
    dk, dv = RET_QK_DIM, RET_V_DIM
    half = dk // 2
    scale = dk ** -0.5

    def rope(x):
        x1, x2 = x[:, :half], x[:, half:]
        return x1 * cos - x2 * sin, x1 * sin + x2 * cos

    def head(hd, carry):
        qo = pl.multiple_of(hd * dk, dk)
        vo = pl.multiple_of(hd * dv, dv)
        q1, q2 = rope(q_ref[0, :, pl.ds(qo, dk)].astype(F32))
        qr = jnp.concatenate([q1, q2], axis=1).astype(BF16)
        k1, k2 = rope(k_ref[0, :, pl.ds(qo, dk)].astype(F32))
        zeta = zeta_ref[hd] * scale
        kr = jnp.concatenate([k1 * scale, k2 * scale], axis=1).astype(BF16)
        kz = jnp.concatenate([k1 * zeta, k2 * zeta], axis=1).astype(BF16)
        v = v_ref[0, :, pl.ds(vo, dv)]

        inner = jnp.dot((_dot_nt(qr, kr) * dm_ref[hd]).astype(BF16), v, preferred_element_type=F32)
        xi = xi_ref[hd]
        cross = jnp.dot(qr, r_sc[hd].astype(BF16), preferred_element_type=F32)
        cross = cross * jnp.concatenate([xi] * (dv // LANES), axis=1)
        r_sc[hd] = cd_ref[hd] * r_sc[hd] + _dot_tn(kz, v)

        y = inner + cross
        y = y * lax.rsqrt(jnp.mean(y * y, axis=-1, keepdims=True) + RMS_EPS) * nw_ref[:, pl.ds(vo, dv)]
        gg = g_ref[0, :, pl.ds(vo, dv)].astype(F32)
        o_ref[0, :, pl.ds(vo, dv)] = (y * (gg * _sigmoid(gg))).astype(BF16)
        return carry

    lax.fori_loop(0, RET_HEADS, head, 0)


def retention(po3, cos, sin, norm_w):
    b, s, _ = po3.shape
    cl = min(CHUNK, s)
    nh, dk, dv = RET_HEADS, RET_QK_DIM, RET_V_DIM
    log_g = jnp.log1p(-jnp.exp2(-5.0 - jnp.arange(nh, dtype=F32)))
    idx = jnp.arange(cl, dtype=F32)
    diff = idx[:, None] - idx[None, :]
    dm = jnp.where(diff >= 0, jnp.exp(jnp.maximum(diff, 0.0) * log_g[:, None, None]), 0.0)
    xi = jnp.broadcast_to(jnp.exp((idx + 1.0) * log_g[:, None])[..., None], (nh, cl, LANES))
    zeta = jnp.broadcast_to(jnp.exp((cl - 1.0 - idx) * log_g[:, None])[..., None], (nh, cl, LANES))
    chunk_decay = jnp.exp(cl * log_g)
    qk_w, v_w = RET_QK_WIDTH, RET_V_WIDTH
    assert v_w == 2 * qk_w
    tab = pl.BlockSpec((1, cl, LANES), lambda bi, c: (bi, c, 0))

    def table(shape):
        return pl.BlockSpec(shape, lambda bi, c: (0, 0, 0))

    return pl.pallas_call(
        functools.partial(_ret_kernel, cl=cl),
        grid=(b, s // cl),
        in_specs=[pl.BlockSpec(memory_space=pltpu.SMEM),
                  pl.BlockSpec((1, cl, qk_w), lambda bi, c: (bi, c, 0)),
                  pl.BlockSpec((1, cl, qk_w), lambda bi, c: (bi, c, 1)),
                  pl.BlockSpec((1, cl, v_w), lambda bi, c: (bi, c, 1)),
                  pl.BlockSpec((1, cl, v_w), lambda bi, c: (bi, c, 2)),
                  tab, tab,
                  pl.BlockSpec((1, v_w), lambda bi, c: (0, 0)),
                  table((nh, cl, cl)), table((nh, cl, LANES)), table((nh, cl, LANES))],
        out_specs=pl.BlockSpec((1, cl, v_w), lambda bi, c: (bi, c, 0)),
        out_shape=jax.ShapeDtypeStruct((b, s, v_w), BF16),
        scratch_shapes=[pltpu.VMEM((nh, dk, dv), F32)],
        compiler_params=_cparams(("parallel", "arbitrary")),
        name="retention",
    )(chunk_decay, po3, po3, po3, po3, cos, sin, norm_w.reshape(1, v_w).astype(F32), dm, xi, zeta)


def _router_kernel(x_ref, g_ref, wr_ref, xn_ref, route_ref, cnt_ref, carry_sc):
    @pl.when(pl.program_id(0) == 0)
    def _():
        carry_sc[...] = jnp.zeros(carry_sc.shape, F32)

    x = x_ref[...]
    t = x.shape[0]
    y = x * lax.rsqrt(jnp.mean(x * x, axis=-1, keepdims=True) + RMS_EPS) * g_ref[...]
    xn_ref[...] = y.astype(BF16)
    logits = jnp.dot(y, wr_ref[...], precision=HIGHEST, preferred_element_type=F32)
    lane = lax.broadcasted_iota(I32, logits.shape, 1)
    lg = jnp.where(lane < N_EXPERTS, logits, LOWEST)
    v1 = jnp.max(lg, axis=-1, keepdims=True)
    i1 = jnp.min(jnp.where(lg == v1, lane, LANES), axis=-1, keepdims=True)
    lg2 = jnp.where(lane == i1, LOWEST, lg)
    v2 = jnp.max(lg2, axis=-1, keepdims=True)
    i2 = jnp.min(jnp.where(lg2 == v2, lane, LANES), axis=-1, keepdims=True)
    e2 = jnp.exp(v2 - v1)
    g1 = 1.0 / (1.0 + e2)
    g2 = e2 / (1.0 + e2)

    chosen = jnp.where((lane == i1) | (lane == i2), 1.0, 0.0)
    r_i = lax.broadcasted_iota(I32, (t, t), 0)
    c_i = lax.broadcasted_iota(I32, (t, t), 1)
    tri = jnp.where(c_i <= r_i, 1.0, 0.0).astype(BF16)
    seen = jnp.dot(tri, chosen.astype(BF16), preferred_element_type=F32) + carry_sc[...]
    rank1 = jnp.sum(jnp.where(lane == i1, seen, 0.0), axis=-1, keepdims=True) - 1.0
    rank2 = jnp.sum(jnp.where(lane == i2, seen, 0.0), axis=-1, keepdims=True) - 1.0
    total = seen[t - 1:t, :]
    carry_sc[...] = total
    cnt_ref[...] = jnp.broadcast_to(total, cnt_ref.shape)

    route = jnp.where(lane == 0, i1.astype(F32), 0.0)
    route = jnp.where(lane == 1, i2.astype(F32), route)
    route = jnp.where(lane == 2, g1, route)
    route = jnp.where(lane == 3, g2, route)
    route = jnp.where(lane == 4, rank1, route)
    route = jnp.where(lane == 5, rank2, route)
    route_ref[...] = route


def moe_route(x, g, w_router, tm=256):
    m, d = x.shape
    tm = _tile(m, tm, 16)
    wr = jnp.zeros((d, LANES), F32).at[:, :N_EXPERTS].set(w_router.astype(F32))
    return pl.pallas_call(
        _router_kernel,
        grid=(m // tm,),
        in_specs=[pl.BlockSpec((tm, d), lambda i: (i, 0)),
                  pl.BlockSpec((1, d), lambda i: (0, 0)),
                  pl.BlockSpec((d, LANES), lambda i: (0, 0))],
        out_specs=[pl.BlockSpec((tm, d), lambda i: (i, 0)),
                   pl.BlockSpec((tm, LANES), lambda i: (i, 0)),
                   pl.BlockSpec((8, LANES), lambda i: (0, 0))],
        out_shape=[jax.ShapeDtypeStruct((m, d), BF16),
                   jax.ShapeDtypeStruct((m, LANES), F32),
                   jax.ShapeDtypeStruct((8, LANES), F32)],
        scratch_shapes=[pltpu.VMEM((1, LANES), F32)],
        compiler_params=_cparams(("arbitrary",)),
        name="moe_route",
    )(x, g.reshape(1, d).astype(F32), wr)


def _dispatch_kernel(idx_ref, src_ref, o_ref, sem, *, rows):
    base = pl.program_id(0) * rows

    def issue(r8, carry):
        for u in range(8):
            r = r8 * 8 + u
            pltpu.make_async_copy(src_ref.at[idx_ref[base + r]], o_ref.at[r], sem).start(priority=u % 2)
        return carry

    lax.fori_loop(0, rows // 8, issue, 0)
    pltpu.make_async_copy(src_ref.at[pl.ds(0, rows)], o_ref, sem).wait()


def moe_dispatch(xn, row_tok, rows=2048):
    m, d = xn.shape
    n_rows = row_tok.shape[0]
    rows = _tile(n_rows, rows, 8)
    src = xn.reshape(m, d // LANES, LANES)
    out = pl.pallas_call(
        functools.partial(_dispatch_kernel, rows=rows),
        grid_spec=pltpu.PrefetchScalarGridSpec(
            num_scalar_prefetch=1,
            grid=(n_rows // rows,),
            in_specs=[pl.BlockSpec(memory_space=pl.ANY)],
            out_specs=pl.BlockSpec((rows, d // LANES, LANES), lambda i, idx: (i, 0, 0)),
            scratch_shapes=[pltpu.SemaphoreType.DMA(())],
        ),
        out_shape=jax.ShapeDtypeStruct((n_rows, d // LANES, LANES), xn.dtype),
        compiler_params=_cparams(("arbitrary",)),
        name="moe_dispatch",
    )(row_tok, src)
    return out.reshape(n_rows, d)


def _combine_kernel(d1_ref, d2_ref, x_ref, route_ref, yb_ref, o_ref, buf1, buf2, sem, *, rows):
    base = pl.program_id(0) * rows

    def issue(r8, carry):
        r0 = pl.multiple_of(r8 * 8, 8)
        for u in range(8):
            pltpu.make_async_copy(yb_ref.at[pl.ds(d1_ref[base + r0 + u], 1)], buf1.at[pl.ds(r0 + u, 1)],
                                  sem.at[0]).start(priority=0)
            pltpu.make_async_copy(yb_ref.at[pl.ds(d2_ref[base + r0 + u], 1)], buf2.at[pl.ds(r0 + u, 1)],
                                  sem.at[1]).start(priority=1)
        return carry

    lax.fori_loop(0, rows // 8, issue, 0)
    pltpu.make_async_copy(yb_ref.at[pl.ds(0, rows)], buf1, sem.at[0]).wait()
    pltpu.make_async_copy(yb_ref.at[pl.ds(0, rows)], buf2, sem.at[1]).wait()
    route = route_ref[...]
    o_ref[...] = x_ref[...] + (buf1[...] * route[:, 2:3] + buf2[...] * route[:, 3:4])


def moe_combine(x, route, yb, dest1, dest2, rows=256):
    m, d = x.shape
    rows = _tile(m, rows, 8)
    return pl.pallas_call(
        functools.partial(_combine_kernel, rows=rows),
        grid_spec=pltpu.PrefetchScalarGridSpec(
            num_scalar_prefetch=2,
            grid=(m // rows,),
            in_specs=[pl.BlockSpec((rows, d), lambda i, a, b: (i, 0)),
                      pl.BlockSpec((rows, LANES), lambda i, a, b: (i, 0)),
                      pl.BlockSpec(memory_space=pl.ANY)],
            out_specs=pl.BlockSpec((rows, d), lambda i, a, b: (i, 0)),
            scratch_shapes=[pltpu.VMEM((rows, d), F32), pltpu.VMEM((rows, d), F32),
                            pltpu.SemaphoreType.DMA((2,))],
        ),
        out_shape=jax.ShapeDtypeStruct((m, d), F32),
        compiler_params=_cparams(("arbitrary",)),
        name="moe_combine",
    )(dest1, dest2, x, route, yb)


def moe_layer(x, norm_g, w_router, w_gate, w_up, w_down):
    m, d = x.shape
    tm = MOE_TM
    xn, route, cnt = moe_route(x, norm_g, w_router)
    expert = route[:, 0:2].astype(I32)
    rank = route[:, 4:6].astype(I32)
    counts = cnt[0, :N_EXPERTS].astype(I32)
    padded = (counts + tm - 1) // tm * tm
    pend = jnp.cumsum(padded)
    pstart = pend - padded
    dest = pstart[expert] + rank
    n_blk = -(-(2 * m) // tm) + N_EXPERTS
    n_rows = n_blk * tm
    tok = jnp.broadcast_to(jnp.arange(m, dtype=I32)[:, None], (m, 2))
    row_tok = jnp.zeros((n_rows,), I32).at[dest.reshape(-1)].set(tok.reshape(-1))
    nused = (pend[-1] // tm).astype(I32)
    blk = jnp.minimum(jnp.arange(n_blk, dtype=I32), nused - 1)
    blk_e = jnp.minimum(jnp.sum(pend[None, :] <= (blk * tm)[:, None], axis=1), N_EXPERTS - 1).astype(I32)
    nu = nused.reshape(1)

    xb = moe_dispatch(xn, row_tok)
    tf = _tile(w_gate.shape[2], 512, LANES)
    hidden = gmm_swiglu(xb, w_gate, w_up, blk_e, nu, tm=tm, tn=tf)
    yb = gmm(hidden, w_down, blk_e, nu, tm=tm, tn=_tile(d, 1024, LANES), out_dtype=F32, single_buffer_w=True)
    return moe_combine(x, route, yb, dest[:, 0], dest[:, 1])


def even_layer(x2, b, s, pos, w_norm, w_in, q_gain, k_gain, w_cmp_k, w_cmp_v, pe_cmp, conv_w, f_bias, m_norm,
               w_out, w_norm_ffn, w_gate, w_up, w_down):
    n, dm = x2.shape
    g, d = NSA_GROUPS, HEAD_DIM
    o_gate = NSA_SLAB
    o_qb = o_gate + NSA_HEADS * 3
    o_if = o_qb + 3 * MLSTM_WIDTH
    o_ob = o_if + 2 * MLSTM_HEADS
    w_ml = jnp.concatenate([w_in[:, o_qb:o_if], w_in[:, o_ob:o_ob + MLSTM_WIDTH]], axis=1)
    per_g = NSA_HPG * 3
    zeros = functools.partial(jnp.zeros, dtype=w_in.dtype)
    w_small = jnp.concatenate(
        [w_in[:, o_gate:o_gate + per_g], zeros((dm, GATE_I_LANE - per_g)),
         w_in[:, o_if:o_ob], zeros((dm, LANES - GATE_I_LANE - 2 * MLSTM_HEADS)),
         w_in[:, o_gate + per_g:o_qb], zeros((dm, LANES - per_g))], axis=1)

    h = rmsnorm_rows(x2, w_norm)
    p3 = mm(h, w_in[:, :o_gate], out_dtype=BF16, tn=NSA_SLAB // 2).reshape(b, s, NSA_SLAB)
    pm3 = mm(h, w_ml, out_dtype=BF16, tn=1024).reshape(b, s, 4 * MLSTM_WIDTH)
    gates3 = mm(h, w_small, out_dtype=F32, tn=2 * LANES).reshape(b, s, 2 * LANES)

    cos_n, sin_n = trig_tables(pos, _nsa_inv_lane())
    q_t, k_n, v_t = nsa_prep(p3, cos_n.reshape(b, s, LANES), sin_n.reshape(b, s, LANES), q_gain, k_gain)
    nh = s // CMP_STRIDE
    cmp_pos = jnp.concatenate([pos[:, CMP_BLOCK - 1::CMP_STRIDE][:, :nh - 1], pos[:, -1:]], axis=1)
    cos_c, sin_c = trig_tables(cmp_pos, _nsa_inv_lane())

    def cmp_blocks(col0):
        tok = p3[:, :, col0:col0 + NSA_KV_WIDTH].reshape(b, nh, CMP_STRIDE, g, d)
        halves = tok.transpose(0, 3, 1, 2, 4).reshape(b, g, nh, CMP_STRIDE * d)
        nxt = jnp.concatenate([halves[:, :, 1:], jnp.zeros_like(halves[:, :, :1])], axis=2)
        return jnp.concatenate([halves, nxt], axis=-1).reshape(b * g * nh, CMP_BLOCK * d)

    tabs = (cos_c.reshape(b, nh, LANES), sin_c.reshape(b, nh, LANES))
    kc = compress(cmp_blocks(NSA_WIDTH), pe_cmp, w_cmp_k, k_gain, *tabs, is_key=True, rows_per_seq=nh, groups=g)
    vc = compress(cmp_blocks(NSA_WIDTH + NSA_KV_WIDTH), pe_cmp, w_cmp_v, k_gain, *tabs, is_key=False,
                  rows_per_seq=nh, groups=g)
    o_nsa = nsa_attention(q_t, k_n, v_t, kc, vc, gates3)

    qk = conv_silu(pm3, conv_w, 0, 2 * MLSTM_WIDTH)
    h_b = mlstm(qk, pm3, gates3, f_bias, m_norm, 2 * MLSTM_WIDTH, 3 * MLSTM_WIDTH)

    mixed = jnp.concatenate([o_nsa, h_b], axis=-1).reshape(n, NSA_WIDTH + MLSTM_WIDTH)
    x2 = mm(mixed, w_out, out_dtype=F32, residual=x2, tm=1024, tn=1024)
    hf = rmsnorm_rows(x2, w_norm_ffn)
    hidden = mm_swiglu(hf, w_gate, w_up, tm=1024)
    return mm(hidden, w_down, out_dtype=F32, residual=x2, tm=512, tn=1024)


def odd_layer(x2, b, s, pos, w_norm, w_in, r_norm, w_out, w_norm_ffn, w_router, e_gate, e_up, e_down):
    n, dm = x2.shape
    h = rmsnorm_rows(x2, w_norm)
    po3 = mm(h, w_in, out_dtype=BF16, tn=1024).reshape(b, s, -1)
    cos_r, sin_r = trig_tables(pos, _ret_inv_lane())
    y = retention(po3, cos_r.reshape(b, s, LANES), sin_r.reshape(b, s, LANES), r_norm)
    x2 = mm(y.reshape(n, RET_V_WIDTH), w_out, out_dtype=F32, residual=x2, tm=512, tn=1024)
    return moe_layer(x2, w_norm_ffn, w_router, e_gate, e_up, e_down)


def kernel(x, positions, norm_mix_even, w_in_even, nsa_q_gain, nsa_k_gain, w_cmp_k, w_cmp_v, pe_cmp, mlstm_conv, mlstm_f_bias, mlstm_norm, w_out_even, norm_ffn_even, ffn_gate, ffn_up, ffn_down, norm_mix_odd, w_in_odd, ret_norm, w_out_odd, norm_ffn_odd, w_router, exp_gate, exp_up, exp_down):
    b, s, dm = x.shape
    depth = norm_mix_even.shape[0] + norm_mix_odd.shape[0]
    x2 = x.reshape(b * s, dm)
    for layer in range(depth):
        j = layer // 2
        if layer % 2 == 0:
            x2 = even_layer(x2, b, s, positions, norm_mix_even[j], w_in_even[j], nsa_q_gain[j], nsa_k_gain[j],
                            w_cmp_k[j], w_cmp_v[j], pe_cmp[j], mlstm_conv[j], mlstm_f_bias[j], mlstm_norm[j],
                            w_out_even[j], norm_ffn_even[j], ffn_gate[j], ffn_up[j], ffn_down[j])
        else:
            x2 = odd_layer(x2, b, s, positions, norm_mix_odd[j], w_in_odd[j], ret_norm[j], w_out_odd[j],
                           norm_ffn_odd[j], w_router[j], exp_gate[j], exp_up[j], exp_down[j])
    return x2.reshape(b, s, dm)
```

```python
import functools

import numpy as np
import jax
import jax.numpy as jnp
from jax import lax
from jax.experimental import pallas as pl
from jax.experimental.pallas import tpu as pltpu

F32 = jnp.float32
BF16 = jnp.bfloat16
I32 = jnp.int32
HIGHEST = lax.Precision.HIGHEST

HEAD_DIM = 128
NSA_HEADS = 8
NSA_GROUPS = 2
NSA_HPG = NSA_HEADS // NSA_GROUPS
NSA_WIDTH = NSA_HEADS * HEAD_DIM
NSA_KV_WIDTH = NSA_GROUPS * HEAD_DIM
NSA_SLAB = NSA_WIDTH + 6 * NSA_KV_WIDTH
CMP_BLOCK = 32
CMP_STRIDE = 16
SEL_BLOCK = 64
SEL_TOPN = 16
WINDOW = 512
ROPE_DIM = HEAD_DIM // 4
ROPE_THETA = 500000.0
SEL_FORCE = 1.0e6
NEG = -1.0e30
LOWEST = -3.0e38
MLSTM_HEADS = 4
MLSTM_DIM = 256
MLSTM_WIDTH = MLSTM_HEADS * MLSTM_DIM
CONV_WIDTH = 4
RET_HEADS = 8
RET_QK_DIM = 256
RET_V_DIM = 512
RET_QK_WIDTH = RET_HEADS * RET_QK_DIM
RET_V_WIDTH = RET_HEADS * RET_V_DIM
RET_ROPE_THETA = 10000.0
N_EXPERTS = 8
RMS_EPS = 1e-6

LANES = 128
BF16_SUBLANES = 16
V7X_VMEM_BYTES = 64 * 1024 * 1024
VMEM_LIMIT = V7X_VMEM_BYTES - 8 * 1024 * 1024

CHUNK = 256
NSA_TQ = 128
LOG2_E = 1.4426950408889634
V_ROWS = HEAD_DIM + BF16_SUBLANES
NSA_KT = 256
SOFTMAX_ROWS = 64
MOE_TM = 512
GATE_I_LANE = 16
GATE_F_LANE = 20


def _cparams(sem, vmem=VMEM_LIMIT):
    return pltpu.CompilerParams(dimension_semantics=sem, vmem_limit_bytes=vmem)


def _tile(n, target, quantum):
    if n <= target:
        return n
    t = (target // quantum) * quantum
    while t > quantum and n % t:
        t -= quantum
    assert n % t == 0, (n, target, quantum)
    return t


def _sigmoid(x):
    return 1.0 / (1.0 + jnp.exp(-x))


def _dot_nt(a, b):
    return lax.dot_general(a, b, (((1,), (1,)), ((), ())), preferred_element_type=F32)


def _dot_tn(a, b):
    return lax.dot_general(a, b, (((0,), (0,)), ((), ())), preferred_element_type=F32)


def _rmsnorm_kernel(x_ref, g_ref, o_ref):
    x = x_ref[...]
    y = x * lax.rsqrt(jnp.mean(x * x, axis=-1, keepdims=True) + RMS_EPS)
    o_ref[...] = (y * g_ref[...]).astype(o_ref.dtype)


def rmsnorm_rows(x, g, tm=512):
    m, d = x.shape
    tm = _tile(m, tm, 8)
    return pl.pallas_call(
        _rmsnorm_kernel,
        grid=(m // tm,),
        in_specs=[pl.BlockSpec((tm, d), lambda i: (i, 0)), pl.BlockSpec((1, d), lambda i: (0, 0))],
        out_specs=pl.BlockSpec((tm, d), lambda i: (i, 0)),
        out_shape=jax.ShapeDtypeStruct((m, d), BF16),
        compiler_params=_cparams(("parallel",)),
        name="rmsnorm",
    )(x, g.reshape(1, d).astype(F32))


def _gmm_kernel(be_ref, nu_ref, x_ref, w_ref, *rest, has_res):
    if has_res:
        r_ref, o_ref, wb_ref = rest
    else:
        o_ref, wb_ref = rest
    i = pl.program_id(1)
    changed = be_ref[i] != be_ref[jnp.maximum(i - 1, 0)]

    @pl.when((i == 0) | changed)
    def _():
        wb_ref[...] = w_ref[0].astype(BF16)

    @pl.when(i < nu_ref[0])
    def _():
        acc = jnp.dot(x_ref[...], wb_ref[...], preferred_element_type=F32)
        if has_res:
            acc = r_ref[...] + acc
        o_ref[...] = acc.astype(o_ref.dtype)

    @pl.when(i >= nu_ref[0])
    def _():
        o_ref[...] = jnp.zeros_like(o_ref)


def _weight_spec(k, tn, single_buffer):
    mode = dict(pipeline_mode=pl.Buffered(1)) if single_buffer else {}
    return pl.BlockSpec((1, k, tn), lambda j, i, be, nu: (be[i], 0, j), **mode)


def gmm(x, w, blk_e, nused, *, tm, tn, out_dtype, residual=None, single_buffer_w=False):
    m, k = x.shape
    e, k2, n = w.shape
    assert k == k2 and m % tm == 0 and n % tn == 0
    nb = m // tm
    in_specs = [
        pl.BlockSpec((tm, k), lambda j, i, be, nu: (jnp.minimum(i, nu[0] - 1), 0)),
        _weight_spec(k, tn, single_buffer_w),
    ]
    args = [x, w]
    if residual is not None:
        in_specs.append(pl.BlockSpec((tm, tn), lambda j, i, be, nu: (i, j)))
        args.append(residual)
    return pl.pallas_call(
        functools.partial(_gmm_kernel, has_res=residual is not None),
        grid_spec=pltpu.PrefetchScalarGridSpec(
            num_scalar_prefetch=2,
            grid=(n // tn, nb),
            in_specs=in_specs,
            out_specs=pl.BlockSpec((tm, tn), lambda j, i, be, nu: (i, j)),
            scratch_shapes=[pltpu.VMEM((k, tn), BF16)],
        ),
        out_shape=jax.ShapeDtypeStruct((m, n), out_dtype),
        compiler_params=_cparams(("arbitrary", "arbitrary")),
        name="gmm",
    )(blk_e, nused, *args)


def _gmm_swiglu_kernel(be_ref, nu_ref, x_ref, wg_ref, wu_ref, o_ref, wgb_ref, wub_ref):
    i = pl.program_id(1)
    changed = be_ref[i] != be_ref[jnp.maximum(i - 1, 0)]

    @pl.when((i == 0) | changed)
    def _():
        wgb_ref[...] = wg_ref[0].astype(BF16)
        wub_ref[...] = wu_ref[0].astype(BF16)

    @pl.when(i < nu_ref[0])
    def _():
        x = x_ref[...]
        g = jnp.dot(x, wgb_ref[...], preferred_element_type=F32)
        u = jnp.dot(x, wub_ref[...], preferred_element_type=F32)
        o_ref[...] = (g * _sigmoid(g) * u).astype(o_ref.dtype)

    @pl.when(i >= nu_ref[0])
    def _():
        o_ref[...] = jnp.zeros_like(o_ref)


def gmm_swiglu(x, wg, wu, blk_e, nused, *, tm, tn, single_buffer_w=False):
    m, k = x.shape
    e, k2, n = wg.shape
    assert k == k2 and wu.shape == wg.shape and m % tm == 0 and n % tn == 0
    nb = m // tm
    w_spec = _weight_spec(k, tn, single_buffer_w)
    return pl.pallas_call(
        _gmm_swiglu_kernel,
        grid_spec=pltpu.PrefetchScalarGridSpec(
            num_scalar_prefetch=2,
            grid=(n // tn, nb),
            in_specs=[pl.BlockSpec((tm, k), lambda j, i, be, nu: (jnp.minimum(i, nu[0] - 1), 0)),
                      w_spec, w_spec],
            out_specs=pl.BlockSpec((tm, tn), lambda j, i, be, nu: (i, j)),
            scratch_shapes=[pltpu.VMEM((k, tn), BF16), pltpu.VMEM((k, tn), BF16)],
        ),
        out_shape=jax.ShapeDtypeStruct((m, n), BF16),
        compiler_params=_cparams(("arbitrary", "arbitrary")),
        name="gmm_swiglu",
    )(blk_e, nused, x, wg, wu)


def _dense_blocks(m, tm):
    nb = m // tm
    return jnp.zeros((nb,), I32), jnp.full((1,), nb, I32)


def mm(x, w, *, out_dtype, residual=None, tm=2048, tn=512):
    m, k = x.shape
    n = w.shape[1]
    tm = _tile(m, tm, 16)
    tn = _tile(n, tn, LANES)
    be, nu = _dense_blocks(m, tm)
    return gmm(x, w[None], be, nu, tm=tm, tn=tn, out_dtype=out_dtype, residual=residual, single_buffer_w=True)


def mm_swiglu(x, wg, wu, *, tm=2048, tn=512):
    m = x.shape[0]
    tm = _tile(m, tm, 16)
    tn = _tile(wg.shape[1], tn, LANES)
    be, nu = _dense_blocks(m, tm)
    return gmm_swiglu(x, wg[None], wu[None], be, nu, tm=tm, tn=tn, single_buffer_w=True)


def _trig_kernel(pos_ref, inv_ref, cos_ref, sin_ref):
    ang = pos_ref[...] * inv_ref[...]
    cos_ref[...] = jnp.cos(ang)
    sin_ref[...] = jnp.sin(ang)


def trig_tables(pos, inv_lane):
    r = pos.size
    pos_b = jnp.broadcast_to(pos.astype(F32).reshape(r, 1), (r, LANES))
    tr = _tile(r, 512, 8)
    spec = pl.BlockSpec((tr, LANES), lambda i: (i, 0))
    return pl.pallas_call(
        _trig_kernel,
        grid=(r // tr,),
        in_specs=[spec, pl.BlockSpec((1, LANES), lambda i: (0, 0))],
        out_specs=[spec, spec],
        out_shape=[jax.ShapeDtypeStruct((r, LANES), F32)] * 2,
        compiler_params=_cparams(("parallel",)),
        name="trig_tables",
    )(pos_b, inv_lane.reshape(1, LANES))


def _nsa_inv_lane():
    half = ROPE_DIM // 2
    inv = jnp.power(jnp.float32(ROPE_THETA), -jnp.arange(half, dtype=F32) * (2.0 / ROPE_DIM))
    return jnp.concatenate([inv, inv, jnp.zeros((LANES - ROPE_DIM,), F32)])


def _ret_inv_lane():
    half = RET_QK_DIM // 2
    return jnp.power(jnp.float32(RET_ROPE_THETA), -jnp.arange(half, dtype=F32) * (2.0 / RET_QK_DIM))


def _norm_rope_head(x, gain, cos, sin):
    half = ROPE_DIM // 2
    y = x * lax.rsqrt(jnp.mean(x * x, axis=-1, keepdims=True) + RMS_EPS) * gain
    lane = lax.broadcasted_iota(I32, y.shape, 1)
    from_hi = jnp.where(lane < half, -sin, 0.0)
    from_lo = jnp.where((lane >= half) & (lane < ROPE_DIM), sin, 0.0)
    return (y * cos + pltpu.roll(y, LANES - half, 1) * from_hi + pltpu.roll(y, half, 1) * from_lo)


def _nsa_prep_kernel(p_ref, cos_ref, sin_ref, qg_ref, kg_ref, qt_ref, kn_ref, vt_ref):
    cos = cos_ref[0]
    sin = sin_ref[0]
    scale = HEAD_DIM ** -0.5 * LOG2_E
    d = HEAD_DIM
    g = NSA_GROUPS
    for hd in range(NSA_HEADS):
        q = _norm_rope_head(p_ref[0, :, hd * d:(hd + 1) * d].astype(F32), qg_ref[...], cos, sin)
        qt_ref[0, hd * d:(hd + 1) * d, :] = (q * scale).T.astype(BF16)
    for n, slab in enumerate((2, 4)):
        for gi in range(g):
            off = NSA_WIDTH + slab * NSA_KV_WIDTH + gi * d
            k = _norm_rope_head(p_ref[0, :, off:off + d].astype(F32), kg_ref[...], cos, sin)
            kn_ref[0, :, (n * g + gi) * d:(n * g + gi + 1) * d] = k.astype(BF16)
    for n, slab in enumerate((3, 5)):
        for gi in range(g):
            off = NSA_WIDTH + slab * NSA_KV_WIDTH + gi * d
            v = p_ref[0, :, off:off + d].astype(F32)
            r0 = (n * g + gi) * V_ROWS
            vt_ref[0, r0:r0 + d, :] = v.T.astype(BF16)
            vt_ref[0, r0 + d:r0 + V_ROWS, :] = jnp.ones((V_ROWS - d, v.shape[0]), BF16)


def nsa_prep(p3, cos, sin, q_gain, k_gain):
    b, s, _ = p3.shape
    t = _tile(s, 256, LANES)
    d = HEAD_DIM
    tab = pl.BlockSpec((1, t, LANES), lambda bi, i: (bi, i, 0))
    gain = pl.BlockSpec((1, d), lambda bi, i: (0, 0))
    kv = 2 * NSA_KV_WIDTH
    vr = 2 * NSA_GROUPS * V_ROWS
    return pl.pallas_call(
        _nsa_prep_kernel,
        grid=(b, s // t),
        in_specs=[pl.BlockSpec((1, t, NSA_SLAB), lambda bi, i: (bi, i, 0)), tab, tab, gain, gain],
        out_specs=[pl.BlockSpec((1, NSA_WIDTH, t), lambda bi, i: (bi, 0, i)),
                   pl.BlockSpec((1, t, kv), lambda bi, i: (bi, i, 0)),
                   pl.BlockSpec((1, vr, t), lambda bi, i: (bi, 0, i))],
        out_shape=[jax.ShapeDtypeStruct((b, NSA_WIDTH, s), BF16),
                   jax.ShapeDtypeStruct((b, s, kv), BF16),
                   jax.ShapeDtypeStruct((b, vr, s), BF16)],
        compiler_params=_cparams(("parallel", "parallel")),
        name="nsa_prep",
    )(p3, cos, sin, q_gain.reshape(1, d), k_gain.reshape(1, d))


def _compress_kernel(blk_ref, pe_ref, w_ref, kg_ref, cos_ref, sin_ref, o_ref, *, is_key):
    a = (blk_ref[...].astype(F32) + pe_ref[...]).astype(BF16)
    y = jnp.dot(a, w_ref[...].astype(BF16), preferred_element_type=F32)
    if is_key:
        o_ref[0] = _norm_rope_head(y, kg_ref[...], cos_ref[0], sin_ref[0]).astype(BF16)
    else:
        o_ref[0] = y.T.astype(BF16)


def compress(blk, pe, w, k_gain, cos_c, sin_c, *, is_key, rows_per_seq, groups):
    r, kdim = blk.shape
    t = rows_per_seq
    d = HEAD_DIM
    tab = pl.BlockSpec((1, t, LANES), lambda i: (i // groups, 0, 0))
    out_blk = (1, t, d) if is_key else (1, d, t)
    return pl.pallas_call(
        functools.partial(_compress_kernel, is_key=is_key),
        grid=(r // t,),
        in_specs=[pl.BlockSpec((t, kdim), lambda i: (i, 0)),
                  pl.BlockSpec((1, kdim), lambda i: (0, 0)),
                  pl.BlockSpec((kdim, d), lambda i: (0, 0)),
                  pl.BlockSpec((1, d), lambda i: (0, 0)),
                  tab, tab],
        out_specs=pl.BlockSpec(out_blk, lambda i: (i, 0, 0)),
        out_shape=jax.ShapeDtypeStruct((r // t,) + out_blk[1:], BF16),
        compiler_params=_cparams(("parallel",)),
        name="nsa_compress",
    )(blk, pe.reshape(1, kdim), w, k_gain.reshape(1, d), cos_c, sin_c)


def _nsa_attn_kernel(qt_ref, kc_ref, vct_ref, ks_ref, kw_ref, vst_ref, vwt_ref, et_ref, ovt_ref, gate_ref,
                     o_ref, acc_sc, val_sc, sa_sc, sb_sc, pa_sc, pb_sc, sc_sc, pc_sc, pg_sc, sw_sc, pw_sc, ow_sc,
                     *, tq, kt, wk, ns, n_top):
    i = pl.program_id(2)
    t0 = i * tq
    cols = NSA_HPG * tq
    d = HEAD_DIM
    q_t = jnp.concatenate([qt_ref[0, p * d:(p + 1) * d, :] for p in range(NSA_HPG)], axis=1)
    t_lane = t0 + (lax.broadcasted_iota(I32, (1, cols), 1) & (tq - 1))

    ncp = kc_ref.shape[1]
    rs = SOFTMAX_ROWS
    sc_sc[...] = jnp.dot(kc_ref[0], q_t, preferred_element_type=F32)
    w0 = pl.multiple_of(jnp.maximum(t0 + tq - wk, 0), LANES)
    sw_sc[...] = jnp.dot(kw_ref[0, pl.ds(w0, wk), :], q_t, preferred_element_type=F32)

    def cmask(r0):
        c_end = (r0 + lax.broadcasted_iota(I32, (rs, cols), 0)) * CMP_STRIDE + (CMP_BLOCK - 1)
        return c_end <= t_lane

    m_c = jnp.full((1, cols), NEG, F32)
    for r0 in range(0, ncp, rs):
        m_c = jnp.maximum(m_c, jnp.max(jnp.where(cmask(r0), sc_sc[r0:r0 + rs, :], NEG), axis=0, keepdims=True))
    den_c = jnp.zeros((1, cols), F32)
    for r0 in range(0, ncp, rs):
        e = jnp.where(cmask(r0), jnp.exp2(sc_sc[r0:r0 + rs, :] - m_c), 0.0)
        sc_sc[r0:r0 + rs, :] = e
        den_c = den_c + jnp.sum(e, axis=0, keepdims=True)
    inv_c = 1.0 / jnp.maximum(den_c, 1e-30)
    for r0 in range(0, ncp, rs):
        p = sc_sc[r0:r0 + rs, :] * inv_c
        pc_sc[r0:r0 + rs, :] = p.astype(BF16)
        p_grp = p[:, 0:tq]
        for hp in range(1, NSA_HPG):
            p_grp = p_grp + p[:, hp * tq:(hp + 1) * tq]
        pg_sc[r0:r0 + rs, :] = p_grp
    o_cmp = jnp.dot(vct_ref[0], pc_sc[...], preferred_element_type=F32)

    m_w = jnp.full((1, cols), NEG, F32)
    for r0 in range(0, wk, rs):
        kpos = w0 + r0 + lax.broadcasted_iota(I32, (rs, cols), 0)
        wmask = (kpos <= t_lane) & (kpos > t_lane - WINDOW)
        s = jnp.where(wmask, sw_sc[r0:r0 + rs, :], NEG)
        sw_sc[r0:r0 + rs, :] = s
        m_w = jnp.maximum(m_w, jnp.max(s, axis=0, keepdims=True))
    for r0 in range(0, wk, rs):
        pw_sc[r0:r0 + rs, :] = jnp.exp2(sw_sc[r0:r0 + rs, :] - m_w).astype(BF16)
    o_win = jnp.dot(vwt_ref[0, :, pl.ds(w0, wk)], pw_sc[...], preferred_element_type=F32)
    ow_sc[...] = o_win[:d] * (1.0 / o_win[d:d + 1])

    imp = jnp.dot(ovt_ref[...], pg_sc[...], precision=HIGHEST, preferred_element_type=F32)
    nsp = val_sc.shape[0]
    jb = lax.broadcasted_iota(I32, (nsp, tq), 0)
    cur = (t0 + lax.broadcasted_iota(I32, (1, tq), 1)) // SEL_BLOCK
    forced = (jb == 0) | (jb == cur) | (jb == cur - 1)
    val = jnp.where(jb <= cur, jnp.where(forced, SEL_FORCE, imp[:nsp]), NEG)
    val = jnp.where(jb < ns, val, LOWEST)
    val_sc[...] = val
    beaten = jnp.zeros((nsp, tq), F32)
    for j2 in range(ns):
        r = val_sc[j2:j2 + 1, :]
        ge = jnp.where(r >= val, 1.0, 0.0)
        gt = jnp.where(r > val, 1.0, 0.0)
        beaten = beaten + jnp.where(jb > j2, ge, gt)
    past = jb < t0 // SEL_BLOCK
    bias = jnp.where(past & (beaten < n_top), 0.0, jnp.where(jb < ns, NEG, 0.0)).astype(BF16)
    bias = jnp.concatenate([bias, jnp.zeros((LANES - nsp, tq), BF16)], axis=0)
    q_aug = jnp.concatenate([q_t, jnp.concatenate([bias] * NSA_HPG, axis=1)], axis=0)

    d0 = pl.multiple_of(t0, tq)
    s_d = jnp.dot(ks_ref[0, pl.ds(d0, tq), :], q_t, preferred_element_type=F32)
    s_d = jnp.where(d0 + lax.broadcasted_iota(I32, (tq, cols), 0) <= t_lane, s_d, NEG)
    m_d = jnp.max(s_d, axis=0, keepdims=True)
    p_d = jnp.exp2(s_d - m_d)
    acc_sc[...] = jnp.dot(vst_ref[0, :, pl.ds(d0, tq)], p_d.astype(BF16), preferred_element_type=F32)

    n_pairs = (t0 + 2 * kt - 1) // (2 * kt)
    last_a = jnp.maximum(n_pairs - 1, 0) * (2 * kt)

    def scores(k0):
        k0 = pl.multiple_of(k0, kt)
        k_aug = jnp.concatenate([ks_ref[0, pl.ds(k0, kt), :], et_ref[pl.ds(k0, kt), :]], axis=1)
        return jnp.dot(k_aug, q_aug, preferred_element_type=F32)

    def values(p_ref, k0):
        k0 = pl.multiple_of(k0, kt)
        return jnp.dot(vst_ref[0, :, pl.ds(k0, kt)], p_ref[...], preferred_element_type=F32)

    def softmax_update(s_ref, p_ref, m_prev):
        m_new = jnp.maximum(m_prev, jnp.max(s_ref[...], axis=0, keepdims=True))
        for r0 in range(0, kt, SOFTMAX_ROWS):
            p_ref[r0:r0 + SOFTMAX_ROWS, :] = jnp.exp2(s_ref[r0:r0 + SOFTMAX_ROWS, :] - m_new).astype(BF16)
        return m_new, jnp.exp2(m_prev - m_new)

    def sel_step(j, carry):
        m_run, alpha_b = carry
        k0 = j * (2 * kt)
        sb_sc[...] = scores(k0 + kt)
        acc_sc[...] = alpha_b * acc_sc[...] + values(pb_sc, jnp.maximum(k0 - kt, 0))
        m_run, alpha_a = softmax_update(sa_sc, pa_sc, m_run)
        sa_sc[...] = scores(jnp.minimum(k0 + 2 * kt, last_a))
        acc_sc[...] = alpha_a * acc_sc[...] + values(pa_sc, k0)
        return softmax_update(sb_sc, pb_sc, m_run)

    sa_sc[...] = scores(0)
    pb_sc[...] = jnp.zeros(pb_sc.shape, BF16)
    _, alpha_last = lax.fori_loop(0, n_pairs, sel_step, (m_d, jnp.ones((1, cols), F32)))
    acc = alpha_last * acc_sc[...] + values(pb_sc, last_a + kt)
    o_sel = acc[:d] * (1.0 / acc[d:d + 1])

    g_t = _sigmoid(gate_ref[0]).T
    for p in range(NSA_HPG):
        sl = slice(p * tq, (p + 1) * tq)
        o_t = (g_t[3 * p:3 * p + 1] * o_cmp[:, sl] + g_t[3 * p + 1:3 * p + 2] * o_sel[:, sl]
               + g_t[3 * p + 2:3 * p + 3] * ow_sc[:, sl])
        o_ref[0, :, p * d:(p + 1) * d] = o_t.T.astype(BF16)


def nsa_attention(q_t, k_n, v_t, kc, vc_t, gates3):
    b, _, s = q_t.shape
    g, d = NSA_GROUPS, HEAD_DIM
    ncp = kc.shape[1]
    tq = NSA_TQ
    kt = min(NSA_KT, s)
    wk = WINDOW + tq
    ns = s // SEL_BLOCK
    assert s % (2 * kt) == 0 and s >= wk and ns <= LANES and tq == LANES
    assert kt % SOFTMAX_ROWS == 0 and ncp % SOFTMAX_ROWS == 0 and wk % SOFTMAX_ROWS == 0
    n_top = min(SEL_TOPN, ns)
    cols = NSA_HPG * tq

    key_blk = np.arange(s) // SEL_BLOCK
    e_t = jnp.asarray(key_blk[:, None] == np.arange(LANES)[None, :], dtype=BF16)
    c_start = np.arange(ncp) * CMP_STRIDE
    j_start = np.arange(LANES) * SEL_BLOCK
    overlap = ((c_start[:, None] < j_start[None, :] + SEL_BLOCK) & (c_start[:, None] + CMP_BLOCK > j_start[None, :])
               & (np.arange(ncp)[:, None] < s // CMP_STRIDE - 1) & (np.arange(LANES)[None, :] < ns))
    overlap_t = jnp.asarray(overlap.T.astype(np.float32))

    q_rows = NSA_HPG * d
    return pl.pallas_call(
        functools.partial(_nsa_attn_kernel, tq=tq, kt=kt, wk=wk, ns=ns, n_top=n_top),
        grid=(b, g, s // tq),
        in_specs=[pl.BlockSpec((1, q_rows, tq), lambda bi, gi, i: (bi, gi, i)),
                  pl.BlockSpec((1, ncp, d), lambda bi, gi, i: (bi * g + gi, 0, 0)),
                  pl.BlockSpec((1, d, ncp), lambda bi, gi, i: (bi * g + gi, 0, 0)),
                  pl.BlockSpec((1, s, d), lambda bi, gi, i: (bi, 0, gi)),
                  pl.BlockSpec((1, s, d), lambda bi, gi, i: (bi, 0, g + gi)),
                  pl.BlockSpec((1, V_ROWS, s), lambda bi, gi, i: (bi, gi, 0)),
                  pl.BlockSpec((1, V_ROWS, s), lambda bi, gi, i: (bi, g + gi, 0)),
                  pl.BlockSpec((s, LANES), lambda bi, gi, i: (0, 0)),
                  pl.BlockSpec((LANES, ncp), lambda bi, gi, i: (0, 0)),
                  pl.BlockSpec((1, tq, LANES), lambda bi, gi, i: (bi, i, gi))],
        out_specs=pl.BlockSpec((1, tq, q_rows), lambda bi, gi, i: (bi, i, gi)),
        out_shape=jax.ShapeDtypeStruct((b, s, NSA_WIDTH), BF16),
        scratch_shapes=[pltpu.VMEM((V_ROWS, cols), F32), pltpu.VMEM((-(-ns // 8) * 8, tq), F32),
                        pltpu.VMEM((kt, cols), F32), pltpu.VMEM((kt, cols), F32),
                        pltpu.VMEM((kt, cols), BF16), pltpu.VMEM((kt, cols), BF16),
                        pltpu.VMEM((ncp, cols), F32), pltpu.VMEM((ncp, cols), BF16), pltpu.VMEM((ncp, tq), F32),
                        pltpu.VMEM((wk, cols), F32), pltpu.VMEM((wk, cols), BF16), pltpu.VMEM((d, cols), F32)],
        compiler_params=_cparams(("parallel", "parallel", "arbitrary")),
        name="nsa_attention",
    )(q_t, kc, vc_t, k_n, k_n, v_t, v_t, e_t, overlap_t, gates3)


def _conv_kernel(cur_ref, prev_ref, w_ref, o_ref, *, tc):
    i = pl.program_id(1)
    cur = cur_ref[0].astype(F32)
    prev = jnp.where(i > 0, prev_ref[0].astype(F32), 0.0)
    pad = prev.shape[0] // 2
    full = jnp.concatenate([prev[pad:], cur], axis=0)
    y = None
    for kk in range(CONV_WIDTH):
        off = pad - (CONV_WIDTH - 1) + kk
        term = w_ref[kk:kk + 1, :] * full[off:off + tc]
        y = term if y is None else y + term
    o_ref[0] = (y * _sigmoid(y)).astype(BF16)


def conv_silu(p3, w, col0, width):
    b, s, _ = p3.shape
    tc = _tile(s, 512, BF16_SUBLANES)
    cw = 512
    assert col0 % cw == 0 and width % cw == 0
    c0 = col0 // cw
    halo = BF16_SUBLANES
    return pl.pallas_call(
        functools.partial(_conv_kernel, tc=tc),
        grid=(b, s // tc, width // cw),
        in_specs=[pl.BlockSpec((1, tc, cw), lambda bi, i, j: (bi, i, c0 + j)),
                  pl.BlockSpec((1, halo, cw), lambda bi, i, j: (bi, jnp.maximum(i * (tc // halo) - 1, 0), c0 + j)),
                  pl.BlockSpec((CONV_WIDTH, cw), lambda bi, i, j: (0, j))],
        out_specs=pl.BlockSpec((1, tc, cw), lambda bi, i, j: (bi, i, j)),
        out_shape=jax.ShapeDtypeStruct((b, s, width), BF16),
        compiler_params=_cparams(("parallel", "parallel", "parallel")),
        name="mlstm_conv",
    )(p3, p3, w)


def _log_sigmoid(x):
    return jnp.minimum(x, 0.0) - jnp.log1p(jnp.exp(-jnp.abs(x)))


def _mlstm_kernel(fb_ref, q_ref, k_ref, v_ref, ob_ref, gate_ref, nw_ref, o_ref, c_sc, n_sc, m_sc, *, cl):
    @pl.when(pl.program_id(1) == 0)
    def _():
        c_sc[...] = jnp.zeros(c_sc.shape, F32)
        n_sc[...] = jnp.zeros(n_sc.shape, F32)
        m_sc[...] = jnp.zeros(m_sc.shape, F32)

    dh = MLSTM_DIM
    slab = gate_ref[0]
    lane = lax.broadcasted_iota(I32, slab.shape, 1)
    r_i = lax.broadcasted_iota(I32, (cl, cl), 0)
    c_i = lax.broadcasted_iota(I32, (cl, cl), 1)
    eye = r_i == c_i
    tri = c_i <= r_i

    def head(hd, carry):
        off = pl.multiple_of(hd * dh, dh)
        q = q_ref[0, :, pl.ds(off, dh)]
        v = v_ref[0, :, pl.ds(off, dh)]
        ks32 = k_ref[0, :, pl.ds(off, dh)].astype(F32) * (dh ** -0.5)
        ks = ks32.astype(BF16)
        i_col = jnp.sum(jnp.where(lane == GATE_I_LANE + hd, slab, 0.0), axis=-1, keepdims=True)
        f_col = jnp.sum(jnp.where(lane == GATE_F_LANE + hd, slab, 0.0), axis=-1, keepdims=True)
        lf_col = _log_sigmoid(f_col + fb_ref[hd])

        lf_row = jnp.sum(jnp.where(eye, lf_col, 0.0), axis=0, keepdims=True)
        ig_row = jnp.sum(jnp.where(eye, i_col, 0.0), axis=0, keepdims=True)
        a_col = jnp.sum(jnp.where(tri, lf_row, 0.0), axis=1, keepdims=True)
        a_row = jnp.sum(jnp.where(r_i <= c_i, lf_col, 0.0), axis=0, keepdims=True)
        m_prev = m_sc[hd]

        dlog = jnp.where(tri, a_col - a_row + ig_row, NEG)
        inter = a_col + m_prev
        mt = jnp.maximum(inter, jnp.max(dlog, axis=-1, keepdims=True))
        wm = jnp.exp(dlog - mt) * _dot_nt(q, ks)
        e_col = jnp.exp(inter - mt)
        num = e_col * jnp.dot(q, c_sc[hd].astype(BF16), preferred_element_type=F32) \
            + jnp.dot(wm.astype(BF16), v, preferred_element_type=F32)
        qn = jnp.sum(q.astype(F32) * n_sc[hd], axis=-1, keepdims=True)
        den = e_col * qn + jnp.sum(wm, axis=-1, keepdims=True)
        hh = num / jnp.maximum(jnp.abs(den), jnp.exp(-mt))

        a_last = jnp.sum(lf_row, axis=-1, keepdims=True)
        gs = a_last - a_col + i_col
        m_new = jnp.maximum(a_last + m_prev, jnp.max(gs, axis=0, keepdims=True))
        decay = jnp.exp(a_last + m_prev - m_new)
        wk = jnp.exp(gs - m_new) * ks32
        c_sc[hd] = decay * c_sc[hd] + _dot_tn(wk.astype(BF16), v)
        n_sc[hd] = decay * n_sc[hd] + jnp.sum(wk, axis=0, keepdims=True)
        m_sc[hd] = m_new

        y = hh * lax.rsqrt(jnp.mean(hh * hh, axis=-1, keepdims=True) + RMS_EPS) * nw_ref[:, pl.ds(off, dh)]
        o_ref[0, :, pl.ds(off, dh)] = (y * _sigmoid(ob_ref[0, :, pl.ds(off, dh)].astype(F32))).astype(BF16)
        return carry

    lax.fori_loop(0, MLSTM_HEADS, head, 0)


def mlstm(qk, p3, gates3, f_bias, norm_w, v_col0, o_col0):
    b, s, _ = qk.shape
    cl = min(CHUNK, s)
    dh = MLSTM_DIM
    nh = MLSTM_HEADS
    w = nh * dh
    assert v_col0 % w == 0 and o_col0 % w == 0 and s % cl == 0
    vb, ob = v_col0 // w, o_col0 // w
    return pl.pallas_call(
        functools.partial(_mlstm_kernel, cl=cl),
        grid=(b, s // cl),
        in_specs=[pl.BlockSpec(memory_space=pltpu.SMEM),
                  pl.BlockSpec((1, cl, w), lambda bi, c: (bi, c, 0)),
                  pl.BlockSpec((1, cl, w), lambda bi, c: (bi, c, 1)),
                  pl.BlockSpec((1, cl, w), lambda bi, c: (bi, c, vb)),
                  pl.BlockSpec((1, cl, w), lambda bi, c: (bi, c, ob)),
                  pl.BlockSpec((1, cl, LANES), lambda bi, c: (bi, c, 0)),
                  pl.BlockSpec((1, w), lambda bi, c: (0, 0))],
        out_specs=pl.BlockSpec((1, cl, w), lambda bi, c: (bi, c, 0)),
        out_shape=jax.ShapeDtypeStruct((b, s, w), BF16),
        scratch_shapes=[pltpu.VMEM((nh, dh, dh), F32), pltpu.VMEM((nh, 1, dh), F32), pltpu.VMEM((nh, 1, 1), F32)],
        compiler_params=_cparams(("parallel", "arbitrary")),
        name="mlstm",
    )(f_bias.astype(F32), qk, qk, p3, p3, gates3, norm_w.reshape(1, w).astype(F32))


def _ret_kernel(cd_ref, q_ref, k_ref, v_ref, g_ref, cos_ref, sin_ref, nw_ref, dm_ref, xi_ref, zeta_ref,
                o_ref, r_sc, *, cl):
    @pl.when(pl.program_id(1) == 0)
    def _():
        r_sc[...] = jnp.zeros(r_sc.shape, F32)

    cos = cos_ref[0]
    sin = sin_ref[0]
    dk, dv = RET_QK_DIM, RET_V_DIM
    half = dk // 2
    scale = dk ** -0.5

    def rope(x):
        x1, x2 = x[:, :half], x[:, half:]
        return x1 * cos - x2 * sin, x1 * sin + x2 * cos

    def head(hd, carry):
        qo = pl.multiple_of(hd * dk, dk)
        vo = pl.multiple_of(hd * dv, dv)
        q1, q2 = rope(q_ref[0, :, pl.ds(qo, dk)].astype(F32))
        qr = jnp.concatenate([q1, q2], axis=1).astype(BF16)
        k1, k2 = rope(k_ref[0, :, pl.ds(qo, dk)].astype(F32))
        zeta = zeta_ref[hd] * scale
        kr = jnp.concatenate([k1 * scale, k2 * scale], axis=1).astype(BF16)
        kz = jnp.concatenate([k1 * zeta, k2 * zeta], axis=1).astype(BF16)
        v = v_ref[0, :, pl.ds(vo, dv)]

        inner = jnp.dot((_dot_nt(qr, kr) * dm_ref[hd]).astype(BF16), v, preferred_element_type=F32)
        xi = xi_ref[hd]
        cross = jnp.dot(qr, r_sc[hd].astype(BF16), preferred_element_type=F32)
        cross = cross * jnp.concatenate([xi] * (dv // LANES), axis=1)
        r_sc[hd] = cd_ref[hd] * r_sc[hd] + _dot_tn(kz, v)

        y = inner + cross
        y = y * lax.rsqrt(jnp.mean(y * y, axis=-1, keepdims=True) + RMS_EPS) * nw_ref[:, pl.ds(vo, dv)]
        gg = g_ref[0, :, pl.ds(vo, dv)].astype(F32)
        o_ref[0, :, pl.ds(vo, dv)] = (y * (gg * _sigmoid(gg))).astype(BF16)
        return carry

    lax.fori_loop(0, RET_HEADS, head, 0)


def retention(po3, cos, sin, norm_w):
    b, s, _ = po3.shape
    cl = min(CHUNK, s)
    nh, dk, dv = RET_HEADS, RET_QK_DIM, RET_V_DIM
    log_g = jnp.log1p(-jnp.exp2(-5.0 - jnp.arange(nh, dtype=F32)))
    idx = jnp.arange(cl, dtype=F32)
    diff = idx[:, None] - idx[None, :]
    dm = jnp.where(diff >= 0, jnp.exp(jnp.maximum(diff, 0.0) * log_g[:, None, None]), 0.0)
    xi = jnp.broadcast_to(jnp.exp((idx + 1.0) * log_g[:, None])[..., None], (nh, cl, LANES))
    zeta = jnp.broadcast_to(jnp.exp((cl - 1.0 - idx) * log_g[:, None])[..., None], (nh, cl, LANES))
    chunk_decay = jnp.exp(cl * log_g)
    qk_w, v_w = RET_QK_WIDTH, RET_V_WIDTH
    assert v_w == 2 * qk_w
    tab = pl.BlockSpec((1, cl, LANES), lambda bi, c: (bi, c, 0))

    def table(shape):
        return pl.BlockSpec(shape, lambda bi, c: (0, 0, 0))

    return pl.pallas_call(
        functools.partial(_ret_kernel, cl=cl),
        grid=(b, s // cl),
        in_specs=[pl.BlockSpec(memory_space=pltpu.SMEM),
                  pl.BlockSpec((1, cl, qk_w), lambda bi, c: (bi, c, 0)),
                  pl.BlockSpec((1, cl, qk_w), lambda bi, c: (bi, c, 1)),
                  pl.BlockSpec((1, cl, v_w), lambda bi, c: (bi, c, 1)),
                  pl.BlockSpec((1, cl, v_w), lambda bi, c: (bi, c, 2)),
                  tab, tab,
                  pl.BlockSpec((1, v_w), lambda bi, c: (0, 0)),
                  table((nh, cl, cl)), table((nh, cl, LANES)), table((nh, cl, LANES))],
        out_specs=pl.BlockSpec((1, cl, v_w), lambda bi, c: (bi, c, 0)),
        out_shape=jax.ShapeDtypeStruct((b, s, v_w), BF16),
        scratch_shapes=[pltpu.VMEM((nh, dk, dv), F32)],
        compiler_params=_cparams(("parallel", "arbitrary")),
        name="retention",
    )(chunk_decay, po3, po3, po3, po3, cos, sin, norm_w.reshape(1, v_w).astype(F32), dm, xi, zeta)


def _router_kernel(x_ref, g_ref, wr_ref, xn_ref, route_ref, cnt_ref, carry_sc):
    @pl.when(pl.program_id(0) == 0)
    def _():
        carry_sc[...] = jnp.zeros(carry_sc.shape, F32)

    x = x_ref[...]
    t = x.shape[0]
    y = x * lax.rsqrt(jnp.mean(x * x, axis=-1, keepdims=True) + RMS_EPS) * g_ref[...]
    xn_ref[...] = y.astype(BF16)
    w = wr_ref[...]
    y_hi = y.astype(BF16)
    y_lo = (y - y_hi.astype(F32)).astype(BF16)
    w_hi = w.astype(BF16)
    w_lo = (w - w_hi.astype(F32)).astype(BF16)
    logits = (jnp.dot(y_hi, w_hi, preferred_element_type=F32) + jnp.dot(y_lo, w_hi, preferred_element_type=F32)
              + jnp.dot(y_hi, w_lo, preferred_element_type=F32))
    lane = lax.broadcasted_iota(I32, logits.shape, 1)
    lg = jnp.where(lane < N_EXPERTS, logits, LOWEST)
    v1 = jnp.max(lg, axis=-1, keepdims=True)
    i1 = jnp.min(jnp.where(lg == v1, lane, LANES), axis=-1, keepdims=True)
    lg2 = jnp.where(lane == i1, LOWEST, lg)
    v2 = jnp.max(lg2, axis=-1, keepdims=True)
    i2 = jnp.min(jnp.where(lg2 == v2, lane, LANES), axis=-1, keepdims=True)
    e2 = jnp.exp(v2 - v1)
    g1 = 1.0 / (1.0 + e2)
    g2 = e2 / (1.0 + e2)

    chosen = jnp.where((lane == i1) | (lane == i2), 1.0, 0.0)
    r_i = lax.broadcasted_iota(I32, (t, t), 0)
    c_i = lax.broadcasted_iota(I32, (t, t), 1)
    tri = jnp.where(c_i <= r_i, 1.0, 0.0).astype(BF16)
    seen = jnp.dot(tri, chosen.astype(BF16), preferred_element_type=F32) + carry_sc[...]
    rank1 = jnp.sum(jnp.where(lane == i1, seen, 0.0), axis=-1, keepdims=True) - 1.0
    rank2 = jnp.sum(jnp.where(lane == i2, seen, 0.0), axis=-1, keepdims=True) - 1.0
    total = seen[t - 1:t, :]
    carry_sc[...] = total
    cnt_ref[...] = jnp.broadcast_to(total, cnt_ref.shape)

    route = jnp.where(lane == 0, i1.astype(F32), 0.0)
    route = jnp.where(lane == 1, i2.astype(F32), route)
    route = jnp.where(lane == 2, g1, route)
    route = jnp.where(lane == 3, g2, route)
    route = jnp.where(lane == 4, rank1, route)
    route = jnp.where(lane == 5, rank2, route)
    route_ref[...] = route


def moe_route(x, g, w_router, tm=256):
    m, d = x.shape
    tm = _tile(m, tm, 16)
    wr = jnp.zeros((d, LANES), F32).at[:, :N_EXPERTS].set(w_router.astype(F32))
    return pl.pallas_call(
        _router_kernel,
        grid=(m // tm,),
        in_specs=[pl.BlockSpec((tm, d), lambda i: (i, 0)),
                  pl.BlockSpec((1, d), lambda i: (0, 0)),
                  pl.BlockSpec((d, LANES), lambda i: (0, 0))],
        out_specs=[pl.BlockSpec((tm, d), lambda i: (i, 0)),
                   pl.BlockSpec((tm, LANES), lambda i: (i, 0)),
                   pl.BlockSpec((8, LANES), lambda i: (0, 0))],
        out_shape=[jax.ShapeDtypeStruct((m, d), BF16),
                   jax.ShapeDtypeStruct((m, LANES), F32),
                   jax.ShapeDtypeStruct((8, LANES), F32)],
        scratch_shapes=[pltpu.VMEM((1, LANES), F32)],
        compiler_params=_cparams(("arbitrary",)),
        name="moe_route",
    )(x, g.reshape(1, d).astype(F32), wr)


def _dispatch_kernel(idx_ref, src_ref, o_ref, sem, *, rows):
    base = pl.program_id(0) * rows

    def issue(r8, carry):
        for u in range(8):
            r = r8 * 8 + u
            pltpu.make_async_copy(src_ref.at[idx_ref[base + r]], o_ref.at[r], sem).start(priority=u % 2)
        return carry

    lax.fori_loop(0, rows // 8, issue, 0)
    pltpu.make_async_copy(src_ref.at[pl.ds(0, rows)], o_ref, sem).wait()


def moe_dispatch(xn, row_tok, rows=2048):
    m, d = xn.shape
    n_rows = row_tok.shape[0]
    rows = _tile(n_rows, rows, 8)
    src = xn.reshape(m, d // LANES, LANES)
    out = pl.pallas_call(
        functools.partial(_dispatch_kernel, rows=rows),
        grid_spec=pltpu.PrefetchScalarGridSpec(
            num_scalar_prefetch=1,
            grid=(n_rows // rows,),
            in_specs=[pl.BlockSpec(memory_space=pl.ANY)],
            out_specs=pl.BlockSpec((rows, d // LANES, LANES), lambda i, idx: (i, 0, 0)),
            scratch_shapes=[pltpu.SemaphoreType.DMA(())],
        ),
        out_shape=jax.ShapeDtypeStruct((n_rows, d // LANES, LANES), xn.dtype),
        compiler_params=_cparams(("arbitrary",)),
        name="moe_dispatch",
    )(row_tok, src)
    return out.reshape(n_rows, d)


def _combine_kernel(d1_ref, d2_ref, x_ref, route_ref, yb_ref, o_ref, buf1, buf2, sem, *, rows):
    i = pl.program_id(0)
    slot = i % 2

    def issue(step, to_slot):
        base = step * rows

        def rows8(r8, carry):
            r0 = pl.multiple_of(r8 * 8, 8)
            for u in range(8):
                pltpu.make_async_copy(yb_ref.at[pl.ds(d1_ref[base + r0 + u], 1)],
                                      buf1.at[to_slot, pl.ds(r0 + u, 1)], sem.at[to_slot, 0]).start(priority=0)
                pltpu.make_async_copy(yb_ref.at[pl.ds(d2_ref[base + r0 + u], 1)],
                                      buf2.at[to_slot, pl.ds(r0 + u, 1)], sem.at[to_slot, 1]).start(priority=1)
            return carry

        lax.fori_loop(0, rows // 8, rows8, 0)

    @pl.when(i == 0)
    def _():
        issue(0, 0)

    @pl.when(i + 1 < pl.num_programs(0))
    def _():
        issue(i + 1, 1 - slot)

    pltpu.make_async_copy(yb_ref.at[pl.ds(0, rows)], buf1.at[slot], sem.at[slot, 0]).wait()
    pltpu.make_async_copy(yb_ref.at[pl.ds(0, rows)], buf2.at[slot], sem.at[slot, 1]).wait()
    route = route_ref[...]
    o_ref[...] = x_ref[...] + (buf1[slot] * route[:, 2:3] + buf2[slot] * route[:, 3:4])


def moe_combine(x, route, yb, dest1, dest2, rows=256):
    m, d = x.shape
    rows = _tile(m, rows, 8)
    return pl.pallas_call(
        functools.partial(_combine_kernel, rows=rows),
        grid_spec=pltpu.PrefetchScalarGridSpec(
            num_scalar_prefetch=2,
            grid=(m // rows,),
            in_specs=[pl.BlockSpec((rows, d), lambda i, a, b: (i, 0)),
                      pl.BlockSpec((rows, LANES), lambda i, a, b: (i, 0)),
                      pl.BlockSpec(memory_space=pl.ANY)],
            out_specs=pl.BlockSpec((rows, d), lambda i, a, b: (i, 0)),
            scratch_shapes=[pltpu.VMEM((2, rows, d), F32), pltpu.VMEM((2, rows, d), F32),
                            pltpu.SemaphoreType.DMA((2, 2))],
        ),
        out_shape=jax.ShapeDtypeStruct((m, d), F32),
        compiler_params=_cparams(("arbitrary",)),
        name="moe_combine",
    )(dest1, dest2, x, route, yb)


def moe_layer(x, norm_g, w_router, w_gate, w_up, w_down):
    m, d = x.shape
    tm = MOE_TM
    xn, route, cnt = moe_route(x, norm_g, w_router)
    expert = route[:, 0:2].astype(I32)
    rank = route[:, 4:6].astype(I32)
    counts = cnt[0, :N_EXPERTS].astype(I32)
    padded = (counts + tm - 1) // tm * tm
    pend = jnp.cumsum(padded)
    pstart = pend - padded
    dest = pstart[expert] + rank
    n_blk = -(-(2 * m) // tm) + N_EXPERTS
    n_rows = n_blk * tm
    tok = jnp.broadcast_to(jnp.arange(m, dtype=I32)[:, None], (m, 2))
    row_tok = jnp.zeros((n_rows,), I32).at[dest.reshape(-1)].set(tok.reshape(-1))
    nused = (pend[-1] // tm).astype(I32)
    blk = jnp.minimum(jnp.arange(n_blk, dtype=I32), nused - 1)
    blk_e = jnp.minimum(jnp.sum(pend[None, :] <= (blk * tm)[:, None], axis=1), N_EXPERTS - 1).astype(I32)
    nu = nused.reshape(1)

    xb = moe_dispatch(xn, row_tok)
    tf = _tile(w_gate.shape[2], 512, LANES)
    hidden = gmm_swiglu(xb, w_gate, w_up, blk_e, nu, tm=tm, tn=tf)
    yb = gmm(hidden, w_down, blk_e, nu, tm=tm, tn=_tile(d, 1024, LANES), out_dtype=F32, single_buffer_w=True)
    return moe_combine(x, route, yb, dest[:, 0], dest[:, 1])


def even_layer(x2, b, s, pos, w_norm, w_in, q_gain, k_gain, w_cmp_k, w_cmp_v, pe_cmp, conv_w, f_bias, m_norm,
               w_out, w_norm_ffn, w_gate, w_up, w_down):
    n, dm = x2.shape
    g, d = NSA_GROUPS, HEAD_DIM
    o_gate = NSA_SLAB
    o_qb = o_gate + NSA_HEADS * 3
    o_if = o_qb + 3 * MLSTM_WIDTH
    o_ob = o_if + 2 * MLSTM_HEADS
    w_ml = jnp.concatenate([w_in[:, o_qb:o_if], w_in[:, o_ob:o_ob + MLSTM_WIDTH]], axis=1)
    per_g = NSA_HPG * 3
    zeros = functools.partial(jnp.zeros, dtype=w_in.dtype)
    w_small = jnp.concatenate(
        [w_in[:, o_gate:o_gate + per_g], zeros((dm, GATE_I_LANE - per_g)),
         w_in[:, o_if:o_ob], zeros((dm, LANES - GATE_I_LANE - 2 * MLSTM_HEADS)),
         w_in[:, o_gate + per_g:o_qb], zeros((dm, LANES - per_g))], axis=1)

    h = rmsnorm_rows(x2, w_norm)
    p3 = mm(h, w_in[:, :o_gate], out_dtype=BF16, tn=NSA_SLAB // 2).reshape(b, s, NSA_SLAB)
    pm3 = mm(h, w_ml, out_dtype=BF16, tn=1024).reshape(b, s, 4 * MLSTM_WIDTH)
    gates3 = mm(h, w_small, out_dtype=F32, tn=2 * LANES).reshape(b, s, 2 * LANES)

    cos_n, sin_n = trig_tables(pos, _nsa_inv_lane())
    q_t, k_n, v_t = nsa_prep(p3, cos_n.reshape(b, s, LANES), sin_n.reshape(b, s, LANES), q_gain, k_gain)
    nh = s // CMP_STRIDE
    cmp_pos = jnp.concatenate([pos[:, CMP_BLOCK - 1::CMP_STRIDE][:, :nh - 1], pos[:, -1:]], axis=1)
    cos_c, sin_c = trig_tables(cmp_pos, _nsa_inv_lane())

    def cmp_blocks(col0):
        tok = p3[:, :, col0:col0 + NSA_KV_WIDTH].reshape(b, nh, CMP_STRIDE, g, d)
        halves = tok.transpose(0, 3, 1, 2, 4).reshape(b, g, nh, CMP_STRIDE * d)
        nxt = jnp.concatenate([halves[:, :, 1:], jnp.zeros_like(halves[:, :, :1])], axis=2)
        return jnp.concatenate([halves, nxt], axis=-1).reshape(b * g * nh, CMP_BLOCK * d)

    tabs = (cos_c.reshape(b, nh, LANES), sin_c.reshape(b, nh, LANES))
    kc = compress(cmp_blocks(NSA_WIDTH), pe_cmp, w_cmp_k, k_gain, *tabs, is_key=True, rows_per_seq=nh, groups=g)
    vc = compress(cmp_blocks(NSA_WIDTH + NSA_KV_WIDTH), pe_cmp, w_cmp_v, k_gain, *tabs, is_key=False,
                  rows_per_seq=nh, groups=g)
    o_nsa = nsa_attention(q_t, k_n, v_t, kc, vc, gates3)

    qk = conv_silu(pm3, conv_w, 0, 2 * MLSTM_WIDTH)
    h_b = mlstm(qk, pm3, gates3, f_bias, m_norm, 2 * MLSTM_WIDTH, 3 * MLSTM_WIDTH)

    mixed = jnp.concatenate([o_nsa, h_b], axis=-1).reshape(n, NSA_WIDTH + MLSTM_WIDTH)
    x2 = mm(mixed, w_out, out_dtype=F32, residual=x2, tm=1024, tn=1024)
    hf = rmsnorm_rows(x2, w_norm_ffn)
    hidden = mm_swiglu(hf, w_gate, w_up, tm=1024)
    return mm(hidden, w_down, out_dtype=F32, residual=x2, tm=512, tn=1024)


def odd_layer(x2, b, s, pos, w_norm, w_in, r_norm, w_out, w_norm_ffn, w_router, e_gate, e_up, e_down):
    n, dm = x2.shape
    h = rmsnorm_rows(x2, w_norm)
    po3 = mm(h, w_in, out_dtype=BF16, tn=1024).reshape(b, s, -1)
    cos_r, sin_r = trig_tables(pos, _ret_inv_lane())
    y = retention(po3, cos_r.reshape(b, s, LANES), sin_r.reshape(b, s, LANES), r_norm)
    x2 = mm(y.reshape(n, RET_V_WIDTH), w_out, out_dtype=F32, residual=x2, tm=512, tn=1024)
    return moe_layer(x2, w_norm_ffn, w_router, e_gate, e_up, e_down)


def kernel(x, positions, norm_mix_even, w_in_even, nsa_q_gain, nsa_k_gain, w_cmp_k, w_cmp_v, pe_cmp, mlstm_conv, mlstm_f_bias, mlstm_norm, w_out_even, norm_ffn_even, ffn_gate, ffn_up, ffn_down, norm_mix_odd, w_in_odd, ret_norm, w_out_odd, norm_ffn_odd, w_router, exp_gate, exp_up, exp_down):
    b, s, dm = x.shape
    depth = norm_mix_even.shape[0] + norm_mix_odd.shape[0]
    x2 = x.reshape(b * s, dm)
    for layer in range(depth):
        j = layer // 2
        if layer % 2 == 0:
            x2 = even_layer(x2, b, s, positions, norm_mix_even[j], w_in_even[j], nsa_q_gain[j], nsa_k_gain[j],
                            w_cmp_k[j], w_cmp_v[j], pe_cmp[j], mlstm_conv[j], mlstm_f_bias[j], mlstm_norm[j],
                            w_out_even[j], norm_ffn_even[j], ffn_gate[j], ffn_up[j], ffn_down[j])
        else:
            x2 = odd_layer(x2, b, s, positions, norm_mix_odd[j], w_in_odd[j], ret_norm[j], w_out_odd[j],
                           norm_ffn_odd[j], w_router[j], exp_gate[j], exp_up[j], exp_down[j])
    return x2.reshape(b, s, dm)
```

```python
import functools

import numpy as np
import jax
import jax.numpy as jnp
from jax import lax
from jax.experimental import pallas as pl
from jax.experimental.pallas import tpu as pltpu

F32 = jnp.float32
BF16 = jnp.bfloat16
I32 = jnp.int32
HIGHEST = lax.Precision.HIGHEST

HEAD_DIM = 128
NSA_HEADS = 8
NSA_GROUPS = 2
NSA_HPG = NSA_HEADS // NSA_GROUPS
NSA_WIDTH = NSA_HEADS * HEAD_DIM
NSA_KV_WIDTH = NSA_GROUPS * HEAD_DIM
NSA_SLAB = NSA_WIDTH + 6 * NSA_KV_WIDTH
CMP_BLOCK = 32
CMP_STRIDE = 16
SEL_BLOCK = 64
SEL_TOPN = 16
WINDOW = 512
ROPE_DIM = HEAD_DIM // 4
ROPE_THETA = 500000.0
SEL_FORCE = 1.0e6
NEG = -1.0e30
LOWEST = -3.0e38
MLSTM_HEADS = 4
MLSTM_DIM = 256
MLSTM_WIDTH = MLSTM_HEADS * MLSTM_DIM
CONV_WIDTH = 4
RET_HEADS = 8
RET_QK_DIM = 256
RET_V_DIM = 512
RET_QK_WIDTH = RET_HEADS * RET_QK_DIM
RET_V_WIDTH = RET_HEADS * RET_V_DIM
RET_ROPE_THETA = 10000.0
N_EXPERTS = 8
RMS_EPS = 1e-6

LANES = 128
BF16_SUBLANES = 16
V7X_VMEM_BYTES = 64 * 1024 * 1024
VMEM_LIMIT = V7X_VMEM_BYTES - 8 * 1024 * 1024

CHUNK = 256
NSA_TQ = 128
LOG2_E = 1.4426950408889634
V_ROWS = HEAD_DIM + BF16_SUBLANES
NSA_KT = 256
SOFTMAX_ROWS = 64
MOE_TM = 512
CAST_ROWS = 512
GATE_I_LANE = 16
GATE_F_LANE = 20


def _cparams(sem, vmem=VMEM_LIMIT):
    return pltpu.CompilerParams(dimension_semantics=sem, vmem_limit_bytes=vmem)


def _tile(n, target, quantum):
    if n <= target:
        return n
    t = (target // quantum) * quantum
    while t > quantum and n % t:
        t -= quantum
    assert n % t == 0, (n, target, quantum)
    return t


def _sigmoid(x):
    return 1.0 / (1.0 + jnp.exp(-x))


def _dot_nt(a, b):
    return lax.dot_general(a, b, (((1,), (1,)), ((), ())), preferred_element_type=F32)


def _dot_tn(a, b):
    return lax.dot_general(a, b, (((0,), (0,)), ((), ())), preferred_element_type=F32)


def _rmsnorm_kernel(x_ref, g_ref, o_ref):
    x = x_ref[...]
    y = x * lax.rsqrt(jnp.mean(x * x, axis=-1, keepdims=True) + RMS_EPS)
    o_ref[...] = (y * g_ref[...]).astype(o_ref.dtype)


def rmsnorm_rows(x, g, tm=512):
    m, d = x.shape
    tm = _tile(m, tm, 8)
    return pl.pallas_call(
        _rmsnorm_kernel,
        grid=(m // tm,),
        in_specs=[pl.BlockSpec((tm, d), lambda i: (i, 0)), pl.BlockSpec((1, d), lambda i: (0, 0))],
        out_specs=pl.BlockSpec((tm, d), lambda i: (i, 0)),
        out_shape=jax.ShapeDtypeStruct((m, d), BF16),
        compiler_params=_cparams(("parallel",)),
        name="rmsnorm",
    )(x, g.reshape(1, d).astype(F32))


def _stream_weights(be_ref, nx_ref, w_refs, land_refs, wb_refs, sems, tn):
    j = pl.program_id(0)
    i = pl.program_id(1)

    def copies(e, jj):
        c0 = pl.multiple_of(jj * tn, tn)
        return [pltpu.make_async_copy(w.at[e, :, pl.ds(c0, tn)], land, sems.at[n])
                for n, (w, land) in enumerate(zip(w_refs, land_refs))]

    @pl.when((i == 0) | (be_ref[i] != be_ref[jnp.maximum(i - 1, 0)]))
    def _():
        @pl.when((i == 0) & (j == 0))
        def _():
            for c in copies(be_ref[0], 0):
                c.start()

        for c in copies(be_ref[i], j):
            c.wait()
        rows = int(np.gcd(land_refs[0].shape[0], CAST_ROWS))

        def cast_rows(s, carry):
            r0 = pl.multiple_of(s * rows, rows)
            for land, wb in zip(land_refs, wb_refs):
                wb[pl.ds(r0, rows), :] = land[pl.ds(r0, rows), :].astype(BF16)
            return carry

        lax.fori_loop(0, land_refs[0].shape[0] // rows, cast_rows, 0)
        nxt = nx_ref[i]

        @pl.when(nxt >= 0)
        def _():
            for c in copies(nxt, j):
                c.start()

        @pl.when((nxt < 0) & (j + 1 < pl.num_programs(0)))
        def _():
            for c in copies(be_ref[0], j + 1):
                c.start()


def _next_group_expert(blk_e, n_experts):
    e = jnp.arange(n_experts, dtype=I32)
    present = jnp.any(blk_e[:, None] == e[None, :], axis=0)
    later = jnp.where((e[None, :] > blk_e[:, None]) & present[None, :], e[None, :], n_experts)
    nxt = jnp.min(later, axis=1)
    return jnp.where(nxt == n_experts, -1, nxt).astype(I32)


def _gmm_kernel(be_ref, nx_ref, nu_ref, x_ref, w_ref, *rest, has_res, tn):
    if has_res:
        r_ref, o_ref, land_ref, wb_ref, sems = rest
    else:
        o_ref, land_ref, wb_ref, sems = rest
    i = pl.program_id(1)
    _stream_weights(be_ref, nx_ref, [w_ref], [land_ref], [wb_ref], sems, tn)

    @pl.when(i < nu_ref[0])
    def _():
        acc = jnp.dot(x_ref[...], wb_ref[...], preferred_element_type=F32)
        if has_res:
            acc = r_ref[...] + acc
        o_ref[...] = acc.astype(o_ref.dtype)

    @pl.when(i >= nu_ref[0])
    def _():
        o_ref[...] = jnp.zeros_like(o_ref)


def gmm(x, w, blk_e, nused, *, tm, tn, out_dtype, residual=None):
    m, k = x.shape
    e, k2, n = w.shape
    assert k == k2 and m % tm == 0 and n % tn == 0
    nb = m // tm
    in_specs = [
        pl.BlockSpec((tm, k), lambda j, i, be, nx, nu: (jnp.minimum(i, nu[0] - 1), 0)),
        pl.BlockSpec(memory_space=pl.ANY),
    ]
    args = [x, w]
    if residual is not None:
        in_specs.append(pl.BlockSpec((tm, tn), lambda j, i, be, nx, nu: (i, j)))
        args.append(residual)
    return pl.pallas_call(
        functools.partial(_gmm_kernel, has_res=residual is not None, tn=tn),
        grid_spec=pltpu.PrefetchScalarGridSpec(
            num_scalar_prefetch=3,
            grid=(n // tn, nb),
            in_specs=in_specs,
            out_specs=pl.BlockSpec((tm, tn), lambda j, i, be, nx, nu: (i, j)),
            scratch_shapes=[pltpu.VMEM((k, tn), w.dtype), pltpu.VMEM((k, tn), BF16),
                            pltpu.SemaphoreType.DMA((1,))],
        ),
        out_shape=jax.ShapeDtypeStruct((m, n), out_dtype),
        compiler_params=_cparams(("arbitrary", "arbitrary")),
        name="gmm",
    )(blk_e, _next_group_expert(blk_e, e), nused, *args)


def _gmm_swiglu_kernel(be_ref, nx_ref, nu_ref, x_ref, wg_ref, wu_ref, o_ref, lg_ref, lu_ref, wgb_ref, wub_ref,
                       sems, *, tn):
    i = pl.program_id(1)
    _stream_weights(be_ref, nx_ref, [wg_ref, wu_ref], [lg_ref, lu_ref], [wgb_ref, wub_ref], sems, tn)

    @pl.when(i < nu_ref[0])
    def _():
        x = x_ref[...]
        g = jnp.dot(x, wgb_ref[...], preferred_element_type=F32)
        u = jnp.dot(x, wub_ref[...], preferred_element_type=F32)
        o_ref[...] = (g * _sigmoid(g) * u).astype(o_ref.dtype)

    @pl.when(i >= nu_ref[0])
    def _():
        o_ref[...] = jnp.zeros_like(o_ref)


def gmm_swiglu(x, wg, wu, blk_e, nused, *, tm, tn):
    m, k = x.shape
    e, k2, n = wg.shape
    assert k == k2 and wu.shape == wg.shape and m % tm == 0 and n % tn == 0
    nb = m // tm
    hbm = pl.BlockSpec(memory_space=pl.ANY)
    return pl.pallas_call(
        functools.partial(_gmm_swiglu_kernel, tn=tn),
        grid_spec=pltpu.PrefetchScalarGridSpec(
            num_scalar_prefetch=3,
            grid=(n // tn, nb),
            in_specs=[pl.BlockSpec((tm, k), lambda j, i, be, nx, nu: (jnp.minimum(i, nu[0] - 1), 0)), hbm, hbm],
            out_specs=pl.BlockSpec((tm, tn), lambda j, i, be, nx, nu: (i, j)),
            scratch_shapes=[pltpu.VMEM((k, tn), wg.dtype), pltpu.VMEM((k, tn), wu.dtype),
                            pltpu.VMEM((k, tn), BF16), pltpu.VMEM((k, tn), BF16),
                            pltpu.SemaphoreType.DMA((2,))],
        ),
        out_shape=jax.ShapeDtypeStruct((m, n), BF16),
        compiler_params=_cparams(("arbitrary", "arbitrary")),
        name="gmm_swiglu",
    )(blk_e, _next_group_expert(blk_e, e), nused, x, wg, wu)


def _dense_blocks(m, tm):
    nb = m // tm
    return jnp.zeros((nb,), I32), jnp.full((1,), nb, I32)


def mm(x, w, *, out_dtype, residual=None, tm=2048, tn=512):
    m, k = x.shape
    n = w.shape[1]
    tm = _tile(m, tm, 16)
    tn = _tile(n, tn, LANES)
    be, nu = _dense_blocks(m, tm)
    return gmm(x, w[None], be, nu, tm=tm, tn=tn, out_dtype=out_dtype, residual=residual)


def mm_swiglu(x, wg, wu, *, tm=2048, tn=512):
    m = x.shape[0]
    tm = _tile(m, tm, 16)
    tn = _tile(wg.shape[1], tn, LANES)
    be, nu = _dense_blocks(m, tm)
    return gmm_swiglu(x, wg[None], wu[None], be, nu, tm=tm, tn=tn)


def _trig_kernel(pos_ref, inv_ref, cos_ref, sin_ref):
    ang = pos_ref[...] * inv_ref[...]
    cos_ref[...] = jnp.cos(ang)
    sin_ref[...] = jnp.sin(ang)


def trig_tables(pos, inv_lane):
    r = pos.size
    pos_b = jnp.broadcast_to(pos.astype(F32).reshape(r, 1), (r, LANES))
    tr = _tile(r, 512, 8)
    spec = pl.BlockSpec((tr, LANES), lambda i: (i, 0))
    return pl.pallas_call(
        _trig_kernel,
        grid=(r // tr,),
        in_specs=[spec, pl.BlockSpec((1, LANES), lambda i: (0, 0))],
        out_specs=[spec, spec],
        out_shape=[jax.ShapeDtypeStruct((r, LANES), F32)] * 2,
        compiler_params=_cparams(("parallel",)),
        name="trig_tables",
    )(pos_b, inv_lane.reshape(1, LANES))


def _nsa_inv_lane():
    half = ROPE_DIM // 2
    inv = jnp.power(jnp.float32(ROPE_THETA), -jnp.arange(half, dtype=F32) * (2.0 / ROPE_DIM))
    return jnp.concatenate([inv, inv, jnp.zeros((LANES - ROPE_DIM,), F32)])


def _ret_inv_lane():
    half = RET_QK_DIM // 2
    return jnp.power(jnp.float32(RET_ROPE_THETA), -jnp.arange(half, dtype=F32) * (2.0 / RET_QK_DIM))


def _norm_rope_head(x, gain, cos, sin):
    half = ROPE_DIM // 2
    y = x * lax.rsqrt(jnp.mean(x * x, axis=-1, keepdims=True) + RMS_EPS) * gain
    lane = lax.broadcasted_iota(I32, y.shape, 1)
    from_hi = jnp.where(lane < half, -sin, 0.0)
    from_lo = jnp.where((lane >= half) & (lane < ROPE_DIM), sin, 0.0)
    return (y * cos + pltpu.roll(y, LANES - half, 1) * from_hi + pltpu.roll(y, half, 1) * from_lo)


def _nsa_prep_kernel(p_ref, cos_ref, sin_ref, qg_ref, kg_ref, qt_ref, kn_ref, vt_ref):
    cos = cos_ref[0]
    sin = sin_ref[0]
    scale = HEAD_DIM ** -0.5 * LOG2_E
    d = HEAD_DIM
    g = NSA_GROUPS
    for hd in range(NSA_HEADS):
        q = _norm_rope_head(p_ref[0, :, hd * d:(hd + 1) * d].astype(F32), qg_ref[...], cos, sin)
        qt_ref[0, hd * d:(hd + 1) * d, :] = (q * scale).T.astype(BF16)
    for n, slab in enumerate((2, 4)):
        for gi in range(g):
            off = NSA_WIDTH + slab * NSA_KV_WIDTH + gi * d
            k = _norm_rope_head(p_ref[0, :, off:off + d].astype(F32), kg_ref[...], cos, sin)
            kn_ref[0, :, (n * g + gi) * d:(n * g + gi + 1) * d] = k.astype(BF16)
    for n, slab in enumerate((3, 5)):
        for gi in range(g):
            off = NSA_WIDTH + slab * NSA_KV_WIDTH + gi * d
            v = p_ref[0, :, off:off + d].astype(F32)
            r0 = (n * g + gi) * V_ROWS
            vt_ref[0, r0:r0 + d, :] = v.T.astype(BF16)
            vt_ref[0, r0 + d:r0 + V_ROWS, :] = jnp.ones((V_ROWS - d, v.shape[0]), BF16)


def nsa_prep(p3, cos, sin, q_gain, k_gain):
    b, s, _ = p3.shape
    t = _tile(s, 256, LANES)
    d = HEAD_DIM
    tab = pl.BlockSpec((1, t, LANES), lambda bi, i: (bi, i, 0))
    gain = pl.BlockSpec((1, d), lambda bi, i: (0, 0))
    kv = 2 * NSA_KV_WIDTH
    vr = 2 * NSA_GROUPS * V_ROWS
    return pl.pallas_call(
        _nsa_prep_kernel,
        grid=(b, s // t),
        in_specs=[pl.BlockSpec((1, t, NSA_SLAB), lambda bi, i: (bi, i, 0)), tab, tab, gain, gain],
        out_specs=[pl.BlockSpec((1, NSA_WIDTH, t), lambda bi, i: (bi, 0, i)),
                   pl.BlockSpec((1, t, kv), lambda bi, i: (bi, i, 0)),
                   pl.BlockSpec((1, vr, t), lambda bi, i: (bi, 0, i))],
        out_shape=[jax.ShapeDtypeStruct((b, NSA_WIDTH, s), BF16),
                   jax.ShapeDtypeStruct((b, s, kv), BF16),
                   jax.ShapeDtypeStruct((b, vr, s), BF16)],
        compiler_params=_cparams(("parallel", "parallel")),
        name="nsa_prep",
    )(p3, cos, sin, q_gain.reshape(1, d), k_gain.reshape(1, d))


def _compress_kernel(blk_ref, pe_ref, w_ref, kg_ref, cos_ref, sin_ref, o_ref, *, is_key):
    a = (blk_ref[...].astype(F32) + pe_ref[...]).astype(BF16)
    y = jnp.dot(a, w_ref[...].astype(BF16), preferred_element_type=F32)
    if is_key:
        o_ref[0] = _norm_rope_head(y, kg_ref[...], cos_ref[0], sin_ref[0]).astype(BF16)
    else:
        o_ref[0] = y.T.astype(BF16)


def compress(blk, pe, w, k_gain, cos_c, sin_c, *, is_key, rows_per_seq, groups):
    r, kdim = blk.shape
    t = rows_per_seq
    d = HEAD_DIM
    tab = pl.BlockSpec((1, t, LANES), lambda i: (i // groups, 0, 0))
    out_blk = (1, t, d) if is_key else (1, d, t)
    return pl.pallas_call(
        functools.partial(_compress_kernel, is_key=is_key),
        grid=(r // t,),
        in_specs=[pl.BlockSpec((t, kdim), lambda i: (i, 0)),
                  pl.BlockSpec((1, kdim), lambda i: (0, 0)),
                  pl.BlockSpec((kdim, d), lambda i: (0, 0)),
                  pl.BlockSpec((1, d), lambda i: (0, 0)),
                  tab, tab],
        out_specs=pl.BlockSpec(out_blk, lambda i: (i, 0, 0)),
        out_shape=jax.ShapeDtypeStruct((r // t,) + out_blk[1:], BF16),
        compiler_params=_cparams(("parallel",)),
        name="nsa_compress",
    )(blk, pe.reshape(1, kdim), w, k_gain.reshape(1, d), cos_c, sin_c)


def _nsa_attn_kernel(qt_ref, kc_ref, vct_ref, ks_ref, kw_ref, vst_ref, vwt_ref, et_ref, ovt_ref, gate_ref,
                     o_ref, acc_sc, val_sc, sa_sc, sb_sc, pa_sc, pb_sc, sc_sc, pc_sc, pg_sc, sw_sc, pw_sc, ow_sc,
                     *, tq, kt, wk, ns, n_top):
    i = pl.program_id(2)
    t0 = i * tq
    cols = NSA_HPG * tq
    d = HEAD_DIM
    q_t = jnp.concatenate([qt_ref[0, p * d:(p + 1) * d, :] for p in range(NSA_HPG)], axis=1)
    t_lane = t0 + (lax.broadcasted_iota(I32, (1, cols), 1) & (tq - 1))

    ncp = kc_ref.shape[1]
    rs = SOFTMAX_ROWS
    sc_sc[...] = jnp.dot(kc_ref[0], q_t, preferred_element_type=F32)
    w0 = pl.multiple_of(jnp.maximum(t0 + tq - wk, 0), LANES)
    sw_sc[...] = jnp.dot(kw_ref[0, pl.ds(w0, wk), :], q_t, preferred_element_type=F32)

    def cmask(r0):
        c_end = (r0 + lax.broadcasted_iota(I32, (rs, cols), 0)) * CMP_STRIDE + (CMP_BLOCK - 1)
        return c_end <= t_lane

    m_c = jnp.full((1, cols), NEG, F32)
    for r0 in range(0, ncp, rs):
        m_c = jnp.maximum(m_c, jnp.max(jnp.where(cmask(r0), sc_sc[r0:r0 + rs, :], NEG), axis=0, keepdims=True))
    den_c = jnp.zeros((1, cols), F32)
    for r0 in range(0, ncp, rs):
        e = jnp.where(cmask(r0), jnp.exp2(sc_sc[r0:r0 + rs, :] - m_c), 0.0)
        sc_sc[r0:r0 + rs, :] = e
        den_c = den_c + jnp.sum(e, axis=0, keepdims=True)
    inv_c = 1.0 / jnp.maximum(den_c, 1e-30)
    for r0 in range(0, ncp, rs):
        p = sc_sc[r0:r0 + rs, :] * inv_c
        pc_sc[r0:r0 + rs, :] = p.astype(BF16)
        p_grp = p[:, 0:tq]
        for hp in range(1, NSA_HPG):
            p_grp = p_grp + p[:, hp * tq:(hp + 1) * tq]
        pg_sc[r0:r0 + rs, :] = p_grp
    o_cmp = jnp.dot(vct_ref[0], pc_sc[...], preferred_element_type=F32)

    m_w = jnp.full((1, cols), NEG, F32)
    for r0 in range(0, wk, rs):
        kpos = w0 + r0 + lax.broadcasted_iota(I32, (rs, cols), 0)
        wmask = (kpos <= t_lane) & (kpos > t_lane - WINDOW)
        s = jnp.where(wmask, sw_sc[r0:r0 + rs, :], NEG)
        sw_sc[r0:r0 + rs, :] = s
        m_w = jnp.maximum(m_w, jnp.max(s, axis=0, keepdims=True))
    for r0 in range(0, wk, rs):
        pw_sc[r0:r0 + rs, :] = jnp.exp2(sw_sc[r0:r0 + rs, :] - m_w).astype(BF16)
    o_win = jnp.dot(vwt_ref[0, :, pl.ds(w0, wk)], pw_sc[...], preferred_element_type=F32)
    ow_sc[...] = o_win[:d] * (1.0 / o_win[d:d + 1])

    imp = jnp.dot(ovt_ref[...], pg_sc[...], precision=HIGHEST, preferred_element_type=F32)
    nsp = val_sc.shape[0]
    jb = lax.broadcasted_iota(I32, (nsp, tq), 0)
    cur = (t0 + lax.broadcasted_iota(I32, (1, tq), 1)) // SEL_BLOCK
    forced = (jb == 0) | (jb == cur) | (jb == cur - 1)
    val = jnp.where(jb <= cur, jnp.where(forced, SEL_FORCE, imp[:nsp]), NEG)
    val = jnp.where(jb < ns, val, LOWEST)
    val_sc[...] = val
    beaten = jnp.zeros((nsp, tq), F32)
    for j2 in range(ns):
        r = val_sc[j2:j2 + 1, :]
        ge = jnp.where(r >= val, 1.0, 0.0)
        gt = jnp.where(r > val, 1.0, 0.0)
        beaten = beaten + jnp.where(jb > j2, ge, gt)
    past = jb < t0 // SEL_BLOCK
    bias = jnp.where(past & (beaten < n_top), 0.0, jnp.where(jb < ns, NEG, 0.0)).astype(BF16)
    bias = jnp.concatenate([bias, jnp.zeros((LANES - nsp, tq), BF16)], axis=0)
    q_aug = jnp.concatenate([q_t, jnp.concatenate([bias] * NSA_HPG, axis=1)], axis=0)

    d0 = pl.multiple_of(t0, tq)
    s_d = jnp.dot(ks_ref[0, pl.ds(d0, tq), :], q_t, preferred_element_type=F32)
    s_d = jnp.where(d0 + lax.broadcasted_iota(I32, (tq, cols), 0) <= t_lane, s_d, NEG)
    m_d = jnp.max(s_d, axis=0, keepdims=True)
    p_d = jnp.exp2(s_d - m_d)
    acc_sc[...] = jnp.dot(vst_ref[0, :, pl.ds(d0, tq)], p_d.astype(BF16), preferred_element_type=F32)

    n_pairs = (t0 + 2 * kt - 1) // (2 * kt)
    last_a = jnp.maximum(n_pairs - 1, 0) * (2 * kt)

    def scores(k0):
        k0 = pl.multiple_of(k0, kt)
        k_aug = jnp.concatenate([ks_ref[0, pl.ds(k0, kt), :], et_ref[pl.ds(k0, kt), :]], axis=1)
        return jnp.dot(k_aug, q_aug, preferred_element_type=F32)

    def values(p_ref, k0):
        k0 = pl.multiple_of(k0, kt)
        return jnp.dot(vst_ref[0, :, pl.ds(k0, kt)], p_ref[...], preferred_element_type=F32)

    def softmax_update(s_ref, p_ref, m_prev):
        m_new = jnp.maximum(m_prev, jnp.max(s_ref[...], axis=0, keepdims=True))
        for r0 in range(0, kt, SOFTMAX_ROWS):
            p_ref[r0:r0 + SOFTMAX_ROWS, :] = jnp.exp2(s_ref[r0:r0 + SOFTMAX_ROWS, :] - m_new).astype(BF16)
        return m_new, jnp.exp2(m_prev - m_new)

    def sel_step(j, carry):
        m_run, alpha_b = carry
        k0 = j * (2 * kt)
        sb_sc[...] = scores(k0 + kt)
        acc_sc[...] = alpha_b * acc_sc[...] + values(pb_sc, jnp.maximum(k0 - kt, 0))
        m_run, alpha_a = softmax_update(sa_sc, pa_sc, m_run)
        sa_sc[...] = scores(jnp.minimum(k0 + 2 * kt, last_a))
        acc_sc[...] = alpha_a * acc_sc[...] + values(pa_sc, k0)
        return softmax_update(sb_sc, pb_sc, m_run)

    sa_sc[...] = scores(0)
    pb_sc[...] = jnp.zeros(pb_sc.shape, BF16)
    _, alpha_last = lax.fori_loop(0, n_pairs, sel_step, (m_d, jnp.ones((1, cols), F32)))
    acc = alpha_last * acc_sc[...] + values(pb_sc, last_a + kt)
    o_sel = acc[:d] * (1.0 / acc[d:d + 1])

    g_t = _sigmoid(gate_ref[0]).T
    for p in range(NSA_HPG):
        sl = slice(p * tq, (p + 1) * tq)
        o_t = (g_t[3 * p:3 * p + 1] * o_cmp[:, sl] + g_t[3 * p + 1:3 * p + 2] * o_sel[:, sl]
               + g_t[3 * p + 2:3 * p + 3] * ow_sc[:, sl])
        o_ref[0, :, p * d:(p + 1) * d] = o_t.T.astype(BF16)


def nsa_attention(q_t, k_n, v_t, kc, vc_t, gates3):
    b, _, s = q_t.shape
    g, d = NSA_GROUPS, HEAD_DIM
    ncp = kc.shape[1]
    tq = NSA_TQ
    kt = min(NSA_KT, s)
    wk = WINDOW + tq
    ns = s // SEL_BLOCK
    assert s % (2 * kt) == 0 and s >= wk and ns <= LANES and tq == LANES
    assert kt % SOFTMAX_ROWS == 0 and ncp % SOFTMAX_ROWS == 0 and wk % SOFTMAX_ROWS == 0
    n_top = min(SEL_TOPN, ns)
    cols = NSA_HPG * tq

    key_blk = np.arange(s) // SEL_BLOCK
    e_t = jnp.asarray(key_blk[:, None] == np.arange(LANES)[None, :], dtype=BF16)
    c_start = np.arange(ncp) * CMP_STRIDE
    j_start = np.arange(LANES) * SEL_BLOCK
    overlap = ((c_start[:, None] < j_start[None, :] + SEL_BLOCK) & (c_start[:, None] + CMP_BLOCK > j_start[None, :])
               & (np.arange(ncp)[:, None] < s // CMP_STRIDE - 1) & (np.arange(LANES)[None, :] < ns))
    overlap_t = jnp.asarray(overlap.T.astype(np.float32))

    q_rows = NSA_HPG * d
    return pl.pallas_call(
        functools.partial(_nsa_attn_kernel, tq=tq, kt=kt, wk=wk, ns=ns, n_top=n_top),
        grid=(b, g, s // tq),
        in_specs=[pl.BlockSpec((1, q_rows, tq), lambda bi, gi, i: (bi, gi, i)),
                  pl.BlockSpec((1, ncp, d), lambda bi, gi, i: (bi * g + gi, 0, 0)),
                  pl.BlockSpec((1, d, ncp), lambda bi, gi, i: (bi * g + gi, 0, 0)),
                  pl.BlockSpec((1, s, d), lambda bi, gi, i: (bi, 0, gi)),
                  pl.BlockSpec((1, s, d), lambda bi, gi, i: (bi, 0, g + gi)),
                  pl.BlockSpec((1, V_ROWS, s), lambda bi, gi, i: (bi, gi, 0)),
                  pl.BlockSpec((1, V_ROWS, s), lambda bi, gi, i: (bi, g + gi, 0)),
                  pl.BlockSpec((s, LANES), lambda bi, gi, i: (0, 0)),
                  pl.BlockSpec((LANES, ncp), lambda bi, gi, i: (0, 0)),
                  pl.BlockSpec((1, tq, LANES), lambda bi, gi, i: (bi, i, gi))],
        out_specs=pl.BlockSpec((1, tq, q_rows), lambda bi, gi, i: (bi, i, gi)),
        out_shape=jax.ShapeDtypeStruct((b, s, NSA_WIDTH), BF16),
        scratch_shapes=[pltpu.VMEM((V_ROWS, cols), F32), pltpu.VMEM((-(-ns // 8) * 8, tq), F32),
                        pltpu.VMEM((kt, cols), F32), pltpu.VMEM((kt, cols), F32),
                        pltpu.VMEM((kt, cols), BF16), pltpu.VMEM((kt, cols), BF16),
                        pltpu.VMEM((ncp, cols), F32), pltpu.VMEM((ncp, cols), BF16), pltpu.VMEM((ncp, tq), F32),
                        pltpu.VMEM((wk, cols), F32), pltpu.VMEM((wk, cols), BF16), pltpu.VMEM((d, cols), F32)],
        compiler_params=_cparams(("parallel", "parallel", "arbitrary")),
        name="nsa_attention",
    )(q_t, kc, vc_t, k_n, k_n, v_t, v_t, e_t, overlap_t, gates3)


def _conv_kernel(cur_ref, prev_ref, w_ref, o_ref, *, tc):
    i = pl.program_id(1)
    cur = cur_ref[0].astype(F32)
    prev = jnp.where(i > 0, prev_ref[0].astype(F32), 0.0)
    pad = prev.shape[0] // 2
    full = jnp.concatenate([prev[pad:], cur], axis=0)
    y = None
    for kk in range(CONV_WIDTH):
        off = pad - (CONV_WIDTH - 1) + kk
        term = w_ref[kk:kk + 1, :] * full[off:off + tc]
        y = term if y is None else y + term
    o_ref[0] = (y * _sigmoid(y)).astype(BF16)


def conv_silu(p3, w, col0, width):
    b, s, _ = p3.shape
    tc = _tile(s, 512, BF16_SUBLANES)
    cw = 512
    assert col0 % cw == 0 and width % cw == 0
    c0 = col0 // cw
    halo = BF16_SUBLANES
    return pl.pallas_call(
        functools.partial(_conv_kernel, tc=tc),
        grid=(b, s // tc, width // cw),
        in_specs=[pl.BlockSpec((1, tc, cw), lambda bi, i, j: (bi, i, c0 + j)),
                  pl.BlockSpec((1, halo, cw), lambda bi, i, j: (bi, jnp.maximum(i * (tc // halo) - 1, 0), c0 + j)),
                  pl.BlockSpec((CONV_WIDTH, cw), lambda bi, i, j: (0, j))],
        out_specs=pl.BlockSpec((1, tc, cw), lambda bi, i, j: (bi, i, j)),
        out_shape=jax.ShapeDtypeStruct((b, s, width), BF16),
        compiler_params=_cparams(("parallel", "parallel", "parallel")),
        name="mlstm_conv",
    )(p3, p3, w)


def _log_sigmoid(x):
    return jnp.minimum(x, 0.0) - jnp.log1p(jnp.exp(-jnp.abs(x)))


def _mlstm_kernel(fb_ref, q_ref, k_ref, v_ref, ob_ref, gate_ref, nw_ref, o_ref, c_sc, n_sc, m_sc, *, cl):
    @pl.when(pl.program_id(1) == 0)
    def _():
        c_sc[...] = jnp.zeros(c_sc.shape, F32)
        n_sc[...] = jnp.zeros(n_sc.shape, F32)
        m_sc[...] = jnp.zeros(m_sc.shape, F32)

    dh = MLSTM_DIM
    slab = gate_ref[0]
    lane = lax.broadcasted_iota(I32, slab.shape, 1)
    r_i = lax.broadcasted_iota(I32, (cl, cl), 0)
    c_i = lax.broadcasted_iota(I32, (cl, cl), 1)
    eye = r_i == c_i
    tri = c_i <= r_i

    def head(hd, carry):
        off = pl.multiple_of(hd * dh, dh)
        q = q_ref[0, :, pl.ds(off, dh)]
        v = v_ref[0, :, pl.ds(off, dh)]
        ks32 = k_ref[0, :, pl.ds(off, dh)].astype(F32) * (dh ** -0.5)
        ks = ks32.astype(BF16)
        i_col = jnp.sum(jnp.where(lane == GATE_I_LANE + hd, slab, 0.0), axis=-1, keepdims=True)
        f_col = jnp.sum(jnp.where(lane == GATE_F_LANE + hd, slab, 0.0), axis=-1, keepdims=True)
        lf_col = _log_sigmoid(f_col + fb_ref[hd])

        lf_row = jnp.sum(jnp.where(eye, lf_col, 0.0), axis=0, keepdims=True)
        ig_row = jnp.sum(jnp.where(eye, i_col, 0.0), axis=0, keepdims=True)
        a_col = jnp.sum(jnp.where(tri, lf_row, 0.0), axis=1, keepdims=True)
        a_row = jnp.sum(jnp.where(r_i <= c_i, lf_col, 0.0), axis=0, keepdims=True)
        m_prev = m_sc[hd]

        dlog = jnp.where(tri, a_col - a_row + ig_row, NEG)
        inter = a_col + m_prev
        mt = jnp.maximum(inter, jnp.max(dlog, axis=-1, keepdims=True))
        wm = jnp.exp(dlog - mt) * _dot_nt(q, ks)
        e_col = jnp.exp(inter - mt)
        num = e_col * jnp.dot(q, c_sc[hd].astype(BF16), preferred_element_type=F32) \
            + jnp.dot(wm.astype(BF16), v, preferred_element_type=F32)
        qn = jnp.sum(q.astype(F32) * n_sc[hd], axis=-1, keepdims=True)
        den = e_col * qn + jnp.sum(wm, axis=-1, keepdims=True)
        hh = num / jnp.maximum(jnp.abs(den), jnp.exp(-mt))

        a_last = jnp.sum(lf_row, axis=-1, keepdims=True)
        gs = a_last - a_col + i_col
        m_new = jnp.maximum(a_last + m_prev, jnp.max(gs, axis=0, keepdims=True))
        decay = jnp.exp(a_last + m_prev - m_new)
        wk = jnp.exp(gs - m_new) * ks32
        c_sc[hd] = decay * c_sc[hd] + _dot_tn(wk.astype(BF16), v)
        n_sc[hd] = decay * n_sc[hd] + jnp.sum(wk, axis=0, keepdims=True)
        m_sc[hd] = m_new

        y = hh * lax.rsqrt(jnp.mean(hh * hh, axis=-1, keepdims=True) + RMS_EPS) * nw_ref[:, pl.ds(off, dh)]
        o_ref[0, :, pl.ds(off, dh)] = (y * _sigmoid(ob_ref[0, :, pl.ds(off, dh)].astype(F32))).astype(BF16)
        return carry

    lax.fori_loop(0, MLSTM_HEADS, head, 0)


def mlstm(qk, p3, gates3, f_bias, norm_w, v_col0, o_col0):
    b, s, _ = qk.shape
    cl = min(CHUNK, s)
    dh = MLSTM_DIM
    nh = MLSTM_HEADS
    w = nh * dh
    assert v_col0 % w == 0 and o_col0 % w == 0 and s % cl == 0
    vb, ob = v_col0 // w, o_col0 // w
    return pl.pallas_call(
        functools.partial(_mlstm_kernel, cl=cl),
        grid=(b, s // cl),
        in_specs=[pl.BlockSpec(memory_space=pltpu.SMEM),
                  pl.BlockSpec((1, cl, w), lambda bi, c: (bi, c, 0)),
                  pl.BlockSpec((1, cl, w), lambda bi, c: (bi, c, 1)),
                  pl.BlockSpec((1, cl, w), lambda bi, c: (bi, c, vb)),
                  pl.BlockSpec((1, cl, w), lambda bi, c: (bi, c, ob)),
                  pl.BlockSpec((1, cl, LANES), lambda bi, c: (bi, c, 0)),
                  pl.BlockSpec((1, w), lambda bi, c: (0, 0))],
        out_specs=pl.BlockSpec((1, cl, w), lambda bi, c: (bi, c, 0)),
        out_shape=jax.ShapeDtypeStruct((b, s, w), BF16),
        scratch_shapes=[pltpu.VMEM((nh, dh, dh), F32), pltpu.VMEM((nh, 1, dh), F32), pltpu.VMEM((nh, 1, 1), F32)],
        compiler_params=_cparams(("parallel", "arbitrary")),
        name="mlstm",
    )(f_bias.astype(F32), qk, qk, p3, p3, gates3, norm_w.reshape(1, w).astype(F32))


def _ret_kernel(cd_ref, q_ref, k_ref, v_ref, g_ref, cos_ref, sin_ref, nw_ref, dm_ref, xi_ref, zeta_ref,
                o_ref, r_sc, *, cl):
    @pl.when(pl.program_id(1) == 0)
    def _():
        r_sc[...] = jnp.zeros(r_sc.shape, F32)

    cos = cos_ref[0]
    sin = sin_ref[0]
    dk, dv = RET_QK_DIM, RET_V_DIM
    half = dk // 2
    scale = dk ** -0.5

    def rope(x):
        x1, x2 = x[:, :half], x[:, half:]
        return x1 * cos - x2 * sin, x1 * sin + x2 * cos

    def head(hd, carry):
        qo = pl.multiple_of(hd * dk, dk)
        vo = pl.multiple_of(hd * dv, dv)
        q1, q2 = rope(q_ref[0, :, pl.ds(qo, dk)].astype(F32))
        qr = jnp.concatenate([q1, q2], axis=1).astype(BF16)
        k1, k2 = rope(k_ref[0, :, pl.ds(qo, dk)].astype(F32))
        zeta = zeta_ref[hd] * scale
        kr = jnp.concatenate([k1 * scale, k2 * scale], axis=1).astype(BF16)
        kz = jnp.concatenate([k1 * zeta, k2 * zeta], axis=1).astype(BF16)
        v = v_ref[0, :, pl.ds(vo, dv)]

        inner = jnp.dot((_dot_nt(qr, kr) * dm_ref[hd]).astype(BF16), v, preferred_element_type=F32)
        xi = xi_ref[hd]
        cross = jnp.dot(qr, r_sc[hd].astype(BF16), preferred_element_type=F32)
        cross = cross * jnp.concatenate([xi] * (dv // LANES), axis=1)
        r_sc[hd] = cd_ref[hd] * r_sc[hd] + _dot_tn(kz, v)

        y = inner + cross
        y = y * lax.rsqrt(jnp.mean(y * y, axis=-1, keepdims=True) + RMS_EPS) * nw_ref[:, pl.ds(vo, dv)]
        gg = g_ref[0, :, pl.ds(vo, dv)].astype(F32)
        o_ref[0, :, pl.ds(vo, dv)] = (y * (gg * _sigmoid(gg))).astype(BF16)
        return carry

    lax.fori_loop(0, RET_HEADS, head, 0)


def retention(po3, cos, sin, norm_w):
    b, s, _ = po3.shape
    cl = min(CHUNK, s)
    nh, dk, dv = RET_HEADS, RET_QK_DIM, RET_V_DIM
    log_g = jnp.log1p(-jnp.exp2(-5.0 - jnp.arange(nh, dtype=F32)))
    idx = jnp.arange(cl, dtype=F32)
    diff = idx[:, None] - idx[None, :]
    dm = jnp.where(diff >= 0, jnp.exp(jnp.maximum(diff, 0.0) * log_g[:, None, None]), 0.0)
    xi = jnp.broadcast_to(jnp.exp((idx + 1.0) * log_g[:, None])[..., None], (nh, cl, LANES))
    zeta = jnp.broadcast_to(jnp.exp((cl - 1.0 - idx) * log_g[:, None])[..., None], (nh, cl, LANES))
    chunk_decay = jnp.exp(cl * log_g)
    qk_w, v_w = RET_QK_WIDTH, RET_V_WIDTH
    assert v_w == 2 * qk_w
    tab = pl.BlockSpec((1, cl, LANES), lambda bi, c: (bi, c, 0))

    def table(shape):
        return pl.BlockSpec(shape, lambda bi, c: (0, 0, 0))

    return pl.pallas_call(
        functools.partial(_ret_kernel, cl=cl),
        grid=(b, s // cl),
        in_specs=[pl.BlockSpec(memory_space=pltpu.SMEM),
                  pl.BlockSpec((1, cl, qk_w), lambda bi, c: (bi, c, 0)),
                  pl.BlockSpec((1, cl, qk_w), lambda bi, c: (bi, c, 1)),
                  pl.BlockSpec((1, cl, v_w), lambda bi, c: (bi, c, 1)),
                  pl.BlockSpec((1, cl, v_w), lambda bi, c: (bi, c, 2)),
                  tab, tab,
                  pl.BlockSpec((1, v_w), lambda bi, c: (0, 0)),
                  table((nh, cl, cl)), table((nh, cl, LANES)), table((nh, cl, LANES))],
        out_specs=pl.BlockSpec((1, cl, v_w), lambda bi, c: (bi, c, 0)),
        out_shape=jax.ShapeDtypeStruct((b, s, v_w), BF16),
        scratch_shapes=[pltpu.VMEM((nh, dk, dv), F32)],
        compiler_params=_cparams(("parallel", "arbitrary")),
        name="retention",
    )(chunk_decay, po3, po3, po3, po3, cos, sin, norm_w.reshape(1, v_w).astype(F32), dm, xi, zeta)


def _router_kernel(x_ref, g_ref, wr_ref, xn_ref, route_ref, cnt_ref, carry_sc):
    @pl.when(pl.program_id(0) == 0)
    def _():
        carry_sc[...] = jnp.zeros(carry_sc.shape, F32)

    x = x_ref[...]
    t = x.shape[0]
    y = x * lax.rsqrt(jnp.mean(x * x, axis=-1, keepdims=True) + RMS_EPS) * g_ref[...]
    xn_ref[...] = y.astype(BF16)
    w = wr_ref[...]
    y_hi = y.astype(BF16)
    y_lo = (y - y_hi.astype(F32)).astype(BF16)
    w_hi = w.astype(BF16)
    w_lo = (w - w_hi.astype(F32)).astype(BF16)
    logits = (jnp.dot(y_hi, w_hi, preferred_element_type=F32) + jnp.dot(y_lo, w_hi, preferred_element_type=F32)
              + jnp.dot(y_hi, w_lo, preferred_element_type=F32))
    lane = lax.broadcasted_iota(I32, logits.shape, 1)
    lg = jnp.where(lane < N_EXPERTS, logits, LOWEST)
    v1 = jnp.max(lg, axis=-1, keepdims=True)
    i1 = jnp.min(jnp.where(lg == v1, lane, LANES), axis=-1, keepdims=True)
    lg2 = jnp.where(lane == i1, LOWEST, lg)
    v2 = jnp.max(lg2, axis=-1, keepdims=True)
    i2 = jnp.min(jnp.where(lg2 == v2, lane, LANES), axis=-1, keepdims=True)
    e2 = jnp.exp(v2 - v1)
    g1 = 1.0 / (1.0 + e2)
    g2 = e2 / (1.0 + e2)

    chosen = jnp.where((lane == i1) | (lane == i2), 1.0, 0.0)
    r_i = lax.broadcasted_iota(I32, (t, t), 0)
    c_i = lax.broadcasted_iota(I32, (t, t), 1)
    tri = jnp.where(c_i <= r_i, 1.0, 0.0).astype(BF16)
    seen = jnp.dot(tri, chosen.astype(BF16), preferred_element_type=F32) + carry_sc[...]
    rank1 = jnp.sum(jnp.where(lane == i1, seen, 0.0), axis=-1, keepdims=True) - 1.0
    rank2 = jnp.sum(jnp.where(lane == i2, seen, 0.0), axis=-1, keepdims=True) - 1.0
    total = seen[t - 1:t, :]
    carry_sc[...] = total
    cnt_ref[...] = jnp.broadcast_to(total, cnt_ref.shape)

    route = jnp.where(lane == 0, i1.astype(F32), 0.0)
    route = jnp.where(lane == 1, i2.astype(F32), route)
    route = jnp.where(lane == 2, g1, route)
    route = jnp.where(lane == 3, g2, route)
    route = jnp.where(lane == 4, rank1, route)
    route = jnp.where(lane == 5, rank2, route)
    route_ref[...] = route


def moe_route(x, g, w_router, tm=256):
    m, d = x.shape
    tm = _tile(m, tm, 16)
    wr = jnp.zeros((d, LANES), F32).at[:, :N_EXPERTS].set(w_router.astype(F32))
    return pl.pallas_call(
        _router_kernel,
        grid=(m // tm,),
        in_specs=[pl.BlockSpec((tm, d), lambda i: (i, 0)),
                  pl.BlockSpec((1, d), lambda i: (0, 0)),
                  pl.BlockSpec((d, LANES), lambda i: (0, 0))],
        out_specs=[pl.BlockSpec((tm, d), lambda i: (i, 0)),
                   pl.BlockSpec((tm, LANES), lambda i: (i, 0)),
                   pl.BlockSpec((8, LANES), lambda i: (0, 0))],
        out_shape=[jax.ShapeDtypeStruct((m, d), BF16),
                   jax.ShapeDtypeStruct((m, LANES), F32),
                   jax.ShapeDtypeStruct((8, LANES), F32)],
        scratch_shapes=[pltpu.VMEM((1, LANES), F32)],
        compiler_params=_cparams(("arbitrary",)),
        name="moe_route",
    )(x, g.reshape(1, d).astype(F32), wr)


def _dispatch_kernel(idx_ref, src_ref, o_ref, sem, *, rows):
    base = pl.program_id(0) * rows

    def issue(r8, carry):
        for u in range(8):
            r = r8 * 8 + u
            pltpu.make_async_copy(src_ref.at[idx_ref[base + r]], o_ref.at[r], sem).start(priority=u % 2)
        return carry

    lax.fori_loop(0, rows // 8, issue, 0)
    pltpu.make_async_copy(src_ref.at[pl.ds(0, rows)], o_ref, sem).wait()


def moe_dispatch(xn, row_tok, rows=2048):
    m, d = xn.shape
    n_rows = row_tok.shape[0]
    rows = _tile(n_rows, rows, 8)
    src = xn.reshape(m, d // LANES, LANES)
    out = pl.pallas_call(
        functools.partial(_dispatch_kernel, rows=rows),
        grid_spec=pltpu.PrefetchScalarGridSpec(
            num_scalar_prefetch=1,
            grid=(n_rows // rows,),
            in_specs=[pl.BlockSpec(memory_space=pl.ANY)],
            out_specs=pl.BlockSpec((rows, d // LANES, LANES), lambda i, idx: (i, 0, 0)),
            scratch_shapes=[pltpu.SemaphoreType.DMA(())],
        ),
        out_shape=jax.ShapeDtypeStruct((n_rows, d // LANES, LANES), xn.dtype),
        compiler_params=_cparams(("arbitrary",)),
        name="moe_dispatch",
    )(row_tok, src)
    return out.reshape(n_rows, d)


def _combine_kernel(d1_ref, d2_ref, x_ref, route_ref, yb_ref, o_ref, buf1, buf2, sem, *, rows):
    i = pl.program_id(0)
    slot = i % 2

    def issue(step, to_slot):
        base = step * rows

        def rows8(r8, carry):
            r0 = pl.multiple_of(r8 * 8, 8)
            for u in range(8):
                pltpu.make_async_copy(yb_ref.at[pl.ds(d1_ref[base + r0 + u], 1)],
                                      buf1.at[to_slot, pl.ds(r0 + u, 1)], sem.at[to_slot, 0]).start(priority=0)
                pltpu.make_async_copy(yb_ref.at[pl.ds(d2_ref[base + r0 + u], 1)],
                                      buf2.at[to_slot, pl.ds(r0 + u, 1)], sem.at[to_slot, 1]).start(priority=1)
            return carry

        lax.fori_loop(0, rows // 8, rows8, 0)

    @pl.when(i == 0)
    def _():
        issue(0, 0)

    @pl.when(i + 1 < pl.num_programs(0))
    def _():
        issue(i + 1, 1 - slot)

    pltpu.make_async_copy(yb_ref.at[pl.ds(0, rows)], buf1.at[slot], sem.at[slot, 0]).wait()
    pltpu.make_async_copy(yb_ref.at[pl.ds(0, rows)], buf2.at[slot], sem.at[slot, 1]).wait()
    route = route_ref[...]
    o_ref[...] = x_ref[...] + (buf1[slot] * route[:, 2:3] + buf2[slot] * route[:, 3:4])


def moe_combine(x, route, yb, dest1, dest2, rows=256):
    m, d = x.shape
    rows = _tile(m, rows, 8)
    return pl.pallas_call(
        functools.partial(_combine_kernel, rows=rows),
        grid_spec=pltpu.PrefetchScalarGridSpec(
            num_scalar_prefetch=2,
            grid=(m // rows,),
            in_specs=[pl.BlockSpec((rows, d), lambda i, a, b: (i, 0)),
                      pl.BlockSpec((rows, LANES), lambda i, a, b: (i, 0)),
                      pl.BlockSpec(memory_space=pl.ANY)],
            out_specs=pl.BlockSpec((rows, d), lambda i, a, b: (i, 0)),
            scratch_shapes=[pltpu.VMEM((2, rows, d), F32), pltpu.VMEM((2, rows, d), F32),
                            pltpu.SemaphoreType.DMA((2, 2))],
        ),
        out_shape=jax.ShapeDtypeStruct((m, d), F32),
        compiler_params=_cparams(("arbitrary",)),
        name="moe_combine",
    )(dest1, dest2, x, route, yb)


def moe_layer(x, norm_g, w_router, w_gate, w_up, w_down):
    m, d = x.shape
    tm = MOE_TM
    xn, route, cnt = moe_route(x, norm_g, w_router)
    expert = route[:, 0:2].astype(I32)
    rank = route[:, 4:6].astype(I32)
    counts = cnt[0, :N_EXPERTS].astype(I32)
    padded = (counts + tm - 1) // tm * tm
    pend = jnp.cumsum(padded)
    pstart = pend - padded
    dest = pstart[expert] + rank
    n_blk = -(-(2 * m) // tm) + N_EXPERTS
    n_rows = n_blk * tm
    tok = jnp.broadcast_to(jnp.arange(m, dtype=I32)[:, None], (m, 2))
    row_tok = jnp.zeros((n_rows,), I32).at[dest.reshape(-1)].set(tok.reshape(-1))
    nused = (pend[-1] // tm).astype(I32)
    blk = jnp.minimum(jnp.arange(n_blk, dtype=I32), nused - 1)
    blk_e = jnp.minimum(jnp.sum(pend[None, :] <= (blk * tm)[:, None], axis=1), N_EXPERTS - 1).astype(I32)
    nu = nused.reshape(1)

    xb = moe_dispatch(xn, row_tok)
    tf = _tile(w_gate.shape[2], 512, LANES)
    hidden = gmm_swiglu(xb, w_gate, w_up, blk_e, nu, tm=tm, tn=tf)
    yb = gmm(hidden, w_down, blk_e, nu, tm=tm, tn=_tile(d, 1024, LANES), out_dtype=F32)
    return moe_combine(x, route, yb, dest[:, 0], dest[:, 1])


def even_layer(x2, b, s, pos, w_norm, w_in, q_gain, k_gain, w_cmp_k, w_cmp_v, pe_cmp, conv_w, f_bias, m_norm,
               w_out, w_norm_ffn, w_gate, w_up, w_down):
    n, dm = x2.shape
    g, d = NSA_GROUPS, HEAD_DIM
    o_gate = NSA_SLAB
    o_qb = o_gate + NSA_HEADS * 3
    o_if = o_qb + 3 * MLSTM_WIDTH
    o_ob = o_if + 2 * MLSTM_HEADS
    w_ml = jnp.concatenate([w_in[:, o_qb:o_if], w_in[:, o_ob:o_ob + MLSTM_WIDTH]], axis=1)
    per_g = NSA_HPG * 3
    zeros = functools.partial(jnp.zeros, dtype=w_in.dtype)
    w_small = jnp.concatenate(
        [w_in[:, o_gate:o_gate + per_g], zeros((dm, GATE_I_LANE - per_g)),
         w_in[:, o_if:o_ob], zeros((dm, LANES - GATE_I_LANE - 2 * MLSTM_HEADS)),
         w_in[:, o_gate + per_g:o_qb], zeros((dm, LANES - per_g))], axis=1)

    h = rmsnorm_rows(x2, w_norm)
    p3 = mm(h, w_in[:, :o_gate], out_dtype=BF16, tn=NSA_SLAB // 2).reshape(b, s, NSA_SLAB)
    pm3 = mm(h, w_ml, out_dtype=BF16, tn=1024).reshape(b, s, 4 * MLSTM_WIDTH)
    gates3 = mm(h, w_small, out_dtype=F32, tn=2 * LANES).reshape(b, s, 2 * LANES)

    cos_n, sin_n = trig_tables(pos, _nsa_inv_lane())
    q_t, k_n, v_t = nsa_prep(p3, cos_n.reshape(b, s, LANES), sin_n.reshape(b, s, LANES), q_gain, k_gain)
    nh = s // CMP_STRIDE
    cmp_pos = jnp.concatenate([pos[:, CMP_BLOCK - 1::CMP_STRIDE][:, :nh - 1], pos[:, -1:]], axis=1)
    cos_c, sin_c = trig_tables(cmp_pos, _nsa_inv_lane())

    def cmp_blocks(col0):
        tok = p3[:, :, col0:col0 + NSA_KV_WIDTH].reshape(b, nh, CMP_STRIDE, g, d)
        halves = tok.transpose(0, 3, 1, 2, 4).reshape(b, g, nh, CMP_STRIDE * d)
        nxt = jnp.concatenate([halves[:, :, 1:], jnp.zeros_like(halves[:, :, :1])], axis=2)
        return jnp.concatenate([halves, nxt], axis=-1).reshape(b * g * nh, CMP_BLOCK * d)

    tabs = (cos_c.reshape(b, nh, LANES), sin_c.reshape(b, nh, LANES))
    kc = compress(cmp_blocks(NSA_WIDTH), pe_cmp, w_cmp_k, k_gain, *tabs, is_key=True, rows_per_seq=nh, groups=g)
    vc = compress(cmp_blocks(NSA_WIDTH + NSA_KV_WIDTH), pe_cmp, w_cmp_v, k_gain, *tabs, is_key=False,
                  rows_per_seq=nh, groups=g)
    o_nsa = nsa_attention(q_t, k_n, v_t, kc, vc, gates3)

    qk = conv_silu(pm3, conv_w, 0, 2 * MLSTM_WIDTH)
    h_b = mlstm(qk, pm3, gates3, f_bias, m_norm, 2 * MLSTM_WIDTH, 3 * MLSTM_WIDTH)

    mixed = jnp.concatenate([o_nsa, h_b], axis=-1).reshape(n, NSA_WIDTH + MLSTM_WIDTH)
    x2 = mm(mixed, w_out, out_dtype=F32, residual=x2, tm=1024, tn=1024)
    hf = rmsnorm_rows(x2, w_norm_ffn)
    hidden = mm_swiglu(hf, w_gate, w_up, tm=1024)
    return mm(hidden, w_down, out_dtype=F32, residual=x2, tm=512, tn=1024)


def odd_layer(x2, b, s, pos, w_norm, w_in, r_norm, w_out, w_norm_ffn, w_router, e_gate, e_up, e_down):
    n, dm = x2.shape
    h = rmsnorm_rows(x2, w_norm)
    po3 = mm(h, w_in, out_dtype=BF16, tn=1024).reshape(b, s, -1)
    cos_r, sin_r = trig_tables(pos, _ret_inv_lane())
    y = retention(po3, cos_r.reshape(b, s, LANES), sin_r.reshape(b, s, LANES), r_norm)
    x2 = mm(y.reshape(n, RET_V_WIDTH), w_out, out_dtype=F32, residual=x2, tm=512, tn=1024)
    return moe_layer(x2, w_norm_ffn, w_router, e_gate, e_up, e_down)


def kernel(x, positions, norm_mix_even, w_in_even, nsa_q_gain, nsa_k_gain, w_cmp_k, w_cmp_v, pe_cmp, mlstm_conv, mlstm_f_bias, mlstm_norm, w_out_even, norm_ffn_even, ffn_gate, ffn_up, ffn_down, norm_mix_odd, w_in_odd, ret_norm, w_out_odd, norm_ffn_odd, w_router, exp_gate, exp_up, exp_down):
    b, s, dm = x.shape
    depth = norm_mix_even.shape[0] + norm_mix_odd.shape[0]
    x2 = x.reshape(b * s, dm)
    for layer in range(depth):
        j = layer // 2
        if layer % 2 == 0:
            x2 = even_layer(x2, b, s, positions, norm_mix_even[j], w_in_even[j], nsa_q_gain[j], nsa_k_gain[j],
                            w_cmp_k[j], w_cmp_v[j], pe_cmp[j], mlstm_conv[j], mlstm_f_bias[j], mlstm_norm[j],
                            w_out_even[j], norm_ffn_even[j], ffn_gate[j], ffn_up[j], ffn_down[j])
        else:
            x2 = odd_layer(x2, b, s, positions, norm_mix_odd[j], w_in_odd[j], ret_norm[j], w_out_odd[j],
                           norm_ffn_odd[j], w_router[j], exp_gate[j], exp_up[j], exp_down[j])
    return x2.reshape(b, s, dm)
```

```python
import functools

import numpy as np
import jax
import jax.numpy as jnp
from jax import lax
from jax.experimental import pallas as pl
from jax.experimental.pallas import tpu as pltpu

F32 = jnp.float32
BF16 = jnp.bfloat16
I32 = jnp.int32
HIGHEST = lax.Precision.HIGHEST

HEAD_DIM = 128
NSA_HEADS = 8
NSA_GROUPS = 2
NSA_HPG = NSA_HEADS // NSA_GROUPS
NSA_WIDTH = NSA_HEADS * HEAD_DIM
NSA_KV_WIDTH = NSA_GROUPS * HEAD_DIM
NSA_SLAB = NSA_WIDTH + 6 * NSA_KV_WIDTH
CMP_BLOCK = 32
CMP_STRIDE = 16
SEL_BLOCK = 64
SEL_TOPN = 16
WINDOW = 512
ROPE_DIM = HEAD_DIM // 4
ROPE_THETA = 500000.0
SEL_FORCE = 1.0e6
NEG = -1.0e30
LOWEST = -3.0e38
MLSTM_HEADS = 4
MLSTM_DIM = 256
MLSTM_WIDTH = MLSTM_HEADS * MLSTM_DIM
CONV_WIDTH = 4
RET_HEADS = 8
RET_QK_DIM = 256
RET_V_DIM = 512
RET_QK_WIDTH = RET_HEADS * RET_QK_DIM
RET_V_WIDTH = RET_HEADS * RET_V_DIM
RET_ROPE_THETA = 10000.0
N_EXPERTS = 8
RMS_EPS = 1e-6

LANES = 128
BF16_SUBLANES = 16
V7X_VMEM_BYTES = 64 * 1024 * 1024
VMEM_LIMIT = V7X_VMEM_BYTES - 8 * 1024 * 1024

CHUNK = 256
NSA_TQ = 128
LOG2_E = 1.4426950408889634
V_ROWS = HEAD_DIM + BF16_SUBLANES
NSA_KT = 256
SOFTMAX_ROWS = 64
MOE_TM = 512
CAST_ROWS = 512
GATE_I_LANE = 16
GATE_F_LANE = 20


def _cparams(sem, vmem=VMEM_LIMIT):
    return pltpu.CompilerParams(dimension_semantics=sem, vmem_limit_bytes=vmem)


def _tile(n, target, quantum):
    if n <= target:
        return n
    t = (target // quantum) * quantum
    while t > quantum and n % t:
        t -= quantum
    assert n % t == 0, (n, target, quantum)
    return t


def _sigmoid(x):
    return 1.0 / (1.0 + jnp.exp(-x))


def _dot_nt(a, b):
    return lax.dot_general(a, b, (((1,), (1,)), ((), ())), preferred_element_type=F32)


def _dot_tn(a, b):
    return lax.dot_general(a, b, (((0,), (0,)), ((), ())), preferred_element_type=F32)


def _rmsnorm_kernel(x_ref, g_ref, o_ref):
    x = x_ref[...]
    y = x * lax.rsqrt(jnp.mean(x * x, axis=-1, keepdims=True) + RMS_EPS)
    o_ref[...] = (y * g_ref[...]).astype(o_ref.dtype)


def rmsnorm_rows(x, g, tm=512):
    m, d = x.shape
    tm = _tile(m, tm, 8)
    return pl.pallas_call(
        _rmsnorm_kernel,
        grid=(m // tm,),
        in_specs=[pl.BlockSpec((tm, d), lambda i: (i, 0)), pl.BlockSpec((1, d), lambda i: (0, 0))],
        out_specs=pl.BlockSpec((tm, d), lambda i: (i, 0)),
        out_shape=jax.ShapeDtypeStruct((m, d), BF16),
        compiler_params=_cparams(("parallel",)),
        name="rmsnorm",
    )(x, g.reshape(1, d).astype(F32))


def _stream_weights(be_ref, nx_ref, w_refs, land_refs, wb_refs, sems, tn):
    j = pl.program_id(0)
    i = pl.program_id(1)

    def copies(e, jj):
        c0 = pl.multiple_of(jj * tn, tn)
        return [pltpu.make_async_copy(w.at[e, :, pl.ds(c0, tn)], land, sems.at[n])
                for n, (w, land) in enumerate(zip(w_refs, land_refs))]

    @pl.when((i == 0) | (be_ref[i] != be_ref[jnp.maximum(i - 1, 0)]))
    def _():
        @pl.when((i == 0) & (j == 0))
        def _():
            for c in copies(be_ref[0], 0):
                c.start()

        for c in copies(be_ref[i], j):
            c.wait()
        rows = int(np.gcd(land_refs[0].shape[0], CAST_ROWS))

        def cast_rows(s, carry):
            r0 = pl.multiple_of(s * rows, rows)
            for land, wb in zip(land_refs, wb_refs):
                wb[pl.ds(r0, rows), :] = land[pl.ds(r0, rows), :].astype(BF16)
            return carry

        lax.fori_loop(0, land_refs[0].shape[0] // rows, cast_rows, 0)
        nxt = nx_ref[i]

        @pl.when(nxt >= 0)
        def _():
            for c in copies(nxt, j):
                c.start()

        @pl.when((nxt < 0) & (j + 1 < pl.num_programs(0)))
        def _():
            for c in copies(be_ref[0], j + 1):
                c.start()


def _next_group_expert(blk_e, n_experts):
    e = jnp.arange(n_experts, dtype=I32)
    present = jnp.any(blk_e[:, None] == e[None, :], axis=0)
    later = jnp.where((e[None, :] > blk_e[:, None]) & present[None, :], e[None, :], n_experts)
    nxt = jnp.min(later, axis=1)
    return jnp.where(nxt == n_experts, -1, nxt).astype(I32)


def _gmm_kernel(be_ref, nx_ref, nu_ref, x_ref, w_ref, *rest, has_res, tn):
    if has_res:
        r_ref, o_ref, land_ref, wb_ref, sems = rest
    else:
        o_ref, land_ref, wb_ref, sems = rest
    i = pl.program_id(1)
    _stream_weights(be_ref, nx_ref, [w_ref], [land_ref], [wb_ref], sems, tn)

    @pl.when(i < nu_ref[0])
    def _():
        acc = jnp.dot(x_ref[...], wb_ref[...], preferred_element_type=F32)
        if has_res:
            acc = r_ref[...] + acc
        o_ref[...] = acc.astype(o_ref.dtype)

    @pl.when(i >= nu_ref[0])
    def _():
        o_ref[...] = jnp.zeros_like(o_ref)


def gmm(x, w, blk_e, nused, *, tm, tn, out_dtype, residual=None):
    m, k = x.shape
    e, k2, n = w.shape
    assert k == k2 and m % tm == 0 and n % tn == 0
    nb = m // tm
    in_specs = [
        pl.BlockSpec((tm, k), lambda j, i, be, nx, nu: (jnp.minimum(i, nu[0] - 1), 0)),
        pl.BlockSpec(memory_space=pl.ANY),
    ]
    args = [x, w]
    if residual is not None:
        in_specs.append(pl.BlockSpec((tm, tn), lambda j, i, be, nx, nu: (i, j)))
        args.append(residual)
    return pl.pallas_call(
        functools.partial(_gmm_kernel, has_res=residual is not None, tn=tn),
        grid_spec=pltpu.PrefetchScalarGridSpec(
            num_scalar_prefetch=3,
            grid=(n // tn, nb),
            in_specs=in_specs,
            out_specs=pl.BlockSpec((tm, tn), lambda j, i, be, nx, nu: (i, j)),
            scratch_shapes=[pltpu.VMEM((k, tn), w.dtype), pltpu.VMEM((k, tn), BF16),
                            pltpu.SemaphoreType.DMA((1,))],
        ),
        out_shape=jax.ShapeDtypeStruct((m, n), out_dtype),
        compiler_params=_cparams(("arbitrary", "arbitrary")),
        name="gmm",
    )(blk_e, _next_group_expert(blk_e, e), nused, *args)


def _gmm_swiglu_kernel(be_ref, nx_ref, nu_ref, x_ref, wg_ref, wu_ref, o_ref, lg_ref, lu_ref, wgb_ref, wub_ref,
                       sems, *, tn):
    i = pl.program_id(1)
    _stream_weights(be_ref, nx_ref, [wg_ref, wu_ref], [lg_ref, lu_ref], [wgb_ref, wub_ref], sems, tn)

    @pl.when(i < nu_ref[0])
    def _():
        x = x_ref[...]
        g = jnp.dot(x, wgb_ref[...], preferred_element_type=F32)
        u = jnp.dot(x, wub_ref[...], preferred_element_type=F32)
        o_ref[...] = (g * _sigmoid(g) * u).astype(o_ref.dtype)

    @pl.when(i >= nu_ref[0])
    def _():
        o_ref[...] = jnp.zeros_like(o_ref)


def gmm_swiglu(x, wg, wu, blk_e, nused, *, tm, tn):
    m, k = x.shape
    e, k2, n = wg.shape
    assert k == k2 and wu.shape == wg.shape and m % tm == 0 and n % tn == 0
    nb = m // tm
    hbm = pl.BlockSpec(memory_space=pl.ANY)
    return pl.pallas_call(
        functools.partial(_gmm_swiglu_kernel, tn=tn),
        grid_spec=pltpu.PrefetchScalarGridSpec(
            num_scalar_prefetch=3,
            grid=(n // tn, nb),
            in_specs=[pl.BlockSpec((tm, k), lambda j, i, be, nx, nu: (jnp.minimum(i, nu[0] - 1), 0)), hbm, hbm],
            out_specs=pl.BlockSpec((tm, tn), lambda j, i, be, nx, nu: (i, j)),
            scratch_shapes=[pltpu.VMEM((k, tn), wg.dtype), pltpu.VMEM((k, tn), wu.dtype),
                            pltpu.VMEM((k, tn), BF16), pltpu.VMEM((k, tn), BF16),
                            pltpu.SemaphoreType.DMA((2,))],
        ),
        out_shape=jax.ShapeDtypeStruct((m, n), BF16),
        compiler_params=_cparams(("arbitrary", "arbitrary")),
        name="gmm_swiglu",
    )(blk_e, _next_group_expert(blk_e, e), nused, x, wg, wu)


def _dense_blocks(m, tm):
    nb = m // tm
    return jnp.zeros((nb,), I32), jnp.full((1,), nb, I32)


def mm(x, w, *, out_dtype, residual=None, tm=2048, tn=512):
    m, k = x.shape
    n = w.shape[1]
    tm = _tile(m, tm, 16)
    tn = _tile(n, tn, LANES)
    be, nu = _dense_blocks(m, tm)
    return gmm(x, w[None], be, nu, tm=tm, tn=tn, out_dtype=out_dtype, residual=residual)


def mm_swiglu(x, wg, wu, *, tm=2048, tn=512):
    m = x.shape[0]
    tm = _tile(m, tm, 16)
    tn = _tile(wg.shape[1], tn, LANES)
    be, nu = _dense_blocks(m, tm)
    return gmm_swiglu(x, wg[None], wu[None], be, nu, tm=tm, tn=tn)


def _trig_kernel(pos_ref, inv_ref, cos_ref, sin_ref):
    ang = pos_ref[...] * inv_ref[...]
    cos_ref[...] = jnp.cos(ang)
    sin_ref[...] = jnp.sin(ang)


def trig_tables(pos, inv_lane):
    r = pos.size
    pos_b = jnp.broadcast_to(pos.astype(F32).reshape(r, 1), (r, LANES))
    tr = _tile(r, 512, 8)
    spec = pl.BlockSpec((tr, LANES), lambda i: (i, 0))
    return pl.pallas_call(
        _trig_kernel,
        grid=(r // tr,),
        in_specs=[spec, pl.BlockSpec((1, LANES), lambda i: (0, 0))],
        out_specs=[spec, spec],
        out_shape=[jax.ShapeDtypeStruct((r, LANES), F32)] * 2,
        compiler_params=_cparams(("parallel",)),
        name="trig_tables",
    )(pos_b, inv_lane.reshape(1, LANES))


def _nsa_inv_lane():
    half = ROPE_DIM // 2
    inv = jnp.power(jnp.float32(ROPE_THETA), -jnp.arange(half, dtype=F32) * (2.0 / ROPE_DIM))
    return jnp.concatenate([inv, inv, jnp.zeros((LANES - ROPE_DIM,), F32)])


def _ret_inv_lane():
    half = RET_QK_DIM // 2
    return jnp.power(jnp.float32(RET_ROPE_THETA), -jnp.arange(half, dtype=F32) * (2.0 / RET_QK_DIM))


def _norm_rope_head(x, gain, cos, sin):
    half = ROPE_DIM // 2
    y = x * lax.rsqrt(jnp.mean(x * x, axis=-1, keepdims=True) + RMS_EPS) * gain
    lane = lax.broadcasted_iota(I32, y.shape, 1)
    from_hi = jnp.where(lane < half, -sin, 0.0)
    from_lo = jnp.where((lane >= half) & (lane < ROPE_DIM), sin, 0.0)
    return (y * cos + pltpu.roll(y, LANES - half, 1) * from_hi + pltpu.roll(y, half, 1) * from_lo)


def _nsa_prep_kernel(p_ref, cos_ref, sin_ref, qg_ref, kg_ref, qt_ref, kn_ref, vt_ref):
    cos = cos_ref[0]
    sin = sin_ref[0]
    scale = HEAD_DIM ** -0.5 * LOG2_E
    d = HEAD_DIM
    g = NSA_GROUPS
    for hd in range(NSA_HEADS):
        q = _norm_rope_head(p_ref[0, :, hd * d:(hd + 1) * d].astype(F32), qg_ref[...], cos, sin)
        qt_ref[0, hd * d:(hd + 1) * d, :] = (q * scale).T.astype(BF16)
    for n, slab in enumerate((2, 4)):
        for gi in range(g):
            off = NSA_WIDTH + slab * NSA_KV_WIDTH + gi * d
            k = _norm_rope_head(p_ref[0, :, off:off + d].astype(F32), kg_ref[...], cos, sin)
            kn_ref[0, :, (n * g + gi) * d:(n * g + gi + 1) * d] = k.astype(BF16)
    for n, slab in enumerate((3, 5)):
        for gi in range(g):
            off = NSA_WIDTH + slab * NSA_KV_WIDTH + gi * d
            v = p_ref[0, :, off:off + d].astype(F32)
            r0 = (n * g + gi) * V_ROWS
            vt_ref[0, r0:r0 + d, :] = v.T.astype(BF16)
            vt_ref[0, r0 + d:r0 + V_ROWS, :] = jnp.ones((V_ROWS - d, v.shape[0]), BF16)


def nsa_prep(p3, cos, sin, q_gain, k_gain):
    b, s, _ = p3.shape
    t = _tile(s, 256, LANES)
    d = HEAD_DIM
    tab = pl.BlockSpec((1, t, LANES), lambda bi, i: (bi, i, 0))
    gain = pl.BlockSpec((1, d), lambda bi, i: (0, 0))
    kv = 2 * NSA_KV_WIDTH
    vr = 2 * NSA_GROUPS * V_ROWS
    return pl.pallas_call(
        _nsa_prep_kernel,
        grid=(b, s // t),
        in_specs=[pl.BlockSpec((1, t, NSA_SLAB), lambda bi, i: (bi, i, 0)), tab, tab, gain, gain],
        out_specs=[pl.BlockSpec((1, NSA_WIDTH, t), lambda bi, i: (bi, 0, i)),
                   pl.BlockSpec((1, t, kv), lambda bi, i: (bi, i, 0)),
                   pl.BlockSpec((1, vr, t), lambda bi, i: (bi, 0, i))],
        out_shape=[jax.ShapeDtypeStruct((b, NSA_WIDTH, s), BF16),
                   jax.ShapeDtypeStruct((b, s, kv), BF16),
                   jax.ShapeDtypeStruct((b, vr, s), BF16)],
        compiler_params=_cparams(("parallel", "parallel")),
        name="nsa_prep",
    )(p3, cos, sin, q_gain.reshape(1, d), k_gain.reshape(1, d))


def _compress_kernel(blk_ref, pe_ref, w_ref, kg_ref, cos_ref, sin_ref, o_ref, *, is_key):
    a = (blk_ref[...].astype(F32) + pe_ref[...]).astype(BF16)
    y = jnp.dot(a, w_ref[...].astype(BF16), preferred_element_type=F32)
    if is_key:
        o_ref[0] = _norm_rope_head(y, kg_ref[...], cos_ref[0], sin_ref[0]).astype(BF16)
    else:
        o_ref[0] = y.T.astype(BF16)


def compress(blk, pe, w, k_gain, cos_c, sin_c, *, is_key, rows_per_seq, groups):
    r, kdim = blk.shape
    t = rows_per_seq
    d = HEAD_DIM
    tab = pl.BlockSpec((1, t, LANES), lambda i: (i // groups, 0, 0))
    out_blk = (1, t, d) if is_key else (1, d, t)
    return pl.pallas_call(
        functools.partial(_compress_kernel, is_key=is_key),
        grid=(r // t,),
        in_specs=[pl.BlockSpec((t, kdim), lambda i: (i, 0)),
                  pl.BlockSpec((1, kdim), lambda i: (0, 0)),
                  pl.BlockSpec((kdim, d), lambda i: (0, 0)),
                  pl.BlockSpec((1, d), lambda i: (0, 0)),
                  tab, tab],
        out_specs=pl.BlockSpec(out_blk, lambda i: (i, 0, 0)),
        out_shape=jax.ShapeDtypeStruct((r // t,) + out_blk[1:], BF16),
        compiler_params=_cparams(("parallel",)),
        name="nsa_compress",
    )(blk, pe.reshape(1, kdim), w, k_gain.reshape(1, d), cos_c, sin_c)


def _nsa_attn_kernel(qt_ref, kc_ref, vct_ref, ks_ref, kw_ref, vst_ref, vwt_ref, et_ref, ovt_ref, gate_ref,
                     o_ref, acc_sc, val_sc, sa_sc, sb_sc, pa_sc, pb_sc, sc_sc, pc_sc, pg_sc, sw_sc, pw_sc, ow_sc,
                     *, tq, kt, wk, ns, n_top):
    i = pl.program_id(2)
    t0 = i * tq
    cols = NSA_HPG * tq
    d = HEAD_DIM
    q_t = jnp.concatenate([qt_ref[0, p * d:(p + 1) * d, :] for p in range(NSA_HPG)], axis=1)
    t_lane = t0 + (lax.broadcasted_iota(I32, (1, cols), 1) & (tq - 1))

    ncp = kc_ref.shape[1]
    rs = SOFTMAX_ROWS
    sc_sc[...] = jnp.dot(kc_ref[0], q_t, preferred_element_type=F32)
    w0 = pl.multiple_of(jnp.maximum(t0 + tq - wk, 0), LANES)
    sw_sc[...] = jnp.dot(kw_ref[0, pl.ds(w0, wk), :], q_t, preferred_element_type=F32)

    def cmask(r0):
        c_end = (r0 + lax.broadcasted_iota(I32, (rs, cols), 0)) * CMP_STRIDE + (CMP_BLOCK - 1)
        return c_end <= t_lane

    m_c = jnp.full((1, cols), NEG, F32)
    for r0 in range(0, ncp, rs):
        m_c = jnp.maximum(m_c, jnp.max(jnp.where(cmask(r0), sc_sc[r0:r0 + rs, :], NEG), axis=0, keepdims=True))
    den_c = jnp.zeros((1, cols), F32)
    for r0 in range(0, ncp, rs):
        e = jnp.where(cmask(r0), jnp.exp2(sc_sc[r0:r0 + rs, :] - m_c), 0.0)
        sc_sc[r0:r0 + rs, :] = e
        den_c = den_c + jnp.sum(e, axis=0, keepdims=True)
    inv_c = 1.0 / jnp.maximum(den_c, 1e-30)
    for r0 in range(0, ncp, rs):
        p = sc_sc[r0:r0 + rs, :] * inv_c
        pc_sc[r0:r0 + rs, :] = p.astype(BF16)
        p_grp = p[:, 0:tq]
        for hp in range(1, NSA_HPG):
            p_grp = p_grp + p[:, hp * tq:(hp + 1) * tq]
        pg_sc[r0:r0 + rs, :] = p_grp
    o_cmp = jnp.dot(vct_ref[0], pc_sc[...], preferred_element_type=F32)

    m_w = jnp.full((1, cols), NEG, F32)
    for r0 in range(0, wk, rs):
        kpos = w0 + r0 + lax.broadcasted_iota(I32, (rs, cols), 0)
        wmask = (kpos <= t_lane) & (kpos > t_lane - WINDOW)
        s = jnp.where(wmask, sw_sc[r0:r0 + rs, :], NEG)
        sw_sc[r0:r0 + rs, :] = s
        m_w = jnp.maximum(m_w, jnp.max(s, axis=0, keepdims=True))
    for r0 in range(0, wk, rs):
        pw_sc[r0:r0 + rs, :] = jnp.exp2(sw_sc[r0:r0 + rs, :] - m_w).astype(BF16)
    o_win = jnp.dot(vwt_ref[0, :, pl.ds(w0, wk)], pw_sc[...], preferred_element_type=F32)
    ow_sc[...] = o_win[:d] * (1.0 / o_win[d:d + 1])

    imp = jnp.dot(ovt_ref[...], pg_sc[...], precision=HIGHEST, preferred_element_type=F32)
    nsp = val_sc.shape[0]
    jb = lax.broadcasted_iota(I32, (nsp, tq), 0)
    cur = (t0 + lax.broadcasted_iota(I32, (1, tq), 1)) // SEL_BLOCK
    forced = (jb == 0) | (jb == cur) | (jb == cur - 1)
    val = jnp.where(jb <= cur, jnp.where(forced, SEL_FORCE, imp[:nsp]), NEG)
    val = jnp.where(jb < ns, val, LOWEST)
    val_sc[...] = val
    beaten = jnp.zeros((nsp, tq), F32)
    for j2 in range(ns):
        r = val_sc[j2:j2 + 1, :]
        ge = jnp.where(r >= val, 1.0, 0.0)
        gt = jnp.where(r > val, 1.0, 0.0)
        beaten = beaten + jnp.where(jb > j2, ge, gt)
    past = jb < t0 // SEL_BLOCK
    bias = jnp.where(past & (beaten < n_top), 0.0, jnp.where(jb < ns, NEG, 0.0)).astype(BF16)
    bias = jnp.concatenate([bias, jnp.zeros((LANES - nsp, tq), BF16)], axis=0)
    q_aug = jnp.concatenate([q_t, jnp.concatenate([bias] * NSA_HPG, axis=1)], axis=0)

    d0 = pl.multiple_of(t0, tq)
    s_d = jnp.dot(ks_ref[0, pl.ds(d0, tq), :], q_t, preferred_element_type=F32)
    s_d = jnp.where(d0 + lax.broadcasted_iota(I32, (tq, cols), 0) <= t_lane, s_d, NEG)
    m_d = jnp.max(s_d, axis=0, keepdims=True)
    p_d = jnp.exp2(s_d - m_d)
    acc_sc[...] = jnp.dot(vst_ref[0, :, pl.ds(d0, tq)], p_d.astype(BF16), preferred_element_type=F32)

    n_pairs = (t0 + 2 * kt - 1) // (2 * kt)
    last_a = jnp.maximum(n_pairs - 1, 0) * (2 * kt)

    def scores(k0):
        k0 = pl.multiple_of(k0, kt)
        k_aug = jnp.concatenate([ks_ref[0, pl.ds(k0, kt), :], et_ref[pl.ds(k0, kt), :]], axis=1)
        return jnp.dot(k_aug, q_aug, preferred_element_type=F32)

    def values(p_ref, k0):
        k0 = pl.multiple_of(k0, kt)
        return jnp.dot(vst_ref[0, :, pl.ds(k0, kt)], p_ref[...], preferred_element_type=F32)

    def softmax_update(s_ref, p_ref, m_prev):
        m_new = jnp.maximum(m_prev, jnp.max(s_ref[...], axis=0, keepdims=True))
        for r0 in range(0, kt, SOFTMAX_ROWS):
            p_ref[r0:r0 + SOFTMAX_ROWS, :] = jnp.exp2(s_ref[r0:r0 + SOFTMAX_ROWS, :] - m_new).astype(BF16)
        return m_new, jnp.exp2(m_prev - m_new)

    def sel_step(j, carry):
        m_run, alpha_b = carry
        k0 = j * (2 * kt)
        sb_sc[...] = scores(k0 + kt)
        acc_sc[...] = alpha_b * acc_sc[...] + values(pb_sc, jnp.maximum(k0 - kt, 0))
        m_run, alpha_a = softmax_update(sa_sc, pa_sc, m_run)
        sa_sc[...] = scores(jnp.minimum(k0 + 2 * kt, last_a))
        acc_sc[...] = alpha_a * acc_sc[...] + values(pa_sc, k0)
        return softmax_update(sb_sc, pb_sc, m_run)

    sa_sc[...] = scores(0)
    pb_sc[...] = jnp.zeros(pb_sc.shape, BF16)
    _, alpha_last = lax.fori_loop(0, n_pairs, sel_step, (m_d, jnp.ones((1, cols), F32)))
    acc = alpha_last * acc_sc[...] + values(pb_sc, last_a + kt)
    o_sel = acc[:d] * (1.0 / acc[d:d + 1])

    g_t = _sigmoid(gate_ref[0]).T
    for p in range(NSA_HPG):
        sl = slice(p * tq, (p + 1) * tq)
        o_t = (g_t[3 * p:3 * p + 1] * o_cmp[:, sl] + g_t[3 * p + 1:3 * p + 2] * o_sel[:, sl]
               + g_t[3 * p + 2:3 * p + 3] * ow_sc[:, sl])
        o_ref[0, :, p * d:(p + 1) * d] = o_t.T.astype(BF16)


def nsa_attention(q_t, k_n, v_t, kc, vc_t, gates3):
    b, _, s = q_t.shape
    g, d = NSA_GROUPS, HEAD_DIM
    ncp = kc.shape[1]
    tq = NSA_TQ
    kt = min(NSA_KT, s)
    wk = WINDOW + tq
    ns = s // SEL_BLOCK
    assert s % (2 * kt) == 0 and s >= wk and ns <= LANES and tq == LANES
    assert kt % SOFTMAX_ROWS == 0 and ncp % SOFTMAX_ROWS == 0 and wk % SOFTMAX_ROWS == 0
    n_top = min(SEL_TOPN, ns)
    cols = NSA_HPG * tq

    key_blk = np.arange(s) // SEL_BLOCK
    e_t = jnp.asarray(key_blk[:, None] == np.arange(LANES)[None, :], dtype=BF16)
    c_start = np.arange(ncp) * CMP_STRIDE
    j_start = np.arange(LANES) * SEL_BLOCK
    overlap = ((c_start[:, None] < j_start[None, :] + SEL_BLOCK) & (c_start[:, None] + CMP_BLOCK > j_start[None, :])
               & (np.arange(ncp)[:, None] < s // CMP_STRIDE - 1) & (np.arange(LANES)[None, :] < ns))
    overlap_t = jnp.asarray(overlap.T.astype(np.float32))

    q_rows = NSA_HPG * d
    return pl.pallas_call(
        functools.partial(_nsa_attn_kernel, tq=tq, kt=kt, wk=wk, ns=ns, n_top=n_top),
        grid=(b, g, s // tq),
        in_specs=[pl.BlockSpec((1, q_rows, tq), lambda bi, gi, i: (bi, gi, i)),
                  pl.BlockSpec((1, ncp, d), lambda bi, gi, i: (bi * g + gi, 0, 0)),
                  pl.BlockSpec((1, d, ncp), lambda bi, gi, i: (bi * g + gi, 0, 0)),
                  pl.BlockSpec((1, s, d), lambda bi, gi, i: (bi, 0, gi)),
                  pl.BlockSpec((1, s, d), lambda bi, gi, i: (bi, 0, g + gi)),
                  pl.BlockSpec((1, V_ROWS, s), lambda bi, gi, i: (bi, gi, 0)),
                  pl.BlockSpec((1, V_ROWS, s), lambda bi, gi, i: (bi, g + gi, 0)),
                  pl.BlockSpec((s, LANES), lambda bi, gi, i: (0, 0)),
                  pl.BlockSpec((LANES, ncp), lambda bi, gi, i: (0, 0)),
                  pl.BlockSpec((1, tq, LANES), lambda bi, gi, i: (bi, i, gi))],
        out_specs=pl.BlockSpec((1, tq, q_rows), lambda bi, gi, i: (bi, i, gi)),
        out_shape=jax.ShapeDtypeStruct((b, s, NSA_WIDTH), BF16),
        scratch_shapes=[pltpu.VMEM((V_ROWS, cols), F32), pltpu.VMEM((-(-ns // 8) * 8, tq), F32),
                        pltpu.VMEM((kt, cols), F32), pltpu.VMEM((kt, cols), F32),
                        pltpu.VMEM((kt, cols), BF16), pltpu.VMEM((kt, cols), BF16),
                        pltpu.VMEM((ncp, cols), F32), pltpu.VMEM((ncp, cols), BF16), pltpu.VMEM((ncp, tq), F32),
                        pltpu.VMEM((wk, cols), F32), pltpu.VMEM((wk, cols), BF16), pltpu.VMEM((d, cols), F32)],
        compiler_params=_cparams(("parallel", "parallel", "arbitrary")),
        name="nsa_attention",
    )(q_t, kc, vc_t, k_n, k_n, v_t, v_t, e_t, overlap_t, gates3)


def _conv_kernel(cur_ref, prev_ref, w_ref, o_ref, *, tc):
    i = pl.program_id(1)
    cur = cur_ref[0].astype(F32)
    prev = jnp.where(i > 0, prev_ref[0].astype(F32), 0.0)
    pad = prev.shape[0] // 2
    full = jnp.concatenate([prev[pad:], cur], axis=0)
    y = None
    for kk in range(CONV_WIDTH):
        off = pad - (CONV_WIDTH - 1) + kk
        term = w_ref[kk:kk + 1, :] * full[off:off + tc]
        y = term if y is None else y + term
    o_ref[0] = (y * _sigmoid(y)).astype(BF16)


def conv_silu(p3, w, col0, width):
    b, s, _ = p3.shape
    tc = _tile(s, 512, BF16_SUBLANES)
    cw = 512
    assert col0 % cw == 0 and width % cw == 0
    c0 = col0 // cw
    halo = BF16_SUBLANES
    return pl.pallas_call(
        functools.partial(_conv_kernel, tc=tc),
        grid=(b, s // tc, width // cw),
        in_specs=[pl.BlockSpec((1, tc, cw), lambda bi, i, j: (bi, i, c0 + j)),
                  pl.BlockSpec((1, halo, cw), lambda bi, i, j: (bi, jnp.maximum(i * (tc // halo) - 1, 0), c0 + j)),
                  pl.BlockSpec((CONV_WIDTH, cw), lambda bi, i, j: (0, j))],
        out_specs=pl.BlockSpec((1, tc, cw), lambda bi, i, j: (bi, i, j)),
        out_shape=jax.ShapeDtypeStruct((b, s, width), BF16),
        compiler_params=_cparams(("parallel", "parallel", "parallel")),
        name="mlstm_conv",
    )(p3, p3, w)


def _log_sigmoid(x):
    return jnp.minimum(x, 0.0) - jnp.log1p(jnp.exp(-jnp.abs(x)))


def _mlstm_kernel(fb_ref, q_ref, k_ref, v_ref, ob_ref, gate_ref, nw_ref, o_ref, c_sc, n_sc, m_sc, *, cl):
    @pl.when(pl.program_id(1) == 0)
    def _():
        c_sc[...] = jnp.zeros(c_sc.shape, F32)
        n_sc[...] = jnp.zeros(n_sc.shape, F32)
        m_sc[...] = jnp.zeros(m_sc.shape, F32)

    dh = MLSTM_DIM
    slab = gate_ref[0]
    lane = lax.broadcasted_iota(I32, slab.shape, 1)
    r_i = lax.broadcasted_iota(I32, (cl, cl), 0)
    c_i = lax.broadcasted_iota(I32, (cl, cl), 1)
    eye = r_i == c_i
    tri = c_i <= r_i

    def head(hd, carry):
        off = pl.multiple_of(hd * dh, dh)
        q = q_ref[0, :, pl.ds(off, dh)]
        v = v_ref[0, :, pl.ds(off, dh)]
        ks32 = k_ref[0, :, pl.ds(off, dh)].astype(F32) * (dh ** -0.5)
        ks = ks32.astype(BF16)
        i_col = jnp.sum(jnp.where(lane == GATE_I_LANE + hd, slab, 0.0), axis=-1, keepdims=True)
        f_col = jnp.sum(jnp.where(lane == GATE_F_LANE + hd, slab, 0.0), axis=-1, keepdims=True)
        lf_col = _log_sigmoid(f_col + fb_ref[hd])

        lf_row = jnp.sum(jnp.where(eye, lf_col, 0.0), axis=0, keepdims=True)
        ig_row = jnp.sum(jnp.where(eye, i_col, 0.0), axis=0, keepdims=True)
        a_col = jnp.sum(jnp.where(tri, lf_row, 0.0), axis=1, keepdims=True)
        a_row = jnp.sum(jnp.where(r_i <= c_i, lf_col, 0.0), axis=0, keepdims=True)
        m_prev = m_sc[hd]

        dlog = jnp.where(tri, a_col - a_row + ig_row, NEG)
        inter = a_col + m_prev
        mt = jnp.maximum(inter, jnp.max(dlog, axis=-1, keepdims=True))
        wm = jnp.exp(dlog - mt) * _dot_nt(q, ks)
        e_col = jnp.exp(inter - mt)
        num = e_col * jnp.dot(q, c_sc[hd].astype(BF16), preferred_element_type=F32) \
            + jnp.dot(wm.astype(BF16), v, preferred_element_type=F32)
        qn = jnp.sum(q.astype(F32) * n_sc[hd], axis=-1, keepdims=True)
        den = e_col * qn + jnp.sum(wm, axis=-1, keepdims=True)
        hh = num / jnp.maximum(jnp.abs(den), jnp.exp(-mt))

        a_last = jnp.sum(lf_row, axis=-1, keepdims=True)
        gs = a_last - a_col + i_col
        m_new = jnp.maximum(a_last + m_prev, jnp.max(gs, axis=0, keepdims=True))
        decay = jnp.exp(a_last + m_prev - m_new)
        wk = jnp.exp(gs - m_new) * ks32
        c_sc[hd] = decay * c_sc[hd] + _dot_tn(wk.astype(BF16), v)
        n_sc[hd] = decay * n_sc[hd] + jnp.sum(wk, axis=0, keepdims=True)
        m_sc[hd] = m_new

        y = hh * lax.rsqrt(jnp.mean(hh * hh, axis=-1, keepdims=True) + RMS_EPS) * nw_ref[:, pl.ds(off, dh)]
        o_ref[0, :, pl.ds(off, dh)] = (y * _sigmoid(ob_ref[0, :, pl.ds(off, dh)].astype(F32))).astype(BF16)
        return carry

    lax.fori_loop(0, MLSTM_HEADS, head, 0)


def mlstm(qk, p3, gates3, f_bias, norm_w, v_col0, o_col0):
    b, s, _ = qk.shape
    cl = min(CHUNK, s)
    dh = MLSTM_DIM
    nh = MLSTM_HEADS
    w = nh * dh
    assert v_col0 % w == 0 and o_col0 % w == 0 and s % cl == 0
    vb, ob = v_col0 // w, o_col0 // w
    return pl.pallas_call(
        functools.partial(_mlstm_kernel, cl=cl),
        grid=(b, s // cl),
        in_specs=[pl.BlockSpec(memory_space=pltpu.SMEM),
                  pl.BlockSpec((1, cl, w), lambda bi, c: (bi, c, 0)),
                  pl.BlockSpec((1, cl, w), lambda bi, c: (bi, c, 1)),
                  pl.BlockSpec((1, cl, w), lambda bi, c: (bi, c, vb)),
                  pl.BlockSpec((1, cl, w), lambda bi, c: (bi, c, ob)),
                  pl.BlockSpec((1, cl, LANES), lambda bi, c: (bi, c, 0)),
                  pl.BlockSpec((1, w), lambda bi, c: (0, 0))],
        out_specs=pl.BlockSpec((1, cl, w), lambda bi, c: (bi, c, 0)),
        out_shape=jax.ShapeDtypeStruct((b, s, w), BF16),
        scratch_shapes=[pltpu.VMEM((nh, dh, dh), F32), pltpu.VMEM((nh, 1, dh), F32), pltpu.VMEM((nh, 1, 1), F32)],
        compiler_params=_cparams(("parallel", "arbitrary")),
        name="mlstm",
    )(f_bias.astype(F32), qk, qk, p3, p3, gates3, norm_w.reshape(1, w).astype(F32))


def _ret_kernel(cd_ref, q_ref, k_ref, v_ref, g_ref, cos_ref, sin_ref, nw_ref, dm_ref, xi_ref, zeta_ref,
                o_ref, r_sc, *, cl):
    @pl.when(pl.program_id(1) == 0)
    def _():
        r_sc[...] = jnp.zeros(r_sc.shape, F32)

    cos = cos_ref[0]
    sin = sin_ref[0]
    dk, dv = RET_QK_DIM, RET_V_DIM
    half = dk // 2
    scale = dk ** -0.5

    def rope(x):
        x1, x2 = x[:, :half], x[:, half:]
        return x1 * cos - x2 * sin, x1 * sin + x2 * cos

    def head(hd, carry):
        qo = pl.multiple_of(hd * dk, dk)
        vo = pl.multiple_of(hd * dv, dv)
        q1, q2 = rope(q_ref[0, :, pl.ds(qo, dk)].astype(F32))
        qr = jnp.concatenate([q1, q2], axis=1).astype(BF16)
        k1, k2 = rope(k_ref[0, :, pl.ds(qo, dk)].astype(F32))
        zeta = zeta_ref[hd] * scale
        kr = jnp.concatenate([k1 * scale, k2 * scale], axis=1).astype(BF16)
        kz = jnp.concatenate([k1 * zeta, k2 * zeta], axis=1).astype(BF16)
        v = v_ref[0, :, pl.ds(vo, dv)]

        inner = jnp.dot((_dot_nt(qr, kr) * dm_ref[hd]).astype(BF16), v, preferred_element_type=F32)
        xi = xi_ref[hd]
        cross = jnp.dot(qr, r_sc[hd].astype(BF16), preferred_element_type=F32)
        cross = cross * jnp.concatenate([xi] * (dv // LANES), axis=1)
        r_sc[hd] = cd_ref[hd] * r_sc[hd] + _dot_tn(kz, v)

        y = inner + cross
        y = y * lax.rsqrt(jnp.mean(y * y, axis=-1, keepdims=True) + RMS_EPS) * nw_ref[:, pl.ds(vo, dv)]
        gg = g_ref[0, :, pl.ds(vo, dv)].astype(F32)
        o_ref[0, :, pl.ds(vo, dv)] = (y * (gg * _sigmoid(gg))).astype(BF16)
        return carry

    lax.fori_loop(0, RET_HEADS, head, 0)


def retention(po3, cos, sin, norm_w):
    b, s, _ = po3.shape
    cl = min(CHUNK, s)
    nh, dk, dv = RET_HEADS, RET_QK_DIM, RET_V_DIM
    log_g = jnp.log1p(-jnp.exp2(-5.0 - jnp.arange(nh, dtype=F32)))
    idx = jnp.arange(cl, dtype=F32)
    diff = idx[:, None] - idx[None, :]
    dm = jnp.where(diff >= 0, jnp.exp(jnp.maximum(diff, 0.0) * log_g[:, None, None]), 0.0)
    xi = jnp.broadcast_to(jnp.exp((idx + 1.0) * log_g[:, None])[..., None], (nh, cl, LANES))
    zeta = jnp.broadcast_to(jnp.exp((cl - 1.0 - idx) * log_g[:, None])[..., None], (nh, cl, LANES))
    chunk_decay = jnp.exp(cl * log_g)
    qk_w, v_w = RET_QK_WIDTH, RET_V_WIDTH
    assert v_w == 2 * qk_w
    tab = pl.BlockSpec((1, cl, LANES), lambda bi, c: (bi, c, 0))

    def table(shape):
        return pl.BlockSpec(shape, lambda bi, c: (0, 0, 0))

    return pl.pallas_call(
        functools.partial(_ret_kernel, cl=cl),
        grid=(b, s // cl),
        in_specs=[pl.BlockSpec(memory_space=pltpu.SMEM),
                  pl.BlockSpec((1, cl, qk_w), lambda bi, c: (bi, c, 0)),
                  pl.BlockSpec((1, cl, qk_w), lambda bi, c: (bi, c, 1)),
                  pl.BlockSpec((1, cl, v_w), lambda bi, c: (bi, c, 1)),
                  pl.BlockSpec((1, cl, v_w), lambda bi, c: (bi, c, 2)),
                  tab, tab,
                  pl.BlockSpec((1, v_w), lambda bi, c: (0, 0)),
                  table((nh, cl, cl)), table((nh, cl, LANES)), table((nh, cl, LANES))],
        out_specs=pl.BlockSpec((1, cl, v_w), lambda bi, c: (bi, c, 0)),
        out_shape=jax.ShapeDtypeStruct((b, s, v_w), BF16),
        scratch_shapes=[pltpu.VMEM((nh, dk, dv), F32)],
        compiler_params=_cparams(("parallel", "arbitrary")),
        name="retention",
    )(chunk_decay, po3, po3, po3, po3, cos, sin, norm_w.reshape(1, v_w).astype(F32), dm, xi, zeta)


def _router_kernel(x_ref, g_ref, wr_ref, xn_ref, route_ref, cnt_ref, carry_sc):
    @pl.when(pl.program_id(0) == 0)
    def _():
        carry_sc[...] = jnp.zeros(carry_sc.shape, F32)

    x = x_ref[...]
    t = x.shape[0]
    y = x * lax.rsqrt(jnp.mean(x * x, axis=-1, keepdims=True) + RMS_EPS) * g_ref[...]
    xn_ref[...] = y.astype(BF16)
    w = wr_ref[...]
    y_hi = y.astype(BF16)
    y_lo = (y - y_hi.astype(F32)).astype(BF16)
    w_hi = w.astype(BF16)
    w_lo = (w - w_hi.astype(F32)).astype(BF16)
    logits = (jnp.dot(y_hi, w_hi, preferred_element_type=F32) + jnp.dot(y_lo, w_hi, preferred_element_type=F32)
              + jnp.dot(y_hi, w_lo, preferred_element_type=F32))
    lane = lax.broadcasted_iota(I32, logits.shape, 1)
    lg = jnp.where(lane < N_EXPERTS, logits, LOWEST)
    v1 = jnp.max(lg, axis=-1, keepdims=True)
    i1 = jnp.min(jnp.where(lg == v1, lane, LANES), axis=-1, keepdims=True)
    lg2 = jnp.where(lane == i1, LOWEST, lg)
    v2 = jnp.max(lg2, axis=-1, keepdims=True)
    i2 = jnp.min(jnp.where(lg2 == v2, lane, LANES), axis=-1, keepdims=True)
    e2 = jnp.exp(v2 - v1)
    g1 = 1.0 / (1.0 + e2)
    g2 = e2 / (1.0 + e2)

    chosen = jnp.where((lane == i1) | (lane == i2), 1.0, 0.0)
    r_i = lax.broadcasted_iota(I32, (t, t), 0)
    c_i = lax.broadcasted_iota(I32, (t, t), 1)
    tri = jnp.where(c_i <= r_i, 1.0, 0.0).astype(BF16)
    seen = jnp.dot(tri, chosen.astype(BF16), preferred_element_type=F32) + carry_sc[...]
    rank1 = jnp.sum(jnp.where(lane == i1, seen, 0.0), axis=-1, keepdims=True) - 1.0
    rank2 = jnp.sum(jnp.where(lane == i2, seen, 0.0), axis=-1, keepdims=True) - 1.0
    total = seen[t - 1:t, :]
    carry_sc[...] = total
    cnt_ref[...] = jnp.broadcast_to(total, cnt_ref.shape)

    route = jnp.where(lane == 0, i1.astype(F32), 0.0)
    route = jnp.where(lane == 1, i2.astype(F32), route)
    route = jnp.where(lane == 2, g1, route)
    route = jnp.where(lane == 3, g2, route)
    route = jnp.where(lane == 4, rank1, route)
    route = jnp.where(lane == 5, rank2, route)
    route_ref[...] = route


def moe_route(x, g, w_router, tm=256):
    m, d = x.shape
    tm = _tile(m, tm, 16)
    wr = jnp.zeros((d, LANES), F32).at[:, :N_EXPERTS].set(w_router.astype(F32))
    return pl.pallas_call(
        _router_kernel,
        grid=(m // tm,),
        in_specs=[pl.BlockSpec((tm, d), lambda i: (i, 0)),
                  pl.BlockSpec((1, d), lambda i: (0, 0)),
                  pl.BlockSpec((d, LANES), lambda i: (0, 0))],
        out_specs=[pl.BlockSpec((tm, d), lambda i: (i, 0)),
                   pl.BlockSpec((tm, LANES), lambda i: (i, 0)),
                   pl.BlockSpec((8, LANES), lambda i: (0, 0))],
        out_shape=[jax.ShapeDtypeStruct((m, d), BF16),
                   jax.ShapeDtypeStruct((m, LANES), F32),
                   jax.ShapeDtypeStruct((8, LANES), F32)],
        scratch_shapes=[pltpu.VMEM((1, LANES), F32)],
        compiler_params=_cparams(("arbitrary",)),
        name="moe_route",
    )(x, g.reshape(1, d).astype(F32), wr)


def _dispatch_kernel(d1_ref, d2_ref, last_ref, src_ref, o_ref, zero_sc, sem, zsem, *, rows, tm, n_experts):
    i = pl.program_id(0)

    @pl.when(i == 0)
    def _():
        zero_sc[...] = jnp.zeros(zero_sc.shape, zero_sc.dtype)
        fills = [pltpu.make_async_copy(zero_sc, o_ref.at[pl.ds(last_ref[e], tm)], zsem) for e in range(n_experts)]
        for c in fills:
            c.start()
        for c in fills:
            c.wait()

        def clear_unused(blk, carry):
            c = pltpu.make_async_copy(zero_sc, o_ref.at[pl.ds(pl.multiple_of(blk * tm, tm), tm)], zsem)
            c.start()
            c.wait()
            return carry

        lax.fori_loop(last_ref[n_experts], o_ref.shape[0] // tm, clear_unused, 0)

    base = i * rows

    def issue(r8, carry):
        for u in range(8):
            r = base + r8 * 8 + u
            pltpu.make_async_copy(src_ref.at[r], o_ref.at[d1_ref[r]], sem).start()
            pltpu.make_async_copy(src_ref.at[r], o_ref.at[d2_ref[r]], sem).start()
        return carry

    lax.fori_loop(0, rows // 8, issue, 0)
    for _ in range(2):
        pltpu.make_async_copy(src_ref.at[pl.ds(0, rows)], o_ref.at[pl.ds(0, rows)], sem).wait()


def moe_dispatch(xn, dest1, dest2, last_block_row, n_rows, tm, rows=1024):
    m, d = xn.shape
    rows = _tile(m, rows, 8)
    slab = (d // LANES, LANES)
    out = pl.pallas_call(
        functools.partial(_dispatch_kernel, rows=rows, tm=tm, n_experts=last_block_row.shape[0] - 1),
        grid_spec=pltpu.PrefetchScalarGridSpec(
            num_scalar_prefetch=3,
            grid=(m // rows,),
            in_specs=[pl.BlockSpec(memory_space=pl.ANY)],
            out_specs=pl.BlockSpec(memory_space=pl.ANY),
            scratch_shapes=[pltpu.VMEM((tm,) + slab, xn.dtype), pltpu.SemaphoreType.DMA(()),
                            pltpu.SemaphoreType.DMA(())],
        ),
        out_shape=jax.ShapeDtypeStruct((n_rows,) + slab, xn.dtype),
        compiler_params=_cparams(("arbitrary",)),
        name="moe_dispatch",
    )(dest1, dest2, last_block_row, xn.reshape((m,) + slab))
    return out.reshape(n_rows, d)


def _combine_kernel(d1_ref, d2_ref, x_ref, route_ref, yb_ref, o_ref, buf1, buf2, sem, *, rows):
    i = pl.program_id(0)
    slot = i % 2

    def issue(step, to_slot):
        base = step * rows

        def rows8(r8, carry):
            r0 = pl.multiple_of(r8 * 8, 8)
            for u in range(8):
                pltpu.make_async_copy(yb_ref.at[pl.ds(d1_ref[base + r0 + u], 1)],
                                      buf1.at[to_slot, pl.ds(r0 + u, 1)], sem.at[to_slot, 0]).start(priority=0)
                pltpu.make_async_copy(yb_ref.at[pl.ds(d2_ref[base + r0 + u], 1)],
                                      buf2.at[to_slot, pl.ds(r0 + u, 1)], sem.at[to_slot, 1]).start(priority=1)
            return carry

        lax.fori_loop(0, rows // 8, rows8, 0)

    @pl.when(i == 0)
    def _():
        issue(0, 0)

    @pl.when(i + 1 < pl.num_programs(0))
    def _():
        issue(i + 1, 1 - slot)

    pltpu.make_async_copy(yb_ref.at[pl.ds(0, rows)], buf1.at[slot], sem.at[slot, 0]).wait()
    pltpu.make_async_copy(yb_ref.at[pl.ds(0, rows)], buf2.at[slot], sem.at[slot, 1]).wait()
    route = route_ref[...]
    o_ref[...] = x_ref[...] + (buf1[slot] * route[:, 2:3] + buf2[slot] * route[:, 3:4])


def moe_combine(x, route, yb, dest1, dest2, rows=256):
    m, d = x.shape
    rows = _tile(m, rows, 8)
    return pl.pallas_call(
        functools.partial(_combine_kernel, rows=rows),
        grid_spec=pltpu.PrefetchScalarGridSpec(
            num_scalar_prefetch=2,
            grid=(m // rows,),
            in_specs=[pl.BlockSpec((rows, d), lambda i, a, b: (i, 0)),
                      pl.BlockSpec((rows, LANES), lambda i, a, b: (i, 0)),
                      pl.BlockSpec(memory_space=pl.ANY)],
            out_specs=pl.BlockSpec((rows, d), lambda i, a, b: (i, 0)),
            scratch_shapes=[pltpu.VMEM((2, rows, d), F32), pltpu.VMEM((2, rows, d), F32),
                            pltpu.SemaphoreType.DMA((2, 2))],
        ),
        out_shape=jax.ShapeDtypeStruct((m, d), F32),
        compiler_params=_cparams(("arbitrary",)),
        name="moe_combine",
    )(dest1, dest2, x, route, yb)


def moe_layer(x, norm_g, w_router, w_gate, w_up, w_down):
    m, d = x.shape
    tm = MOE_TM
    xn, route, cnt = moe_route(x, norm_g, w_router)
    expert = route[:, 0:2].astype(I32)
    rank = route[:, 4:6].astype(I32)
    counts = cnt[0, :N_EXPERTS].astype(I32)
    padded = (counts + tm - 1) // tm * tm
    pend = jnp.cumsum(padded)
    pstart = pend - padded
    dest = pstart[expert] + rank
    n_blk = -(-(2 * m) // tm) + N_EXPERTS
    n_rows = n_blk * tm
    nused = (pend[-1] // tm).astype(I32)
    last_block_row = jnp.concatenate([jnp.maximum(pend - tm, 0).astype(I32), nused.reshape(1)])
    blk = jnp.minimum(jnp.arange(n_blk, dtype=I32), nused - 1)
    blk_e = jnp.minimum(jnp.sum(pend[None, :] <= (blk * tm)[:, None], axis=1), N_EXPERTS - 1).astype(I32)
    nu = nused.reshape(1)

    xb = moe_dispatch(xn, dest[:, 0], dest[:, 1], last_block_row, n_rows, tm)
    tf = _tile(w_gate.shape[2], 512, LANES)
    hidden = gmm_swiglu(xb, w_gate, w_up, blk_e, nu, tm=tm, tn=tf)
    yb = gmm(hidden, w_down, blk_e, nu, tm=tm, tn=_tile(d, 1024, LANES), out_dtype=F32)
    return moe_combine(x, route, yb, dest[:, 0], dest[:, 1])


def even_layer(x2, b, s, pos, w_norm, w_in, q_gain, k_gain, w_cmp_k, w_cmp_v, pe_cmp, conv_w, f_bias, m_norm,
               w_out, w_norm_ffn, w_gate, w_up, w_down):
    n, dm = x2.shape
    g, d = NSA_GROUPS, HEAD_DIM
    o_gate = NSA_SLAB
    o_qb = o_gate + NSA_HEADS * 3
    o_if = o_qb + 3 * MLSTM_WIDTH
    o_ob = o_if + 2 * MLSTM_HEADS
    w_ml = jnp.concatenate([w_in[:, o_qb:o_if], w_in[:, o_ob:o_ob + MLSTM_WIDTH]], axis=1)
    per_g = NSA_HPG * 3
    zeros = functools.partial(jnp.zeros, dtype=w_in.dtype)
    w_small = jnp.concatenate(
        [w_in[:, o_gate:o_gate + per_g], zeros((dm, GATE_I_LANE - per_g)),
         w_in[:, o_if:o_ob], zeros((dm, LANES - GATE_I_LANE - 2 * MLSTM_HEADS)),
         w_in[:, o_gate + per_g:o_qb], zeros((dm, LANES - per_g))], axis=1)

    h = rmsnorm_rows(x2, w_norm)
    p3 = mm(h, w_in[:, :o_gate], out_dtype=BF16, tn=NSA_SLAB // 2).reshape(b, s, NSA_SLAB)
    pm3 = mm(h, w_ml, out_dtype=BF16, tn=1024).reshape(b, s, 4 * MLSTM_WIDTH)
    gates3 = mm(h, w_small, out_dtype=F32, tn=2 * LANES).reshape(b, s, 2 * LANES)

    cos_n, sin_n = trig_tables(pos, _nsa_inv_lane())
    q_t, k_n, v_t = nsa_prep(p3, cos_n.reshape(b, s, LANES), sin_n.reshape(b, s, LANES), q_gain, k_gain)
    nh = s // CMP_STRIDE
    cmp_pos = jnp.concatenate([pos[:, CMP_BLOCK - 1::CMP_STRIDE][:, :nh - 1], pos[:, -1:]], axis=1)
    cos_c, sin_c = trig_tables(cmp_pos, _nsa_inv_lane())

    def cmp_blocks(col0):
        tok = p3[:, :, col0:col0 + NSA_KV_WIDTH].reshape(b, nh, CMP_STRIDE, g, d)
        halves = tok.transpose(0, 3, 1, 2, 4).reshape(b, g, nh, CMP_STRIDE * d)
        nxt = jnp.concatenate([halves[:, :, 1:], jnp.zeros_like(halves[:, :, :1])], axis=2)
        return jnp.concatenate([halves, nxt], axis=-1).reshape(b * g * nh, CMP_BLOCK * d)

    tabs = (cos_c.reshape(b, nh, LANES), sin_c.reshape(b, nh, LANES))
    kc = compress(cmp_blocks(NSA_WIDTH), pe_cmp, w_cmp_k, k_gain, *tabs, is_key=True, rows_per_seq=nh, groups=g)
    vc = compress(cmp_blocks(NSA_WIDTH + NSA_KV_WIDTH), pe_cmp, w_cmp_v, k_gain, *tabs, is_key=False,
                  rows_per_seq=nh, groups=g)
    o_nsa = nsa_attention(q_t, k_n, v_t, kc, vc, gates3)

    qk = conv_silu(pm3, conv_w, 0, 2 * MLSTM_WIDTH)
    h_b = mlstm(qk, pm3, gates3, f_bias, m_norm, 2 * MLSTM_WIDTH, 3 * MLSTM_WIDTH)

    mixed = jnp.concatenate([o_nsa, h_b], axis=-1).reshape(n, NSA_WIDTH + MLSTM_WIDTH)
    x2 = mm(mixed, w_out, out_dtype=F32, residual=x2, tm=1024, tn=1024)
    hf = rmsnorm_rows(x2, w_norm_ffn)
    hidden = mm_swiglu(hf, w_gate, w_up, tm=1024)
    return mm(hidden, w_down, out_dtype=F32, residual=x2, tm=512, tn=1024)


def odd_layer(x2, b, s, pos, w_norm, w_in, r_norm, w_out, w_norm_ffn, w_router, e_gate, e_up, e_down):
    n, dm = x2.shape
    h = rmsnorm_rows(x2, w_norm)
    po3 = mm(h, w_in, out_dtype=BF16, tn=1024).reshape(b, s, -1)
    cos_r, sin_r = trig_tables(pos, _ret_inv_lane())
    y = retention(po3, cos_r.reshape(b, s, LANES), sin_r.reshape(b, s, LANES), r_norm)
    x2 = mm(y.reshape(n, RET_V_WIDTH), w_out, out_dtype=F32, residual=x2, tm=512, tn=1024)
    return moe_layer(x2, w_norm_ffn, w_router, e_gate, e_up, e_down)


def kernel(x, positions, norm_mix_even, w_in_even, nsa_q_gain, nsa_k_gain, w_cmp_k, w_cmp_v, pe_cmp, mlstm_conv, mlstm_f_bias, mlstm_norm, w_out_even, norm_ffn_even, ffn_gate, ffn_up, ffn_down, norm_mix_odd, w_in_odd, ret_norm, w_out_odd, norm_ffn_odd, w_router, exp_gate, exp_up, exp_down):
    b, s, dm = x.shape
    depth = norm_mix_even.shape[0] + norm_mix_odd.shape[0]
    x2 = x.reshape(b * s, dm)
    for layer in range(depth):
        j = layer // 2
        if layer % 2 == 0:
            x2 = even_layer(x2, b, s, positions, norm_mix_even[j], w_in_even[j], nsa_q_gain[j], nsa_k_gain[j],
                            w_cmp_k[j], w_cmp_v[j], pe_cmp[j], mlstm_conv[j], mlstm_f_bias[j], mlstm_norm[j],
                            w_out_even[j], norm_ffn_even[j], ffn_gate[j], ffn_up[j], ffn_down[j])
        else:
            x2 = odd_layer(x2, b, s, positions, norm_mix_odd[j], w_in_odd[j], ret_norm[j], w_out_odd[j],
                           norm_ffn_odd[j], w_router[j], exp_gate[j], exp_up[j], exp_down[j])
    return x2.reshape(b, s, dm)
```

```python
import functools

import numpy as np
import jax
import jax.numpy as jnp
from jax import lax
from jax.experimental import pallas as pl
from jax.experimental.pallas import tpu as pltpu

F32 = jnp.float32
BF16 = jnp.bfloat16
I32 = jnp.int32
HIGHEST = lax.Precision.HIGHEST

HEAD_DIM = 128
NSA_HEADS = 8
NSA_GROUPS = 2
NSA_HPG = NSA_HEADS // NSA_GROUPS
NSA_WIDTH = NSA_HEADS * HEAD_DIM
NSA_KV_WIDTH = NSA_GROUPS * HEAD_DIM
NSA_SLAB = NSA_WIDTH + 6 * NSA_KV_WIDTH
CMP_BLOCK = 32
CMP_STRIDE = 16
SEL_BLOCK = 64
SEL_TOPN = 16
WINDOW = 512
ROPE_DIM = HEAD_DIM // 4
ROPE_THETA = 500000.0
SEL_FORCE = 1.0e6
NEG = -1.0e30
LOWEST = -3.0e38
MLSTM_HEADS = 4
MLSTM_DIM = 256
MLSTM_WIDTH = MLSTM_HEADS * MLSTM_DIM
CONV_WIDTH = 4
RET_HEADS = 8
RET_QK_DIM = 256
RET_V_DIM = 512
RET_QK_WIDTH = RET_HEADS * RET_QK_DIM
RET_V_WIDTH = RET_HEADS * RET_V_DIM
RET_ROPE_THETA = 10000.0
N_EXPERTS = 8
RMS_EPS = 1e-6

LANES = 128
BF16_SUBLANES = 16
V7X_VMEM_BYTES = 64 * 1024 * 1024
VMEM_LIMIT = V7X_VMEM_BYTES - 8 * 1024 * 1024

CHUNK = 256
NSA_TQ = 128
LOG2_E = 1.4426950408889634
V_ROWS = HEAD_DIM + BF16_SUBLANES
NSA_KT = 256
SOFTMAX_ROWS = 64
MOE_TM = 512
CAST_ROWS = 512
GATE_I_LANE = 16
GATE_F_LANE = 20


def _cparams(sem, vmem=VMEM_LIMIT):
    return pltpu.CompilerParams(dimension_semantics=sem, vmem_limit_bytes=vmem)


def _tile(n, target, quantum):
    if n <= target:
        return n
    t = (target // quantum) * quantum
    while t > quantum and n % t:
        t -= quantum
    assert n % t == 0, (n, target, quantum)
    return t


def _sigmoid(x):
    return 1.0 / (1.0 + jnp.exp(-x))


def _dot_nt(a, b):
    return lax.dot_general(a, b, (((1,), (1,)), ((), ())), preferred_element_type=F32)


def _dot_tn(a, b):
    return lax.dot_general(a, b, (((0,), (0,)), ((), ())), preferred_element_type=F32)


def _rmsnorm_kernel(x_ref, g_ref, o_ref):
    x = x_ref[...]
    y = x * lax.rsqrt(jnp.mean(x * x, axis=-1, keepdims=True) + RMS_EPS)
    o_ref[...] = (y * g_ref[...]).astype(o_ref.dtype)


def rmsnorm_rows(x, g, tm=512):
    m, d = x.shape
    tm = _tile(m, tm, 8)
    return pl.pallas_call(
        _rmsnorm_kernel,
        grid=(m // tm,),
        in_specs=[pl.BlockSpec((tm, d), lambda i: (i, 0)), pl.BlockSpec((1, d), lambda i: (0, 0))],
        out_specs=pl.BlockSpec((tm, d), lambda i: (i, 0)),
        out_shape=jax.ShapeDtypeStruct((m, d), BF16),
        compiler_params=_cparams(("parallel",)),
        name="rmsnorm",
    )(x, g.reshape(1, d).astype(F32))


def _stream_weights(be_ref, nx_ref, w_refs, land_refs, wb_refs, sems, tn):
    j = pl.program_id(0)
    i = pl.program_id(1)

    def copies(e, jj):
        c0 = pl.multiple_of(jj * tn, tn)
        return [pltpu.make_async_copy(w.at[e, :, pl.ds(c0, tn)], land, sems.at[n])
                for n, (w, land) in enumerate(zip(w_refs, land_refs))]

    @pl.when((i == 0) | (be_ref[i] != be_ref[jnp.maximum(i - 1, 0)]))
    def _():
        @pl.when((i == 0) & (j == 0))
        def _():
            for c in copies(be_ref[0], 0):
                c.start()

        for c in copies(be_ref[i], j):
            c.wait()
        rows = int(np.gcd(land_refs[0].shape[0], CAST_ROWS))

        def cast_rows(s, carry):
            r0 = pl.multiple_of(s * rows, rows)
            for land, wb in zip(land_refs, wb_refs):
                wb[pl.ds(r0, rows), :] = land[pl.ds(r0, rows), :].astype(BF16)
            return carry

        lax.fori_loop(0, land_refs[0].shape[0] // rows, cast_rows, 0)
        nxt = nx_ref[i]

        @pl.when(nxt >= 0)
        def _():
            for c in copies(nxt, j):
                c.start()

        @pl.when((nxt < 0) & (j + 1 < pl.num_programs(0)))
        def _():
            for c in copies(be_ref[0], j + 1):
                c.start()


def _next_group_expert(blk_e, n_experts):
    e = jnp.arange(n_experts, dtype=I32)
    present = jnp.any(blk_e[:, None] == e[None, :], axis=0)
    later = jnp.where((e[None, :] > blk_e[:, None]) & present[None, :], e[None, :], n_experts)
    nxt = jnp.min(later, axis=1)
    return jnp.where(nxt == n_experts, -1, nxt).astype(I32)


def _gmm_kernel(be_ref, nx_ref, nu_ref, x_ref, w_ref, *rest, has_res, tn):
    if has_res:
        r_ref, o_ref, land_ref, wb_ref, sems = rest
    else:
        o_ref, land_ref, wb_ref, sems = rest
    i = pl.program_id(1)
    _stream_weights(be_ref, nx_ref, [w_ref], [land_ref], [wb_ref], sems, tn)

    @pl.when(i < nu_ref[0])
    def _():
        acc = jnp.dot(x_ref[...], wb_ref[...], preferred_element_type=F32)
        if has_res:
            acc = r_ref[...] + acc
        o_ref[...] = acc.astype(o_ref.dtype)

    @pl.when(i >= nu_ref[0])
    def _():
        o_ref[...] = jnp.zeros_like(o_ref)


def gmm(x, w, blk_e, nused, *, tm, tn, out_dtype, residual=None):
    m, k = x.shape
    e, k2, n = w.shape
    assert k == k2 and m % tm == 0 and n % tn == 0
    nb = m // tm
    in_specs = [
        pl.BlockSpec((tm, k), lambda j, i, be, nx, nu: (jnp.minimum(i, nu[0] - 1), 0)),
        pl.BlockSpec(memory_space=pl.ANY),
    ]
    args = [x, w]
    if residual is not None:
        in_specs.append(pl.BlockSpec((tm, tn), lambda j, i, be, nx, nu: (i, j)))
        args.append(residual)
    return pl.pallas_call(
        functools.partial(_gmm_kernel, has_res=residual is not None, tn=tn),
        grid_spec=pltpu.PrefetchScalarGridSpec(
            num_scalar_prefetch=3,
            grid=(n // tn, nb),
            in_specs=in_specs,
            out_specs=pl.BlockSpec((tm, tn), lambda j, i, be, nx, nu: (i, j)),
            scratch_shapes=[pltpu.VMEM((k, tn), w.dtype), pltpu.VMEM((k, tn), BF16),
                            pltpu.SemaphoreType.DMA((1,))],
        ),
        out_shape=jax.ShapeDtypeStruct((m, n), out_dtype),
        compiler_params=_cparams(("arbitrary", "arbitrary")),
        name="gmm",
    )(blk_e, _next_group_expert(blk_e, e), nused, *args)


def _gmm_swiglu_kernel(be_ref, nx_ref, nu_ref, x_ref, wg_ref, wu_ref, o_ref, lg_ref, lu_ref, wgb_ref, wub_ref,
                       sems, *, tn):
    i = pl.program_id(1)
    _stream_weights(be_ref, nx_ref, [wg_ref, wu_ref], [lg_ref, lu_ref], [wgb_ref, wub_ref], sems, tn)

    @pl.when(i < nu_ref[0])
    def _():
        x = x_ref[...]
        g = jnp.dot(x, wgb_ref[...], preferred_element_type=F32)
        u = jnp.dot(x, wub_ref[...], preferred_element_type=F32)
        o_ref[...] = (g * _sigmoid(g) * u).astype(o_ref.dtype)

    @pl.when(i >= nu_ref[0])
    def _():
        o_ref[...] = jnp.zeros_like(o_ref)


def gmm_swiglu(x, wg, wu, blk_e, nused, *, tm, tn):
    m, k = x.shape
    e, k2, n = wg.shape
    assert k == k2 and wu.shape == wg.shape and m % tm == 0 and n % tn == 0
    nb = m // tm
    hbm = pl.BlockSpec(memory_space=pl.ANY)
    return pl.pallas_call(
        functools.partial(_gmm_swiglu_kernel, tn=tn),
        grid_spec=pltpu.PrefetchScalarGridSpec(
            num_scalar_prefetch=3,
            grid=(n // tn, nb),
            in_specs=[pl.BlockSpec((tm, k), lambda j, i, be, nx, nu: (jnp.minimum(i, nu[0] - 1), 0)), hbm, hbm],
            out_specs=pl.BlockSpec((tm, tn), lambda j, i, be, nx, nu: (i, j)),
            scratch_shapes=[pltpu.VMEM((k, tn), wg.dtype), pltpu.VMEM((k, tn), wu.dtype),
                            pltpu.VMEM((k, tn), BF16), pltpu.VMEM((k, tn), BF16),
                            pltpu.SemaphoreType.DMA((2,))],
        ),
        out_shape=jax.ShapeDtypeStruct((m, n), BF16),
        compiler_params=_cparams(("arbitrary", "arbitrary")),
        name="gmm_swiglu",
    )(blk_e, _next_group_expert(blk_e, e), nused, x, wg, wu)


def _dense_blocks(m, tm):
    nb = m // tm
    return jnp.zeros((nb,), I32), jnp.full((1,), nb, I32)


def mm(x, w, *, out_dtype, residual=None, tm=2048, tn=512):
    m, k = x.shape
    n = w.shape[1]
    tm = _tile(m, tm, 16)
    tn = _tile(n, tn, LANES)
    be, nu = _dense_blocks(m, tm)
    return gmm(x, w[None], be, nu, tm=tm, tn=tn, out_dtype=out_dtype, residual=residual)


def mm_swiglu(x, wg, wu, *, tm=2048, tn=512):
    m = x.shape[0]
    tm = _tile(m, tm, 16)
    tn = _tile(wg.shape[1], tn, LANES)
    be, nu = _dense_blocks(m, tm)
    return gmm_swiglu(x, wg[None], wu[None], be, nu, tm=tm, tn=tn)


def _trig_kernel(pos_ref, inv_ref, cos_ref, sin_ref):
    ang = pos_ref[...] * inv_ref[...]
    cos_ref[...] = jnp.cos(ang)
    sin_ref[...] = jnp.sin(ang)


def trig_tables(pos, inv_lane):
    r = pos.size
    pos_b = jnp.broadcast_to(pos.astype(F32).reshape(r, 1), (r, LANES))
    tr = _tile(r, 512, 8)
    spec = pl.BlockSpec((tr, LANES), lambda i: (i, 0))
    return pl.pallas_call(
        _trig_kernel,
        grid=(r // tr,),
        in_specs=[spec, pl.BlockSpec((1, LANES), lambda i: (0, 0))],
        out_specs=[spec, spec],
        out_shape=[jax.ShapeDtypeStruct((r, LANES), F32)] * 2,
        compiler_params=_cparams(("parallel",)),
        name="trig_tables",
    )(pos_b, inv_lane.reshape(1, LANES))


def _nsa_inv_lane():
    half = ROPE_DIM // 2
    inv = jnp.power(jnp.float32(ROPE_THETA), -jnp.arange(half, dtype=F32) * (2.0 / ROPE_DIM))
    return jnp.concatenate([inv, inv, jnp.zeros((LANES - ROPE_DIM,), F32)])


def _ret_inv_lane():
    half = RET_QK_DIM // 2
    return jnp.power(jnp.float32(RET_ROPE_THETA), -jnp.arange(half, dtype=F32) * (2.0 / RET_QK_DIM))


def _norm_rope_head(x, gain, cos, sin):
    half = ROPE_DIM // 2
    y = x * lax.rsqrt(jnp.mean(x * x, axis=-1, keepdims=True) + RMS_EPS) * gain
    lane = lax.broadcasted_iota(I32, y.shape, 1)
    from_hi = jnp.where(lane < half, -sin, 0.0)
    from_lo = jnp.where((lane >= half) & (lane < ROPE_DIM), sin, 0.0)
    return (y * cos + pltpu.roll(y, LANES - half, 1) * from_hi + pltpu.roll(y, half, 1) * from_lo)


def _nsa_prep_kernel(p_ref, cos_ref, sin_ref, qg_ref, kg_ref, qt_ref, kn_ref, vt_ref):
    cos = cos_ref[0]
    sin = sin_ref[0]
    scale = HEAD_DIM ** -0.5 * LOG2_E
    d = HEAD_DIM
    g = NSA_GROUPS
    for hd in range(NSA_HEADS):
        q = _norm_rope_head(p_ref[0, :, hd * d:(hd + 1) * d].astype(F32), qg_ref[...], cos, sin)
        qt_ref[0, hd * d:(hd + 1) * d, :] = (q * scale).T.astype(BF16)
    for n, slab in enumerate((2, 4)):
        for gi in range(g):
            off = NSA_WIDTH + slab * NSA_KV_WIDTH + gi * d
            k = _norm_rope_head(p_ref[0, :, off:off + d].astype(F32), kg_ref[...], cos, sin)
            kn_ref[0, :, (n * g + gi) * d:(n * g + gi + 1) * d] = k.astype(BF16)
    for n, slab in enumerate((3, 5)):
        for gi in range(g):
            off = NSA_WIDTH + slab * NSA_KV_WIDTH + gi * d
            v = p_ref[0, :, off:off + d].astype(F32)
            r0 = (n * g + gi) * V_ROWS
            vt_ref[0, r0:r0 + d, :] = v.T.astype(BF16)
            vt_ref[0, r0 + d:r0 + V_ROWS, :] = jnp.ones((V_ROWS - d, v.shape[0]), BF16)


def nsa_prep(p3, cos, sin, q_gain, k_gain):
    b, s, _ = p3.shape
    t = _tile(s, 256, LANES)
    d = HEAD_DIM
    tab = pl.BlockSpec((1, t, LANES), lambda bi, i: (bi, i, 0))
    gain = pl.BlockSpec((1, d), lambda bi, i: (0, 0))
    kv = 2 * NSA_KV_WIDTH
    vr = 2 * NSA_GROUPS * V_ROWS
    return pl.pallas_call(
        _nsa_prep_kernel,
        grid=(b, s // t),
        in_specs=[pl.BlockSpec((1, t, NSA_SLAB), lambda bi, i: (bi, i, 0)), tab, tab, gain, gain],
        out_specs=[pl.BlockSpec((1, NSA_WIDTH, t), lambda bi, i: (bi, 0, i)),
                   pl.BlockSpec((1, t, kv), lambda bi, i: (bi, i, 0)),
                   pl.BlockSpec((1, vr, t), lambda bi, i: (bi, 0, i))],
        out_shape=[jax.ShapeDtypeStruct((b, NSA_WIDTH, s), BF16),
                   jax.ShapeDtypeStruct((b, s, kv), BF16),
                   jax.ShapeDtypeStruct((b, vr, s), BF16)],
        compiler_params=_cparams(("parallel", "parallel")),
        name="nsa_prep",
    )(p3, cos, sin, q_gain.reshape(1, d), k_gain.reshape(1, d))


def _compress_kernel(blk_ref, pe_ref, w_ref, kg_ref, cos_ref, sin_ref, o_ref, *, is_key):
    a = (blk_ref[...].astype(F32) + pe_ref[...]).astype(BF16)
    y = jnp.dot(a, w_ref[...].astype(BF16), preferred_element_type=F32)
    if is_key:
        o_ref[0] = _norm_rope_head(y, kg_ref[...], cos_ref[0], sin_ref[0]).astype(BF16)
    else:
        o_ref[0] = y.T.astype(BF16)


def compress(blk, pe, w, k_gain, cos_c, sin_c, *, is_key, rows_per_seq, groups):
    r, kdim = blk.shape
    t = rows_per_seq
    d = HEAD_DIM
    tab = pl.BlockSpec((1, t, LANES), lambda i: (i // groups, 0, 0))
    out_blk = (1, t, d) if is_key else (1, d, t)
    return pl.pallas_call(
        functools.partial(_compress_kernel, is_key=is_key),
        grid=(r // t,),
        in_specs=[pl.BlockSpec((t, kdim), lambda i: (i, 0)),
                  pl.BlockSpec((1, kdim), lambda i: (0, 0)),
                  pl.BlockSpec((kdim, d), lambda i: (0, 0)),
                  pl.BlockSpec((1, d), lambda i: (0, 0)),
                  tab, tab],
        out_specs=pl.BlockSpec(out_blk, lambda i: (i, 0, 0)),
        out_shape=jax.ShapeDtypeStruct((r // t,) + out_blk[1:], BF16),
        compiler_params=_cparams(("parallel",)),
        name="nsa_compress",
    )(blk, pe.reshape(1, kdim), w, k_gain.reshape(1, d), cos_c, sin_c)


def _nsa_attn_kernel(qt_ref, kc_ref, vct_ref, ks_ref, kw_ref, vst_ref, vwt_ref, et_ref, ovt_ref, gate_ref,
                     o_ref, acc_sc, val_sc, sa_sc, sb_sc, pa_sc, pb_sc, sc_sc, pc_sc, pg_sc, sw_sc, pw_sc, ow_sc,
                     *, tq, kt, wk, ns, n_top):
    i = pl.program_id(2)
    t0 = i * tq
    cols = NSA_HPG * tq
    d = HEAD_DIM
    q_t = jnp.concatenate([qt_ref[0, p * d:(p + 1) * d, :] for p in range(NSA_HPG)], axis=1)
    t_lane = t0 + (lax.broadcasted_iota(I32, (1, cols), 1) & (tq - 1))

    ncp = kc_ref.shape[1]
    rs = SOFTMAX_ROWS
    sc_sc[...] = jnp.dot(kc_ref[0], q_t, preferred_element_type=F32)
    w0 = pl.multiple_of(jnp.maximum(t0 + tq - wk, 0), LANES)
    sw_sc[...] = jnp.dot(kw_ref[0, pl.ds(w0, wk), :], q_t, preferred_element_type=F32)

    def cmask(r0):
        c_end = (r0 + lax.broadcasted_iota(I32, (rs, cols), 0)) * CMP_STRIDE + (CMP_BLOCK - 1)
        return c_end <= t_lane

    m_c = jnp.full((1, cols), NEG, F32)
    for r0 in range(0, ncp, rs):
        m_c = jnp.maximum(m_c, jnp.max(jnp.where(cmask(r0), sc_sc[r0:r0 + rs, :], NEG), axis=0, keepdims=True))
    den_c = jnp.zeros((1, cols), F32)
    for r0 in range(0, ncp, rs):
        e = jnp.where(cmask(r0), jnp.exp2(sc_sc[r0:r0 + rs, :] - m_c), 0.0)
        sc_sc[r0:r0 + rs, :] = e
        den_c = den_c + jnp.sum(e, axis=0, keepdims=True)
    inv_c = 1.0 / jnp.maximum(den_c, 1e-30)
    for r0 in range(0, ncp, rs):
        p = sc_sc[r0:r0 + rs, :] * inv_c
        pc_sc[r0:r0 + rs, :] = p.astype(BF16)
        p_grp = p[:, 0:tq]
        for hp in range(1, NSA_HPG):
            p_grp = p_grp + p[:, hp * tq:(hp + 1) * tq]
        pg_sc[r0:r0 + rs, :] = p_grp
    o_cmp = jnp.dot(vct_ref[0], pc_sc[...], preferred_element_type=F32)

    m_w = jnp.full((1, cols), NEG, F32)
    for r0 in range(0, wk, rs):
        kpos = w0 + r0 + lax.broadcasted_iota(I32, (rs, cols), 0)
        wmask = (kpos <= t_lane) & (kpos > t_lane - WINDOW)
        s = jnp.where(wmask, sw_sc[r0:r0 + rs, :], NEG)
        sw_sc[r0:r0 + rs, :] = s
        m_w = jnp.maximum(m_w, jnp.max(s, axis=0, keepdims=True))
    for r0 in range(0, wk, rs):
        pw_sc[r0:r0 + rs, :] = jnp.exp2(sw_sc[r0:r0 + rs, :] - m_w).astype(BF16)
    o_win = jnp.dot(vwt_ref[0, :, pl.ds(w0, wk)], pw_sc[...], preferred_element_type=F32)
    ow_sc[...] = o_win[:d] * (1.0 / o_win[d:d + 1])

    imp = jnp.dot(ovt_ref[...], pg_sc[...], precision=HIGHEST, preferred_element_type=F32)
    nsp = val_sc.shape[0]
    jb = lax.broadcasted_iota(I32, (nsp, tq), 0)
    cur = (t0 + lax.broadcasted_iota(I32, (1, tq), 1)) // SEL_BLOCK
    forced = (jb == 0) | (jb == cur) | (jb == cur - 1)
    val = jnp.where(jb <= cur, jnp.where(forced, SEL_FORCE, imp[:nsp]), NEG)
    val = jnp.where(jb < ns, val, LOWEST)
    val_sc[...] = val
    beaten = jnp.zeros((nsp, tq), F32)
    for j2 in range(ns):
        r = val_sc[j2:j2 + 1, :]
        ge = jnp.where(r >= val, 1.0, 0.0)
        gt = jnp.where(r > val, 1.0, 0.0)
        beaten = beaten + jnp.where(jb > j2, ge, gt)
    past = jb < t0 // SEL_BLOCK
    bias = jnp.where(past & (beaten < n_top), 0.0, jnp.where(jb < ns, NEG, 0.0)).astype(BF16)
    bias = jnp.concatenate([bias, jnp.zeros((LANES - nsp, tq), BF16)], axis=0)
    q_aug = jnp.concatenate([q_t, jnp.concatenate([bias] * NSA_HPG, axis=1)], axis=0)

    d0 = pl.multiple_of(t0, tq)
    s_d = jnp.dot(ks_ref[0, pl.ds(d0, tq), :], q_t, preferred_element_type=F32)
    s_d = jnp.where(d0 + lax.broadcasted_iota(I32, (tq, cols), 0) <= t_lane, s_d, NEG)
    m_d = jnp.max(s_d, axis=0, keepdims=True)
    p_d = jnp.exp2(s_d - m_d)
    acc_sc[...] = jnp.dot(vst_ref[0, :, pl.ds(d0, tq)], p_d.astype(BF16), preferred_element_type=F32)

    n_pairs = (t0 + 2 * kt - 1) // (2 * kt)
    last_a = jnp.maximum(n_pairs - 1, 0) * (2 * kt)

    def scores(k0):
        k0 = pl.multiple_of(k0, kt)
        k_aug = jnp.concatenate([ks_ref[0, pl.ds(k0, kt), :], et_ref[pl.ds(k0, kt), :]], axis=1)
        return jnp.dot(k_aug, q_aug, preferred_element_type=F32)

    def values(p_ref, k0):
        k0 = pl.multiple_of(k0, kt)
        return jnp.dot(vst_ref[0, :, pl.ds(k0, kt)], p_ref[...], preferred_element_type=F32)

    def softmax_update(s_ref, p_ref, m_prev):
        m_new = jnp.maximum(m_prev, jnp.max(s_ref[...], axis=0, keepdims=True))
        for r0 in range(0, kt, SOFTMAX_ROWS):
            p_ref[r0:r0 + SOFTMAX_ROWS, :] = jnp.exp2(s_ref[r0:r0 + SOFTMAX_ROWS, :] - m_new).astype(BF16)
        return m_new, jnp.exp2(m_prev - m_new)

    def sel_step(j, carry):
        m_run, alpha_b = carry
        k0 = j * (2 * kt)
        sb_sc[...] = scores(k0 + kt)
        acc_sc[...] = alpha_b * acc_sc[...] + values(pb_sc, jnp.maximum(k0 - kt, 0))
        m_run, alpha_a = softmax_update(sa_sc, pa_sc, m_run)
        sa_sc[...] = scores(jnp.minimum(k0 + 2 * kt, last_a))
        acc_sc[...] = alpha_a * acc_sc[...] + values(pa_sc, k0)
        return softmax_update(sb_sc, pb_sc, m_run)

    sa_sc[...] = scores(0)
    pb_sc[...] = jnp.zeros(pb_sc.shape, BF16)
    _, alpha_last = lax.fori_loop(0, n_pairs, sel_step, (m_d, jnp.ones((1, cols), F32)))
    acc = alpha_last * acc_sc[...] + values(pb_sc, last_a + kt)
    o_sel = acc[:d] * (1.0 / acc[d:d + 1])

    g_t = _sigmoid(gate_ref[0]).T
    for p in range(NSA_HPG):
        sl = slice(p * tq, (p + 1) * tq)
        o_t = (g_t[3 * p:3 * p + 1] * o_cmp[:, sl] + g_t[3 * p + 1:3 * p + 2] * o_sel[:, sl]
               + g_t[3 * p + 2:3 * p + 3] * ow_sc[:, sl])
        o_ref[0, :, p * d:(p + 1) * d] = o_t.T.astype(BF16)


def nsa_attention(q_t, k_n, v_t, kc, vc_t, gates3):
    b, _, s = q_t.shape
    g, d = NSA_GROUPS, HEAD_DIM
    ncp = kc.shape[1]
    tq = NSA_TQ
    kt = min(NSA_KT, s)
    wk = WINDOW + tq
    ns = s // SEL_BLOCK
    assert s % (2 * kt) == 0 and s >= wk and ns <= LANES and tq == LANES
    assert kt % SOFTMAX_ROWS == 0 and ncp % SOFTMAX_ROWS == 0 and wk % SOFTMAX_ROWS == 0
    n_top = min(SEL_TOPN, ns)
    cols = NSA_HPG * tq

    key_blk = np.arange(s) // SEL_BLOCK
    e_t = jnp.asarray(key_blk[:, None] == np.arange(LANES)[None, :], dtype=BF16)
    c_start = np.arange(ncp) * CMP_STRIDE
    j_start = np.arange(LANES) * SEL_BLOCK
    overlap = ((c_start[:, None] < j_start[None, :] + SEL_BLOCK) & (c_start[:, None] + CMP_BLOCK > j_start[None, :])
               & (np.arange(ncp)[:, None] < s // CMP_STRIDE - 1) & (np.arange(LANES)[None, :] < ns))
    overlap_t = jnp.asarray(overlap.T.astype(np.float32))

    q_rows = NSA_HPG * d
    return pl.pallas_call(
        functools.partial(_nsa_attn_kernel, tq=tq, kt=kt, wk=wk, ns=ns, n_top=n_top),
        grid=(b, g, s // tq),
        in_specs=[pl.BlockSpec((1, q_rows, tq), lambda bi, gi, i: (bi, gi, i)),
                  pl.BlockSpec((1, ncp, d), lambda bi, gi, i: (bi * g + gi, 0, 0)),
                  pl.BlockSpec((1, d, ncp), lambda bi, gi, i: (bi * g + gi, 0, 0)),
                  pl.BlockSpec((1, s, d), lambda bi, gi, i: (bi, 0, gi)),
                  pl.BlockSpec((1, s, d), lambda bi, gi, i: (bi, 0, g + gi)),
                  pl.BlockSpec((1, V_ROWS, s), lambda bi, gi, i: (bi, gi, 0)),
                  pl.BlockSpec((1, V_ROWS, s), lambda bi, gi, i: (bi, g + gi, 0)),
                  pl.BlockSpec((s, LANES), lambda bi, gi, i: (0, 0)),
                  pl.BlockSpec((LANES, ncp), lambda bi, gi, i: (0, 0)),
                  pl.BlockSpec((1, tq, LANES), lambda bi, gi, i: (bi, i, gi))],
        out_specs=pl.BlockSpec((1, tq, q_rows), lambda bi, gi, i: (bi, i, gi)),
        out_shape=jax.ShapeDtypeStruct((b, s, NSA_WIDTH), BF16),
        scratch_shapes=[pltpu.VMEM((V_ROWS, cols), F32), pltpu.VMEM((-(-ns // 8) * 8, tq), F32),
                        pltpu.VMEM((kt, cols), F32), pltpu.VMEM((kt, cols), F32),
                        pltpu.VMEM((kt, cols), BF16), pltpu.VMEM((kt, cols), BF16),
                        pltpu.VMEM((ncp, cols), F32), pltpu.VMEM((ncp, cols), BF16), pltpu.VMEM((ncp, tq), F32),
                        pltpu.VMEM((wk, cols), F32), pltpu.VMEM((wk, cols), BF16), pltpu.VMEM((d, cols), F32)],
        compiler_params=_cparams(("parallel", "parallel", "arbitrary")),
        name="nsa_attention",
    )(q_t, kc, vc_t, k_n, k_n, v_t, v_t, e_t, overlap_t, gates3)


def _conv_kernel(cur_ref, prev_ref, w_ref, o_ref, *, tc):
    i = pl.program_id(1)
    cur = cur_ref[0].astype(F32)
    prev = jnp.where(i > 0, prev_ref[0].astype(F32), 0.0)
    pad = prev.shape[0] // 2
    full = jnp.concatenate([prev[pad:], cur], axis=0)
    y = None
    for kk in range(CONV_WIDTH):
        off = pad - (CONV_WIDTH - 1) + kk
        term = w_ref[kk:kk + 1, :] * full[off:off + tc]
        y = term if y is None else y + term
    o_ref[0] = (y * _sigmoid(y)).astype(BF16)


def conv_silu(p3, w, col0, width):
    b, s, _ = p3.shape
    tc = _tile(s, 512, BF16_SUBLANES)
    cw = 512
    assert col0 % cw == 0 and width % cw == 0
    c0 = col0 // cw
    halo = BF16_SUBLANES
    return pl.pallas_call(
        functools.partial(_conv_kernel, tc=tc),
        grid=(b, s // tc, width // cw),
        in_specs=[pl.BlockSpec((1, tc, cw), lambda bi, i, j: (bi, i, c0 + j)),
                  pl.BlockSpec((1, halo, cw), lambda bi, i, j: (bi, jnp.maximum(i * (tc // halo) - 1, 0), c0 + j)),
                  pl.BlockSpec((CONV_WIDTH, cw), lambda bi, i, j: (0, j))],
        out_specs=pl.BlockSpec((1, tc, cw), lambda bi, i, j: (bi, i, j)),
        out_shape=jax.ShapeDtypeStruct((b, s, width), BF16),
        compiler_params=_cparams(("parallel", "parallel", "parallel")),
        name="mlstm_conv",
    )(p3, p3, w)


def _log_sigmoid(x):
    return jnp.minimum(x, 0.0) - jnp.log1p(jnp.exp(-jnp.abs(x)))


def _mlstm_kernel(fb_ref, q_ref, k_ref, v_ref, ob_ref, gate_ref, nw_ref, o_ref, c_sc, n_sc, m_sc, *, cl):
    @pl.when(pl.program_id(1) == 0)
    def _():
        c_sc[...] = jnp.zeros(c_sc.shape, F32)
        n_sc[...] = jnp.zeros(n_sc.shape, F32)
        m_sc[...] = jnp.zeros(m_sc.shape, F32)

    dh = MLSTM_DIM
    slab = gate_ref[0]
    lane = lax.broadcasted_iota(I32, slab.shape, 1)
    r_i = lax.broadcasted_iota(I32, (cl, cl), 0)
    c_i = lax.broadcasted_iota(I32, (cl, cl), 1)
    eye = r_i == c_i
    tri = c_i <= r_i

    def head(hd, carry):
        off = pl.multiple_of(hd * dh, dh)
        q = q_ref[0, :, pl.ds(off, dh)]
        v = v_ref[0, :, pl.ds(off, dh)]
        ks32 = k_ref[0, :, pl.ds(off, dh)].astype(F32) * (dh ** -0.5)
        ks = ks32.astype(BF16)
        i_col = jnp.sum(jnp.where(lane == GATE_I_LANE + hd, slab, 0.0), axis=-1, keepdims=True)
        f_col = jnp.sum(jnp.where(lane == GATE_F_LANE + hd, slab, 0.0), axis=-1, keepdims=True)
        lf_col = _log_sigmoid(f_col + fb_ref[hd])

        lf_row = jnp.sum(jnp.where(eye, lf_col, 0.0), axis=0, keepdims=True)
        ig_row = jnp.sum(jnp.where(eye, i_col, 0.0), axis=0, keepdims=True)
        a_col = jnp.sum(jnp.where(tri, lf_row, 0.0), axis=1, keepdims=True)
        a_row = jnp.sum(jnp.where(r_i <= c_i, lf_col, 0.0), axis=0, keepdims=True)
        m_prev = m_sc[hd]

        dlog = jnp.where(tri, a_col - a_row + ig_row, NEG)
        inter = a_col + m_prev
        mt = jnp.maximum(inter, jnp.max(dlog, axis=-1, keepdims=True))
        wm = jnp.exp(dlog - mt) * _dot_nt(q, ks)
        e_col = jnp.exp(inter - mt)
        num = e_col * jnp.dot(q, c_sc[hd].astype(BF16), preferred_element_type=F32) \
            + jnp.dot(wm.astype(BF16), v, preferred_element_type=F32)
        qn = jnp.sum(q.astype(F32) * n_sc[hd], axis=-1, keepdims=True)
        den = e_col * qn + jnp.sum(wm, axis=-1, keepdims=True)
        hh = num / jnp.maximum(jnp.abs(den), jnp.exp(-mt))

        a_last = jnp.sum(lf_row, axis=-1, keepdims=True)
        gs = a_last - a_col + i_col
        m_new = jnp.maximum(a_last + m_prev, jnp.max(gs, axis=0, keepdims=True))
        decay = jnp.exp(a_last + m_prev - m_new)
        wk = jnp.exp(gs - m_new) * ks32
        c_sc[hd] = decay * c_sc[hd] + _dot_tn(wk.astype(BF16), v)
        n_sc[hd] = decay * n_sc[hd] + jnp.sum(wk, axis=0, keepdims=True)
        m_sc[hd] = m_new

        y = hh * lax.rsqrt(jnp.mean(hh * hh, axis=-1, keepdims=True) + RMS_EPS) * nw_ref[:, pl.ds(off, dh)]
        o_ref[0, :, pl.ds(off, dh)] = (y * _sigmoid(ob_ref[0, :, pl.ds(off, dh)].astype(F32))).astype(BF16)
        return carry

    lax.fori_loop(0, MLSTM_HEADS, head, 0)


def mlstm(qk, p3, gates3, f_bias, norm_w, v_col0, o_col0):
    b, s, _ = qk.shape
    cl = min(CHUNK, s)
    dh = MLSTM_DIM
    nh = MLSTM_HEADS
    w = nh * dh
    assert v_col0 % w == 0 and o_col0 % w == 0 and s % cl == 0
    vb, ob = v_col0 // w, o_col0 // w
    return pl.pallas_call(
        functools.partial(_mlstm_kernel, cl=cl),
        grid=(b, s // cl),
        in_specs=[pl.BlockSpec(memory_space=pltpu.SMEM),
                  pl.BlockSpec((1, cl, w), lambda bi, c: (bi, c, 0)),
                  pl.BlockSpec((1, cl, w), lambda bi, c: (bi, c, 1)),
                  pl.BlockSpec((1, cl, w), lambda bi, c: (bi, c, vb)),
                  pl.BlockSpec((1, cl, w), lambda bi, c: (bi, c, ob)),
                  pl.BlockSpec((1, cl, LANES), lambda bi, c: (bi, c, 0)),
                  pl.BlockSpec((1, w), lambda bi, c: (0, 0))],
        out_specs=pl.BlockSpec((1, cl, w), lambda bi, c: (bi, c, 0)),
        out_shape=jax.ShapeDtypeStruct((b, s, w), BF16),
        scratch_shapes=[pltpu.VMEM((nh, dh, dh), F32), pltpu.VMEM((nh, 1, dh), F32), pltpu.VMEM((nh, 1, 1), F32)],
        compiler_params=_cparams(("parallel", "arbitrary")),
        name="mlstm",
    )(f_bias.astype(F32), qk, qk, p3, p3, gates3, norm_w.reshape(1, w).astype(F32))


def _ret_kernel(cd_ref, q_ref, k_ref, v_ref, g_ref, cos_ref, sin_ref, nw_ref, dm_ref, xi_ref, zeta_ref,
                o_ref, r_sc, *, cl):
    @pl.when(pl.program_id(1) == 0)
    def _():
        r_sc[...] = jnp.zeros(r_sc.shape, F32)

    cos = cos_ref[0]
    sin = sin_ref[0]
    dk, dv = RET_QK_DIM, RET_V_DIM
    half = dk // 2
    scale = dk ** -0.5

    def rope(x):
        x1, x2 = x[:, :half], x[:, half:]
        return x1 * cos - x2 * sin, x1 * sin + x2 * cos

    def head(hd, carry):
        qo = pl.multiple_of(hd * dk, dk)
        vo = pl.multiple_of(hd * dv, dv)
        q1, q2 = rope(q_ref[0, :, pl.ds(qo, dk)].astype(F32))
        qr = jnp.concatenate([q1, q2], axis=1).astype(BF16)
        k1, k2 = rope(k_ref[0, :, pl.ds(qo, dk)].astype(F32))
        zeta = zeta_ref[hd] * scale
        kr = jnp.concatenate([k1 * scale, k2 * scale], axis=1).astype(BF16)
        kz = jnp.concatenate([k1 * zeta, k2 * zeta], axis=1).astype(BF16)
        v = v_ref[0, :, pl.ds(vo, dv)]

        inner = jnp.dot((_dot_nt(qr, kr) * dm_ref[hd]).astype(BF16), v, preferred_element_type=F32)
        xi = xi_ref[hd]
        cross = jnp.dot(qr, r_sc[hd].astype(BF16), preferred_element_type=F32)
        cross = cross * jnp.concatenate([xi] * (dv // LANES), axis=1)
        r_sc[hd] = cd_ref[hd] * r_sc[hd] + _dot_tn(kz, v)

        y = inner + cross
        y = y * lax.rsqrt(jnp.mean(y * y, axis=-1, keepdims=True) + RMS_EPS) * nw_ref[:, pl.ds(vo, dv)]
        gg = g_ref[0, :, pl.ds(vo, dv)].astype(F32)
        o_ref[0, :, pl.ds(vo, dv)] = (y * (gg * _sigmoid(gg))).astype(BF16)
        return carry

    lax.fori_loop(0, RET_HEADS, head, 0)


def retention(po3, cos, sin, norm_w):
    b, s, _ = po3.shape
    cl = min(CHUNK, s)
    nh, dk, dv = RET_HEADS, RET_QK_DIM, RET_V_DIM
    log_g = jnp.log1p(-jnp.exp2(-5.0 - jnp.arange(nh, dtype=F32)))
    idx = jnp.arange(cl, dtype=F32)
    diff = idx[:, None] - idx[None, :]
    dm = jnp.where(diff >= 0, jnp.exp(jnp.maximum(diff, 0.0) * log_g[:, None, None]), 0.0)
    xi = jnp.broadcast_to(jnp.exp((idx + 1.0) * log_g[:, None])[..., None], (nh, cl, LANES))
    zeta = jnp.broadcast_to(jnp.exp((cl - 1.0 - idx) * log_g[:, None])[..., None], (nh, cl, LANES))
    chunk_decay = jnp.exp(cl * log_g)
    qk_w, v_w = RET_QK_WIDTH, RET_V_WIDTH
    assert v_w == 2 * qk_w
    tab = pl.BlockSpec((1, cl, LANES), lambda bi, c: (bi, c, 0))

    def table(shape):
        return pl.BlockSpec(shape, lambda bi, c: (0, 0, 0))

    return pl.pallas_call(
        functools.partial(_ret_kernel, cl=cl),
        grid=(b, s // cl),
        in_specs=[pl.BlockSpec(memory_space=pltpu.SMEM),
                  pl.BlockSpec((1, cl, qk_w), lambda bi, c: (bi, c, 0)),
                  pl.BlockSpec((1, cl, qk_w), lambda bi, c: (bi, c, 1)),
                  pl.BlockSpec((1, cl, v_w), lambda bi, c: (bi, c, 1)),
                  pl.BlockSpec((1, cl, v_w), lambda bi, c: (bi, c, 2)),
                  tab, tab,
                  pl.BlockSpec((1, v_w), lambda bi, c: (0, 0)),
                  table((nh, cl, cl)), table((nh, cl, LANES)), table((nh, cl, LANES))],
        out_specs=pl.BlockSpec((1, cl, v_w), lambda bi, c: (bi, c, 0)),
        out_shape=jax.ShapeDtypeStruct((b, s, v_w), BF16),
        scratch_shapes=[pltpu.VMEM((nh, dk, dv), F32)],
        compiler_params=_cparams(("parallel", "arbitrary")),
        name="retention",
    )(chunk_decay, po3, po3, po3, po3, cos, sin, norm_w.reshape(1, v_w).astype(F32), dm, xi, zeta)


def _router_kernel(x_ref, g_ref, wr_ref, xn_ref, route_ref, cnt_ref, carry_sc):
    @pl.when(pl.program_id(0) == 0)
    def _():
        carry_sc[...] = jnp.zeros(carry_sc.shape, F32)

    x = x_ref[...]
    t = x.shape[0]
    y = x * lax.rsqrt(jnp.mean(x * x, axis=-1, keepdims=True) + RMS_EPS) * g_ref[...]
    xn_ref[...] = y.astype(BF16)
    w = wr_ref[...]
    y_hi = y.astype(BF16)
    y_lo = (y - y_hi.astype(F32)).astype(BF16)
    w_hi = w.astype(BF16)
    w_lo = (w - w_hi.astype(F32)).astype(BF16)
    logits = (jnp.dot(y_hi, w_hi, preferred_element_type=F32) + jnp.dot(y_lo, w_hi, preferred_element_type=F32)
              + jnp.dot(y_hi, w_lo, preferred_element_type=F32))
    lane = lax.broadcasted_iota(I32, logits.shape, 1)
    lg = jnp.where(lane < N_EXPERTS, logits, LOWEST)
    v1 = jnp.max(lg, axis=-1, keepdims=True)
    i1 = jnp.min(jnp.where(lg == v1, lane, LANES), axis=-1, keepdims=True)
    lg2 = jnp.where(lane == i1, LOWEST, lg)
    v2 = jnp.max(lg2, axis=-1, keepdims=True)
    i2 = jnp.min(jnp.where(lg2 == v2, lane, LANES), axis=-1, keepdims=True)
    e2 = jnp.exp(v2 - v1)
    g1 = 1.0 / (1.0 + e2)
    g2 = e2 / (1.0 + e2)

    chosen = jnp.where((lane == i1) | (lane == i2), 1.0, 0.0)
    r_i = lax.broadcasted_iota(I32, (t, t), 0)
    c_i = lax.broadcasted_iota(I32, (t, t), 1)
    tri = jnp.where(c_i <= r_i, 1.0, 0.0).astype(BF16)
    seen = jnp.dot(tri, chosen.astype(BF16), preferred_element_type=F32) + carry_sc[...]
    rank1 = jnp.sum(jnp.where(lane == i1, seen, 0.0), axis=-1, keepdims=True) - 1.0
    rank2 = jnp.sum(jnp.where(lane == i2, seen, 0.0), axis=-1, keepdims=True) - 1.0
    total = seen[t - 1:t, :]
    carry_sc[...] = total
    cnt_ref[...] = jnp.broadcast_to(total, cnt_ref.shape)

    route = jnp.where(lane == 0, i1.astype(F32), 0.0)
    route = jnp.where(lane == 1, i2.astype(F32), route)
    route = jnp.where(lane == 2, g1, route)
    route = jnp.where(lane == 3, g2, route)
    route = jnp.where(lane == 4, rank1, route)
    route = jnp.where(lane == 5, rank2, route)
    route_ref[...] = route


def moe_route(x, g, w_router, tm=256):
    m, d = x.shape
    tm = _tile(m, tm, 16)
    wr = jnp.zeros((d, LANES), F32).at[:, :N_EXPERTS].set(w_router.astype(F32))
    return pl.pallas_call(
        _router_kernel,
        grid=(m // tm,),
        in_specs=[pl.BlockSpec((tm, d), lambda i: (i, 0)),
                  pl.BlockSpec((1, d), lambda i: (0, 0)),
                  pl.BlockSpec((d, LANES), lambda i: (0, 0))],
        out_specs=[pl.BlockSpec((tm, d), lambda i: (i, 0)),
                   pl.BlockSpec((tm, LANES), lambda i: (i, 0)),
                   pl.BlockSpec((8, LANES), lambda i: (0, 0))],
        out_shape=[jax.ShapeDtypeStruct((m, d), BF16),
                   jax.ShapeDtypeStruct((m, LANES), F32),
                   jax.ShapeDtypeStruct((8, LANES), F32)],
        scratch_shapes=[pltpu.VMEM((1, LANES), F32)],
        compiler_params=_cparams(("arbitrary",)),
        name="moe_route",
    )(x, g.reshape(1, d).astype(F32), wr)


def _dispatch_kernel(d1_ref, d2_ref, last_ref, x_ref, o_ref, zero_sc, sem, zsem, *, rows, tm, n_experts):
    i = pl.program_id(0)

    @pl.when(i == 0)
    def _():
        zero_sc[...] = jnp.zeros(zero_sc.shape, zero_sc.dtype)
        fills = [pltpu.make_async_copy(zero_sc, o_ref.at[pl.ds(last_ref[e], tm)], zsem) for e in range(n_experts)]
        for c in fills:
            c.start()
        for c in fills:
            c.wait()

        def clear_unused(blk, carry):
            c = pltpu.make_async_copy(zero_sc, o_ref.at[pl.ds(pl.multiple_of(blk * tm, tm), tm)], zsem)
            c.start()
            c.wait()
            return carry

        lax.fori_loop(last_ref[n_experts], o_ref.shape[0] // tm, clear_unused, 0)

    base = i * rows

    def issue(r8, carry):
        for u in range(8):
            r = r8 * 8 + u
            pltpu.make_async_copy(x_ref.at[r], o_ref.at[d1_ref[base + r]], sem).start(priority=0)
            pltpu.make_async_copy(x_ref.at[r], o_ref.at[d2_ref[base + r]], sem).start(priority=1)
        return carry

    lax.fori_loop(0, rows // 8, issue, 0)
    for _ in range(2):
        pltpu.make_async_copy(x_ref, o_ref.at[pl.ds(0, rows)], sem).wait()


def moe_dispatch(xn, dest1, dest2, last_block_row, n_rows, tm, rows=1024):
    m, d = xn.shape
    rows = _tile(m, rows, 8)
    slab = (d // LANES, LANES)
    out = pl.pallas_call(
        functools.partial(_dispatch_kernel, rows=rows, tm=tm, n_experts=last_block_row.shape[0] - 1),
        grid_spec=pltpu.PrefetchScalarGridSpec(
            num_scalar_prefetch=3,
            grid=(m // rows,),
            in_specs=[pl.BlockSpec((rows,) + slab, lambda i, a, b, c: (i, 0, 0))],
            out_specs=pl.BlockSpec(memory_space=pl.ANY),
            scratch_shapes=[pltpu.VMEM((tm,) + slab, xn.dtype), pltpu.SemaphoreType.DMA(()),
                            pltpu.SemaphoreType.DMA(())],
        ),
        out_shape=jax.ShapeDtypeStruct((n_rows,) + slab, xn.dtype),
        compiler_params=_cparams(("arbitrary",)),
        name="moe_dispatch",
    )(dest1, dest2, last_block_row, xn.reshape((m,) + slab))
    return out.reshape(n_rows, d)


def _combine_kernel(d1_ref, d2_ref, x_ref, route_ref, yb_ref, o_ref, buf1, buf2, sem, *, rows):
    i = pl.program_id(0)
    slot = i % 2

    def issue(step, to_slot):
        base = step * rows

        def rows8(r8, carry):
            r0 = pl.multiple_of(r8 * 8, 8)
            for u in range(8):
                pltpu.make_async_copy(yb_ref.at[pl.ds(d1_ref[base + r0 + u], 1)],
                                      buf1.at[to_slot, pl.ds(r0 + u, 1)], sem.at[to_slot, 0]).start(priority=0)
                pltpu.make_async_copy(yb_ref.at[pl.ds(d2_ref[base + r0 + u], 1)],
                                      buf2.at[to_slot, pl.ds(r0 + u, 1)], sem.at[to_slot, 1]).start(priority=1)
            return carry

        lax.fori_loop(0, rows // 8, rows8, 0)

    @pl.when(i == 0)
    def _():
        issue(0, 0)

    @pl.when(i + 1 < pl.num_programs(0))
    def _():
        issue(i + 1, 1 - slot)

    pltpu.make_async_copy(yb_ref.at[pl.ds(0, rows)], buf1.at[slot], sem.at[slot, 0]).wait()
    pltpu.make_async_copy(yb_ref.at[pl.ds(0, rows)], buf2.at[slot], sem.at[slot, 1]).wait()
    route = route_ref[...]
    o_ref[...] = x_ref[...] + (buf1[slot] * route[:, 2:3] + buf2[slot] * route[:, 3:4])


def moe_combine(x, route, yb, dest1, dest2, rows=256):
    m, d = x.shape
    rows = _tile(m, rows, 8)
    return pl.pallas_call(
        functools.partial(_combine_kernel, rows=rows),
        grid_spec=pltpu.PrefetchScalarGridSpec(
            num_scalar_prefetch=2,
            grid=(m // rows,),
            in_specs=[pl.BlockSpec((rows, d), lambda i, a, b: (i, 0)),
                      pl.BlockSpec((rows, LANES), lambda i, a, b: (i, 0)),
                      pl.BlockSpec(memory_space=pl.ANY)],
            out_specs=pl.BlockSpec((rows, d), lambda i, a, b: (i, 0)),
            scratch_shapes=[pltpu.VMEM((2, rows, d), F32), pltpu.VMEM((2, rows, d), F32),
                            pltpu.SemaphoreType.DMA((2, 2))],
        ),
        out_shape=jax.ShapeDtypeStruct((m, d), F32),
        compiler_params=_cparams(("arbitrary",)),
        name="moe_combine",
    )(dest1, dest2, x, route, yb)


def moe_layer(x, norm_g, w_router, w_gate, w_up, w_down):
    m, d = x.shape
    tm = MOE_TM
    xn, route, cnt = moe_route(x, norm_g, w_router)
    expert = route[:, 0:2].astype(I32)
    rank = route[:, 4:6].astype(I32)
    counts = cnt[0, :N_EXPERTS].astype(I32)
    padded = (counts + tm - 1) // tm * tm
    pend = jnp.cumsum(padded)
    pstart = pend - padded
    dest = pstart[expert] + rank
    n_blk = -(-(2 * m) // tm) + N_EXPERTS
    n_rows = n_blk * tm
    nused = (pend[-1] // tm).astype(I32)
    last_block_row = jnp.concatenate([jnp.maximum(pend - tm, 0).astype(I32), nused.reshape(1)])
    blk = jnp.minimum(jnp.arange(n_blk, dtype=I32), nused - 1)
    blk_e = jnp.minimum(jnp.sum(pend[None, :] <= (blk * tm)[:, None], axis=1), N_EXPERTS - 1).astype(I32)
    nu = nused.reshape(1)

    xb = moe_dispatch(xn, dest[:, 0], dest[:, 1], last_block_row, n_rows, tm)
    tf = _tile(w_gate.shape[2], 512, LANES)
    hidden = gmm_swiglu(xb, w_gate, w_up, blk_e, nu, tm=tm, tn=tf)
    yb = gmm(hidden, w_down, blk_e, nu, tm=tm, tn=_tile(d, 1024, LANES), out_dtype=F32)
    return moe_combine(x, route, yb, dest[:, 0], dest[:, 1])


def even_layer(x2, b, s, pos, w_norm, w_in, q_gain, k_gain, w_cmp_k, w_cmp_v, pe_cmp, conv_w, f_bias, m_norm,
               w_out, w_norm_ffn, w_gate, w_up, w_down):
    n, dm = x2.shape
    g, d = NSA_GROUPS, HEAD_DIM
    o_gate = NSA_SLAB
    o_qb = o_gate + NSA_HEADS * 3
    o_if = o_qb + 3 * MLSTM_WIDTH
    o_ob = o_if + 2 * MLSTM_HEADS
    w_ml = jnp.concatenate([w_in[:, o_qb:o_if], w_in[:, o_ob:o_ob + MLSTM_WIDTH]], axis=1)
    per_g = NSA_HPG * 3
    zeros = functools.partial(jnp.zeros, dtype=w_in.dtype)
    w_small = jnp.concatenate(
        [w_in[:, o_gate:o_gate + per_g], zeros((dm, GATE_I_LANE - per_g)),
         w_in[:, o_if:o_ob], zeros((dm, LANES - GATE_I_LANE - 2 * MLSTM_HEADS)),
         w_in[:, o_gate + per_g:o_qb], zeros((dm, LANES - per_g))], axis=1)

    h = rmsnorm_rows(x2, w_norm)
    p3 = mm(h, w_in[:, :o_gate], out_dtype=BF16, tn=NSA_SLAB // 2).reshape(b, s, NSA_SLAB)
    pm3 = mm(h, w_ml, out_dtype=BF16, tn=1024).reshape(b, s, 4 * MLSTM_WIDTH)
    gates3 = mm(h, w_small, out_dtype=F32, tn=2 * LANES).reshape(b, s, 2 * LANES)

    cos_n, sin_n = trig_tables(pos, _nsa_inv_lane())
    q_t, k_n, v_t = nsa_prep(p3, cos_n.reshape(b, s, LANES), sin_n.reshape(b, s, LANES), q_gain, k_gain)
    nh = s // CMP_STRIDE
    cmp_pos = jnp.concatenate([pos[:, CMP_BLOCK - 1::CMP_STRIDE][:, :nh - 1], pos[:, -1:]], axis=1)
    cos_c, sin_c = trig_tables(cmp_pos, _nsa_inv_lane())

    def cmp_blocks(col0):
        tok = p3[:, :, col0:col0 + NSA_KV_WIDTH].reshape(b, nh, CMP_STRIDE, g, d)
        halves = tok.transpose(0, 3, 1, 2, 4).reshape(b, g, nh, CMP_STRIDE * d)
        nxt = jnp.concatenate([halves[:, :, 1:], jnp.zeros_like(halves[:, :, :1])], axis=2)
        return jnp.concatenate([halves, nxt], axis=-1).reshape(b * g * nh, CMP_BLOCK * d)

    tabs = (cos_c.reshape(b, nh, LANES), sin_c.reshape(b, nh, LANES))
    kc = compress(cmp_blocks(NSA_WIDTH), pe_cmp, w_cmp_k, k_gain, *tabs, is_key=True, rows_per_seq=nh, groups=g)
    vc = compress(cmp_blocks(NSA_WIDTH + NSA_KV_WIDTH), pe_cmp, w_cmp_v, k_gain, *tabs, is_key=False,
                  rows_per_seq=nh, groups=g)
    o_nsa = nsa_attention(q_t, k_n, v_t, kc, vc, gates3)

    qk = conv_silu(pm3, conv_w, 0, 2 * MLSTM_WIDTH)
    h_b = mlstm(qk, pm3, gates3, f_bias, m_norm, 2 * MLSTM_WIDTH, 3 * MLSTM_WIDTH)

    mixed = jnp.concatenate([o_nsa, h_b], axis=-1).reshape(n, NSA_WIDTH + MLSTM_WIDTH)
    x2 = mm(mixed, w_out, out_dtype=F32, residual=x2, tm=1024, tn=1024)
    hf = rmsnorm_rows(x2, w_norm_ffn)
    hidden = mm_swiglu(hf, w_gate, w_up, tm=1024)
    return mm(hidden, w_down, out_dtype=F32, residual=x2, tm=512, tn=1024)


def odd_layer(x2, b, s, pos, w_norm, w_in, r_norm, w_out, w_norm_ffn, w_router, e_gate, e_up, e_down):
    n, dm = x2.shape
    h = rmsnorm_rows(x2, w_norm)
    po3 = mm(h, w_in, out_dtype=BF16, tn=1024).reshape(b, s, -1)
    cos_r, sin_r = trig_tables(pos, _ret_inv_lane())
    y = retention(po3, cos_r.reshape(b, s, LANES), sin_r.reshape(b, s, LANES), r_norm)
    x2 = mm(y.reshape(n, RET_V_WIDTH), w_out, out_dtype=F32, residual=x2, tm=512, tn=1024)
    return moe_layer(x2, w_norm_ffn, w_router, e_gate, e_up, e_down)


def kernel(x, positions, norm_mix_even, w_in_even, nsa_q_gain, nsa_k_gain, w_cmp_k, w_cmp_v, pe_cmp, mlstm_conv, mlstm_f_bias, mlstm_norm, w_out_even, norm_ffn_even, ffn_gate, ffn_up, ffn_down, norm_mix_odd, w_in_odd, ret_norm, w_out_odd, norm_ffn_odd, w_router, exp_gate, exp_up, exp_down):
    b, s, dm = x.shape
    depth = norm_mix_even.shape[0] + norm_mix_odd.shape[0]
    x2 = x.reshape(b * s, dm)
    for layer in range(depth):
        j = layer // 2
        if layer % 2 == 0:
            x2 = even_layer(x2, b, s, positions, norm_mix_even[j], w_in_even[j], nsa_q_gain[j], nsa_k_gain[j],
                            w_cmp_k[j], w_cmp_v[j], pe_cmp[j], mlstm_conv[j], mlstm_f_bias[j], mlstm_norm[j],
                            w_out_even[j], norm_ffn_even[j], ffn_gate[j], ffn_up[j], ffn_down[j])
        else:
            x2 = odd_layer(x2, b, s, positions, norm_mix_odd[j], w_in_odd[j], ret_norm[j], w_out_odd[j],
                           norm_ffn_odd[j], w_router[j], exp_gate[j], exp_up[j], exp_down[j])
    return x2.reshape(b, s, dm)
```

```python
import functools

import numpy as np
import jax
import jax.numpy as jnp
from jax import lax
from jax.experimental import pallas as pl
from jax.experimental.pallas import tpu as pltpu

F32 = jnp.float32
BF16 = jnp.bfloat16
I32 = jnp.int32
HIGHEST = lax.Precision.HIGHEST

HEAD_DIM = 128
NSA_HEADS = 8
NSA_GROUPS = 2
NSA_HPG = NSA_HEADS // NSA_GROUPS
NSA_WIDTH = NSA_HEADS * HEAD_DIM
NSA_KV_WIDTH = NSA_GROUPS * HEAD_DIM
NSA_SLAB = NSA_WIDTH + 6 * NSA_KV_WIDTH
CMP_BLOCK = 32
CMP_STRIDE = 16
SEL_BLOCK = 64
SEL_TOPN = 16
WINDOW = 512
ROPE_DIM = HEAD_DIM // 4
ROPE_THETA = 500000.0
SEL_FORCE = 1.0e6
NEG = -1.0e30
LOWEST = -3.0e38
MLSTM_HEADS = 4
MLSTM_DIM = 256
MLSTM_WIDTH = MLSTM_HEADS * MLSTM_DIM
CONV_WIDTH = 4
RET_HEADS = 8
RET_QK_DIM = 256
RET_V_DIM = 512
RET_QK_WIDTH = RET_HEADS * RET_QK_DIM
RET_V_WIDTH = RET_HEADS * RET_V_DIM
RET_ROPE_THETA = 10000.0
N_EXPERTS = 8
RMS_EPS = 1e-6

LANES = 128
BF16_SUBLANES = 16
V7X_VMEM_BYTES = 64 * 1024 * 1024
VMEM_LIMIT = V7X_VMEM_BYTES - 8 * 1024 * 1024

CHUNK = 256
NSA_TQ = 128
LOG2_E = 1.4426950408889634
V_ROWS = HEAD_DIM + BF16_SUBLANES
NSA_KT = 256
SOFTMAX_ROWS = 64
MOE_TM = 512
CAST_ROWS = 512
GATE_I_LANE = 16
GATE_F_LANE = 20


def _cparams(sem, vmem=VMEM_LIMIT):
    return pltpu.CompilerParams(dimension_semantics=sem, vmem_limit_bytes=vmem)


def _tile(n, target, quantum):
    if n <= target:
        return n
    t = (target // quantum) * quantum
    while t > quantum and n % t:
        t -= quantum
    assert n % t == 0, (n, target, quantum)
    return t


def _sigmoid(x):
    return 1.0 / (1.0 + jnp.exp(-x))


def _dot_nt(a, b):
    return lax.dot_general(a, b, (((1,), (1,)), ((), ())), preferred_element_type=F32)


def _dot_tn(a, b):
    return lax.dot_general(a, b, (((0,), (0,)), ((), ())), preferred_element_type=F32)


def _rmsnorm_kernel(x_ref, g_ref, o_ref):
    x = x_ref[...]
    y = x * lax.rsqrt(jnp.mean(x * x, axis=-1, keepdims=True) + RMS_EPS)
    o_ref[...] = (y * g_ref[...]).astype(o_ref.dtype)


def rmsnorm_rows(x, g, tm=1024):
    m, d = x.shape
    tm = _tile(m, tm, 8)
    return pl.pallas_call(
        _rmsnorm_kernel,
        grid=(m // tm,),
        in_specs=[pl.BlockSpec((tm, d), lambda i: (i, 0)), pl.BlockSpec((1, d), lambda i: (0, 0))],
        out_specs=pl.BlockSpec((tm, d), lambda i: (i, 0)),
        out_shape=jax.ShapeDtypeStruct((m, d), BF16),
        compiler_params=_cparams(("parallel",)),
        name="rmsnorm",
    )(x, g.reshape(1, d).astype(F32))


def _stream_weights(be_ref, nx_ref, w_refs, land_refs, wb_refs, sems, tn):
    j = pl.program_id(0)
    i = pl.program_id(1)

    def copies(e, jj):
        c0 = pl.multiple_of(jj * tn, tn)
        return [pltpu.make_async_copy(w.at[e, :, pl.ds(c0, tn)], land, sems.at[n])
                for n, (w, land) in enumerate(zip(w_refs, land_refs))]

    @pl.when((i == 0) | (be_ref[i] != be_ref[jnp.maximum(i - 1, 0)]))
    def _():
        @pl.when((i == 0) & (j == 0))
        def _():
            for c in copies(be_ref[0], 0):
                c.start()

        for c in copies(be_ref[i], j):
            c.wait()
        rows = int(np.gcd(land_refs[0].shape[0], CAST_ROWS))

        def cast_rows(s, carry):
            r0 = pl.multiple_of(s * rows, rows)
            for land, wb in zip(land_refs, wb_refs):
                wb[pl.ds(r0, rows), :] = land[pl.ds(r0, rows), :].astype(BF16)
            return carry

        lax.fori_loop(0, land_refs[0].shape[0] // rows, cast_rows, 0)
        nxt = nx_ref[i]

        @pl.when(nxt >= 0)
        def _():
            for c in copies(nxt, j):
                c.start()

        @pl.when((nxt < 0) & (j + 1 < pl.num_programs(0)))
        def _():
            for c in copies(be_ref[0], j + 1):
                c.start()


def _next_group_expert(blk_e, n_experts):
    e = jnp.arange(n_experts, dtype=I32)
    present = jnp.any(blk_e[:, None] == e[None, :], axis=0)
    later = jnp.where((e[None, :] > blk_e[:, None]) & present[None, :], e[None, :], n_experts)
    nxt = jnp.min(later, axis=1)
    return jnp.where(nxt == n_experts, -1, nxt).astype(I32)


def _gmm_kernel(be_ref, nx_ref, nu_ref, x_ref, w_ref, *rest, has_res, tn):
    if has_res:
        r_ref, o_ref, land_ref, wb_ref, sems = rest
    else:
        o_ref, land_ref, wb_ref, sems = rest
    i = pl.program_id(1)
    _stream_weights(be_ref, nx_ref, [w_ref], [land_ref], [wb_ref], sems, tn)

    @pl.when(i < nu_ref[0])
    def _():
        acc = jnp.dot(x_ref[...], wb_ref[...], preferred_element_type=F32)
        if has_res:
            acc = r_ref[...] + acc
        o_ref[...] = acc.astype(o_ref.dtype)

    @pl.when(i >= nu_ref[0])
    def _():
        o_ref[...] = jnp.zeros_like(o_ref)


def gmm(x, w, blk_e, nused, *, tm, tn, out_dtype, residual=None):
    m, k = x.shape
    e, k2, n = w.shape
    assert k == k2 and m % tm == 0 and n % tn == 0
    nb = m // tm
    in_specs = [
        pl.BlockSpec((tm, k), lambda j, i, be, nx, nu: (jnp.minimum(i, nu[0] - 1), 0)),
        pl.BlockSpec(memory_space=pl.ANY),
    ]
    args = [x, w]
    if residual is not None:
        in_specs.append(pl.BlockSpec((tm, tn), lambda j, i, be, nx, nu: (i, j)))
        args.append(residual)
    return pl.pallas_call(
        functools.partial(_gmm_kernel, has_res=residual is not None, tn=tn),
        grid_spec=pltpu.PrefetchScalarGridSpec(
            num_scalar_prefetch=3,
            grid=(n // tn, nb),
            in_specs=in_specs,
            out_specs=pl.BlockSpec((tm, tn), lambda j, i, be, nx, nu: (i, j)),
            scratch_shapes=[pltpu.VMEM((k, tn), w.dtype), pltpu.VMEM((k, tn), BF16),
                            pltpu.SemaphoreType.DMA((1,))],
        ),
        out_shape=jax.ShapeDtypeStruct((m, n), out_dtype),
        compiler_params=_cparams(("arbitrary", "arbitrary")),
        name="gmm",
    )(blk_e, _next_group_expert(blk_e, e), nused, *args)


def _gmm_swiglu_kernel(be_ref, nx_ref, nu_ref, x_ref, wg_ref, wu_ref, o_ref, lg_ref, lu_ref, wgb_ref, wub_ref,
                       sems, *, tn):
    i = pl.program_id(1)
    _stream_weights(be_ref, nx_ref, [wg_ref, wu_ref], [lg_ref, lu_ref], [wgb_ref, wub_ref], sems, tn)

    @pl.when(i < nu_ref[0])
    def _():
        x = x_ref[...]
        g = jnp.dot(x, wgb_ref[...], preferred_element_type=F32)
        u = jnp.dot(x, wub_ref[...], preferred_element_type=F32)
        o_ref[...] = (g * _sigmoid(g) * u).astype(o_ref.dtype)

    @pl.when(i >= nu_ref[0])
    def _():
        o_ref[...] = jnp.zeros_like(o_ref)


def gmm_swiglu(x, wg, wu, blk_e, nused, *, tm, tn):
    m, k = x.shape
    e, k2, n = wg.shape
    assert k == k2 and wu.shape == wg.shape and m % tm == 0 and n % tn == 0
    nb = m // tm
    hbm = pl.BlockSpec(memory_space=pl.ANY)
    return pl.pallas_call(
        functools.partial(_gmm_swiglu_kernel, tn=tn),
        grid_spec=pltpu.PrefetchScalarGridSpec(
            num_scalar_prefetch=3,
            grid=(n // tn, nb),
            in_specs=[pl.BlockSpec((tm, k), lambda j, i, be, nx, nu: (jnp.minimum(i, nu[0] - 1), 0)), hbm, hbm],
            out_specs=pl.BlockSpec((tm, tn), lambda j, i, be, nx, nu: (i, j)),
            scratch_shapes=[pltpu.VMEM((k, tn), wg.dtype), pltpu.VMEM((k, tn), wu.dtype),
                            pltpu.VMEM((k, tn), BF16), pltpu.VMEM((k, tn), BF16),
                            pltpu.SemaphoreType.DMA((2,))],
        ),
        out_shape=jax.ShapeDtypeStruct((m, n), BF16),
        compiler_params=_cparams(("arbitrary", "arbitrary")),
        name="gmm_swiglu",
    )(blk_e, _next_group_expert(blk_e, e), nused, x, wg, wu)


def _dense_blocks(m, tm):
    nb = m // tm
    return jnp.zeros((nb,), I32), jnp.full((1,), nb, I32)


def mm(x, w, *, out_dtype, residual=None, tm=2048, tn=512):
    m, k = x.shape
    n = w.shape[1]
    tm = _tile(m, tm, 16)
    tn = _tile(n, tn, LANES)
    be, nu = _dense_blocks(m, tm)
    return gmm(x, w[None], be, nu, tm=tm, tn=tn, out_dtype=out_dtype, residual=residual)


def mm_swiglu(x, wg, wu, *, tm=2048, tn=512):
    m = x.shape[0]
    tm = _tile(m, tm, 16)
    tn = _tile(wg.shape[1], tn, LANES)
    be, nu = _dense_blocks(m, tm)
    return gmm_swiglu(x, wg[None], wu[None], be, nu, tm=tm, tn=tn)


def _trig_kernel(pos_ref, inv_ref, cos_ref, sin_ref):
    ang = pos_ref[...] * inv_ref[...]
    cos_ref[...] = jnp.cos(ang)
    sin_ref[...] = jnp.sin(ang)


def trig_tables(pos, inv_lane):
    r = pos.size
    pos_b = jnp.broadcast_to(pos.astype(F32).reshape(r, 1), (r, LANES))
    tr = _tile(r, 2048, 8)
    spec = pl.BlockSpec((tr, LANES), lambda i: (i, 0))
    return pl.pallas_call(
        _trig_kernel,
        grid=(r // tr,),
        in_specs=[spec, pl.BlockSpec((1, LANES), lambda i: (0, 0))],
        out_specs=[spec, spec],
        out_shape=[jax.ShapeDtypeStruct((r, LANES), F32)] * 2,
        compiler_params=_cparams(("parallel",)),
        name="trig_tables",
    )(pos_b, inv_lane.reshape(1, LANES))


def _nsa_inv_lane():
    half = ROPE_DIM // 2
    inv = jnp.power(jnp.float32(ROPE_THETA), -jnp.arange(half, dtype=F32) * (2.0 / ROPE_DIM))
    return jnp.concatenate([inv, inv, jnp.zeros((LANES - ROPE_DIM,), F32)])


def _ret_inv_lane():
    half = RET_QK_DIM // 2
    return jnp.power(jnp.float32(RET_ROPE_THETA), -jnp.arange(half, dtype=F32) * (2.0 / RET_QK_DIM))


def _norm_rope_head(x, gain, cos, sin):
    half = ROPE_DIM // 2
    y = x * lax.rsqrt(jnp.mean(x * x, axis=-1, keepdims=True) + RMS_EPS) * gain
    lane = lax.broadcasted_iota(I32, y.shape, 1)
    from_hi = jnp.where(lane < half, -sin, 0.0)
    from_lo = jnp.where((lane >= half) & (lane < ROPE_DIM), sin, 0.0)
    return (y * cos + pltpu.roll(y, LANES - half, 1) * from_hi + pltpu.roll(y, half, 1) * from_lo)


def _nsa_prep_kernel(p_ref, cos_ref, sin_ref, qg_ref, kg_ref, qt_ref, kn_ref, vt_ref):
    cos = cos_ref[0]
    sin = sin_ref[0]
    scale = HEAD_DIM ** -0.5 * LOG2_E
    d = HEAD_DIM
    g = NSA_GROUPS
    for hd in range(NSA_HEADS):
        q = _norm_rope_head(p_ref[0, :, hd * d:(hd + 1) * d].astype(F32), qg_ref[...], cos, sin)
        qt_ref[0, hd * d:(hd + 1) * d, :] = (q * scale).T.astype(BF16)
    for n, slab in enumerate((2, 4)):
        for gi in range(g):
            off = NSA_WIDTH + slab * NSA_KV_WIDTH + gi * d
            k = _norm_rope_head(p_ref[0, :, off:off + d].astype(F32), kg_ref[...], cos, sin)
            kn_ref[0, :, (n * g + gi) * d:(n * g + gi + 1) * d] = k.astype(BF16)
    for n, slab in enumerate((3, 5)):
        for gi in range(g):
            off = NSA_WIDTH + slab * NSA_KV_WIDTH + gi * d
            v = p_ref[0, :, off:off + d].astype(F32)
            r0 = (n * g + gi) * V_ROWS
            vt_ref[0, r0:r0 + d, :] = v.T.astype(BF16)
            vt_ref[0, r0 + d:r0 + V_ROWS, :] = jnp.ones((V_ROWS - d, v.shape[0]), BF16)


def nsa_prep(p3, cos, sin, q_gain, k_gain):
    b, s, _ = p3.shape
    t = _tile(s, 256, LANES)
    d = HEAD_DIM
    tab = pl.BlockSpec((1, t, LANES), lambda bi, i: (bi, i, 0))
    gain = pl.BlockSpec((1, d), lambda bi, i: (0, 0))
    kv = 2 * NSA_KV_WIDTH
    vr = 2 * NSA_GROUPS * V_ROWS
    return pl.pallas_call(
        _nsa_prep_kernel,
        grid=(b, s // t),
        in_specs=[pl.BlockSpec((1, t, NSA_SLAB), lambda bi, i: (bi, i, 0)), tab, tab, gain, gain],
        out_specs=[pl.BlockSpec((1, NSA_WIDTH, t), lambda bi, i: (bi, 0, i)),
                   pl.BlockSpec((1, t, kv), lambda bi, i: (bi, i, 0)),
                   pl.BlockSpec((1, vr, t), lambda bi, i: (bi, 0, i))],
        out_shape=[jax.ShapeDtypeStruct((b, NSA_WIDTH, s), BF16),
                   jax.ShapeDtypeStruct((b, s, kv), BF16),
                   jax.ShapeDtypeStruct((b, vr, s), BF16)],
        compiler_params=_cparams(("parallel", "parallel")),
        name="nsa_prep",
    )(p3, cos, sin, q_gain.reshape(1, d), k_gain.reshape(1, d))


def _compress_kernel(blk_ref, pe_ref, w_ref, kg_ref, cos_ref, sin_ref, o_ref, *, is_key):
    a = (blk_ref[...].astype(F32) + pe_ref[...]).astype(BF16)
    y = jnp.dot(a, w_ref[...].astype(BF16), preferred_element_type=F32)
    if is_key:
        o_ref[0] = _norm_rope_head(y, kg_ref[...], cos_ref[0], sin_ref[0]).astype(BF16)
    else:
        o_ref[0] = y.T.astype(BF16)


def compress(blk, pe, w, k_gain, cos_c, sin_c, *, is_key, rows_per_seq, groups):
    r, kdim = blk.shape
    t = rows_per_seq
    d = HEAD_DIM
    tab = pl.BlockSpec((1, t, LANES), lambda i: (i // groups, 0, 0))
    out_blk = (1, t, d) if is_key else (1, d, t)
    return pl.pallas_call(
        functools.partial(_compress_kernel, is_key=is_key),
        grid=(r // t,),
        in_specs=[pl.BlockSpec((t, kdim), lambda i: (i, 0)),
                  pl.BlockSpec((1, kdim), lambda i: (0, 0)),
                  pl.BlockSpec((kdim, d), lambda i: (0, 0)),
                  pl.BlockSpec((1, d), lambda i: (0, 0)),
                  tab, tab],
        out_specs=pl.BlockSpec(out_blk, lambda i: (i, 0, 0)),
        out_shape=jax.ShapeDtypeStruct((r // t,) + out_blk[1:], BF16),
        compiler_params=_cparams(("parallel",)),
        name="nsa_compress",
    )(blk, pe.reshape(1, kdim), w, k_gain.reshape(1, d), cos_c, sin_c)


def _nsa_attn_kernel(qt_ref, kc_ref, vct_ref, ks_ref, kw_ref, vst_ref, vwt_ref, et_ref, ovt_ref, gate_ref,
                     o_ref, acc_sc, val_sc, sa_sc, sb_sc, pa_sc, pb_sc, sc_sc, pc_sc, pg_sc, sw_sc, pw_sc, ow_sc,
                     *, tq, kt, wk, ns, n_top):
    i = pl.program_id(2)
    t0 = i * tq
    cols = NSA_HPG * tq
    d = HEAD_DIM
    q_t = jnp.concatenate([qt_ref[0, p * d:(p + 1) * d, :] for p in range(NSA_HPG)], axis=1)
    t_lane = t0 + (lax.broadcasted_iota(I32, (1, cols), 1) & (tq - 1))

    ncp = kc_ref.shape[1]
    rs = SOFTMAX_ROWS
    sc_sc[...] = jnp.dot(kc_ref[0], q_t, preferred_element_type=F32)
    w0 = pl.multiple_of(jnp.maximum(t0 + tq - wk, 0), LANES)
    sw_sc[...] = jnp.dot(kw_ref[0, pl.ds(w0, wk), :], q_t, preferred_element_type=F32)

    def cmask(r0):
        c_end = (r0 + lax.broadcasted_iota(I32, (rs, cols), 0)) * CMP_STRIDE + (CMP_BLOCK - 1)
        return c_end <= t_lane

    m_c = jnp.full((1, cols), NEG, F32)
    for r0 in range(0, ncp, rs):
        m_c = jnp.maximum(m_c, jnp.max(jnp.where(cmask(r0), sc_sc[r0:r0 + rs, :], NEG), axis=0, keepdims=True))
    den_c = jnp.zeros((1, cols), F32)
    for r0 in range(0, ncp, rs):
        e = jnp.where(cmask(r0), jnp.exp2(sc_sc[r0:r0 + rs, :] - m_c), 0.0)
        sc_sc[r0:r0 + rs, :] = e
        den_c = den_c + jnp.sum(e, axis=0, keepdims=True)
    inv_c = 1.0 / jnp.maximum(den_c, 1e-30)
    for r0 in range(0, ncp, rs):
        p = sc_sc[r0:r0 + rs, :] * inv_c
        pc_sc[r0:r0 + rs, :] = p.astype(BF16)
        p_grp = p[:, 0:tq]
        for hp in range(1, NSA_HPG):
            p_grp = p_grp + p[:, hp * tq:(hp + 1) * tq]
        pg_sc[r0:r0 + rs, :] = p_grp
    o_cmp = jnp.dot(vct_ref[0], pc_sc[...], preferred_element_type=F32)

    m_w = jnp.full((1, cols), NEG, F32)
    for r0 in range(0, wk, rs):
        kpos = w0 + r0 + lax.broadcasted_iota(I32, (rs, cols), 0)
        wmask = (kpos <= t_lane) & (kpos > t_lane - WINDOW)
        s = jnp.where(wmask, sw_sc[r0:r0 + rs, :], NEG)
        sw_sc[r0:r0 + rs, :] = s
        m_w = jnp.maximum(m_w, jnp.max(s, axis=0, keepdims=True))
    for r0 in range(0, wk, rs):
        pw_sc[r0:r0 + rs, :] = jnp.exp2(sw_sc[r0:r0 + rs, :] - m_w).astype(BF16)
    o_win = jnp.dot(vwt_ref[0, :, pl.ds(w0, wk)], pw_sc[...], preferred_element_type=F32)
    ow_sc[...] = o_win[:d] * (1.0 / o_win[d:d + 1])

    imp = jnp.dot(ovt_ref[...], pg_sc[...], precision=HIGHEST, preferred_element_type=F32)
    nsp = val_sc.shape[0]
    jb = lax.broadcasted_iota(I32, (nsp, tq), 0)
    cur = (t0 + lax.broadcasted_iota(I32, (1, tq), 1)) // SEL_BLOCK
    forced = (jb == 0) | (jb == cur) | (jb == cur - 1)
    val = jnp.where(jb <= cur, jnp.where(forced, SEL_FORCE, imp[:nsp]), NEG)
    val = jnp.where(jb < ns, val, LOWEST)
    val_sc[...] = val
    beaten = jnp.zeros((nsp, tq), F32)
    for j2 in range(ns):
        r = val_sc[j2:j2 + 1, :]
        ge = jnp.where(r >= val, 1.0, 0.0)
        gt = jnp.where(r > val, 1.0, 0.0)
        beaten = beaten + jnp.where(jb > j2, ge, gt)
    past = jb < t0 // SEL_BLOCK
    bias = jnp.where(past & (beaten < n_top), 0.0, jnp.where(jb < ns, NEG, 0.0)).astype(BF16)
    bias = jnp.concatenate([bias, jnp.zeros((LANES - nsp, tq), BF16)], axis=0)
    q_aug = jnp.concatenate([q_t, jnp.concatenate([bias] * NSA_HPG, axis=1)], axis=0)

    d0 = pl.multiple_of(t0, tq)
    s_d = jnp.dot(ks_ref[0, pl.ds(d0, tq), :], q_t, preferred_element_type=F32)
    s_d = jnp.where(d0 + lax.broadcasted_iota(I32, (tq, cols), 0) <= t_lane, s_d, NEG)
    m_d = jnp.max(s_d, axis=0, keepdims=True)
    p_d = jnp.exp2(s_d - m_d)
    acc_sc[...] = jnp.dot(vst_ref[0, :, pl.ds(d0, tq)], p_d.astype(BF16), preferred_element_type=F32)

    n_pairs = (t0 + 2 * kt - 1) // (2 * kt)
    last_a = jnp.maximum(n_pairs - 1, 0) * (2 * kt)

    def scores(k0):
        k0 = pl.multiple_of(k0, kt)
        k_aug = jnp.concatenate([ks_ref[0, pl.ds(k0, kt), :], et_ref[pl.ds(k0, kt), :]], axis=1)
        return jnp.dot(k_aug, q_aug, preferred_element_type=F32)

    def values(p_ref, k0):
        k0 = pl.multiple_of(k0, kt)
        return jnp.dot(vst_ref[0, :, pl.ds(k0, kt)], p_ref[...], preferred_element_type=F32)

    def softmax_update(s_ref, p_ref, m_prev):
        m_new = jnp.maximum(m_prev, jnp.max(s_ref[...], axis=0, keepdims=True))
        for r0 in range(0, kt, SOFTMAX_ROWS):
            p_ref[r0:r0 + SOFTMAX_ROWS, :] = jnp.exp2(s_ref[r0:r0 + SOFTMAX_ROWS, :] - m_new).astype(BF16)
        return m_new, jnp.exp2(m_prev - m_new)

    def sel_step(j, carry):
        m_run, alpha_b = carry
        k0 = j * (2 * kt)
        sb_sc[...] = scores(k0 + kt)
        acc_sc[...] = alpha_b * acc_sc[...] + values(pb_sc, jnp.maximum(k0 - kt, 0))
        m_run, alpha_a = softmax_update(sa_sc, pa_sc, m_run)
        sa_sc[...] = scores(jnp.minimum(k0 + 2 * kt, last_a))
        acc_sc[...] = alpha_a * acc_sc[...] + values(pa_sc, k0)
        return softmax_update(sb_sc, pb_sc, m_run)

    sa_sc[...] = scores(0)
    pb_sc[...] = jnp.zeros(pb_sc.shape, BF16)
    _, alpha_last = lax.fori_loop(0, n_pairs, sel_step, (m_d, jnp.ones((1, cols), F32)))
    acc = alpha_last * acc_sc[...] + values(pb_sc, last_a + kt)
    o_sel = acc[:d] * (1.0 / acc[d:d + 1])

    g_t = _sigmoid(gate_ref[0]).T
    for p in range(NSA_HPG):
        sl = slice(p * tq, (p + 1) * tq)
        o_t = (g_t[3 * p:3 * p + 1] * o_cmp[:, sl] + g_t[3 * p + 1:3 * p + 2] * o_sel[:, sl]
               + g_t[3 * p + 2:3 * p + 3] * ow_sc[:, sl])
        o_ref[0, :, p * d:(p + 1) * d] = o_t.T.astype(BF16)


def nsa_attention(q_t, k_n, v_t, kc, vc_t, gates3):
    b, _, s = q_t.shape
    g, d = NSA_GROUPS, HEAD_DIM
    ncp = kc.shape[1]
    tq = NSA_TQ
    kt = min(NSA_KT, s)
    wk = WINDOW + tq
    ns = s // SEL_BLOCK
    assert s % (2 * kt) == 0 and s >= wk and ns <= LANES and tq == LANES
    assert kt % SOFTMAX_ROWS == 0 and ncp % SOFTMAX_ROWS == 0 and wk % SOFTMAX_ROWS == 0
    n_top = min(SEL_TOPN, ns)
    cols = NSA_HPG * tq

    key_blk = np.arange(s) // SEL_BLOCK
    e_t = jnp.asarray(key_blk[:, None] == np.arange(LANES)[None, :], dtype=BF16)
    c_start = np.arange(ncp) * CMP_STRIDE
    j_start = np.arange(LANES) * SEL_BLOCK
    overlap = ((c_start[:, None] < j_start[None, :] + SEL_BLOCK) & (c_start[:, None] + CMP_BLOCK > j_start[None, :])
               & (np.arange(ncp)[:, None] < s // CMP_STRIDE - 1) & (np.arange(LANES)[None, :] < ns))
    overlap_t = jnp.asarray(overlap.T.astype(np.float32))

    q_rows = NSA_HPG * d
    return pl.pallas_call(
        functools.partial(_nsa_attn_kernel, tq=tq, kt=kt, wk=wk, ns=ns, n_top=n_top),
        grid=(b, g, s // tq),
        in_specs=[pl.BlockSpec((1, q_rows, tq), lambda bi, gi, i: (bi, gi, i)),
                  pl.BlockSpec((1, ncp, d), lambda bi, gi, i: (bi * g + gi, 0, 0)),
                  pl.BlockSpec((1, d, ncp), lambda bi, gi, i: (bi * g + gi, 0, 0)),
                  pl.BlockSpec((1, s, d), lambda bi, gi, i: (bi, 0, gi)),
                  pl.BlockSpec((1, s, d), lambda bi, gi, i: (bi, 0, g + gi)),
                  pl.BlockSpec((1, V_ROWS, s), lambda bi, gi, i: (bi, gi, 0)),
                  pl.BlockSpec((1, V_ROWS, s), lambda bi, gi, i: (bi, g + gi, 0)),
                  pl.BlockSpec((s, LANES), lambda bi, gi, i: (0, 0)),
                  pl.BlockSpec((LANES, ncp), lambda bi, gi, i: (0, 0)),
                  pl.BlockSpec((1, tq, LANES), lambda bi, gi, i: (bi, i, gi))],
        out_specs=pl.BlockSpec((1, tq, q_rows), lambda bi, gi, i: (bi, i, gi)),
        out_shape=jax.ShapeDtypeStruct((b, s, NSA_WIDTH), BF16),
        scratch_shapes=[pltpu.VMEM((V_ROWS, cols), F32), pltpu.VMEM((-(-ns // 8) * 8, tq), F32),
                        pltpu.VMEM((kt, cols), F32), pltpu.VMEM((kt, cols), F32),
                        pltpu.VMEM((kt, cols), BF16), pltpu.VMEM((kt, cols), BF16),
                        pltpu.VMEM((ncp, cols), F32), pltpu.VMEM((ncp, cols), BF16), pltpu.VMEM((ncp, tq), F32),
                        pltpu.VMEM((wk, cols), F32), pltpu.VMEM((wk, cols), BF16), pltpu.VMEM((d, cols), F32)],
        compiler_params=_cparams(("parallel", "parallel", "arbitrary")),
        name="nsa_attention",
    )(q_t, kc, vc_t, k_n, k_n, v_t, v_t, e_t, overlap_t, gates3)


def _conv_kernel(cur_ref, prev_ref, w_ref, o_ref, *, tc):
    i = pl.program_id(1)
    cur = cur_ref[0].astype(F32)
    prev = jnp.where(i > 0, prev_ref[0].astype(F32), 0.0)
    pad = prev.shape[0] // 2
    full = jnp.concatenate([prev[pad:], cur], axis=0)
    y = None
    for kk in range(CONV_WIDTH):
        off = pad - (CONV_WIDTH - 1) + kk
        term = w_ref[kk:kk + 1, :] * full[off:off + tc]
        y = term if y is None else y + term
    o_ref[0] = (y * _sigmoid(y)).astype(BF16)


def conv_silu(p3, w, col0, width):
    b, s, _ = p3.shape
    tc = _tile(s, 512, BF16_SUBLANES)
    cw = 1024
    assert col0 % cw == 0 and width % cw == 0
    c0 = col0 // cw
    halo = BF16_SUBLANES
    return pl.pallas_call(
        functools.partial(_conv_kernel, tc=tc),
        grid=(b, s // tc, width // cw),
        in_specs=[pl.BlockSpec((1, tc, cw), lambda bi, i, j: (bi, i, c0 + j)),
                  pl.BlockSpec((1, halo, cw), lambda bi, i, j: (bi, jnp.maximum(i * (tc // halo) - 1, 0), c0 + j)),
                  pl.BlockSpec((CONV_WIDTH, cw), lambda bi, i, j: (0, j))],
        out_specs=pl.BlockSpec((1, tc, cw), lambda bi, i, j: (bi, i, j)),
        out_shape=jax.ShapeDtypeStruct((b, s, width), BF16),
        compiler_params=_cparams(("parallel", "parallel", "parallel")),
        name="mlstm_conv",
    )(p3, p3, w)


def _log_sigmoid(x):
    return jnp.minimum(x, 0.0) - jnp.log1p(jnp.exp(-jnp.abs(x)))


def _mlstm_kernel(fb_ref, q_ref, k_ref, v_ref, ob_ref, gate_ref, nw_ref, o_ref, c_sc, n_sc, m_sc, *, cl):
    @pl.when(pl.program_id(1) == 0)
    def _():
        c_sc[...] = jnp.zeros(c_sc.shape, F32)
        n_sc[...] = jnp.zeros(n_sc.shape, F32)
        m_sc[...] = jnp.zeros(m_sc.shape, F32)

    dh = MLSTM_DIM
    slab = gate_ref[0]
    lane = lax.broadcasted_iota(I32, slab.shape, 1)
    r_i = lax.broadcasted_iota(I32, (cl, cl), 0)
    c_i = lax.broadcasted_iota(I32, (cl, cl), 1)
    eye = r_i == c_i
    tri = c_i <= r_i

    def head(hd, carry):
        off = pl.multiple_of(hd * dh, dh)
        q = q_ref[0, :, pl.ds(off, dh)]
        v = v_ref[0, :, pl.ds(off, dh)]
        ks32 = k_ref[0, :, pl.ds(off, dh)].astype(F32) * (dh ** -0.5)
        ks = ks32.astype(BF16)
        i_col = jnp.sum(jnp.where(lane == GATE_I_LANE + hd, slab, 0.0), axis=-1, keepdims=True)
        f_col = jnp.sum(jnp.where(lane == GATE_F_LANE + hd, slab, 0.0), axis=-1, keepdims=True)
        lf_col = _log_sigmoid(f_col + fb_ref[hd])

        lf_row = jnp.sum(jnp.where(eye, lf_col, 0.0), axis=0, keepdims=True)
        ig_row = jnp.sum(jnp.where(eye, i_col, 0.0), axis=0, keepdims=True)
        a_col = jnp.sum(jnp.where(tri, lf_row, 0.0), axis=1, keepdims=True)
        a_row = jnp.sum(jnp.where(r_i <= c_i, lf_col, 0.0), axis=0, keepdims=True)
        m_prev = m_sc[hd]

        dlog = jnp.where(tri, a_col - a_row + ig_row, NEG)
        inter = a_col + m_prev
        mt = jnp.maximum(inter, jnp.max(dlog, axis=-1, keepdims=True))
        wm = jnp.exp(dlog - mt) * _dot_nt(q, ks)
        e_col = jnp.exp(inter - mt)
        num = e_col * jnp.dot(q, c_sc[hd].astype(BF16), preferred_element_type=F32) \
            + jnp.dot(wm.astype(BF16), v, preferred_element_type=F32)
        qn = jnp.sum(q.astype(F32) * n_sc[hd], axis=-1, keepdims=True)
        den = e_col * qn + jnp.sum(wm, axis=-1, keepdims=True)
        hh = num / jnp.maximum(jnp.abs(den), jnp.exp(-mt))

        a_last = jnp.sum(lf_row, axis=-1, keepdims=True)
        gs = a_last - a_col + i_col
        m_new = jnp.maximum(a_last + m_prev, jnp.max(gs, axis=0, keepdims=True))
        decay = jnp.exp(a_last + m_prev - m_new)
        wk = jnp.exp(gs - m_new) * ks32
        c_sc[hd] = decay * c_sc[hd] + _dot_tn(wk.astype(BF16), v)
        n_sc[hd] = decay * n_sc[hd] + jnp.sum(wk, axis=0, keepdims=True)
        m_sc[hd] = m_new

        y = hh * lax.rsqrt(jnp.mean(hh * hh, axis=-1, keepdims=True) + RMS_EPS) * nw_ref[:, pl.ds(off, dh)]
        o_ref[0, :, pl.ds(off, dh)] = (y * _sigmoid(ob_ref[0, :, pl.ds(off, dh)].astype(F32))).astype(BF16)
        return carry

    lax.fori_loop(0, MLSTM_HEADS, head, 0)


def mlstm(qk, p3, gates3, f_bias, norm_w, v_col0, o_col0):
    b, s, _ = qk.shape
    cl = min(CHUNK, s)
    dh = MLSTM_DIM
    nh = MLSTM_HEADS
    w = nh * dh
    assert v_col0 % w == 0 and o_col0 % w == 0 and s % cl == 0
    vb, ob = v_col0 // w, o_col0 // w
    return pl.pallas_call(
        functools.partial(_mlstm_kernel, cl=cl),
        grid=(b, s // cl),
        in_specs=[pl.BlockSpec(memory_space=pltpu.SMEM),
                  pl.BlockSpec((1, cl, w), lambda bi, c: (bi, c, 0)),
                  pl.BlockSpec((1, cl, w), lambda bi, c: (bi, c, 1)),
                  pl.BlockSpec((1, cl, w), lambda bi, c: (bi, c, vb)),
                  pl.BlockSpec((1, cl, w), lambda bi, c: (bi, c, ob)),
                  pl.BlockSpec((1, cl, LANES), lambda bi, c: (bi, c, 0)),
                  pl.BlockSpec((1, w), lambda bi, c: (0, 0))],
        out_specs=pl.BlockSpec((1, cl, w), lambda bi, c: (bi, c, 0)),
        out_shape=jax.ShapeDtypeStruct((b, s, w), BF16),
        scratch_shapes=[pltpu.VMEM((nh, dh, dh), F32), pltpu.VMEM((nh, 1, dh), F32), pltpu.VMEM((nh, 1, 1), F32)],
        compiler_params=_cparams(("parallel", "arbitrary")),
        name="mlstm",
    )(f_bias.astype(F32), qk, qk, p3, p3, gates3, norm_w.reshape(1, w).astype(F32))


def _ret_kernel(cd_ref, q_ref, k_ref, v_ref, g_ref, cos_ref, sin_ref, nw_ref, dm_ref, xi_ref, zeta_ref,
                o_ref, r_sc, *, cl):
    @pl.when(pl.program_id(1) == 0)
    def _():
        r_sc[...] = jnp.zeros(r_sc.shape, F32)

    cos = cos_ref[0]
    sin = sin_ref[0]
    dk, dv = RET_QK_DIM, RET_V_DIM
    half = dk // 2
    scale = dk ** -0.5

    def rope(x):
        x1, x2 = x[:, :half], x[:, half:]
        return x1 * cos - x2 * sin, x1 * sin + x2 * cos

    def head(hd, carry):
        qo = pl.multiple_of(hd * dk, dk)
        vo = pl.multiple_of(hd * dv, dv)
        q1, q2 = rope(q_ref[0, :, pl.ds(qo, dk)].astype(F32))
        qr = jnp.concatenate([q1, q2], axis=1).astype(BF16)
        k1, k2 = rope(k_ref[0, :, pl.ds(qo, dk)].astype(F32))
        zeta = zeta_ref[hd] * scale
        kr = jnp.concatenate([k1 * scale, k2 * scale], axis=1).astype(BF16)
        kz = jnp.concatenate([k1 * zeta, k2 * zeta], axis=1).astype(BF16)
        v = v_ref[0, :, pl.ds(vo, dv)]

        inner = jnp.dot((_dot_nt(qr, kr) * dm_ref[hd]).astype(BF16), v, preferred_element_type=F32)
        xi = xi_ref[hd]
        cross = jnp.dot(qr, r_sc[hd].astype(BF16), preferred_element_type=F32)
        cross = cross * jnp.concatenate([xi] * (dv // LANES), axis=1)
        r_sc[hd] = cd_ref[hd] * r_sc[hd] + _dot_tn(kz, v)

        y = inner + cross
        y = y * lax.rsqrt(jnp.mean(y * y, axis=-1, keepdims=True) + RMS_EPS) * nw_ref[:, pl.ds(vo, dv)]
        gg = g_ref[0, :, pl.ds(vo, dv)].astype(F32)
        o_ref[0, :, pl.ds(vo, dv)] = (y * (gg * _sigmoid(gg))).astype(BF16)
        return carry

    lax.fori_loop(0, RET_HEADS, head, 0)


def retention(po3, cos, sin, norm_w):
    b, s, _ = po3.shape
    cl = min(CHUNK, s)
    nh, dk, dv = RET_HEADS, RET_QK_DIM, RET_V_DIM
    log_g = jnp.log1p(-jnp.exp2(-5.0 - jnp.arange(nh, dtype=F32)))
    idx = jnp.arange(cl, dtype=F32)
    diff = idx[:, None] - idx[None, :]
    dm = jnp.where(diff >= 0, jnp.exp(jnp.maximum(diff, 0.0) * log_g[:, None, None]), 0.0)
    xi = jnp.broadcast_to(jnp.exp((idx + 1.0) * log_g[:, None])[..., None], (nh, cl, LANES))
    zeta = jnp.broadcast_to(jnp.exp((cl - 1.0 - idx) * log_g[:, None])[..., None], (nh, cl, LANES))
    chunk_decay = jnp.exp(cl * log_g)
    qk_w, v_w = RET_QK_WIDTH, RET_V_WIDTH
    assert v_w == 2 * qk_w
    tab = pl.BlockSpec((1, cl, LANES), lambda bi, c: (bi, c, 0))

    def table(shape):
        return pl.BlockSpec(shape, lambda bi, c: (0, 0, 0))

    return pl.pallas_call(
        functools.partial(_ret_kernel, cl=cl),
        grid=(b, s // cl),
        in_specs=[pl.BlockSpec(memory_space=pltpu.SMEM),
                  pl.BlockSpec((1, cl, qk_w), lambda bi, c: (bi, c, 0)),
                  pl.BlockSpec((1, cl, qk_w), lambda bi, c: (bi, c, 1)),
                  pl.BlockSpec((1, cl, v_w), lambda bi, c: (bi, c, 1)),
                  pl.BlockSpec((1, cl, v_w), lambda bi, c: (bi, c, 2)),
                  tab, tab,
                  pl.BlockSpec((1, v_w), lambda bi, c: (0, 0)),
                  table((nh, cl, cl)), table((nh, cl, LANES)), table((nh, cl, LANES))],
        out_specs=pl.BlockSpec((1, cl, v_w), lambda bi, c: (bi, c, 0)),
        out_shape=jax.ShapeDtypeStruct((b, s, v_w), BF16),
        scratch_shapes=[pltpu.VMEM((nh, dk, dv), F32)],
        compiler_params=_cparams(("parallel", "arbitrary")),
        name="retention",
    )(chunk_decay, po3, po3, po3, po3, cos, sin, norm_w.reshape(1, v_w).astype(F32), dm, xi, zeta)


def _router_kernel(x_ref, g_ref, wr_ref, xn_ref, route_ref, cnt_ref, carry_sc):
    @pl.when(pl.program_id(0) == 0)
    def _():
        carry_sc[...] = jnp.zeros(carry_sc.shape, F32)

    x = x_ref[...]
    t = x.shape[0]
    y = x * lax.rsqrt(jnp.mean(x * x, axis=-1, keepdims=True) + RMS_EPS) * g_ref[...]
    xn_ref[...] = y.astype(BF16)
    w = wr_ref[...]
    y_hi = y.astype(BF16)
    y_lo = (y - y_hi.astype(F32)).astype(BF16)
    w_hi = w.astype(BF16)
    w_lo = (w - w_hi.astype(F32)).astype(BF16)
    logits = (jnp.dot(y_hi, w_hi, preferred_element_type=F32) + jnp.dot(y_lo, w_hi, preferred_element_type=F32)
              + jnp.dot(y_hi, w_lo, preferred_element_type=F32))
    lane = lax.broadcasted_iota(I32, logits.shape, 1)
    lg = jnp.where(lane < N_EXPERTS, logits, LOWEST)
    v1 = jnp.max(lg, axis=-1, keepdims=True)
    i1 = jnp.min(jnp.where(lg == v1, lane, LANES), axis=-1, keepdims=True)
    lg2 = jnp.where(lane == i1, LOWEST, lg)
    v2 = jnp.max(lg2, axis=-1, keepdims=True)
    i2 = jnp.min(jnp.where(lg2 == v2, lane, LANES), axis=-1, keepdims=True)
    e2 = jnp.exp(v2 - v1)
    g1 = 1.0 / (1.0 + e2)
    g2 = e2 / (1.0 + e2)

    chosen = jnp.where((lane == i1) | (lane == i2), 1.0, 0.0)
    r_i = lax.broadcasted_iota(I32, (t, t), 0)
    c_i = lax.broadcasted_iota(I32, (t, t), 1)
    tri = jnp.where(c_i <= r_i, 1.0, 0.0).astype(BF16)
    seen = jnp.dot(tri, chosen.astype(BF16), preferred_element_type=F32) + carry_sc[...]
    rank1 = jnp.sum(jnp.where(lane == i1, seen, 0.0), axis=-1, keepdims=True) - 1.0
    rank2 = jnp.sum(jnp.where(lane == i2, seen, 0.0), axis=-1, keepdims=True) - 1.0
    total = seen[t - 1:t, :]
    carry_sc[...] = total
    cnt_ref[...] = jnp.broadcast_to(total, cnt_ref.shape)

    route = jnp.where(lane == 0, i1.astype(F32), 0.0)
    route = jnp.where(lane == 1, i2.astype(F32), route)
    route = jnp.where(lane == 2, g1, route)
    route = jnp.where(lane == 3, g2, route)
    route = jnp.where(lane == 4, rank1, route)
    route = jnp.where(lane == 5, rank2, route)
    route_ref[...] = route


def moe_route(x, g, w_router, tm=256):
    m, d = x.shape
    tm = _tile(m, tm, 16)
    wr = jnp.zeros((d, LANES), F32).at[:, :N_EXPERTS].set(w_router.astype(F32))
    return pl.pallas_call(
        _router_kernel,
        grid=(m // tm,),
        in_specs=[pl.BlockSpec((tm, d), lambda i: (i, 0)),
                  pl.BlockSpec((1, d), lambda i: (0, 0)),
                  pl.BlockSpec((d, LANES), lambda i: (0, 0))],
        out_specs=[pl.BlockSpec((tm, d), lambda i: (i, 0)),
                   pl.BlockSpec((tm, LANES), lambda i: (i, 0)),
                   pl.BlockSpec((8, LANES), lambda i: (0, 0))],
        out_shape=[jax.ShapeDtypeStruct((m, d), BF16),
                   jax.ShapeDtypeStruct((m, LANES), F32),
                   jax.ShapeDtypeStruct((8, LANES), F32)],
        scratch_shapes=[pltpu.VMEM((1, LANES), F32)],
        compiler_params=_cparams(("arbitrary",)),
        name="moe_route",
    )(x, g.reshape(1, d).astype(F32), wr)


def _dispatch_kernel(d1_ref, d2_ref, last_ref, x_ref, o_ref, zero_sc, sem, zsem, *, rows, tm, n_experts):
    i = pl.program_id(0)

    @pl.when(i == 0)
    def _():
        zero_sc[...] = jnp.zeros(zero_sc.shape, zero_sc.dtype)
        fills = [pltpu.make_async_copy(zero_sc, o_ref.at[pl.ds(last_ref[e], tm)], zsem) for e in range(n_experts)]
        for c in fills:
            c.start()
        for c in fills:
            c.wait()

        def clear_unused(blk, carry):
            c = pltpu.make_async_copy(zero_sc, o_ref.at[pl.ds(pl.multiple_of(blk * tm, tm), tm)], zsem)
            c.start()
            c.wait()
            return carry

        lax.fori_loop(last_ref[n_experts], o_ref.shape[0] // tm, clear_unused, 0)

    base = i * rows

    def issue(r8, carry):
        for u in range(8):
            r = r8 * 8 + u
            pltpu.make_async_copy(x_ref.at[r], o_ref.at[d1_ref[base + r]], sem).start(priority=0)
            pltpu.make_async_copy(x_ref.at[r], o_ref.at[d2_ref[base + r]], sem).start(priority=1)
        return carry

    lax.fori_loop(0, rows // 8, issue, 0)
    for _ in range(2):
        pltpu.make_async_copy(x_ref, o_ref.at[pl.ds(0, rows)], sem).wait()


def moe_dispatch(xn, dest1, dest2, last_block_row, n_rows, tm, rows=1024):
    m, d = xn.shape
    rows = _tile(m, rows, 8)
    slab = (d // LANES, LANES)
    out = pl.pallas_call(
        functools.partial(_dispatch_kernel, rows=rows, tm=tm, n_experts=last_block_row.shape[0] - 1),
        grid_spec=pltpu.PrefetchScalarGridSpec(
            num_scalar_prefetch=3,
            grid=(m // rows,),
            in_specs=[pl.BlockSpec((rows,) + slab, lambda i, a, b, c: (i, 0, 0))],
            out_specs=pl.BlockSpec(memory_space=pl.ANY),
            scratch_shapes=[pltpu.VMEM((tm,) + slab, xn.dtype), pltpu.SemaphoreType.DMA(()),
                            pltpu.SemaphoreType.DMA(())],
        ),
        out_shape=jax.ShapeDtypeStruct((n_rows,) + slab, xn.dtype),
        compiler_params=_cparams(("arbitrary",)),
        name="moe_dispatch",
    )(dest1, dest2, last_block_row, xn.reshape((m,) + slab))
    return out.reshape(n_rows, d)


def _combine_kernel(d1_ref, d2_ref, x_ref, route_ref, yb_ref, o_ref, buf1, buf2, sem, *, rows):
    i = pl.program_id(0)
    slot = i % 2

    def issue(step, to_slot):
        base = step * rows

        def rows8(r8, carry):
            r0 = pl.multiple_of(r8 * 8, 8)
            for u in range(8):
                pltpu.make_async_copy(yb_ref.at[pl.ds(d1_ref[base + r0 + u], 1)],
                                      buf1.at[to_slot, pl.ds(r0 + u, 1)], sem.at[to_slot, 0]).start(priority=0)
                pltpu.make_async_copy(yb_ref.at[pl.ds(d2_ref[base + r0 + u], 1)],
                                      buf2.at[to_slot, pl.ds(r0 + u, 1)], sem.at[to_slot, 1]).start(priority=1)
            return carry

        lax.fori_loop(0, rows // 8, rows8, 0)

    @pl.when(i == 0)
    def _():
        issue(0, 0)

    @pl.when(i + 1 < pl.num_programs(0))
    def _():
        issue(i + 1, 1 - slot)

    pltpu.make_async_copy(yb_ref.at[pl.ds(0, rows)], buf1.at[slot], sem.at[slot, 0]).wait()
    pltpu.make_async_copy(yb_ref.at[pl.ds(0, rows)], buf2.at[slot], sem.at[slot, 1]).wait()
    route = route_ref[...]
    o_ref[...] = x_ref[...] + (buf1[slot] * route[:, 2:3] + buf2[slot] * route[:, 3:4])


def moe_combine(x, route, yb, dest1, dest2, rows=256):
    m, d = x.shape
    rows = _tile(m, rows, 8)
    return pl.pallas_call(
        functools.partial(_combine_kernel, rows=rows),
        grid_spec=pltpu.PrefetchScalarGridSpec(
            num_scalar_prefetch=2,
            grid=(m // rows,),
            in_specs=[pl.BlockSpec((rows, d), lambda i, a, b: (i, 0)),
                      pl.BlockSpec((rows, LANES), lambda i, a, b: (i, 0)),
                      pl.BlockSpec(memory_space=pl.ANY)],
            out_specs=pl.BlockSpec((rows, d), lambda i, a, b: (i, 0)),
            scratch_shapes=[pltpu.VMEM((2, rows, d), F32), pltpu.VMEM((2, rows, d), F32),
                            pltpu.SemaphoreType.DMA((2, 2))],
        ),
        out_shape=jax.ShapeDtypeStruct((m, d), F32),
        compiler_params=_cparams(("arbitrary",)),
        name="moe_combine",
    )(dest1, dest2, x, route, yb)


def moe_layer(x, norm_g, w_router, w_gate, w_up, w_down):
    m, d = x.shape
    tm = MOE_TM
    xn, route, cnt = moe_route(x, norm_g, w_router)
    expert = route[:, 0:2].astype(I32)
    rank = route[:, 4:6].astype(I32)
    counts = cnt[0, :N_EXPERTS].astype(I32)
    padded = (counts + tm - 1) // tm * tm
    pend = jnp.cumsum(padded)
    pstart = pend - padded
    dest = pstart[expert] + rank
    n_blk = -(-(2 * m) // tm) + N_EXPERTS
    n_rows = n_blk * tm
    nused = (pend[-1] // tm).astype(I32)
    last_block_row = jnp.concatenate([jnp.maximum(pend - tm, 0).astype(I32), nused.reshape(1)])
    blk = jnp.minimum(jnp.arange(n_blk, dtype=I32), nused - 1)
    blk_e = jnp.minimum(jnp.sum(pend[None, :] <= (blk * tm)[:, None], axis=1), N_EXPERTS - 1).astype(I32)
    nu = nused.reshape(1)

    xb = moe_dispatch(xn, dest[:, 0], dest[:, 1], last_block_row, n_rows, tm)
    tf = _tile(w_gate.shape[2], 512, LANES)
    hidden = gmm_swiglu(xb, w_gate, w_up, blk_e, nu, tm=tm, tn=tf)
    yb = gmm(hidden, w_down, blk_e, nu, tm=tm, tn=_tile(d, 1024, LANES), out_dtype=F32)
    return moe_combine(x, route, yb, dest[:, 0], dest[:, 1])


def even_layer(x2, b, s, pos, w_norm, w_in, q_gain, k_gain, w_cmp_k, w_cmp_v, pe_cmp, conv_w, f_bias, m_norm,
               w_out, w_norm_ffn, w_gate, w_up, w_down):
    n, dm = x2.shape
    g, d = NSA_GROUPS, HEAD_DIM
    o_gate = NSA_SLAB
    o_qb = o_gate + NSA_HEADS * 3
    o_if = o_qb + 3 * MLSTM_WIDTH
    o_ob = o_if + 2 * MLSTM_HEADS
    w_ml = jnp.concatenate([w_in[:, o_qb:o_if], w_in[:, o_ob:o_ob + MLSTM_WIDTH]], axis=1)
    per_g = NSA_HPG * 3
    zeros = functools.partial(jnp.zeros, dtype=w_in.dtype)
    w_small = jnp.concatenate(
        [w_in[:, o_gate:o_gate + per_g], zeros((dm, GATE_I_LANE - per_g)),
         w_in[:, o_if:o_ob], zeros((dm, LANES - GATE_I_LANE - 2 * MLSTM_HEADS)),
         w_in[:, o_gate + per_g:o_qb], zeros((dm, LANES - per_g))], axis=1)

    h = rmsnorm_rows(x2, w_norm)
    p3 = mm(h, w_in[:, :o_gate], out_dtype=BF16, tn=NSA_SLAB // 2).reshape(b, s, NSA_SLAB)
    pm3 = mm(h, w_ml, out_dtype=BF16, tn=1024).reshape(b, s, 4 * MLSTM_WIDTH)
    gates3 = mm(h, w_small, out_dtype=F32, tn=2 * LANES).reshape(b, s, 2 * LANES)

    cos_n, sin_n = trig_tables(pos, _nsa_inv_lane())
    q_t, k_n, v_t = nsa_prep(p3, cos_n.reshape(b, s, LANES), sin_n.reshape(b, s, LANES), q_gain, k_gain)
    nh = s // CMP_STRIDE
    cmp_pos = jnp.concatenate([pos[:, CMP_BLOCK - 1::CMP_STRIDE][:, :nh - 1], pos[:, -1:]], axis=1)
    cos_c, sin_c = trig_tables(cmp_pos, _nsa_inv_lane())

    def cmp_blocks(col0):
        tok = p3[:, :, col0:col0 + NSA_KV_WIDTH].reshape(b, nh, CMP_STRIDE, g, d)
        halves = tok.transpose(0, 3, 1, 2, 4).reshape(b, g, nh, CMP_STRIDE * d)
        nxt = jnp.concatenate([halves[:, :, 1:], jnp.zeros_like(halves[:, :, :1])], axis=2)
        return jnp.concatenate([halves, nxt], axis=-1).reshape(b * g * nh, CMP_BLOCK * d)

    tabs = (cos_c.reshape(b, nh, LANES), sin_c.reshape(b, nh, LANES))
    kc = compress(cmp_blocks(NSA_WIDTH), pe_cmp, w_cmp_k, k_gain, *tabs, is_key=True, rows_per_seq=nh, groups=g)
    vc = compress(cmp_blocks(NSA_WIDTH + NSA_KV_WIDTH), pe_cmp, w_cmp_v, k_gain, *tabs, is_key=False,
                  rows_per_seq=nh, groups=g)
    o_nsa = nsa_attention(q_t, k_n, v_t, kc, vc, gates3)

    qk = conv_silu(pm3, conv_w, 0, 2 * MLSTM_WIDTH)
    h_b = mlstm(qk, pm3, gates3, f_bias, m_norm, 2 * MLSTM_WIDTH, 3 * MLSTM_WIDTH)

    mixed = jnp.concatenate([o_nsa, h_b], axis=-1).reshape(n, NSA_WIDTH + MLSTM_WIDTH)
    x2 = mm(mixed, w_out, out_dtype=F32, residual=x2, tm=1024, tn=1024)
    hf = rmsnorm_rows(x2, w_norm_ffn)
    hidden = mm_swiglu(hf, w_gate, w_up, tm=1024)
    return mm(hidden, w_down, out_dtype=F32, residual=x2, tm=512, tn=1024)


def odd_layer(x2, b, s, pos, w_norm, w_in, r_norm, w_out, w_norm_ffn, w_router, e_gate, e_up, e_down):
    n, dm = x2.shape
    h = rmsnorm_rows(x2, w_norm)
    po3 = mm(h, w_in, out_dtype=BF16, tn=1024).reshape(b, s, -1)
    cos_r, sin_r = trig_tables(pos, _ret_inv_lane())
    y = retention(po3, cos_r.reshape(b, s, LANES), sin_r.reshape(b, s, LANES), r_norm)
    x2 = mm(y.reshape(n, RET_V_WIDTH), w_out, out_dtype=F32, residual=x2, tm=512, tn=1024)
    return moe_layer(x2, w_norm_ffn, w_router, e_gate, e_up, e_down)


def kernel(x, positions, norm_mix_even, w_in_even, nsa_q_gain, nsa_k_gain, w_cmp_k, w_cmp_v, pe_cmp, mlstm_conv, mlstm_f_bias, mlstm_norm, w_out_even, norm_ffn_even, ffn_gate, ffn_up, ffn_down, norm_mix_odd, w_in_odd, ret_norm, w_out_odd, norm_ffn_odd, w_router, exp_gate, exp_up, exp_down):
    b, s, dm = x.shape
    depth = norm_mix_even.shape[0] + norm_mix_odd.shape[0]
    x2 = x.reshape(b * s, dm)
    for layer in range(depth):
        j = layer // 2
        if layer % 2 == 0:
            x2 = even_layer(x2, b, s, positions, norm_mix_even[j], w_in_even[j], nsa_q_gain[j], nsa_k_gain[j],
                            w_cmp_k[j], w_cmp_v[j], pe_cmp[j], mlstm_conv[j], mlstm_f_bias[j], mlstm_norm[j],
                            w_out_even[j], norm_ffn_even[j], ffn_gate[j], ffn_up[j], ffn_down[j])
        else:
            x2 = odd_layer(x2, b, s, positions, norm_mix_odd[j], w_in_odd[j], ret_norm[j], w_out_odd[j],
                           norm_ffn_odd[j], w_router[j], exp_gate[j], exp_up[j], exp_down[j])
    return x2.reshape(b, s, dm)
```

```python
import functools

import numpy as np
import jax
import jax.numpy as jnp
from jax import lax
from jax.experimental import pallas as pl
from jax.experimental.pallas import tpu as pltpu

F32 = jnp.float32
BF16 = jnp.bfloat16
I32 = jnp.int32
HIGHEST = lax.Precision.HIGHEST

HEAD_DIM = 128
NSA_HEADS = 8
NSA_GROUPS = 2
NSA_HPG = NSA_HEADS // NSA_GROUPS
NSA_WIDTH = NSA_HEADS * HEAD_DIM
NSA_KV_WIDTH = NSA_GROUPS * HEAD_DIM
NSA_SLAB = NSA_WIDTH + 6 * NSA_KV_WIDTH
CMP_BLOCK = 32
CMP_STRIDE = 16
SEL_BLOCK = 64
SEL_TOPN = 16
WINDOW = 512
ROPE_DIM = HEAD_DIM // 4
ROPE_THETA = 500000.0
SEL_FORCE = 1.0e6
NEG = -1.0e30
LOWEST = -3.0e38
MLSTM_HEADS = 4
MLSTM_DIM = 256
MLSTM_WIDTH = MLSTM_HEADS * MLSTM_DIM
CONV_WIDTH = 4
RET_HEADS = 8
RET_QK_DIM = 256
RET_V_DIM = 512
RET_QK_WIDTH = RET_HEADS * RET_QK_DIM
RET_V_WIDTH = RET_HEADS * RET_V_DIM
RET_ROPE_THETA = 10000.0
N_EXPERTS = 8
RMS_EPS = 1e-6

LANES = 128
BF16_SUBLANES = 16
V7X_VMEM_BYTES = 64 * 1024 * 1024
VMEM_LIMIT = V7X_VMEM_BYTES - 8 * 1024 * 1024

CHUNK = 256
NSA_TQ = 128
LOG2_E = 1.4426950408889634
V_ROWS = HEAD_DIM + BF16_SUBLANES
NSA_KT = 256
SOFTMAX_ROWS = 64
MOE_TM = 512
CAST_ROWS = 512
GATE_I_LANE = 16
GATE_F_LANE = 20


def _cparams(sem, vmem=VMEM_LIMIT):
    return pltpu.CompilerParams(dimension_semantics=sem, vmem_limit_bytes=vmem)


def _tile(n, target, quantum):
    if n <= target:
        return n
    t = (target // quantum) * quantum
    while t > quantum and n % t:
        t -= quantum
    assert n % t == 0, (n, target, quantum)
    return t


def _sigmoid(x):
    return 1.0 / (1.0 + jnp.exp(-x))


def _dot_nt(a, b):
    return lax.dot_general(a, b, (((1,), (1,)), ((), ())), preferred_element_type=F32)


def _dot_tn(a, b):
    return lax.dot_general(a, b, (((0,), (0,)), ((), ())), preferred_element_type=F32)


def _rmsnorm_kernel(x_ref, g_ref, o_ref):
    x = x_ref[...]
    y = x * lax.rsqrt(jnp.mean(x * x, axis=-1, keepdims=True) + RMS_EPS)
    o_ref[...] = (y * g_ref[...]).astype(o_ref.dtype)


def rmsnorm_rows(x, g, tm=1024):
    m, d = x.shape
    tm = _tile(m, tm, 8)
    return pl.pallas_call(
        _rmsnorm_kernel,
        grid=(m // tm,),
        in_specs=[pl.BlockSpec((tm, d), lambda i: (i, 0)), pl.BlockSpec((1, d), lambda i: (0, 0))],
        out_specs=pl.BlockSpec((tm, d), lambda i: (i, 0)),
        out_shape=jax.ShapeDtypeStruct((m, d), BF16),
        compiler_params=_cparams(("parallel",)),
        name="rmsnorm",
    )(x, g.reshape(1, d).astype(F32))


def _stream_weights(be_ref, nx_ref, w_refs, land_refs, wb_refs, sems, tn):
    j = pl.program_id(0)
    i = pl.program_id(1)

    def copies(e, jj):
        c0 = pl.multiple_of(jj * tn, tn)
        return [pltpu.make_async_copy(w.at[e, :, pl.ds(c0, tn)], land, sems.at[n])
                for n, (w, land) in enumerate(zip(w_refs, land_refs))]

    @pl.when((i == 0) | (be_ref[i] != be_ref[jnp.maximum(i - 1, 0)]))
    def _():
        @pl.when((i == 0) & (j == 0))
        def _():
            for c in copies(be_ref[0], 0):
                c.start()

        for c in copies(be_ref[i], j):
            c.wait()
        rows = int(np.gcd(land_refs[0].shape[0], CAST_ROWS))

        def cast_rows(s, carry):
            r0 = pl.multiple_of(s * rows, rows)
            for land, wb in zip(land_refs, wb_refs):
                wb[pl.ds(r0, rows), :] = land[pl.ds(r0, rows), :].astype(BF16)
            return carry

        lax.fori_loop(0, land_refs[0].shape[0] // rows, cast_rows, 0)
        nxt = nx_ref[i]

        @pl.when(nxt >= 0)
        def _():
            for c in copies(nxt, j):
                c.start()

        @pl.when((nxt < 0) & (j + 1 < pl.num_programs(0)))
        def _():
            for c in copies(be_ref[0], j + 1):
                c.start()


def _next_group_expert(blk_e, n_experts):
    e = jnp.arange(n_experts, dtype=I32)
    present = jnp.any(blk_e[:, None] == e[None, :], axis=0)
    later = jnp.where((e[None, :] > blk_e[:, None]) & present[None, :], e[None, :], n_experts)
    nxt = jnp.min(later, axis=1)
    return jnp.where(nxt == n_experts, -1, nxt).astype(I32)


def _gmm_kernel(be_ref, nx_ref, nu_ref, x_ref, w_ref, *rest, has_res, tn):
    if has_res:
        r_ref, o_ref, land_ref, wb_ref, sems = rest
    else:
        o_ref, land_ref, wb_ref, sems = rest
    i = pl.program_id(1)
    _stream_weights(be_ref, nx_ref, [w_ref], [land_ref], [wb_ref], sems, tn)

    @pl.when(i < nu_ref[0])
    def _():
        acc = jnp.dot(x_ref[...], wb_ref[...], preferred_element_type=F32)
        if has_res:
            acc = r_ref[...] + acc
        o_ref[...] = acc.astype(o_ref.dtype)

    @pl.when(i >= nu_ref[0])
    def _():
        o_ref[...] = jnp.zeros_like(o_ref)


def gmm(x, w, blk_e, nused, *, tm, tn, out_dtype, residual=None):
    m, k = x.shape
    e, k2, n = w.shape
    assert k == k2 and m % tm == 0 and n % tn == 0
    nb = m // tm
    in_specs = [
        pl.BlockSpec((tm, k), lambda j, i, be, nx, nu: (jnp.minimum(i, nu[0] - 1), 0)),
        pl.BlockSpec(memory_space=pl.ANY),
    ]
    args = [x, w]
    if residual is not None:
        in_specs.append(pl.BlockSpec((tm, tn), lambda j, i, be, nx, nu: (i, j)))
        args.append(residual)
    return pl.pallas_call(
        functools.partial(_gmm_kernel, has_res=residual is not None, tn=tn),
        grid_spec=pltpu.PrefetchScalarGridSpec(
            num_scalar_prefetch=3,
            grid=(n // tn, nb),
            in_specs=in_specs,
            out_specs=pl.BlockSpec((tm, tn), lambda j, i, be, nx, nu: (i, j)),
            scratch_shapes=[pltpu.VMEM((k, tn), w.dtype), pltpu.VMEM((k, tn), BF16),
                            pltpu.SemaphoreType.DMA((1,))],
        ),
        out_shape=jax.ShapeDtypeStruct((m, n), out_dtype),
        compiler_params=_cparams(("arbitrary", "arbitrary")),
        name="gmm",
    )(blk_e, _next_group_expert(blk_e, e), nused, *args)


def _gmm_swiglu_kernel(be_ref, nx_ref, nu_ref, x_ref, wg_ref, wu_ref, o_ref, lg_ref, lu_ref, wgb_ref, wub_ref,
                       sems, *, tn):
    i = pl.program_id(1)
    _stream_weights(be_ref, nx_ref, [wg_ref, wu_ref], [lg_ref, lu_ref], [wgb_ref, wub_ref], sems, tn)

    @pl.when(i < nu_ref[0])
    def _():
        x = x_ref[...]
        g = jnp.dot(x, wgb_ref[...], preferred_element_type=F32)
        u = jnp.dot(x, wub_ref[...], preferred_element_type=F32)
        o_ref[...] = (g * _sigmoid(g) * u).astype(o_ref.dtype)

    @pl.when(i >= nu_ref[0])
    def _():
        o_ref[...] = jnp.zeros_like(o_ref)


def gmm_swiglu(x, wg, wu, blk_e, nused, *, tm, tn):
    m, k = x.shape
    e, k2, n = wg.shape
    assert k == k2 and wu.shape == wg.shape and m % tm == 0 and n % tn == 0
    nb = m // tm
    hbm = pl.BlockSpec(memory_space=pl.ANY)
    return pl.pallas_call(
        functools.partial(_gmm_swiglu_kernel, tn=tn),
        grid_spec=pltpu.PrefetchScalarGridSpec(
            num_scalar_prefetch=3,
            grid=(n // tn, nb),
            in_specs=[pl.BlockSpec((tm, k), lambda j, i, be, nx, nu: (jnp.minimum(i, nu[0] - 1), 0)), hbm, hbm],
            out_specs=pl.BlockSpec((tm, tn), lambda j, i, be, nx, nu: (i, j)),
            scratch_shapes=[pltpu.VMEM((k, tn), wg.dtype), pltpu.VMEM((k, tn), wu.dtype),
                            pltpu.VMEM((k, tn), BF16), pltpu.VMEM((k, tn), BF16),
                            pltpu.SemaphoreType.DMA((2,))],
        ),
        out_shape=jax.ShapeDtypeStruct((m, n), BF16),
        compiler_params=_cparams(("arbitrary", "arbitrary")),
        name="gmm_swiglu",
    )(blk_e, _next_group_expert(blk_e, e), nused, x, wg, wu)


def _dense_blocks(m, tm):
    nb = m // tm
    return jnp.zeros((nb,), I32), jnp.full((1,), nb, I32)


def mm(x, w, *, out_dtype, residual=None, tm=2048, tn=512):
    m, k = x.shape
    n = w.shape[1]
    tm = _tile(m, tm, 16)
    tn = _tile(n, tn, LANES)
    be, nu = _dense_blocks(m, tm)
    return gmm(x, w[None], be, nu, tm=tm, tn=tn, out_dtype=out_dtype, residual=residual)


def mm_swiglu(x, wg, wu, *, tm=2048, tn=512):
    m = x.shape[0]
    tm = _tile(m, tm, 16)
    tn = _tile(wg.shape[1], tn, LANES)
    be, nu = _dense_blocks(m, tm)
    return gmm_swiglu(x, wg[None], wu[None], be, nu, tm=tm, tn=tn)


def _proj_res_norm_kernel(xa_ref, xb_ref, w_ref, r_ref, g_ref, o_ref, xn_ref, wb_ref):
    @pl.when(pl.program_id(0) == 0)
    def _():
        rows = int(np.gcd(w_ref.shape[0], CAST_ROWS))

        def cast_rows(s, carry):
            r0 = pl.multiple_of(s * rows, rows)
            wb_ref[pl.ds(r0, rows), :] = w_ref[pl.ds(r0, rows), :].astype(BF16)
            return carry

        lax.fori_loop(0, w_ref.shape[0] // rows, cast_rows, 0)

    ka = xa_ref.shape[1]
    acc = jnp.dot(xa_ref[...], wb_ref[0:ka, :], preferred_element_type=F32)
    acc = acc + jnp.dot(xb_ref[...], wb_ref[ka:, :], preferred_element_type=F32)
    y = r_ref[...] + acc
    o_ref[...] = y
    xn_ref[...] = (y * lax.rsqrt(jnp.mean(y * y, axis=-1, keepdims=True) + RMS_EPS) * g_ref[...]).astype(BF16)


def proj_res_norm(xa, xb, w, residual, g, tm=512):
    m, ka = xa.shape
    kb = xb.shape[1]
    k, n = w.shape
    assert k == ka + kb and ka % BF16_SUBLANES == 0
    tm = _tile(m, tm, 16)
    row = lambda i: (i, 0)
    return pl.pallas_call(
        _proj_res_norm_kernel,
        grid=(m // tm,),
        in_specs=[pl.BlockSpec((tm, ka), row), pl.BlockSpec((tm, kb), row),
                  pl.BlockSpec((k, n), lambda i: (0, 0), pipeline_mode=pl.Buffered(1)),
                  pl.BlockSpec((tm, n), row), pl.BlockSpec((1, n), lambda i: (0, 0))],
        out_specs=[pl.BlockSpec((tm, n), row), pl.BlockSpec((tm, n), row)],
        out_shape=[jax.ShapeDtypeStruct((m, n), F32), jax.ShapeDtypeStruct((m, n), BF16)],
        scratch_shapes=[pltpu.VMEM((k, n), BF16)],
        compiler_params=_cparams(("arbitrary",)),
        name="proj_res_norm",
    )(xa, xb, w, residual, g.reshape(1, n).astype(F32))


def _trig_kernel(pos_ref, inv_ref, cos_ref, sin_ref):
    ang = pos_ref[...] * inv_ref[...]
    cos_ref[...] = jnp.cos(ang)
    sin_ref[...] = jnp.sin(ang)


def trig_tables(pos, inv_lane):
    r = pos.size
    pos_b = jnp.broadcast_to(pos.astype(F32).reshape(r, 1), (r, LANES))
    tr = _tile(r, 2048, 8)
    spec = pl.BlockSpec((tr, LANES), lambda i: (i, 0))
    return pl.pallas_call(
        _trig_kernel,
        grid=(r // tr,),
        in_specs=[spec, pl.BlockSpec((1, LANES), lambda i: (0, 0))],
        out_specs=[spec, spec],
        out_shape=[jax.ShapeDtypeStruct((r, LANES), F32)] * 2,
        compiler_params=_cparams(("parallel",)),
        name="trig_tables",
    )(pos_b, inv_lane.reshape(1, LANES))


def _nsa_inv_lane():
    half = ROPE_DIM // 2
    inv = jnp.power(jnp.float32(ROPE_THETA), -jnp.arange(half, dtype=F32) * (2.0 / ROPE_DIM))
    return jnp.concatenate([inv, inv, jnp.zeros((LANES - ROPE_DIM,), F32)])


def _ret_inv_lane():
    half = RET_QK_DIM // 2
    return jnp.power(jnp.float32(RET_ROPE_THETA), -jnp.arange(half, dtype=F32) * (2.0 / RET_QK_DIM))


def _norm_rope_head(x, gain, cos, sin):
    half = ROPE_DIM // 2
    y = x * lax.rsqrt(jnp.mean(x * x, axis=-1, keepdims=True) + RMS_EPS) * gain
    lane = lax.broadcasted_iota(I32, y.shape, 1)
    from_hi = jnp.where(lane < half, -sin, 0.0)
    from_lo = jnp.where((lane >= half) & (lane < ROPE_DIM), sin, 0.0)
    return (y * cos + pltpu.roll(y, LANES - half, 1) * from_hi + pltpu.roll(y, half, 1) * from_lo)


def _nsa_prep_kernel(p_ref, cos_ref, sin_ref, qg_ref, kg_ref, qt_ref, kn_ref, vt_ref):
    cos = cos_ref[0]
    sin = sin_ref[0]
    scale = HEAD_DIM ** -0.5 * LOG2_E
    d = HEAD_DIM
    g = NSA_GROUPS
    for hd in range(NSA_HEADS):
        q = _norm_rope_head(p_ref[0, :, hd * d:(hd + 1) * d].astype(F32), qg_ref[...], cos, sin)
        qt_ref[0, hd * d:(hd + 1) * d, :] = (q * scale).T.astype(BF16)
    for n, slab in enumerate((2, 4)):
        for gi in range(g):
            off = NSA_WIDTH + slab * NSA_KV_WIDTH + gi * d
            k = _norm_rope_head(p_ref[0, :, off:off + d].astype(F32), kg_ref[...], cos, sin)
            kn_ref[0, :, (n * g + gi) * d:(n * g + gi + 1) * d] = k.astype(BF16)
    for n, slab in enumerate((3, 5)):
        for gi in range(g):
            off = NSA_WIDTH + slab * NSA_KV_WIDTH + gi * d
            v = p_ref[0, :, off:off + d].astype(F32)
            r0 = (n * g + gi) * V_ROWS
            vt_ref[0, r0:r0 + d, :] = v.T.astype(BF16)
            vt_ref[0, r0 + d:r0 + V_ROWS, :] = jnp.ones((V_ROWS - d, v.shape[0]), BF16)


def nsa_prep(p3, cos, sin, q_gain, k_gain):
    b, s, _ = p3.shape
    t = _tile(s, 256, LANES)
    d = HEAD_DIM
    tab = pl.BlockSpec((1, t, LANES), lambda bi, i: (bi, i, 0))
    gain = pl.BlockSpec((1, d), lambda bi, i: (0, 0))
    kv = 2 * NSA_KV_WIDTH
    vr = 2 * NSA_GROUPS * V_ROWS
    return pl.pallas_call(
        _nsa_prep_kernel,
        grid=(b, s // t),
        in_specs=[pl.BlockSpec((1, t, NSA_SLAB), lambda bi, i: (bi, i, 0)), tab, tab, gain, gain],
        out_specs=[pl.BlockSpec((1, NSA_WIDTH, t), lambda bi, i: (bi, 0, i)),
                   pl.BlockSpec((1, t, kv), lambda bi, i: (bi, i, 0)),
                   pl.BlockSpec((1, vr, t), lambda bi, i: (bi, 0, i))],
        out_shape=[jax.ShapeDtypeStruct((b, NSA_WIDTH, s), BF16),
                   jax.ShapeDtypeStruct((b, s, kv), BF16),
                   jax.ShapeDtypeStruct((b, vr, s), BF16)],
        compiler_params=_cparams(("parallel", "parallel")),
        name="nsa_prep",
    )(p3, cos, sin, q_gain.reshape(1, d), k_gain.reshape(1, d))


def _compress_kernel(blk_ref, pe_ref, w_ref, kg_ref, cos_ref, sin_ref, o_ref, *, is_key):
    a = (blk_ref[...].astype(F32) + pe_ref[...]).astype(BF16)
    y = jnp.dot(a, w_ref[...].astype(BF16), preferred_element_type=F32)
    if is_key:
        o_ref[0] = _norm_rope_head(y, kg_ref[...], cos_ref[0], sin_ref[0]).astype(BF16)
    else:
        o_ref[0] = y.T.astype(BF16)


def compress(blk, pe, w, k_gain, cos_c, sin_c, *, is_key, rows_per_seq, groups):
    r, kdim = blk.shape
    t = rows_per_seq
    d = HEAD_DIM
    tab = pl.BlockSpec((1, t, LANES), lambda i: (i // groups, 0, 0))
    out_blk = (1, t, d) if is_key else (1, d, t)
    return pl.pallas_call(
        functools.partial(_compress_kernel, is_key=is_key),
        grid=(r // t,),
        in_specs=[pl.BlockSpec((t, kdim), lambda i: (i, 0)),
                  pl.BlockSpec((1, kdim), lambda i: (0, 0)),
                  pl.BlockSpec((kdim, d), lambda i: (0, 0)),
                  pl.BlockSpec((1, d), lambda i: (0, 0)),
                  tab, tab],
        out_specs=pl.BlockSpec(out_blk, lambda i: (i, 0, 0)),
        out_shape=jax.ShapeDtypeStruct((r // t,) + out_blk[1:], BF16),
        compiler_params=_cparams(("parallel",)),
        name="nsa_compress",
    )(blk, pe.reshape(1, kdim), w, k_gain.reshape(1, d), cos_c, sin_c)


def _nsa_attn_kernel(qt_ref, kc_ref, vct_ref, ks_ref, kw_ref, vst_ref, vwt_ref, et_ref, ovt_ref, gate_ref,
                     o_ref, acc_sc, val_sc, sa_sc, sb_sc, pa_sc, pb_sc, sc_sc, pc_sc, pg_sc, sw_sc, pw_sc, ow_sc,
                     *, tq, kt, wk, ns, n_top):
    i = pl.program_id(2)
    t0 = i * tq
    cols = NSA_HPG * tq
    d = HEAD_DIM
    q_t = jnp.concatenate([qt_ref[0, p * d:(p + 1) * d, :] for p in range(NSA_HPG)], axis=1)
    t_lane = t0 + (lax.broadcasted_iota(I32, (1, cols), 1) & (tq - 1))

    ncp = kc_ref.shape[1]
    rs = SOFTMAX_ROWS
    sc_sc[...] = jnp.dot(kc_ref[0], q_t, preferred_element_type=F32)
    w0 = pl.multiple_of(jnp.maximum(t0 + tq - wk, 0), LANES)
    sw_sc[...] = jnp.dot(kw_ref[0, pl.ds(w0, wk), :], q_t, preferred_element_type=F32)

    def cmask(r0):
        c_end = (r0 + lax.broadcasted_iota(I32, (rs, cols), 0)) * CMP_STRIDE + (CMP_BLOCK - 1)
        return c_end <= t_lane

    m_c = jnp.full((1, cols), NEG, F32)
    for r0 in range(0, ncp, rs):
        m_c = jnp.maximum(m_c, jnp.max(jnp.where(cmask(r0), sc_sc[r0:r0 + rs, :], NEG), axis=0, keepdims=True))
    den_c = jnp.zeros((1, cols), F32)
    for r0 in range(0, ncp, rs):
        e = jnp.where(cmask(r0), jnp.exp2(sc_sc[r0:r0 + rs, :] - m_c), 0.0)
        sc_sc[r0:r0 + rs, :] = e
        den_c = den_c + jnp.sum(e, axis=0, keepdims=True)
    inv_c = 1.0 / jnp.maximum(den_c, 1e-30)
    for r0 in range(0, ncp, rs):
        p = sc_sc[r0:r0 + rs, :] * inv_c
        pc_sc[r0:r0 + rs, :] = p.astype(BF16)
        p_grp = p[:, 0:tq]
        for hp in range(1, NSA_HPG):
            p_grp = p_grp + p[:, hp * tq:(hp + 1) * tq]
        pg_sc[r0:r0 + rs, :] = p_grp
    o_cmp = jnp.dot(vct_ref[0], pc_sc[...], preferred_element_type=F32)

    m_w = jnp.full((1, cols), NEG, F32)
    for r0 in range(0, wk, rs):
        kpos = w0 + r0 + lax.broadcasted_iota(I32, (rs, cols), 0)
        wmask = (kpos <= t_lane) & (kpos > t_lane - WINDOW)
        s = jnp.where(wmask, sw_sc[r0:r0 + rs, :], NEG)
        sw_sc[r0:r0 + rs, :] = s
        m_w = jnp.maximum(m_w, jnp.max(s, axis=0, keepdims=True))
    for r0 in range(0, wk, rs):
        pw_sc[r0:r0 + rs, :] = jnp.exp2(sw_sc[r0:r0 + rs, :] - m_w).astype(BF16)
    o_win = jnp.dot(vwt_ref[0, :, pl.ds(w0, wk)], pw_sc[...], preferred_element_type=F32)
    ow_sc[...] = o_win[:d] * (1.0 / o_win[d:d + 1])

    imp = jnp.dot(ovt_ref[...], pg_sc[...], precision=HIGHEST, preferred_element_type=F32)
    nsp = val_sc.shape[0]
    jb = lax.broadcasted_iota(I32, (nsp, tq), 0)
    cur = (t0 + lax.broadcasted_iota(I32, (1, tq), 1)) // SEL_BLOCK
    forced = (jb == 0) | (jb == cur) | (jb == cur - 1)
    val = jnp.where(jb <= cur, jnp.where(forced, SEL_FORCE, imp[:nsp]), NEG)
    val = jnp.where(jb < ns, val, LOWEST)
    val_sc[...] = val
    beaten = jnp.zeros((nsp, tq), F32)
    for j2 in range(ns):
        r = val_sc[j2:j2 + 1, :]
        ge = jnp.where(r >= val, 1.0, 0.0)
        gt = jnp.where(r > val, 1.0, 0.0)
        beaten = beaten + jnp.where(jb > j2, ge, gt)
    past = jb < t0 // SEL_BLOCK
    bias = jnp.where(past & (beaten < n_top), 0.0, jnp.where(jb < ns, NEG, 0.0)).astype(BF16)
    bias = jnp.concatenate([bias, jnp.zeros((LANES - nsp, tq), BF16)], axis=0)
    q_aug = jnp.concatenate([q_t, jnp.concatenate([bias] * NSA_HPG, axis=1)], axis=0)

    d0 = pl.multiple_of(t0, tq)
    s_d = jnp.dot(ks_ref[0, pl.ds(d0, tq), :], q_t, preferred_element_type=F32)
    s_d = jnp.where(d0 + lax.broadcasted_iota(I32, (tq, cols), 0) <= t_lane, s_d, NEG)
    m_d = jnp.max(s_d, axis=0, keepdims=True)
    p_d = jnp.exp2(s_d - m_d)
    acc_sc[...] = jnp.dot(vst_ref[0, :, pl.ds(d0, tq)], p_d.astype(BF16), preferred_element_type=F32)

    n_pairs = (t0 + 2 * kt - 1) // (2 * kt)
    last_a = jnp.maximum(n_pairs - 1, 0) * (2 * kt)

    def scores(k0):
        k0 = pl.multiple_of(k0, kt)
        k_aug = jnp.concatenate([ks_ref[0, pl.ds(k0, kt), :], et_ref[pl.ds(k0, kt), :]], axis=1)
        return jnp.dot(k_aug, q_aug, preferred_element_type=F32)

    def values(p_ref, k0):
        k0 = pl.multiple_of(k0, kt)
        return jnp.dot(vst_ref[0, :, pl.ds(k0, kt)], p_ref[...], preferred_element_type=F32)

    def softmax_update(s_ref, p_ref, m_prev):
        m_new = jnp.maximum(m_prev, jnp.max(s_ref[...], axis=0, keepdims=True))
        for r0 in range(0, kt, SOFTMAX_ROWS):
            p_ref[r0:r0 + SOFTMAX_ROWS, :] = jnp.exp2(s_ref[r0:r0 + SOFTMAX_ROWS, :] - m_new).astype(BF16)
        return m_new, jnp.exp2(m_prev - m_new)

    def sel_step(j, carry):
        m_run, alpha_b = carry
        k0 = j * (2 * kt)
        sb_sc[...] = scores(k0 + kt)
        acc_sc[...] = alpha_b * acc_sc[...] + values(pb_sc, jnp.maximum(k0 - kt, 0))
        m_run, alpha_a = softmax_update(sa_sc, pa_sc, m_run)
        sa_sc[...] = scores(jnp.minimum(k0 + 2 * kt, last_a))
        acc_sc[...] = alpha_a * acc_sc[...] + values(pa_sc, k0)
        return softmax_update(sb_sc, pb_sc, m_run)

    sa_sc[...] = scores(0)
    pb_sc[...] = jnp.zeros(pb_sc.shape, BF16)
    _, alpha_last = lax.fori_loop(0, n_pairs, sel_step, (m_d, jnp.ones((1, cols), F32)))
    acc = alpha_last * acc_sc[...] + values(pb_sc, last_a + kt)
    o_sel = acc[:d] * (1.0 / acc[d:d + 1])

    g_t = _sigmoid(gate_ref[0]).T
    for p in range(NSA_HPG):
        sl = slice(p * tq, (p + 1) * tq)
        o_t = (g_t[3 * p:3 * p + 1] * o_cmp[:, sl] + g_t[3 * p + 1:3 * p + 2] * o_sel[:, sl]
               + g_t[3 * p + 2:3 * p + 3] * ow_sc[:, sl])
        o_ref[0, :, p * d:(p + 1) * d] = o_t.T.astype(BF16)


def nsa_attention(q_t, k_n, v_t, kc, vc_t, gates3):
    b, _, s = q_t.shape
    g, d = NSA_GROUPS, HEAD_DIM
    ncp = kc.shape[1]
    tq = NSA_TQ
    kt = min(NSA_KT, s)
    wk = WINDOW + tq
    ns = s // SEL_BLOCK
    assert s % (2 * kt) == 0 and s >= wk and ns <= LANES and tq == LANES
    assert kt % SOFTMAX_ROWS == 0 and ncp % SOFTMAX_ROWS == 0 and wk % SOFTMAX_ROWS == 0
    n_top = min(SEL_TOPN, ns)
    cols = NSA_HPG * tq

    key_blk = np.arange(s) // SEL_BLOCK
    e_t = jnp.asarray(key_blk[:, None] == np.arange(LANES)[None, :], dtype=BF16)
    c_start = np.arange(ncp) * CMP_STRIDE
    j_start = np.arange(LANES) * SEL_BLOCK
    overlap = ((c_start[:, None] < j_start[None, :] + SEL_BLOCK) & (c_start[:, None] + CMP_BLOCK > j_start[None, :])
               & (np.arange(ncp)[:, None] < s // CMP_STRIDE - 1) & (np.arange(LANES)[None, :] < ns))
    overlap_t = jnp.asarray(overlap.T.astype(np.float32))

    q_rows = NSA_HPG * d
    return pl.pallas_call(
        functools.partial(_nsa_attn_kernel, tq=tq, kt=kt, wk=wk, ns=ns, n_top=n_top),
        grid=(b, g, s // tq),
        in_specs=[pl.BlockSpec((1, q_rows, tq), lambda bi, gi, i: (bi, gi, i)),
                  pl.BlockSpec((1, ncp, d), lambda bi, gi, i: (bi * g + gi, 0, 0)),
                  pl.BlockSpec((1, d, ncp), lambda bi, gi, i: (bi * g + gi, 0, 0)),
                  pl.BlockSpec((1, s, d), lambda bi, gi, i: (bi, 0, gi)),
                  pl.BlockSpec((1, s, d), lambda bi, gi, i: (bi, 0, g + gi)),
                  pl.BlockSpec((1, V_ROWS, s), lambda bi, gi, i: (bi, gi, 0)),
                  pl.BlockSpec((1, V_ROWS, s), lambda bi, gi, i: (bi, g + gi, 0)),
                  pl.BlockSpec((s, LANES), lambda bi, gi, i: (0, 0)),
                  pl.BlockSpec((LANES, ncp), lambda bi, gi, i: (0, 0)),
                  pl.BlockSpec((1, tq, LANES), lambda bi, gi, i: (bi, i, gi))],
        out_specs=pl.BlockSpec((1, tq, q_rows), lambda bi, gi, i: (bi, i, gi)),
        out_shape=jax.ShapeDtypeStruct((b, s, NSA_WIDTH), BF16),
        scratch_shapes=[pltpu.VMEM((V_ROWS, cols), F32), pltpu.VMEM((-(-ns // 8) * 8, tq), F32),
                        pltpu.VMEM((kt, cols), F32), pltpu.VMEM((kt, cols), F32),
                        pltpu.VMEM((kt, cols), BF16), pltpu.VMEM((kt, cols), BF16),
                        pltpu.VMEM((ncp, cols), F32), pltpu.VMEM((ncp, cols), BF16), pltpu.VMEM((ncp, tq), F32),
                        pltpu.VMEM((wk, cols), F32), pltpu.VMEM((wk, cols), BF16), pltpu.VMEM((d, cols), F32)],
        compiler_params=_cparams(("parallel", "parallel", "arbitrary")),
        name="nsa_attention",
    )(q_t, kc, vc_t, k_n, k_n, v_t, v_t, e_t, overlap_t, gates3)


def _conv_kernel(cur_ref, prev_ref, w_ref, o_ref, *, tc):
    i = pl.program_id(1)
    cur = cur_ref[0].astype(F32)
    prev = jnp.where(i > 0, prev_ref[0].astype(F32), 0.0)
    pad = prev.shape[0] // 2
    full = jnp.concatenate([prev[pad:], cur], axis=0)
    y = None
    for kk in range(CONV_WIDTH):
        off = pad - (CONV_WIDTH - 1) + kk
        term = w_ref[kk:kk + 1, :] * full[off:off + tc]
        y = term if y is None else y + term
    o_ref[0] = (y * _sigmoid(y)).astype(BF16)


def conv_silu(p3, w, col0, width):
    b, s, _ = p3.shape
    tc = _tile(s, 512, BF16_SUBLANES)
    cw = 1024
    assert col0 % cw == 0 and width % cw == 0
    c0 = col0 // cw
    halo = BF16_SUBLANES
    return pl.pallas_call(
        functools.partial(_conv_kernel, tc=tc),
        grid=(b, s // tc, width // cw),
        in_specs=[pl.BlockSpec((1, tc, cw), lambda bi, i, j: (bi, i, c0 + j)),
                  pl.BlockSpec((1, halo, cw), lambda bi, i, j: (bi, jnp.maximum(i * (tc // halo) - 1, 0), c0 + j)),
                  pl.BlockSpec((CONV_WIDTH, cw), lambda bi, i, j: (0, j))],
        out_specs=pl.BlockSpec((1, tc, cw), lambda bi, i, j: (bi, i, j)),
        out_shape=jax.ShapeDtypeStruct((b, s, width), BF16),
        compiler_params=_cparams(("parallel", "parallel", "parallel")),
        name="mlstm_conv",
    )(p3, p3, w)


def _log_sigmoid(x):
    return jnp.minimum(x, 0.0) - jnp.log1p(jnp.exp(-jnp.abs(x)))


def _mlstm_kernel(fb_ref, q_ref, k_ref, v_ref, ob_ref, gate_ref, nw_ref, o_ref, c_sc, n_sc, m_sc, *, cl):
    @pl.when(pl.program_id(1) == 0)
    def _():
        c_sc[...] = jnp.zeros(c_sc.shape, F32)
        n_sc[...] = jnp.zeros(n_sc.shape, F32)
        m_sc[...] = jnp.zeros(m_sc.shape, F32)

    dh = MLSTM_DIM
    slab = gate_ref[0]
    lane = lax.broadcasted_iota(I32, slab.shape, 1)
    r_i = lax.broadcasted_iota(I32, (cl, cl), 0)
    c_i = lax.broadcasted_iota(I32, (cl, cl), 1)
    eye = r_i == c_i
    tri = c_i <= r_i

    def head(hd, carry):
        off = pl.multiple_of(hd * dh, dh)
        q = q_ref[0, :, pl.ds(off, dh)]
        v = v_ref[0, :, pl.ds(off, dh)]
        ks32 = k_ref[0, :, pl.ds(off, dh)].astype(F32) * (dh ** -0.5)
        ks = ks32.astype(BF16)
        i_col = jnp.sum(jnp.where(lane == GATE_I_LANE + hd, slab, 0.0), axis=-1, keepdims=True)
        f_col = jnp.sum(jnp.where(lane == GATE_F_LANE + hd, slab, 0.0), axis=-1, keepdims=True)
        lf_col = _log_sigmoid(f_col + fb_ref[hd])

        lf_row = jnp.sum(jnp.where(eye, lf_col, 0.0), axis=0, keepdims=True)
        ig_row = jnp.sum(jnp.where(eye, i_col, 0.0), axis=0, keepdims=True)
        a_col = jnp.sum(jnp.where(tri, lf_row, 0.0), axis=1, keepdims=True)
        a_row = jnp.sum(jnp.where(r_i <= c_i, lf_col, 0.0), axis=0, keepdims=True)
        m_prev = m_sc[hd]

        dlog = jnp.where(tri, a_col - a_row + ig_row, NEG)
        inter = a_col + m_prev
        mt = jnp.maximum(inter, jnp.max(dlog, axis=-1, keepdims=True))
        wm = jnp.exp(dlog - mt) * _dot_nt(q, ks)
        e_col = jnp.exp(inter - mt)
        num = e_col * jnp.dot(q, c_sc[hd].astype(BF16), preferred_element_type=F32) \
            + jnp.dot(wm.astype(BF16), v, preferred_element_type=F32)
        qn = jnp.sum(q.astype(F32) * n_sc[hd], axis=-1, keepdims=True)
        den = e_col * qn + jnp.sum(wm, axis=-1, keepdims=True)
        hh = num / jnp.maximum(jnp.abs(den), jnp.exp(-mt))

        a_last = jnp.sum(lf_row, axis=-1, keepdims=True)
        gs = a_last - a_col + i_col
        m_new = jnp.maximum(a_last + m_prev, jnp.max(gs, axis=0, keepdims=True))
        decay = jnp.exp(a_last + m_prev - m_new)
        wk = jnp.exp(gs - m_new) * ks32
        c_sc[hd] = decay * c_sc[hd] + _dot_tn(wk.astype(BF16), v)
        n_sc[hd] = decay * n_sc[hd] + jnp.sum(wk, axis=0, keepdims=True)
        m_sc[hd] = m_new

        y = hh * lax.rsqrt(jnp.mean(hh * hh, axis=-1, keepdims=True) + RMS_EPS) * nw_ref[:, pl.ds(off, dh)]
        o_ref[0, :, pl.ds(off, dh)] = (y * _sigmoid(ob_ref[0, :, pl.ds(off, dh)].astype(F32))).astype(BF16)
        return carry

    lax.fori_loop(0, MLSTM_HEADS, head, 0)


def mlstm(qk, p3, gates3, f_bias, norm_w, v_col0, o_col0):
    b, s, _ = qk.shape
    cl = min(CHUNK, s)
    dh = MLSTM_DIM
    nh = MLSTM_HEADS
    w = nh * dh
    assert v_col0 % w == 0 and o_col0 % w == 0 and s % cl == 0
    vb, ob = v_col0 // w, o_col0 // w
    return pl.pallas_call(
        functools.partial(_mlstm_kernel, cl=cl),
        grid=(b, s // cl),
        in_specs=[pl.BlockSpec(memory_space=pltpu.SMEM),
                  pl.BlockSpec((1, cl, w), lambda bi, c: (bi, c, 0)),
                  pl.BlockSpec((1, cl, w), lambda bi, c: (bi, c, 1)),
                  pl.BlockSpec((1, cl, w), lambda bi, c: (bi, c, vb)),
                  pl.BlockSpec((1, cl, w), lambda bi, c: (bi, c, ob)),
                  pl.BlockSpec((1, cl, LANES), lambda bi, c: (bi, c, 0)),
                  pl.BlockSpec((1, w), lambda bi, c: (0, 0))],
        out_specs=pl.BlockSpec((1, cl, w), lambda bi, c: (bi, c, 0)),
        out_shape=jax.ShapeDtypeStruct((b, s, w), BF16),
        scratch_shapes=[pltpu.VMEM((nh, dh, dh), F32), pltpu.VMEM((nh, 1, dh), F32), pltpu.VMEM((nh, 1, 1), F32)],
        compiler_params=_cparams(("parallel", "arbitrary")),
        name="mlstm",
    )(f_bias.astype(F32), qk, qk, p3, p3, gates3, norm_w.reshape(1, w).astype(F32))


def _ret_kernel(cd_ref, q_ref, k_ref, v_ref, g_ref, cos_ref, sin_ref, nw_ref, dm_ref, xi_ref, zeta_ref,
                o_ref, r_sc, *, cl):
    @pl.when(pl.program_id(1) == 0)
    def _():
        r_sc[...] = jnp.zeros(r_sc.shape, F32)

    cos = cos_ref[0]
    sin = sin_ref[0]
    dk, dv = RET_QK_DIM, RET_V_DIM
    half = dk // 2
    scale = dk ** -0.5

    def rope(x):
        x1, x2 = x[:, :half], x[:, half:]
        return x1 * cos - x2 * sin, x1 * sin + x2 * cos

    def head(hd, carry):
        qo = pl.multiple_of(hd * dk, dk)
        vo = pl.multiple_of(hd * dv, dv)
        q1, q2 = rope(q_ref[0, :, pl.ds(qo, dk)].astype(F32))
        qr = jnp.concatenate([q1, q2], axis=1).astype(BF16)
        k1, k2 = rope(k_ref[0, :, pl.ds(qo, dk)].astype(F32))
        zeta = zeta_ref[hd] * scale
        kr = jnp.concatenate([k1 * scale, k2 * scale], axis=1).astype(BF16)
        kz = jnp.concatenate([k1 * zeta, k2 * zeta], axis=1).astype(BF16)
        v = v_ref[0, :, pl.ds(vo, dv)]

        inner = jnp.dot((_dot_nt(qr, kr) * dm_ref[hd]).astype(BF16), v, preferred_element_type=F32)
        xi = xi_ref[hd]
        cross = jnp.dot(qr, r_sc[hd].astype(BF16), preferred_element_type=F32)
        cross = cross * jnp.concatenate([xi] * (dv // LANES), axis=1)
        r_sc[hd] = cd_ref[hd] * r_sc[hd] + _dot_tn(kz, v)

        y = inner + cross
        y = y * lax.rsqrt(jnp.mean(y * y, axis=-1, keepdims=True) + RMS_EPS) * nw_ref[:, pl.ds(vo, dv)]
        gg = g_ref[0, :, pl.ds(vo, dv)].astype(F32)
        o_ref[0, :, pl.ds(vo, dv)] = (y * (gg * _sigmoid(gg))).astype(BF16)
        return carry

    lax.fori_loop(0, RET_HEADS, head, 0)


def retention(po3, cos, sin, norm_w):
    b, s, _ = po3.shape
    cl = min(CHUNK, s)
    nh, dk, dv = RET_HEADS, RET_QK_DIM, RET_V_DIM
    log_g = jnp.log1p(-jnp.exp2(-5.0 - jnp.arange(nh, dtype=F32)))
    idx = jnp.arange(cl, dtype=F32)
    diff = idx[:, None] - idx[None, :]
    dm = jnp.where(diff >= 0, jnp.exp(jnp.maximum(diff, 0.0) * log_g[:, None, None]), 0.0)
    xi = jnp.broadcast_to(jnp.exp((idx + 1.0) * log_g[:, None])[..., None], (nh, cl, LANES))
    zeta = jnp.broadcast_to(jnp.exp((cl - 1.0 - idx) * log_g[:, None])[..., None], (nh, cl, LANES))
    chunk_decay = jnp.exp(cl * log_g)
    qk_w, v_w = RET_QK_WIDTH, RET_V_WIDTH
    assert v_w == 2 * qk_w
    tab = pl.BlockSpec((1, cl, LANES), lambda bi, c: (bi, c, 0))

    def table(shape):
        return pl.BlockSpec(shape, lambda bi, c: (0, 0, 0))

    return pl.pallas_call(
        functools.partial(_ret_kernel, cl=cl),
        grid=(b, s // cl),
        in_specs=[pl.BlockSpec(memory_space=pltpu.SMEM),
                  pl.BlockSpec((1, cl, qk_w), lambda bi, c: (bi, c, 0)),
                  pl.BlockSpec((1, cl, qk_w), lambda bi, c: (bi, c, 1)),
                  pl.BlockSpec((1, cl, v_w), lambda bi, c: (bi, c, 1)),
                  pl.BlockSpec((1, cl, v_w), lambda bi, c: (bi, c, 2)),
                  tab, tab,
                  pl.BlockSpec((1, v_w), lambda bi, c: (0, 0)),
                  table((nh, cl, cl)), table((nh, cl, LANES)), table((nh, cl, LANES))],
        out_specs=pl.BlockSpec((1, cl, v_w), lambda bi, c: (bi, c, 0)),
        out_shape=jax.ShapeDtypeStruct((b, s, v_w), BF16),
        scratch_shapes=[pltpu.VMEM((nh, dk, dv), F32)],
        compiler_params=_cparams(("parallel", "arbitrary")),
        name="retention",
    )(chunk_decay, po3, po3, po3, po3, cos, sin, norm_w.reshape(1, v_w).astype(F32), dm, xi, zeta)


def _router_kernel(x_ref, g_ref, wr_ref, xn_ref, route_ref, cnt_ref, carry_sc):
    @pl.when(pl.program_id(0) == 0)
    def _():
        carry_sc[...] = jnp.zeros(carry_sc.shape, F32)

    x = x_ref[...]
    t = x.shape[0]
    y = x * lax.rsqrt(jnp.mean(x * x, axis=-1, keepdims=True) + RMS_EPS) * g_ref[...]
    xn_ref[...] = y.astype(BF16)
    w = wr_ref[...]
    y_hi = y.astype(BF16)
    y_lo = (y - y_hi.astype(F32)).astype(BF16)
    w_hi = w.astype(BF16)
    w_lo = (w - w_hi.astype(F32)).astype(BF16)
    logits = (jnp.dot(y_hi, w_hi, preferred_element_type=F32) + jnp.dot(y_lo, w_hi, preferred_element_type=F32)
              + jnp.dot(y_hi, w_lo, preferred_element_type=F32))
    lane = lax.broadcasted_iota(I32, logits.shape, 1)
    lg = jnp.where(lane < N_EXPERTS, logits, LOWEST)
    v1 = jnp.max(lg, axis=-1, keepdims=True)
    i1 = jnp.min(jnp.where(lg == v1, lane, LANES), axis=-1, keepdims=True)
    lg2 = jnp.where(lane == i1, LOWEST, lg)
    v2 = jnp.max(lg2, axis=-1, keepdims=True)
    i2 = jnp.min(jnp.where(lg2 == v2, lane, LANES), axis=-1, keepdims=True)
    e2 = jnp.exp(v2 - v1)
    g1 = 1.0 / (1.0 + e2)
    g2 = e2 / (1.0 + e2)

    chosen = jnp.where((lane == i1) | (lane == i2), 1.0, 0.0)
    r_i = lax.broadcasted_iota(I32, (t, t), 0)
    c_i = lax.broadcasted_iota(I32, (t, t), 1)
    tri = jnp.where(c_i <= r_i, 1.0, 0.0).astype(BF16)
    seen = jnp.dot(tri, chosen.astype(BF16), preferred_element_type=F32) + carry_sc[...]
    rank1 = jnp.sum(jnp.where(lane == i1, seen, 0.0), axis=-1, keepdims=True) - 1.0
    rank2 = jnp.sum(jnp.where(lane == i2, seen, 0.0), axis=-1, keepdims=True) - 1.0
    total = seen[t - 1:t, :]
    carry_sc[...] = total
    cnt_ref[...] = jnp.broadcast_to(total, cnt_ref.shape)

    route = jnp.where(lane == 0, i1.astype(F32), 0.0)
    route = jnp.where(lane == 1, i2.astype(F32), route)
    route = jnp.where(lane == 2, g1, route)
    route = jnp.where(lane == 3, g2, route)
    route = jnp.where(lane == 4, rank1, route)
    route = jnp.where(lane == 5, rank2, route)
    route_ref[...] = route


def moe_route(x, g, w_router, tm=256):
    m, d = x.shape
    tm = _tile(m, tm, 16)
    wr = jnp.zeros((d, LANES), F32).at[:, :N_EXPERTS].set(w_router.astype(F32))
    return pl.pallas_call(
        _router_kernel,
        grid=(m // tm,),
        in_specs=[pl.BlockSpec((tm, d), lambda i: (i, 0)),
                  pl.BlockSpec((1, d), lambda i: (0, 0)),
                  pl.BlockSpec((d, LANES), lambda i: (0, 0))],
        out_specs=[pl.BlockSpec((tm, d), lambda i: (i, 0)),
                   pl.BlockSpec((tm, LANES), lambda i: (i, 0)),
                   pl.BlockSpec((8, LANES), lambda i: (0, 0))],
        out_shape=[jax.ShapeDtypeStruct((m, d), BF16),
                   jax.ShapeDtypeStruct((m, LANES), F32),
                   jax.ShapeDtypeStruct((8, LANES), F32)],
        scratch_shapes=[pltpu.VMEM((1, LANES), F32)],
        compiler_params=_cparams(("arbitrary",)),
        name="moe_route",
    )(x, g.reshape(1, d).astype(F32), wr)


def _dispatch_kernel(d1_ref, d2_ref, last_ref, x_ref, o_ref, zero_sc, sem, zsem, *, rows, tm, n_experts):
    i = pl.program_id(0)

    @pl.when(i == 0)
    def _():
        zero_sc[...] = jnp.zeros(zero_sc.shape, zero_sc.dtype)
        fills = [pltpu.make_async_copy(zero_sc, o_ref.at[pl.ds(last_ref[e], tm)], zsem) for e in range(n_experts)]
        for c in fills:
            c.start()
        for c in fills:
            c.wait()

        def clear_unused(blk, carry):
            c = pltpu.make_async_copy(zero_sc, o_ref.at[pl.ds(pl.multiple_of(blk * tm, tm), tm)], zsem)
            c.start()
            c.wait()
            return carry

        lax.fori_loop(last_ref[n_experts], o_ref.shape[0] // tm, clear_unused, 0)

    base = i * rows

    def issue(r8, carry):
        for u in range(8):
            r = r8 * 8 + u
            pltpu.make_async_copy(x_ref.at[r], o_ref.at[d1_ref[base + r]], sem).start(priority=0)
            pltpu.make_async_copy(x_ref.at[r], o_ref.at[d2_ref[base + r]], sem).start(priority=1)
        return carry

    lax.fori_loop(0, rows // 8, issue, 0)
    for _ in range(2):
        pltpu.make_async_copy(x_ref, o_ref.at[pl.ds(0, rows)], sem).wait()


def moe_dispatch(xn, dest1, dest2, last_block_row, n_rows, tm, rows=1024):
    m, d = xn.shape
    rows = _tile(m, rows, 8)
    slab = (d // LANES, LANES)
    out = pl.pallas_call(
        functools.partial(_dispatch_kernel, rows=rows, tm=tm, n_experts=last_block_row.shape[0] - 1),
        grid_spec=pltpu.PrefetchScalarGridSpec(
            num_scalar_prefetch=3,
            grid=(m // rows,),
            in_specs=[pl.BlockSpec((rows,) + slab, lambda i, a, b, c: (i, 0, 0))],
            out_specs=pl.BlockSpec(memory_space=pl.ANY),
            scratch_shapes=[pltpu.VMEM((tm,) + slab, xn.dtype), pltpu.SemaphoreType.DMA(()),
                            pltpu.SemaphoreType.DMA(())],
        ),
        out_shape=jax.ShapeDtypeStruct((n_rows,) + slab, xn.dtype),
        compiler_params=_cparams(("arbitrary",)),
        name="moe_dispatch",
    )(dest1, dest2, last_block_row, xn.reshape((m,) + slab))
    return out.reshape(n_rows, d)


def _combine_kernel(d1_ref, d2_ref, x_ref, route_ref, yb_ref, o_ref, buf1, buf2, sem, *, rows):
    i = pl.program_id(0)
    slot = i % 2

    def issue(step, to_slot):
        base = step * rows

        def rows8(r8, carry):
            r0 = pl.multiple_of(r8 * 8, 8)
            for u in range(8):
                pltpu.make_async_copy(yb_ref.at[pl.ds(d1_ref[base + r0 + u], 1)],
                                      buf1.at[to_slot, pl.ds(r0 + u, 1)], sem.at[to_slot, 0]).start(priority=0)
                pltpu.make_async_copy(yb_ref.at[pl.ds(d2_ref[base + r0 + u], 1)],
                                      buf2.at[to_slot, pl.ds(r0 + u, 1)], sem.at[to_slot, 1]).start(priority=1)
            return carry

        lax.fori_loop(0, rows // 8, rows8, 0)

    @pl.when(i == 0)
    def _():
        issue(0, 0)

    @pl.when(i + 1 < pl.num_programs(0))
    def _():
        issue(i + 1, 1 - slot)

    pltpu.make_async_copy(yb_ref.at[pl.ds(0, rows)], buf1.at[slot], sem.at[slot, 0]).wait()
    pltpu.make_async_copy(yb_ref.at[pl.ds(0, rows)], buf2.at[slot], sem.at[slot, 1]).wait()
    route = route_ref[...]
    o_ref[...] = x_ref[...] + (buf1[slot] * route[:, 2:3] + buf2[slot] * route[:, 3:4])


def moe_combine(x, route, yb, dest1, dest2, rows=256):
    m, d = x.shape
    rows = _tile(m, rows, 8)
    return pl.pallas_call(
        functools.partial(_combine_kernel, rows=rows),
        grid_spec=pltpu.PrefetchScalarGridSpec(
            num_scalar_prefetch=2,
            grid=(m // rows,),
            in_specs=[pl.BlockSpec((rows, d), lambda i, a, b: (i, 0)),
                      pl.BlockSpec((rows, LANES), lambda i, a, b: (i, 0)),
                      pl.BlockSpec(memory_space=pl.ANY)],
            out_specs=pl.BlockSpec((rows, d), lambda i, a, b: (i, 0)),
            scratch_shapes=[pltpu.VMEM((2, rows, d), F32), pltpu.VMEM((2, rows, d), F32),
                            pltpu.SemaphoreType.DMA((2, 2))],
        ),
        out_shape=jax.ShapeDtypeStruct((m, d), F32),
        compiler_params=_cparams(("arbitrary",)),
        name="moe_combine",
    )(dest1, dest2, x, route, yb)


def moe_layer(x, norm_g, w_router, w_gate, w_up, w_down):
    m, d = x.shape
    tm = MOE_TM
    xn, route, cnt = moe_route(x, norm_g, w_router)
    expert = route[:, 0:2].astype(I32)
    rank = route[:, 4:6].astype(I32)
    counts = cnt[0, :N_EXPERTS].astype(I32)
    padded = (counts + tm - 1) // tm * tm
    pend = jnp.cumsum(padded)
    pstart = pend - padded
    dest = pstart[expert] + rank
    n_blk = -(-(2 * m) // tm) + N_EXPERTS
    n_rows = n_blk * tm
    nused = (pend[-1] // tm).astype(I32)
    last_block_row = jnp.concatenate([jnp.maximum(pend - tm, 0).astype(I32), nused.reshape(1)])
    blk = jnp.minimum(jnp.arange(n_blk, dtype=I32), nused - 1)
    blk_e = jnp.minimum(jnp.sum(pend[None, :] <= (blk * tm)[:, None], axis=1), N_EXPERTS - 1).astype(I32)
    nu = nused.reshape(1)

    xb = moe_dispatch(xn, dest[:, 0], dest[:, 1], last_block_row, n_rows, tm)
    tf = _tile(w_gate.shape[2], 512, LANES)
    hidden = gmm_swiglu(xb, w_gate, w_up, blk_e, nu, tm=tm, tn=tf)
    yb = gmm(hidden, w_down, blk_e, nu, tm=tm, tn=_tile(d, 1024, LANES), out_dtype=F32)
    return moe_combine(x, route, yb, dest[:, 0], dest[:, 1])


def even_layer(x2, b, s, pos, w_norm, w_in, q_gain, k_gain, w_cmp_k, w_cmp_v, pe_cmp, conv_w, f_bias, m_norm,
               w_out, w_norm_ffn, w_gate, w_up, w_down):
    n, dm = x2.shape
    g, d = NSA_GROUPS, HEAD_DIM
    o_gate = NSA_SLAB
    o_qb = o_gate + NSA_HEADS * 3
    o_if = o_qb + 3 * MLSTM_WIDTH
    o_ob = o_if + 2 * MLSTM_HEADS
    w_ml = jnp.concatenate([w_in[:, o_qb:o_if], w_in[:, o_ob:o_ob + MLSTM_WIDTH]], axis=1)
    per_g = NSA_HPG * 3
    zeros = functools.partial(jnp.zeros, dtype=w_in.dtype)
    w_small = jnp.concatenate(
        [w_in[:, o_gate:o_gate + per_g], zeros((dm, GATE_I_LANE - per_g)),
         w_in[:, o_if:o_ob], zeros((dm, LANES - GATE_I_LANE - 2 * MLSTM_HEADS)),
         w_in[:, o_gate + per_g:o_qb], zeros((dm, LANES - per_g))], axis=1)

    h = rmsnorm_rows(x2, w_norm)
    p3 = mm(h, w_in[:, :o_gate], out_dtype=BF16, tn=NSA_SLAB // 2).reshape(b, s, NSA_SLAB)
    pm3 = mm(h, w_ml, out_dtype=BF16, tn=1024).reshape(b, s, 4 * MLSTM_WIDTH)
    gates3 = mm(h, w_small, out_dtype=F32, tn=2 * LANES).reshape(b, s, 2 * LANES)

    cos_n, sin_n = trig_tables(pos, _nsa_inv_lane())
    q_t, k_n, v_t = nsa_prep(p3, cos_n.reshape(b, s, LANES), sin_n.reshape(b, s, LANES), q_gain, k_gain)
    nh = s // CMP_STRIDE
    cmp_pos = jnp.concatenate([pos[:, CMP_BLOCK - 1::CMP_STRIDE][:, :nh - 1], pos[:, -1:]], axis=1)
    cos_c, sin_c = trig_tables(cmp_pos, _nsa_inv_lane())

    def cmp_blocks(col0):
        tok = p3[:, :, col0:col0 + NSA_KV_WIDTH].reshape(b, nh, CMP_STRIDE, g, d)
        halves = tok.transpose(0, 3, 1, 2, 4).reshape(b, g, nh, CMP_STRIDE * d)
        nxt = jnp.concatenate([halves[:, :, 1:], jnp.zeros_like(halves[:, :, :1])], axis=2)
        return jnp.concatenate([halves, nxt], axis=-1).reshape(b * g * nh, CMP_BLOCK * d)

    tabs = (cos_c.reshape(b, nh, LANES), sin_c.reshape(b, nh, LANES))
    kc = compress(cmp_blocks(NSA_WIDTH), pe_cmp, w_cmp_k, k_gain, *tabs, is_key=True, rows_per_seq=nh, groups=g)
    vc = compress(cmp_blocks(NSA_WIDTH + NSA_KV_WIDTH), pe_cmp, w_cmp_v, k_gain, *tabs, is_key=False,
                  rows_per_seq=nh, groups=g)
    o_nsa = nsa_attention(q_t, k_n, v_t, kc, vc, gates3)

    qk = conv_silu(pm3, conv_w, 0, 2 * MLSTM_WIDTH)
    h_b = mlstm(qk, pm3, gates3, f_bias, m_norm, 2 * MLSTM_WIDTH, 3 * MLSTM_WIDTH)

    x2, hf = proj_res_norm(o_nsa.reshape(n, NSA_WIDTH), h_b.reshape(n, MLSTM_WIDTH), w_out, x2, w_norm_ffn)
    hidden = mm_swiglu(hf, w_gate, w_up, tm=1024)
    return mm(hidden, w_down, out_dtype=F32, residual=x2, tm=512, tn=1024)


def odd_layer(x2, b, s, pos, w_norm, w_in, r_norm, w_out, w_norm_ffn, w_router, e_gate, e_up, e_down):
    n, dm = x2.shape
    h = rmsnorm_rows(x2, w_norm)
    po3 = mm(h, w_in, out_dtype=BF16, tn=1024).reshape(b, s, -1)
    cos_r, sin_r = trig_tables(pos, _ret_inv_lane())
    y = retention(po3, cos_r.reshape(b, s, LANES), sin_r.reshape(b, s, LANES), r_norm)
    x2 = mm(y.reshape(n, RET_V_WIDTH), w_out, out_dtype=F32, residual=x2, tm=512, tn=1024)
    return moe_layer(x2, w_norm_ffn, w_router, e_gate, e_up, e_down)


def kernel(x, positions, norm_mix_even, w_in_even, nsa_q_gain, nsa_k_gain, w_cmp_k, w_cmp_v, pe_cmp, mlstm_conv, mlstm_f_bias, mlstm_norm, w_out_even, norm_ffn_even, ffn_gate, ffn_up, ffn_down, norm_mix_odd, w_in_odd, ret_norm, w_out_odd, norm_ffn_odd, w_router, exp_gate, exp_up, exp_down):
    b, s, dm = x.shape
    depth = norm_mix_even.shape[0] + norm_mix_odd.shape[0]
    x2 = x.reshape(b * s, dm)
    for layer in range(depth):
        j = layer // 2
        if layer % 2 == 0:
            x2 = even_layer(x2, b, s, positions, norm_mix_even[j], w_in_even[j], nsa_q_gain[j], nsa_k_gain[j],
                            w_cmp_k[j], w_cmp_v[j], pe_cmp[j], mlstm_conv[j], mlstm_f_bias[j], mlstm_norm[j],
                            w_out_even[j], norm_ffn_even[j], ffn_gate[j], ffn_up[j], ffn_down[j])
        else:
            x2 = odd_layer(x2, b, s, positions, norm_mix_odd[j], w_in_odd[j], ret_norm[j], w_out_odd[j],
                           norm_ffn_odd[j], w_router[j], exp_gate[j], exp_up[j], exp_down[j])
    return x2.reshape(b, s, dm)
```
